```python
import jax, jax.numpy as jnp
from jax import lax
import numpy as np

D_MODEL = 1024
BATCH = 2
SEQ = 8192
DEPTH = 1

GRID_W = 64
CTX_LEN = 256
N_MOD = 6
EPS = 1e-6
HGRN_HEADS = 8
HGRN_DK = 128
HGRN_DV = 128
HGRN_W = HGRN_HEADS * HGRN_DK
HGRN_CHUNK = 64
ATT_HEADS = 8
ATT_KV_HEADS = 2
ATT_GROUPS = ATT_HEADS // ATT_KV_HEADS
HEAD_DIM = 128
ATT_W = ATT_HEADS * HEAD_DIM
KV_W = ATT_KV_HEADS * HEAD_DIM
ROPE_AXIS_DIM = HEAD_DIM // 2
ROPE_THETA = 10000.0
Q_BLOCK = 128
N_BRANCH = 2
N_EXPERTS = 32
TOP_K = 4
D_EXPERT = 1024
SWIGLU_LIMIT = 7.0
SWIGLU_ALPHA = 1.702
IN_SIZES = (HGRN_W, HGRN_W, HGRN_W, HGRN_W, HGRN_W, ATT_W, KV_W, KV_W, D_MODEL, D_MODEL)
IN_COLS = sum(IN_SIZES)

kernel_name = "hybrid_hgrn2_gqa_moe_dit_layer"


def rmsnorm(x, g):
    xf = x.astype(jnp.float32)
    y = xf * lax.rsqrt(jnp.mean(xf * xf, axis=-1, keepdims=True) + EPS)
    return (y * g.astype(jnp.float32)).astype(x.dtype)


def modulation(cv, w, b):
    m = jax.nn.silu(cv) @ w + b
    return [a[:, None, :] for a in jnp.split(m, N_MOD, axis=-1)]


def modulate(xn, shift, scale):
    return xn * (1.0 + scale) + shift


def split_cols(p):
    return jnp.split(p, np.cumsum(IN_SIZES)[:-1].tolist(), axis=-1)


def axial_rope(x, rows, cols):
    inv = ROPE_THETA ** (-jnp.arange(0, ROPE_AXIS_DIM, 2, dtype=jnp.float32) / ROPE_AXIS_DIM)
    bshape = (x.shape[1],) + (1,) * (x.ndim - 3) + (ROPE_AXIS_DIM // 2,)

    def rot(xa, pos):
        ang = (pos.astype(jnp.float32)[:, None] * inv[None, :]).reshape(bshape)
        cos, sin = jnp.cos(ang), jnp.sin(ang)
        xf = xa.astype(jnp.float32)
        x1, x2 = xf[..., : ROPE_AXIS_DIM // 2], xf[..., ROPE_AXIS_DIM // 2:]
        return jnp.concatenate([x1 * cos - x2 * sin, x2 * cos + x1 * sin], axis=-1).astype(xa.dtype)

    return jnp.concatenate([rot(x[..., :ROPE_AXIS_DIM], rows), rot(x[..., ROPE_AXIS_DIM:], cols)], axis=-1)


def hgrn_gates(q_raw, i_raw, ff_raw, fb_raw, lb):
    B, L, _ = q_raw.shape

    def heads(a):
        return a.reshape(B, L, HGRN_HEADS, HGRN_DK).transpose(0, 2, 1, 3).astype(jnp.float32)

    def both(a_fwd, a_bwd):
        return jnp.stack([a_fwd, a_bwd[:, :, ::-1]], axis=0)

    q, v = heads(q_raw), heads(i_raw)
    lbh = lb.reshape(2, 1, HGRN_HEADS, 1, HGRN_DK)
    f = lbh + (1.0 - lbh) * jax.nn.sigmoid(both(heads(ff_raw), heads(fb_raw)))
    return both(q, q), 1.0 - f, both(v, v), jnp.log(f)


def gla_scan(q, k, v, log_f, s0):
    N, B, H, L, K = q.shape
    V = v.shape[-1]
    C = HGRN_CHUNK
    nc = L // C
    mask = jnp.tril(jnp.ones((C, C), dtype=bool))

    def to_chunks(a):
        return jnp.moveaxis(a.reshape(a.shape[:3] + (nc, C, a.shape[-1])), 3, 0)

    def step(S, inp):
        qc, kc, vc, gc = inp
        b = jnp.cumsum(gc, axis=-2)
        diff = b[..., :, None, :] - b[..., None, :, :]
        decay = jnp.exp(jnp.where(mask[:, :, None], diff, -jnp.inf))
        att = jnp.einsum('nbhtk,nbhtsk,nbhsk->nbhts', qc, decay, kc)
        o = jnp.einsum('nbhts,nbhsv->nbhtv', att, vc) + jnp.einsum('nbhtk,nbhkv->nbhtv', qc * jnp.exp(b), S)
        b_end = b[..., -1:, :]
        S = jnp.exp(b_end)[..., 0, :, None] * S + jnp.einsum('nbhsk,nbhsv->nbhkv', kc * jnp.exp(b_end - b), vc)
        return S, o

    s_fin, o = lax.scan(step, s0, (to_chunks(q), to_chunks(k), to_chunks(v), to_chunks(log_f)))
    return jnp.moveaxis(o, 0, 3).reshape(N, B, H, L, V), s_fin


def hgrn_readout(o_d, g_raw, norm_g):
    o = o_d[0] + o_d[1][:, :, ::-1]
    B, _, L, _ = o.shape
    o = rmsnorm(o, norm_g).transpose(0, 2, 1, 3).reshape(B, L, HGRN_W)
    return o.astype(g_raw.dtype) * jax.nn.silu(g_raw)


def attn_qkv(q_raw, k_raw, v_raw, qk_norm_g):
    B, L, _ = q_raw.shape
    q = rmsnorm(q_raw.reshape(B, L, ATT_KV_HEADS, ATT_GROUPS, HEAD_DIM), qk_norm_g[0])
    k = rmsnorm(k_raw.reshape(B, L, ATT_KV_HEADS, HEAD_DIM), qk_norm_g[1])
    v = v_raw.reshape(B, L, ATT_KV_HEADS, HEAD_DIM)
    return q, k, v


def attend(qblk, k, v):
    s = jnp.einsum('bqkgd,bskd->bkgqs', qblk, k).astype(jnp.float32) * (HEAD_DIM ** -0.5)
    p = jax.nn.softmax(s, axis=-1).astype(v.dtype)
    return jnp.einsum('bkgqs,bskd->bqkgd', p, v)


def blocked_attention(q, k_all, v_all):
    B, L = q.shape[:2]
    nb = L // Q_BLOCK
    qb = jnp.moveaxis(q.reshape(B, nb, Q_BLOCK, ATT_KV_HEADS, ATT_GROUPS, HEAD_DIM), 1, 0)
    o = lax.map(lambda qblk: attend(qblk, k_all, v_all), qb)
    return jnp.moveaxis(o, 0, 1).reshape(B, L, ATT_W)


def merge(o_h, o_a, gate_h_raw, gate_a_raw, w_branch, w_out):
    y = jax.nn.sigmoid(gate_h_raw) * (o_h @ w_branch[0]) + jax.nn.sigmoid(gate_a_raw) * (o_a @ w_branch[1])
    return y @ w_out


def token_mixer(u_lat, u_ctx, rows, cols, w_in, lb, hgrn_norm_g, qk_norm_g, w_branch, w_out, need_ctx):
    pl = split_cols(u_lat @ w_in)
    pc = split_cols(u_ctx @ w_in)
    qd, kd, vd, fd = hgrn_gates(pc[0], pc[1], pc[2], pc[3], lb)
    s0 = jnp.zeros((2, u_ctx.shape[0], HGRN_HEADS, HGRN_DK, HGRN_DV), jnp.float32)
    o_ctx_d, s_ctx = gla_scan(qd, kd, vd, fd, s0)
    ql, kl, vl, fl = hgrn_gates(pl[0], pl[1], pl[2], pl[3], lb)
    o_lat_d, _ = gla_scan(ql, kl, vl, fl, s_ctx)
    o_h = hgrn_readout(o_lat_d, pl[4], hgrn_norm_g)
    q, k, v = attn_qkv(pl[5], pl[6], pl[7], qk_norm_g)
    q, k = axial_rope(q, rows, cols), axial_rope(k, rows, cols)
    qc, kc, vc = attn_qkv(pc[5], pc[6], pc[7], qk_norm_g)
    k_all = jnp.concatenate([k, kc], axis=1)
    v_all = jnp.concatenate([v, vc], axis=1)
    o_a = blocked_attention(q, k_all, v_all)
    y_lat = merge(o_h, o_a, pl[8], pl[9], w_branch, w_out)
    y_ctx = None
    if need_ctx:
        oh_c = hgrn_readout(o_ctx_d, pc[4], hgrn_norm_g)
        oa_c = attend(qc, kc, vc).reshape(u_ctx.shape[0], u_ctx.shape[1], ATT_W)
        y_ctx = merge(oh_c, oa_c, pc[8], pc[9], w_branch, w_out)
    return y_lat, y_ctx


def moe(u, router_w, router_b, w_up, b_up, w_down, b_down):
    B, L, D = u.shape
    t = u.reshape(B * L, D)
    logits = (t @ router_w + router_b).astype(jnp.float32)
    top_v, top_i = lax.top_k(logits, TOP_K)
    w = jax.nn.softmax(top_v, axis=-1)
    combine = jnp.sum(jax.nn.one_hot(top_i, N_EXPERTS, dtype=jnp.float32) * w[..., None], axis=1)
    out = jnp.zeros((B * L, D), jnp.float32)
    for e in range(N_EXPERTS):
        h = t @ w_up[e] + b_up[e]
        glu, lin = h[:, :D_EXPERT], h[:, D_EXPERT:]
        glu = jnp.minimum(glu, SWIGLU_LIMIT)
        lin = jnp.clip(lin, -SWIGLU_LIMIT, SWIGLU_LIMIT)
        a = glu * jax.nn.sigmoid(SWIGLU_ALPHA * glu) * (lin + 1.0)
        out = out + combine[:, e:e + 1] * (a @ w_down[e] + b_down[e])
    return out.astype(u.dtype).reshape(B, L, D)


def setup_inputs(seed: int = 0) -> dict:
    key = jax.random.key(seed)
    ks = jax.random.split(key, 24)
    f32 = jnp.float32
    D = D_MODEL

    def nrm(k, shape, scale):
        return jax.random.normal(k, shape, f32) * scale

    return {
        "x": nrm(ks[0], (BATCH, SEQ, D), 1.0),
        "c": nrm(ks[1], (BATCH, D), 1.0),
        "ctx": nrm(ks[2], (BATCH, CTX_LEN, D), 1.0),
        "c_ctx": nrm(ks[3], (D,), 1.0),
        "w_mod": nrm(ks[4], (DEPTH, D, N_MOD * D), 0.5 * D ** -0.5),
        "b_mod": nrm(ks[5], (DEPTH, N_MOD * D), 0.02),
        "norm_g": 1.0 + nrm(ks[6], (DEPTH, 4, D), 0.05),
        "w_in": nrm(ks[7], (DEPTH, D, IN_COLS), D ** -0.5),
        "hgrn_lb": nrm(ks[8], (2, DEPTH + 1, HGRN_W), 0.1) + jnp.arange(DEPTH + 1, dtype=f32)[None, :, None],
        "hgrn_norm_g": 1.0 + nrm(ks[9], (DEPTH, HGRN_DV), 0.05),
        "qk_norm_g": 1.0 + nrm(ks[10], (DEPTH, 2, HEAD_DIM), 0.05),
        "w_branch": nrm(ks[11], (DEPTH, N_BRANCH, HGRN_W, D), HGRN_W ** -0.5),
        "w_out": nrm(ks[12], (DEPTH, D, D), D ** -0.5),
        "router_w": nrm(ks[13], (DEPTH, D, N_EXPERTS), D ** -0.5),
        "router_b": nrm(ks[14], (DEPTH, N_EXPERTS), 0.01),
        "w_up": nrm(ks[15], (DEPTH, N_EXPERTS, D, 2 * D_EXPERT), D ** -0.5),
        "b_up": nrm(ks[16], (DEPTH, N_EXPERTS, 2 * D_EXPERT), 0.02),
        "w_down": nrm(ks[17], (DEPTH, N_EXPERTS, D_EXPERT, D), D_EXPERT ** -0.5),
        "b_down": nrm(ks[18], (DEPTH, N_EXPERTS, D), 0.02),
    }


def reference(x, c, ctx, c_ctx, w_mod, b_mod, norm_g, w_in, hgrn_lb, hgrn_norm_g, qk_norm_g,
              w_branch, w_out, router_w, router_b, w_up, b_up, w_down, b_down):
    B, L, _ = x.shape
    n_rows = L // GRID_W
    rows = jnp.repeat(jnp.arange(n_rows, dtype=jnp.int32), GRID_W)
    cols = jnp.tile(jnp.arange(GRID_W, dtype=jnp.int32), n_rows)
    lb_all = jnp.cumsum(jax.nn.softmax(hgrn_lb.astype(jnp.float32), axis=1), axis=1)
    h = ctx
    for l in range(DEPTH):
        last = l == DEPTH - 1
        g = norm_g[l]
        m_lat = modulation(c, w_mod[l], b_mod[l])
        m_ctx = modulation(c_ctx[None], w_mod[l], b_mod[l])
        u_lat = modulate(rmsnorm(x, g[0]), m_lat[0], m_lat[1])
        u_ctx = modulate(rmsnorm(h, g[0]), m_ctx[0], m_ctx[1])
        y_lat, y_ctx = token_mixer(u_lat, u_ctx, rows, cols, w_in[l], lb_all[:, l], hgrn_norm_g[l],
                                   qk_norm_g[l], w_branch[l], w_out[l], not last)
        x = x + m_lat[2] * rmsnorm(y_lat, g[1])
        f_lat = moe(modulate(rmsnorm(x, g[2]), m_lat[3], m_lat[4]),
                    router_w[l], router_b[l], w_up[l], b_up[l], w_down[l], b_down[l])
        x = x + m_lat[5] * rmsnorm(f_lat, g[3])
        if not last:
            h = h + m_ctx[2] * rmsnorm(y_ctx, g[1])
            f_ctx = moe(modulate(rmsnorm(h, g[2]), m_ctx[3], m_ctx[4]),
                        router_w[l], router_b[l], w_up[l], b_up[l], w_down[l], b_down[l])
            h = h + m_ctx[5] * rmsnorm(f_ctx, g[3])
    return x
```

```python
import functools

import jax
import jax.numpy as jnp
import numpy as np
from jax import lax
from jax.experimental import pallas as pl
from jax.experimental.pallas import tpu as pltpu

F32 = jnp.float32
BF16 = jnp.bfloat16
I32 = jnp.int32

D_MODEL = 1024
GRID_W = 64
N_MOD = 6
EPS = 1e-6
HGRN_HEADS = 8
HGRN_DK = 128
HGRN_W = HGRN_HEADS * HGRN_DK
ATT_HEADS = 8
ATT_KV_HEADS = 2
ATT_GROUPS = ATT_HEADS // ATT_KV_HEADS
HEAD_DIM = 128
ATT_W = ATT_HEADS * HEAD_DIM
KV_W = ATT_KV_HEADS * HEAD_DIM
ROPE_AXIS_DIM = HEAD_DIM // 2
ROPE_THETA = 10000.0
N_EXPERTS = 32
TOP_K = 4
D_EXPERT = 1024
SWIGLU_LIMIT = 7.0
SWIGLU_ALPHA = 1.702

COL_HQ, COL_HI, COL_FF, COL_FB, COL_HG, COL_AQ, COL_GH, COL_GA = (i * 1024 for i in range(8))
COL_K = 8 * 1024
COL_V = COL_K + KV_W
IN_COLS = COL_V + KV_W

V7X_VMEM_BYTES = 64 * 1024 * 1024
LANES = 128

HGRN_CHUNK = 64
HGRN_SUB = 16
MOE_TILE = 256
TOK_TILE = 256


def _vmem_limit(nbytes):
    return int(min(V7X_VMEM_BYTES - 6 * 1024 * 1024, max(nbytes, 32 * 1024 * 1024)))


def _largest_tile(n, cap, mult):
    best = None
    for t in range(mult, min(n, cap) + 1, mult):
        if n % t == 0:
            best = t
    assert best is not None, (n, cap, mult)
    return best


def _rms(x, g):
    return x * lax.rsqrt(jnp.mean(x * x, axis=-1, keepdims=True) + EPS) * g


def _dot(a, b):
    return jnp.dot(a, b, preferred_element_type=F32)


def _dot_nt(a, b):
    return lax.dot_general(a, b, (((1,), (1,)), ((), ())), preferred_element_type=F32)


def _dot_tn(a, b):
    return lax.dot_general(a, b, (((0,), (0,)), ((), ())), preferred_element_type=F32)


def _mod_kernel(cv_ref, w_ref, b_ref, o_ref):
    cv = cv_ref[...]
    s = cv * jax.nn.sigmoid(cv)
    o_ref[...] = jnp.dot(s, w_ref[...], preferred_element_type=F32,
                         precision=lax.Precision.HIGHEST) + b_ref[...]


def _modulation(cv, w_mod, b_mod):
    rows, d = cv.shape
    n = w_mod.shape[1]
    tn = 1024
    return pl.pallas_call(
        _mod_kernel,
        grid=(n // tn,),
        in_specs=[pl.BlockSpec((rows, d), lambda j: (0, 0)),
                  pl.BlockSpec((d, tn), lambda j: (0, j)),
                  pl.BlockSpec((1, tn), lambda j: (0, j))],
        out_specs=pl.BlockSpec((rows, tn), lambda j: (0, j)),
        out_shape=jax.ShapeDtypeStruct((rows, n), F32),
        name="modulation",
    )(cv, w_mod, b_mod.reshape(1, n))


def _inproj_kernel(x_ref, mod_ref, g_ref, w_ref, o_ref, *, tm, n_ctx, n_batch):
    b = pl.program_id(1)
    i = pl.program_id(2)
    x = x_ref[0]
    xn = _rms(x, g_ref[0:1, :])
    row = i * tm + lax.broadcasted_iota(I32, (tm, 1), 0)
    is_ctx = row < n_ctx
    m_lat = mod_ref[pl.ds(b, 1), :]
    m_ctx = mod_ref[n_batch:n_batch + 1, :]
    shift = jnp.where(is_ctx, m_ctx[:, 0:D_MODEL], m_lat[:, 0:D_MODEL])
    scale = jnp.where(is_ctx, m_ctx[:, D_MODEL:2 * D_MODEL], m_lat[:, D_MODEL:2 * D_MODEL])
    u = xn * (1.0 + scale) + shift
    o_ref[0] = _dot(u.astype(BF16), w_ref[...]).astype(BF16)


def _in_projection(xc, mod, norm_g, w_in_bf, n_ctx):
    nb, lc, d = xc.shape
    n = w_in_bf.shape[1]
    tm = _largest_tile(lc, 1056, 16)
    tn = 2176
    assert n % tn == 0
    kern = functools.partial(_inproj_kernel, tm=tm, n_ctx=n_ctx, n_batch=nb)
    vmem = 2 * (tm * d * 4 + d * tn * 2 + tm * tn * 2) + 6 * tm * d * 4
    return pl.pallas_call(
        kern,
        grid=(n // tn, nb, lc // tm),
        in_specs=[pl.BlockSpec((1, tm, d), lambda j, b, i: (b, i, 0)),
                  pl.BlockSpec(mod.shape, lambda j, b, i: (0, 0)),
                  pl.BlockSpec(norm_g.shape, lambda j, b, i: (0, 0)),
                  pl.BlockSpec((d, tn), lambda j, b, i: (0, j))],
        out_specs=pl.BlockSpec((1, tm, tn), lambda j, b, i: (b, i, j)),
        out_shape=jax.ShapeDtypeStruct((nb, lc, n), BF16),
        compiler_params=pltpu.CompilerParams(
            dimension_semantics=("arbitrary", "arbitrary", "arbitrary"),
            vmem_limit_bytes=_vmem_limit(vmem)),
        name="in_projection",
    )(xc, mod, norm_g, w_in_bf)


def _rope(xn, cos, sin_a, sin_b):
    return (xn * cos + pltpu.roll(xn, HEAD_DIM - ROPE_AXIS_DIM // 2, 1) * sin_a
            + pltpu.roll(xn, ROPE_AXIS_DIM // 2, 1) * sin_b)


def _qkprep_kernel(q_ref, k_ref, cos_ref, sa_ref, sb_ref, g_ref, qo_ref, kt_ref):
    cos, sa, sb = cos_ref[...], sa_ref[...], sb_ref[...]
    gq, gk = g_ref[0:1, :], g_ref[1:2, :]
    qscale = HEAD_DIM ** -0.5
    for h in range(ATT_HEADS):
        hs = slice(h * HEAD_DIM, (h + 1) * HEAD_DIM)
        xn = _rms(q_ref[0, :, hs].astype(F32), gq)
        qo_ref[0, :, hs] = (_rope(xn, cos, sa, sb) * qscale).astype(BF16)
    for h in range(ATT_KV_HEADS):
        hs = slice(h * HEAD_DIM, (h + 1) * HEAD_DIM)
        xn = _rms(k_ref[0, :, hs].astype(F32), gk)
        kt_ref[0, hs, :] = _rope(xn, cos, sa, sb).T.astype(BF16)


def _qk_prep(p, cos, sin_a, sin_b, qk_norm_g):
    nb, lc, _ = p.shape
    tm = _largest_tile(lc, 512, 128)
    tab = pl.BlockSpec((tm, HEAD_DIM), lambda b, i: (i, 0))
    return pl.pallas_call(
        _qkprep_kernel,
        grid=(nb, lc // tm),
        in_specs=[pl.BlockSpec((1, tm, ATT_W), lambda b, i: (b, i, COL_AQ // ATT_W)),
                  pl.BlockSpec((1, tm, KV_W), lambda b, i: (b, i, COL_K // KV_W)),
                  tab, tab, tab,
                  pl.BlockSpec(qk_norm_g.shape, lambda b, i: (0, 0))],
        out_specs=[pl.BlockSpec((1, tm, ATT_W), lambda b, i: (b, i, 0)),
                   pl.BlockSpec((1, KV_W, tm), lambda b, i: (b, 0, i))],
        out_shape=[jax.ShapeDtypeStruct((nb, lc, ATT_W), BF16),
                   jax.ShapeDtypeStruct((nb, KV_W, lc), BF16)],
        name="qk_prep",
    )(p, p, cos, sin_a, sin_b, qk_norm_g)


def _attn_kernel(q_ref, kt_ref, v_ref, o_ref, m_ref, l_ref, acc_ref, *, tq, kc, n_kc):
    qs = jnp.concatenate([q_ref[0, :, g * HEAD_DIM:(g + 1) * HEAD_DIM] for g in range(ATT_GROUPS)], axis=0)
    m_ref[...] = jnp.full(m_ref.shape, -jnp.inf, F32)
    l_ref[...] = jnp.zeros(l_ref.shape, F32)
    acc_ref[...] = jnp.zeros(acc_ref.shape, F32)

    def body(c, carry):
        off = pl.multiple_of(c * kc, LANES)
        s = _dot(qs, kt_ref[0, :, pl.ds(off, kc)])
        m_old = m_ref[...]
        m_new = jnp.maximum(m_old, jnp.max(s, axis=-1, keepdims=True))
        alpha = jnp.exp(m_old - m_new)
        p = jnp.exp(s - m_new)
        l_ref[...] = l_ref[...] * alpha + jnp.sum(p, axis=-1, keepdims=True)
        acc_ref[...] = acc_ref[...] * alpha + _dot(p.astype(BF16), v_ref[0, pl.ds(off, kc), :])
        m_ref[...] = m_new
        return carry

    lax.fori_loop(0, n_kc, body, 0)
    o = acc_ref[...] / l_ref[...]
    for g in range(ATT_GROUPS):
        o_ref[0, :, g * HEAD_DIM:(g + 1) * HEAD_DIM] = o[g * tq:(g + 1) * tq].astype(BF16)


def _attention(qn, knt, p, n_ctx):
    nb, lc, _ = qn.shape
    n_lat = lc - n_ctx
    tq = _largest_tile(n_lat, 128, 16)
    assert n_ctx % tq == 0
    kc = _largest_tile(lc, 1408, 128)
    gw = ATT_GROUPS * HEAD_DIM
    kern = functools.partial(_attn_kernel, tq=tq, kc=kc, n_kc=lc // kc)
    rows = ATT_GROUPS * tq
    vmem = 4 * lc * HEAD_DIM * 2 + 6 * rows * kc * 4 + 8 * rows * HEAD_DIM * 4
    return pl.pallas_call(
        kern,
        grid=(nb, ATT_KV_HEADS, n_lat // tq),
        in_specs=[pl.BlockSpec((1, tq, gw), lambda b, h, i: (b, n_ctx // tq + i, h)),
                  pl.BlockSpec((1, HEAD_DIM, lc), lambda b, h, i: (b, h, 0)),
                  pl.BlockSpec((1, lc, HEAD_DIM), lambda b, h, i: (b, 0, COL_V // HEAD_DIM + h))],
        out_specs=pl.BlockSpec((1, tq, gw), lambda b, h, i: (b, i, h)),
        out_shape=jax.ShapeDtypeStruct((nb, n_lat, ATT_W), BF16),
        scratch_shapes=[pltpu.VMEM((rows, 1), F32), pltpu.VMEM((rows, 1), F32),
                        pltpu.VMEM((rows, HEAD_DIM), F32)],
        compiler_params=pltpu.CompilerParams(
            dimension_semantics=("arbitrary", "arbitrary", "arbitrary"),
            vmem_limit_bytes=_vmem_limit(vmem)),
        name="attention",
    )(qn, knt, p)


def _hgrn_chunk(q_ref, v_ref, r_ref, lb, s_ref, o_ref, reverse):
    c = HGRN_CHUNK
    q = q_ref[0].astype(F32)
    v = v_ref[0]
    r = r_ref[0].astype(F32)
    sig = jax.nn.sigmoid(r)
    f = lb + (1.0 - lb) * sig
    logf = jnp.log(f)
    k = (1.0 - lb) * (1.0 - sig)
    ti = lax.broadcasted_iota(I32, (c, c), 0)
    si = lax.broadcasted_iota(I32, (c, c), 1)
    tri = jnp.where((si >= ti) if reverse else (si <= ti), 1.0, 0.0).astype(BF16)
    hi = logf.astype(BF16)
    lo = (logf - hi.astype(F32)).astype(BF16)
    bcum = _dot(tri, hi) + _dot(tri, lo)
    b_end = bcum[0:1, :] if reverse else bcum[c - 1:c, :]
    qt = (q * jnp.exp(bcum)).astype(BF16)
    kt = (k * jnp.exp(b_end - bcum)).astype(BF16)
    dec = jnp.exp(b_end)

    nsub = c // HGRN_SUB
    qp, kp, cols = [], [], []
    for blk in range(nsub):
        rs = slice(blk * HGRN_SUB, (blk + 1) * HGRN_SUB)
        cs = slice(blk * HGRN_SUB, c) if reverse else slice(0, (blk + 1) * HGRN_SUB)
        mid = blk * HGRN_SUB + HGRN_SUB // 2
        ref = bcum[mid:mid + 1, :]
        qp.append((q[rs] * jnp.exp(bcum[rs] - ref)).astype(BF16))
        kp.append((k[cs] * jnp.exp(ref - bcum[cs])).astype(BF16))
        cols.append(cs)

    outs = []
    for h in range(HGRN_HEADS):
        hs = slice(h * HGRN_DK, (h + 1) * HGRN_DK)
        st = s_ref[h]
        o_inter = _dot_nt(qt[:, hs], st.astype(BF16))
        parts = []
        for blk in range(nsub):
            cs = cols[blk]
            a = _dot_nt(qp[blk][:, hs], kp[blk][:, hs])
            n_cols = a.shape[1]
            trow = blk * HGRN_SUB + lax.broadcasted_iota(I32, (HGRN_SUB, n_cols), 0)
            scol = cs.start + lax.broadcasted_iota(I32, (HGRN_SUB, n_cols), 1)
            keep = (scol >= trow) if reverse else (scol <= trow)
            a = jnp.where(keep, a, 0.0).astype(BF16)
            parts.append(_dot(a, v[cs, hs]))
        outs.append(o_inter + jnp.concatenate(parts, axis=0))
        s_ref[h] = st * dec[:, hs] + _dot_tn(v[:, hs], kt[:, hs])
    o_ref[0] = jnp.concatenate(outs, axis=1).astype(BF16)


def _hgrn_kernel(qf_ref, vf_ref, rf_ref, qb_ref, vb_ref, rb_ref, lb_ref, of_ref, ob_ref, sf_ref, sb_ref):
    @pl.when(pl.program_id(1) == 0)
    def _():
        sf_ref[...] = jnp.zeros(sf_ref.shape, F32)
        sb_ref[...] = jnp.zeros(sb_ref.shape, F32)

    n_layers = lb_ref.shape[0] // 2

    def lower_bound(direction):
        rows = [lb_ref[direction * n_layers + l:direction * n_layers + l + 1, :] for l in range(n_layers)]
        amax = functools.reduce(jnp.maximum, rows)
        e = [jnp.exp(a - amax) for a in rows]
        return e[0] / functools.reduce(lambda u, w: u + w, e)

    _hgrn_chunk(qf_ref, vf_ref, rf_ref, lower_bound(0), sf_ref, of_ref, False)
    _hgrn_chunk(qb_ref, vb_ref, rb_ref, lower_bound(1), sb_ref, ob_ref, True)


def _hgrn_scan(p, hgrn_lb, n_ctx):
    nb, lc, _ = p.shape
    c = HGRN_CHUNK
    assert n_ctx % c == 0 and lc % c == 0
    n_chunks = lc // c
    ctx_chunks = n_ctx // c

    def fwd(col):
        return lambda b, s: (b, s, col)

    def bwd_chunk(s):
        return jnp.where(s < ctx_chunks, ctx_chunks - 1 - s, n_chunks - 1 + ctx_chunks - s)

    def bwd(col):
        return lambda b, s: (b, bwd_chunk(s), col)

    blk = (1, c, HGRN_W)
    lb2 = hgrn_lb.reshape(-1, HGRN_W)
    return pl.pallas_call(
        _hgrn_kernel,
        grid=(nb, n_chunks),
        in_specs=[pl.BlockSpec(blk, fwd(COL_HQ // HGRN_W)), pl.BlockSpec(blk, fwd(COL_HI // HGRN_W)),
                  pl.BlockSpec(blk, fwd(COL_FF // HGRN_W)),
                  pl.BlockSpec(blk, bwd(COL_HQ // HGRN_W)), pl.BlockSpec(blk, bwd(COL_HI // HGRN_W)),
                  pl.BlockSpec(blk, bwd(COL_FB // HGRN_W)),
                  pl.BlockSpec(lb2.shape, lambda b, s: (0, 0))],
        out_specs=[pl.BlockSpec(blk, fwd(0)), pl.BlockSpec(blk, bwd(0))],
        out_shape=[jax.ShapeDtypeStruct((nb, lc, HGRN_W), BF16)] * 2,
        scratch_shapes=[pltpu.VMEM((HGRN_HEADS, HGRN_DK, HGRN_DK), F32)] * 2,
        compiler_params=pltpu.CompilerParams(dimension_semantics=("arbitrary", "arbitrary")),
        name="hgrn_scan",
    )(p, p, p, p, p, p, lb2)


def _merge_kernel(of_ref, ob_ref, hg_ref, gh_ref, ga_ref, oa_ref, x_ref, mod_ref, ng_ref, hng_ref,
                  wb0_ref, wb1_ref, wo_ref, rw_ref, rb_ref,
                  x1_ref, t_ref, ids_ref, wts_ref):
    b = pl.program_id(0)
    m = mod_ref[pl.ds(b, 1), :]
    gate_mix = m[:, 2 * D_MODEL:3 * D_MODEL]
    shift_f = m[:, 3 * D_MODEL:4 * D_MODEL]
    scale_f = m[:, 4 * D_MODEL:5 * D_MODEL]

    o = of_ref[0].astype(F32) + ob_ref[0].astype(F32)
    hng = hng_ref[...]
    o_h = jnp.concatenate(
        [_rms(o[:, h * HGRN_DK:(h + 1) * HGRN_DK], hng) for h in range(HGRN_HEADS)], axis=1)
    g_raw = hg_ref[0].astype(F32)
    o_h = o_h * (g_raw * jax.nn.sigmoid(g_raw))
    y = (jax.nn.sigmoid(gh_ref[0].astype(F32)) * _dot(o_h.astype(BF16), wb0_ref[...])
         + jax.nn.sigmoid(ga_ref[0].astype(F32)) * _dot(oa_ref[0], wb1_ref[...]))
    y = _dot(y.astype(BF16), wo_ref[...])
    x1 = x_ref[0] + gate_mix * _rms(y, ng_ref[1:2, :])
    x1_ref[0] = x1
    t = _rms(x1, ng_ref[2:3, :]) * (1.0 + scale_f) + shift_f
    t_ref[...] = t

    t_hi = t.astype(BF16)
    t_lo = (t - t_hi.astype(F32)).astype(BF16)
    rw = rw_ref[...]
    rw_hi = rw.astype(BF16)
    rw_lo = (rw - rw_hi.astype(F32)).astype(BF16)
    logits = _dot_nt(rw_hi, t_hi) + _dot_nt(rw_hi, t_lo) + _dot_nt(rw_lo, t_hi) + rb_ref[...]
    eidx = lax.broadcasted_iota(I32, logits.shape, 0).astype(F32)
    vals = []
    for j in range(TOP_K):
        mx = jnp.max(logits, axis=0, keepdims=True)
        idx = jnp.min(jnp.where(logits == mx, eidx, float(N_EXPERTS)), axis=0, keepdims=True)
        ids_ref[j:j + 1, :] = idx.astype(I32)
        vals.append(mx)
        logits = jnp.where(eidx == idx, -jnp.inf, logits)
    ex = [jnp.exp(vj - vals[0]) for vj in vals]
    den = ex[0] + ex[1] + ex[2] + ex[3]
    for j in range(TOP_K):
        wts_ref[j:j + 1, :] = ex[j] / den


def _merge(o_f, o_b, p, o_a, x, mod, norm_g, hgrn_norm_g, wb0, wb1, wo, router_wt, router_b, n_ctx):
    nb, n_lat, d = x.shape
    tm = _largest_tile(n_lat, 256, 128)
    assert n_ctx % tm == 0
    off = n_ctx // tm
    nt = n_lat // tm
    row = (1, tm, d)

    def pcol(col):
        return pl.BlockSpec(row, lambda b, i: (b, off + i, col // d))

    def full(a):
        return pl.BlockSpec(a.shape, lambda b, i: (0,) * a.ndim)

    tok = nb * n_lat
    return pl.pallas_call(
        _merge_kernel,
        grid=(nb, nt),
        in_specs=[pl.BlockSpec(row, lambda b, i: (b, off + i, 0)),
                  pl.BlockSpec(row, lambda b, i: (b, off + i, 0)),
                  pcol(COL_HG), pcol(COL_GH), pcol(COL_GA),
                  pl.BlockSpec(row, lambda b, i: (b, i, 0)),
                  pl.BlockSpec(row, lambda b, i: (b, i, 0)),
                  full(mod), full(norm_g), full(hgrn_norm_g), full(wb0), full(wb1), full(wo),
                  full(router_wt), full(router_b)],
        out_specs=[pl.BlockSpec(row, lambda b, i: (b, i, 0)),
                   pl.BlockSpec((tm, d), lambda b, i: (b * nt + i, 0)),
                   pl.BlockSpec((TOP_K, tm), lambda b, i: (0, b * nt + i)),
                   pl.BlockSpec((TOP_K, tm), lambda b, i: (0, b * nt + i))],
        out_shape=[jax.ShapeDtypeStruct((nb, n_lat, d), F32),
                   jax.ShapeDtypeStruct((tok, d), F32),
                   jax.ShapeDtypeStruct((TOP_K, tok), I32),
                   jax.ShapeDtypeStruct((TOP_K, tok), F32)],
        compiler_params=pltpu.CompilerParams(
            dimension_semantics=("arbitrary", "arbitrary"),
            vmem_limit_bytes=_vmem_limit(40 * 1024 * 1024)),
        name="merge_router",
    )(o_f, o_b, p, p, p, o_a, x, mod, norm_g, hgrn_norm_g, wb0, wb1, wo, router_wt, router_b)


def _rank_kernel(ids_ref, rank_ref, cnt_ref, carry_ref, *, tt):
    @pl.when(pl.program_id(0) == 0)
    def _():
        carry_ref[...] = jnp.zeros(carry_ref.shape, F32)

    eidx = lax.broadcasted_iota(I32, (N_EXPERTS, tt), 0)
    si = lax.broadcasted_iota(I32, (tt, tt), 0)
    ti = lax.broadcasted_iota(I32, (tt, tt), 1)
    before = jnp.where(si < ti, 1.0, 0.0).astype(BF16)
    carry = carry_ref[...]
    for j in range(TOP_K):
        onehot = eidx == ids_ref[j:j + 1, :]
        oh = jnp.where(onehot, 1.0, 0.0)
        earlier = _dot(oh.astype(BF16), before)
        rank = jnp.sum(jnp.where(onehot, carry + earlier, 0.0), axis=0, keepdims=True)
        rank_ref[j:j + 1, :] = rank.astype(I32)
        carry = carry + jnp.sum(oh, axis=1, keepdims=True)
    carry_ref[...] = carry
    cnt_ref[...] = carry.astype(I32)


def _expert_rank(ids):
    _, tok = ids.shape
    tt = _largest_tile(tok, 512, 128)
    return pl.pallas_call(
        functools.partial(_rank_kernel, tt=tt),
        grid=(tok // tt,),
        in_specs=[pl.BlockSpec((TOP_K, tt), lambda i: (0, i))],
        out_specs=[pl.BlockSpec((TOP_K, tt), lambda i: (0, i)),
                   pl.BlockSpec((N_EXPERTS, 1), lambda i: (0, 0))],
        out_shape=[jax.ShapeDtypeStruct((TOP_K, tok), I32),
                   jax.ShapeDtypeStruct((N_EXPERTS, 1), I32)],
        scratch_shapes=[pltpu.VMEM((N_EXPERTS, 1), F32)],
        compiler_params=pltpu.CompilerParams(dimension_semantics=("arbitrary",)),
        name="expert_rank",
    )(ids)


def _row_copies_wait(src, dst, sem):
    pltpu.make_async_copy(src, dst, sem).wait()


def _scatter_kernel(zt_ref, zvalid_ref, pos_ref, t_ref, xs_ref, zero_buf, sem, zsem, *, tt):
    @pl.when(pl.program_id(0) == 0)
    def _():
        zero_buf[...] = jnp.zeros(zero_buf.shape, F32)

        def zero_copy(k):
            return pltpu.make_async_copy(zero_buf, xs_ref.at[pl.ds(zt_ref[k] * MOE_TILE, MOE_TILE)], zsem)

        def start(k, carry):
            @pl.when(zvalid_ref[k] == 1)
            def _():
                zero_copy(k).start()
            return carry

        def wait(k, carry):
            @pl.when(zvalid_ref[k] == 1)
            def _():
                zero_copy(k).wait()
            return carry

        lax.fori_loop(0, zt_ref.shape[0], start, 0)
        lax.fori_loop(0, zt_ref.shape[0], wait, 0)

    def body(r, carry):
        for j in range(TOP_K):
            pltpu.make_async_copy(t_ref.at[pl.ds(r, 1)], xs_ref.at[pl.ds(pos_ref[0, j, r], 1)], sem).start()
        return carry

    lax.fori_loop(0, tt, body, 0, unroll=8)
    for j in range(TOP_K):
        _row_copies_wait(t_ref, xs_ref.at[pl.ds(0, tt)], sem)


def _scatter_rows(zero_tiles, zero_valid, pos3, t, n_rows):
    tok, d = t.shape
    tt = pos3.shape[2]
    return pl.pallas_call(
        functools.partial(_scatter_kernel, tt=tt),
        grid_spec=pltpu.PrefetchScalarGridSpec(
            num_scalar_prefetch=2,
            grid=(tok // tt,),
            in_specs=[pl.BlockSpec((1, TOP_K, tt), lambda i, zt, zv: (i, 0, 0), memory_space=pltpu.SMEM),
                      pl.BlockSpec((tt, d), lambda i, zt, zv: (i, 0))],
            out_specs=pl.BlockSpec(memory_space=pl.ANY),
            scratch_shapes=[pltpu.VMEM((MOE_TILE, d), F32), pltpu.SemaphoreType.DMA(()),
                            pltpu.SemaphoreType.DMA(())]),
        out_shape=jax.ShapeDtypeStruct((n_rows, d), F32),
        compiler_params=pltpu.CompilerParams(dimension_semantics=("arbitrary",)),
        name="moe_scatter",
    )(zero_tiles, zero_valid, pos3, t)


def _expert_kernel(te_ref, first_ref, nused_ref, xs_ref, wu_ref, bu_ref, wd_ref, bd_ref, ys_ref,
                   wu_bf, wd_bf):
    i = pl.program_id(0)

    @pl.when(i >= nused_ref[0])
    def _():
        ys_ref[...] = jnp.zeros(ys_ref.shape, F32)

    @pl.when(i < nused_ref[0])
    def _():
        @pl.when(first_ref[i] == 1)
        def _():
            wu_bf[...] = wu_ref[0].astype(BF16)
            wd_bf[...] = wd_ref[0].astype(BF16)

        h = _dot(xs_ref[...].astype(BF16), wu_bf[...]) + bu_ref[0]
        glu = jnp.minimum(h[:, :D_EXPERT], SWIGLU_LIMIT)
        lin = jnp.clip(h[:, D_EXPERT:], -SWIGLU_LIMIT, SWIGLU_LIMIT)
        a = glu * jax.nn.sigmoid(SWIGLU_ALPHA * glu) * (lin + 1.0)
        ys_ref[...] = _dot(a.astype(BF16), wd_bf[...]) + bd_ref[0]


def _experts(xs, tile_expert, tile_first, n_used, w_up, b_up, w_down, b_down):
    n_rows, d = xs.shape
    tm = MOE_TILE
    n_tiles = n_rows // tm

    def tile(i, te, first, nused):
        return (jnp.minimum(i, nused[0] - 1), 0)

    def wsel(i, te, first, nused):
        return (te[i], 0, 0)

    vmem = 2 * (d * 2 * D_EXPERT * 4 + D_EXPERT * d * 4) + d * 2 * D_EXPERT * 2 + D_EXPERT * d * 2 \
        + 4 * tm * d * 4 + 6 * tm * 2 * D_EXPERT * 4
    return pl.pallas_call(
        _expert_kernel,
        grid_spec=pltpu.PrefetchScalarGridSpec(
            num_scalar_prefetch=3,
            grid=(n_tiles,),
            in_specs=[pl.BlockSpec((tm, d), tile),
                      pl.BlockSpec((1, d, 2 * D_EXPERT), wsel),
                      pl.BlockSpec((1, 1, 2 * D_EXPERT), wsel),
                      pl.BlockSpec((1, D_EXPERT, d), wsel),
                      pl.BlockSpec((1, 1, d), wsel)],
            out_specs=pl.BlockSpec((tm, d), lambda i, te, first, nused: (i, 0)),
            scratch_shapes=[pltpu.VMEM((d, 2 * D_EXPERT), BF16), pltpu.VMEM((D_EXPERT, d), BF16)]),
        out_shape=jax.ShapeDtypeStruct((n_rows, d), F32),
        compiler_params=pltpu.CompilerParams(
            dimension_semantics=("arbitrary",), vmem_limit_bytes=_vmem_limit(vmem)),
        name="moe_experts",
    )(tile_expert, tile_first, n_used, xs, w_up, b_up.reshape(N_EXPERTS, 1, -1), w_down,
      b_down.reshape(N_EXPERTS, 1, -1))


def _combine_kernel(pos_ref, ys_ref, w_ref, x1_ref, mod_ref, ng_ref, o_ref, buf, sem, *, tt, tiles_per_batch):
    def body(r, carry):
        for j in range(TOP_K):
            pltpu.make_async_copy(ys_ref.at[pl.ds(pos_ref[0, j, r], 1)], buf.at[j, pl.ds(r, 1)], sem).start()
        return carry

    lax.fori_loop(0, tt, body, 0, unroll=8)
    b = pl.program_id(0) // tiles_per_batch
    gate_f = mod_ref[pl.ds(b, 1), 5 * D_MODEL:6 * D_MODEL]
    w = w_ref[...]
    for j in range(TOP_K):
        _row_copies_wait(ys_ref.at[pl.ds(0, tt)], buf.at[j], sem)
    f = w[:, 0:1] * buf[0]
    for j in range(1, TOP_K):
        f = f + w[:, j:j + 1] * buf[j]
    o_ref[...] = x1_ref[...] + gate_f * _rms(f, ng_ref[3:4, :])


def _combine(pos3, ys, wts_t, x1, mod, norm_g, tiles_per_batch):
    tok, d = x1.shape
    tt = pos3.shape[2]
    kern = functools.partial(_combine_kernel, tt=tt, tiles_per_batch=tiles_per_batch)
    return pl.pallas_call(
        kern,
        grid=(tok // tt,),
        in_specs=[pl.BlockSpec((1, TOP_K, tt), lambda i: (i, 0, 0), memory_space=pltpu.SMEM),
                  pl.BlockSpec(memory_space=pl.ANY),
                  pl.BlockSpec((tt, TOP_K), lambda i: (i, 0)),
                  pl.BlockSpec((tt, d), lambda i: (i, 0)),
                  pl.BlockSpec(mod.shape, lambda i: (0, 0)),
                  pl.BlockSpec(norm_g.shape, lambda i: (0, 0))],
        out_specs=pl.BlockSpec((tt, d), lambda i: (i, 0)),
        out_shape=jax.ShapeDtypeStruct((tok, d), F32),
        scratch_shapes=[pltpu.VMEM((TOP_K, tt, d), F32), pltpu.SemaphoreType.DMA(())],
        compiler_params=pltpu.CompilerParams(dimension_semantics=("arbitrary",)),
        name="moe_combine",
    )(pos3, ys, wts_t, x1, mod, norm_g)


def _rope_tables(n_ctx, n_lat):
    inv = ROPE_THETA ** (-np.arange(0, ROPE_AXIS_DIM, 2, dtype=np.float32) / ROPE_AXIS_DIM)
    idx = jnp.arange(n_lat, dtype=I32)
    rows = (idx // GRID_W).astype(F32)
    cols = (idx % GRID_W).astype(F32)
    half = ROPE_AXIS_DIM // 2
    inv = jnp.asarray(inv, F32)
    ang_r = rows[:, None] * inv[None, :]
    ang_c = cols[:, None] * inv[None, :]
    ang = jnp.concatenate([ang_r, ang_r, ang_c, ang_c], axis=1)
    cos, sin = jnp.cos(ang), jnp.sin(ang)
    lane = np.arange(HEAD_DIM)
    first = jnp.asarray((lane % ROPE_AXIS_DIM) < half)
    sin_a = jnp.where(first, -sin, 0.0)
    sin_b = jnp.where(first, 0.0, sin)
    pad1 = jnp.ones((n_ctx, HEAD_DIM), F32)
    pad0 = jnp.zeros((n_ctx, HEAD_DIM), F32)
    return (jnp.concatenate([pad1, cos], axis=0), jnp.concatenate([pad0, sin_a], axis=0),
            jnp.concatenate([pad0, sin_b], axis=0))


def kernel(x, c, ctx, c_ctx, w_mod, b_mod, norm_g, w_in, hgrn_lb, hgrn_norm_g, qk_norm_g, w_branch, w_out,
           router_w, router_b, w_up, b_up, w_down, b_down):
    nb, n_lat, d = x.shape
    n_ctx = ctx.shape[1]
    assert d == D_MODEL and w_mod.shape[0] == 1, "single-layer kernel"
    tok = nb * n_lat

    cv = jnp.zeros((8, d), F32).at[:nb].set(c).at[nb].set(c_ctx)
    mod = _modulation(cv, w_mod[0], b_mod[0])

    xc = jnp.concatenate([ctx, x], axis=1)
    sizes = np.cumsum([0, HGRN_W, HGRN_W, HGRN_W, HGRN_W, HGRN_W, ATT_W, KV_W, KV_W, D_MODEL, D_MODEL])
    order = [0, 1, 2, 3, 4, 5, 8, 9, 6, 7]
    w_in_bf = jnp.concatenate([w_in[0][:, sizes[i]:sizes[i + 1]] for i in order], axis=1).astype(BF16)
    p = _in_projection(xc, mod, norm_g[0], w_in_bf, n_ctx)

    cos, sin_a, sin_b = _rope_tables(n_ctx, n_lat)
    qn, knt = _qk_prep(p, cos, sin_a, sin_b, qk_norm_g[0])
    o_a = _attention(qn, knt, p, n_ctx)
    o_f, o_b = _hgrn_scan(p, hgrn_lb, n_ctx)

    x1, t, ids, wts = _merge(
        o_f, o_b, p, o_a, x, mod, norm_g[0], hgrn_norm_g, w_branch[0, 0].astype(BF16),
        w_branch[0, 1].astype(BF16), w_out[0].astype(BF16), router_w[0].T, router_b[0].reshape(N_EXPERTS, 1),
        n_ctx)

    rank, counts = _expert_rank(ids)
    counts = counts[:, 0]
    tiles_e = (counts + MOE_TILE - 1) // MOE_TILE
    tile_end = jnp.cumsum(tiles_e)
    starts = (tile_end - tiles_e) * MOE_TILE
    n_tiles = (tok * TOP_K) // MOE_TILE + N_EXPERTS
    tile_idx = jnp.arange(n_tiles, dtype=I32)
    tile_expert = jnp.sum((tile_end[None, :] <= tile_idx[:, None]).astype(I32), axis=1)
    tile_expert = jnp.minimum(tile_expert, N_EXPERTS - 1)
    n_used = tile_end[-1:].astype(I32)
    tile_expert = jnp.where(tile_idx < n_used[0], tile_expert, tile_expert[jnp.maximum(n_used[0] - 1, 0)])
    tile_first = jnp.concatenate([jnp.ones((1,), I32), (tile_expert[1:] != tile_expert[:-1]).astype(I32)])
    pos = jnp.take(starts.astype(I32), ids, axis=0) + rank
    tt = _largest_tile(n_lat, TOK_TILE, 8)
    pos3 = pos.reshape(TOP_K, tok // tt, tt).transpose(1, 0, 2)

    tail = n_used[0] + jnp.arange(N_EXPERTS, dtype=I32)
    zero_tiles = jnp.clip(jnp.concatenate([tile_end.astype(I32) - 1, tail]), 0, n_tiles - 1)
    zero_valid = jnp.concatenate([counts % MOE_TILE != 0, tail < n_tiles]).astype(I32)

    xs = _scatter_rows(zero_tiles, zero_valid, pos3, t, n_tiles * MOE_TILE)
    ys = _experts(xs, tile_expert, tile_first, n_used, w_up[0], b_up[0], w_down[0], b_down[0])
    out = _combine(pos3, ys, wts.T, x1.reshape(tok, d), mod, norm_g[0], n_lat // tt)
    return out.reshape(nb, n_lat, d)
```

```python
import functools

import jax
import jax.numpy as jnp
import numpy as np
from jax import lax
from jax.experimental import pallas as pl
from jax.experimental.pallas import tpu as pltpu

F32 = jnp.float32
BF16 = jnp.bfloat16
I32 = jnp.int32

D_MODEL = 1024
GRID_W = 64
N_MOD = 6
EPS = 1e-6
HGRN_HEADS = 8
HGRN_DK = 128
HGRN_W = HGRN_HEADS * HGRN_DK
ATT_HEADS = 8
ATT_KV_HEADS = 2
ATT_GROUPS = ATT_HEADS // ATT_KV_HEADS
HEAD_DIM = 128
ATT_W = ATT_HEADS * HEAD_DIM
KV_W = ATT_KV_HEADS * HEAD_DIM
ROPE_AXIS_DIM = HEAD_DIM // 2
ROPE_THETA = 10000.0
N_EXPERTS = 32
TOP_K = 4
D_EXPERT = 1024
SWIGLU_LIMIT = 7.0
SWIGLU_ALPHA = 1.702

COL_HQ, COL_HI, COL_FF, COL_FB, COL_HG, COL_AQ, COL_GH, COL_GA = (i * 1024 for i in range(8))
COL_K = 8 * 1024
COL_V = COL_K + KV_W
IN_COLS = COL_V + KV_W

V7X_VMEM_BYTES = 64 * 1024 * 1024
LANES = 128

HGRN_CHUNK = 64
HGRN_SUB = 16
MOE_TILE = 256
TOK_TILE = 256


def _vmem_limit(nbytes):
    return int(min(V7X_VMEM_BYTES - 6 * 1024 * 1024, max(nbytes, 32 * 1024 * 1024)))


def _largest_tile(n, cap, mult):
    best = None
    for t in range(mult, min(n, cap) + 1, mult):
        if n % t == 0:
            best = t
    assert best is not None, (n, cap, mult)
    return best


def _rms(x, g):
    return x * lax.rsqrt(jnp.mean(x * x, axis=-1, keepdims=True) + EPS) * g


def _dot(a, b):
    return jnp.dot(a, b, preferred_element_type=F32)


def _dot_nt(a, b):
    return lax.dot_general(a, b, (((1,), (1,)), ((), ())), preferred_element_type=F32)


def _dot_tn(a, b):
    return lax.dot_general(a, b, (((0,), (0,)), ((), ())), preferred_element_type=F32)


def _mod_kernel(cv_ref, w_ref, b_ref, o_ref):
    cv = cv_ref[...]
    s = cv * jax.nn.sigmoid(cv)
    o_ref[...] = jnp.dot(s, w_ref[...], preferred_element_type=F32,
                         precision=lax.Precision.HIGHEST) + b_ref[...]


def _modulation(cv, w_mod, b_mod):
    rows, d = cv.shape
    n = w_mod.shape[1]
    tn = 1024
    return pl.pallas_call(
        _mod_kernel,
        grid=(n // tn,),
        in_specs=[pl.BlockSpec((rows, d), lambda j: (0, 0)),
                  pl.BlockSpec((d, tn), lambda j: (0, j)),
                  pl.BlockSpec((1, tn), lambda j: (0, j))],
        out_specs=pl.BlockSpec((rows, tn), lambda j: (0, j)),
        out_shape=jax.ShapeDtypeStruct((rows, n), F32),
        name="modulation",
    )(cv, w_mod, b_mod.reshape(1, n))


def _inproj_kernel(x_ref, mod_ref, g_ref, w_ref, o_ref, *, tm, n_ctx, n_batch):
    b = pl.program_id(1)
    i = pl.program_id(2)
    x = x_ref[0]
    xn = _rms(x, g_ref[0:1, :])
    row = i * tm + lax.broadcasted_iota(I32, (tm, 1), 0)
    is_ctx = row < n_ctx
    m_lat = mod_ref[pl.ds(b, 1), :]
    m_ctx = mod_ref[n_batch:n_batch + 1, :]
    shift = jnp.where(is_ctx, m_ctx[:, 0:D_MODEL], m_lat[:, 0:D_MODEL])
    scale = jnp.where(is_ctx, m_ctx[:, D_MODEL:2 * D_MODEL], m_lat[:, D_MODEL:2 * D_MODEL])
    u = xn * (1.0 + scale) + shift
    o_ref[0] = _dot(u.astype(BF16), w_ref[...]).astype(BF16)


def _in_projection(xc, mod, norm_g, w_in_bf, n_ctx):
    nb, lc, d = xc.shape
    n = w_in_bf.shape[1]
    tm = _largest_tile(lc, 1056, 16)
    tn = 2176
    assert n % tn == 0
    kern = functools.partial(_inproj_kernel, tm=tm, n_ctx=n_ctx, n_batch=nb)
    vmem = 2 * (tm * d * 4 + d * tn * 2 + tm * tn * 2) + 6 * tm * d * 4
    return pl.pallas_call(
        kern,
        grid=(n // tn, nb, lc // tm),
        in_specs=[pl.BlockSpec((1, tm, d), lambda j, b, i: (b, i, 0)),
                  pl.BlockSpec(mod.shape, lambda j, b, i: (0, 0)),
                  pl.BlockSpec(norm_g.shape, lambda j, b, i: (0, 0)),
                  pl.BlockSpec((d, tn), lambda j, b, i: (0, j))],
        out_specs=pl.BlockSpec((1, tm, tn), lambda j, b, i: (b, i, j)),
        out_shape=jax.ShapeDtypeStruct((nb, lc, n), BF16),
        compiler_params=pltpu.CompilerParams(
            dimension_semantics=("arbitrary", "arbitrary", "arbitrary"),
            vmem_limit_bytes=_vmem_limit(vmem)),
        name="in_projection",
    )(xc, mod, norm_g, w_in_bf)


def _rope(xn, cos, sin_a, sin_b):
    return (xn * cos + pltpu.roll(xn, HEAD_DIM - ROPE_AXIS_DIM // 2, 1) * sin_a
            + pltpu.roll(xn, ROPE_AXIS_DIM // 2, 1) * sin_b)


def _qkprep_kernel(q_ref, k_ref, v_ref, cos_ref, sa_ref, sb_ref, g_ref, qt_ref, ko_ref, vt_ref):
    cos, sa, sb = cos_ref[...], sa_ref[...], sb_ref[...]
    gq, gk = g_ref[0:1, :], g_ref[1:2, :]
    qscale = HEAD_DIM ** -0.5 * np.log2(np.e)
    for h in range(ATT_HEADS):
        hs = slice(h * HEAD_DIM, (h + 1) * HEAD_DIM)
        xn = _rms(q_ref[0, :, hs].astype(F32), gq)
        qt_ref[0, hs, :] = (_rope(xn, cos, sa, sb) * qscale).T.astype(BF16)
    for h in range(ATT_KV_HEADS):
        hs = slice(h * HEAD_DIM, (h + 1) * HEAD_DIM)
        xn = _rms(k_ref[0, :, hs].astype(F32), gk)
        ko_ref[0, :, hs] = _rope(xn, cos, sa, sb).astype(BF16)
        vt_ref[0, hs, :] = v_ref[0, :, hs].astype(F32).T.astype(BF16)


def _qk_prep(p, cos, sin_a, sin_b, qk_norm_g):
    nb, lc, _ = p.shape
    tm = _largest_tile(lc, 512, 128)
    tab = pl.BlockSpec((tm, HEAD_DIM), lambda b, i: (i, 0))
    return pl.pallas_call(
        _qkprep_kernel,
        grid=(nb, lc // tm),
        in_specs=[pl.BlockSpec((1, tm, ATT_W), lambda b, i: (b, i, COL_AQ // ATT_W)),
                  pl.BlockSpec((1, tm, KV_W), lambda b, i: (b, i, COL_K // KV_W)),
                  pl.BlockSpec((1, tm, KV_W), lambda b, i: (b, i, COL_V // KV_W)),
                  tab, tab, tab,
                  pl.BlockSpec(qk_norm_g.shape, lambda b, i: (0, 0))],
        out_specs=[pl.BlockSpec((1, ATT_W, tm), lambda b, i: (b, 0, i)),
                   pl.BlockSpec((1, tm, KV_W), lambda b, i: (b, i, 0)),
                   pl.BlockSpec((1, KV_W, tm), lambda b, i: (b, 0, i))],
        out_shape=[jax.ShapeDtypeStruct((nb, ATT_W, lc), BF16),
                   jax.ShapeDtypeStruct((nb, lc, KV_W), BF16),
                   jax.ShapeDtypeStruct((nb, KV_W, lc), BF16)],
        name="qk_prep",
    )(p, p, p, cos, sin_a, sin_b, qk_norm_g)


def _attn_kernel(qt_ref, k_ref, vt_ref, o_ref, acc_ref, *, tq, kc, n_kc):
    acc_ref[...] = jnp.zeros(acc_ref.shape, F32)
    qts = [qt_ref[0, g * HEAD_DIM:(g + 1) * HEAD_DIM, :] for g in range(ATT_GROUPS)]

    def body(c, carry):
        off = pl.multiple_of(c * kc, kc)
        kch = k_ref[0, pl.ds(off, kc), :]
        vch = vt_ref[0, :, pl.ds(off, kc)]
        scores = [_dot(kch, qts[g]) for g in range(ATT_GROUPS)]
        out = []
        for g in range(ATT_GROUPS):
            m_old, l_old = carry[g]
            s = scores[g]
            m_new = jnp.maximum(m_old, jnp.max(s, axis=0, keepdims=True))
            alpha = jnp.exp2(m_old - m_new)
            p = jnp.exp2(s - m_new)
            l_new = l_old * alpha + jnp.sum(p, axis=0, keepdims=True)
            acc_ref[g] = acc_ref[g] * alpha + _dot(vch, p.astype(BF16))
            out.append((m_new, l_new))
        return tuple(out)

    init = tuple((jnp.full((1, tq), -jnp.inf, F32), jnp.zeros((1, tq), F32)) for _ in range(ATT_GROUPS))
    fin = lax.fori_loop(0, n_kc, body, init)
    for g in range(ATT_GROUPS):
        o_ref[0, :, g * HEAD_DIM:(g + 1) * HEAD_DIM] = (acc_ref[g] / fin[g][1]).T.astype(BF16)


def _attention(qt, kn, vt, n_ctx):
    nb, lc, _ = kn.shape
    n_lat = lc - n_ctx
    tq = _largest_tile(n_lat, 256, 128)
    assert n_ctx % tq == 0
    kc = _largest_tile(lc, 768, 256)
    gw = ATT_GROUPS * HEAD_DIM
    kern = functools.partial(_attn_kernel, tq=tq, kc=kc, n_kc=lc // kc)
    vmem = 8 * lc * HEAD_DIM * 2 + 4 * ATT_GROUPS * kc * tq * 4 + 8 * gw * tq * 4
    return pl.pallas_call(
        kern,
        grid=(nb, ATT_KV_HEADS, n_lat // tq),
        in_specs=[pl.BlockSpec((1, gw, tq), lambda b, h, i: (b, h, n_ctx // tq + i)),
                  pl.BlockSpec((1, lc, HEAD_DIM), lambda b, h, i: (b, 0, h)),
                  pl.BlockSpec((1, HEAD_DIM, lc), lambda b, h, i: (b, h, 0))],
        out_specs=pl.BlockSpec((1, tq, gw), lambda b, h, i: (b, i, h)),
        out_shape=jax.ShapeDtypeStruct((nb, n_lat, ATT_W), BF16),
        scratch_shapes=[pltpu.VMEM((ATT_GROUPS, HEAD_DIM, tq), F32)],
        compiler_params=pltpu.CompilerParams(
            dimension_semantics=("arbitrary", "arbitrary", "arbitrary"),
            vmem_limit_bytes=_vmem_limit(vmem)),
        name="attention",
    )(qt, kn, vt)


def _hgrn_chunk(q_ref, v_ref, r_ref, lb, reverse):
    c = HGRN_CHUNK
    q = q_ref[0].astype(F32)
    v = v_ref[0]
    r = r_ref[0].astype(F32)
    sig = jax.nn.sigmoid(r)
    f = lb + (1.0 - lb) * sig
    logf = jnp.log(f)
    k = (1.0 - lb) * (1.0 - sig)
    ti = lax.broadcasted_iota(I32, (c, c), 0)
    si = lax.broadcasted_iota(I32, (c, c), 1)
    tri = jnp.where((si >= ti) if reverse else (si <= ti), 1.0, 0.0).astype(BF16)
    hi = logf.astype(BF16)
    lo = (logf - hi.astype(F32)).astype(BF16)
    bcum = _dot(tri, hi) + _dot(tri, lo)
    b_end = bcum[0:1, :] if reverse else bcum[c - 1:c, :]
    qt = (q * jnp.exp(bcum)).astype(BF16)
    kt = (k * jnp.exp(b_end - bcum)).astype(BF16)
    dec = jnp.exp(b_end)

    nsub = c // HGRN_SUB
    qp, kp, cols = [], [], []
    for blk in range(nsub):
        rs = slice(blk * HGRN_SUB, (blk + 1) * HGRN_SUB)
        cs = slice(blk * HGRN_SUB, c) if reverse else slice(0, (blk + 1) * HGRN_SUB)
        mid = blk * HGRN_SUB + HGRN_SUB // 2
        ref = bcum[mid:mid + 1, :]
        qp.append((q[rs] * jnp.exp(bcum[rs] - ref)).astype(BF16))
        kp.append((k[cs] * jnp.exp(ref - bcum[cs])).astype(BF16))
        cols.append(cs)

    keep = []
    for blk in range(nsub):
        n_cols = cols[blk].stop - cols[blk].start
        trow = blk * HGRN_SUB + lax.broadcasted_iota(I32, (HGRN_SUB, n_cols), 0)
        scol = cols[blk].start + lax.broadcasted_iota(I32, (HGRN_SUB, n_cols), 1)
        keep.append((scol >= trow) if reverse else (scol <= trow))
    return dict(v=v, qt=qt, kt=kt, dec=dec, qp=qp, kp=kp, cols=cols, keep=keep)


def _hgrn_emit(dirs, s_refs, o_refs):
    nsub = HGRN_CHUNK // HGRN_SUB
    heads = [slice(h * HGRN_DK, (h + 1) * HGRN_DK) for h in range(HGRN_HEADS)]
    inter = []
    for d, s_ref in zip(dirs, s_refs):
        row = []
        for h, hs in enumerate(heads):
            st = s_ref[h]
            row.append(_dot_nt(d["qt"][:, hs], st.astype(BF16)))
            s_ref[h] = st * d["dec"][:, hs] + _dot_tn(d["v"][:, hs], d["kt"][:, hs])
        inter.append(row)
    scores = [[[_dot_nt(d["qp"][blk][:, hs], d["kp"][blk][:, hs]) for blk in range(nsub)] for hs in heads]
              for d in dirs]
    for d, sc, it, o_ref in zip(dirs, scores, inter, o_refs):
        outs = []
        for h, hs in enumerate(heads):
            parts = [_dot(jnp.where(d["keep"][blk], sc[h][blk], 0.0).astype(BF16), d["v"][d["cols"][blk], hs])
                     for blk in range(nsub)]
            outs.append(it[h] + jnp.concatenate(parts, axis=0))
        o_ref[0] = jnp.concatenate(outs, axis=1).astype(BF16)


def _hgrn_kernel(qf_ref, vf_ref, rf_ref, qb_ref, vb_ref, rb_ref, lb_ref, of_ref, ob_ref, sf_ref, sb_ref):
    @pl.when(pl.program_id(1) == 0)
    def _():
        sf_ref[...] = jnp.zeros(sf_ref.shape, F32)
        sb_ref[...] = jnp.zeros(sb_ref.shape, F32)

    n_layers = lb_ref.shape[0] // 2

    def lower_bound(direction):
        rows = [lb_ref[direction * n_layers + l:direction * n_layers + l + 1, :] for l in range(n_layers)]
        amax = functools.reduce(jnp.maximum, rows)
        e = [jnp.exp(a - amax) for a in rows]
        return e[0] / functools.reduce(lambda u, w: u + w, e)

    fwd = _hgrn_chunk(qf_ref, vf_ref, rf_ref, lower_bound(0), False)
    bwd = _hgrn_chunk(qb_ref, vb_ref, rb_ref, lower_bound(1), True)
    _hgrn_emit([fwd, bwd], [sf_ref, sb_ref], [of_ref, ob_ref])


def _hgrn_scan(p, hgrn_lb, n_ctx):
    nb, lc, _ = p.shape
    c = HGRN_CHUNK
    assert n_ctx % c == 0 and lc % c == 0
    n_chunks = lc // c
    ctx_chunks = n_ctx // c

    def fwd(col):
        return lambda b, s: (b, s, col)

    def bwd_chunk(s):
        return jnp.where(s < ctx_chunks, ctx_chunks - 1 - s, n_chunks - 1 + ctx_chunks - s)

    def bwd(col):
        return lambda b, s: (b, bwd_chunk(s), col)

    blk = (1, c, HGRN_W)
    lb2 = hgrn_lb.reshape(-1, HGRN_W)
    return pl.pallas_call(
        _hgrn_kernel,
        grid=(nb, n_chunks),
        in_specs=[pl.BlockSpec(blk, fwd(COL_HQ // HGRN_W)), pl.BlockSpec(blk, fwd(COL_HI // HGRN_W)),
                  pl.BlockSpec(blk, fwd(COL_FF // HGRN_W)),
                  pl.BlockSpec(blk, bwd(COL_HQ // HGRN_W)), pl.BlockSpec(blk, bwd(COL_HI // HGRN_W)),
                  pl.BlockSpec(blk, bwd(COL_FB // HGRN_W)),
                  pl.BlockSpec(lb2.shape, lambda b, s: (0, 0))],
        out_specs=[pl.BlockSpec(blk, fwd(0)), pl.BlockSpec(blk, bwd(0))],
        out_shape=[jax.ShapeDtypeStruct((nb, lc, HGRN_W), BF16)] * 2,
        scratch_shapes=[pltpu.VMEM((HGRN_HEADS, HGRN_DK, HGRN_DK), F32)] * 2,
        compiler_params=pltpu.CompilerParams(dimension_semantics=("arbitrary", "arbitrary")),
        name="hgrn_scan",
    )(p, p, p, p, p, p, lb2)


def _merge_kernel(of_ref, ob_ref, hg_ref, gh_ref, ga_ref, oa_ref, x_ref, mod_ref, ng_ref, hng_ref,
                  wb0_ref, wb1_ref, wo_ref, rw_ref, rb_ref,
                  x1_ref, t_ref, ids_ref, wts_ref):
    b = pl.program_id(0)
    m = mod_ref[pl.ds(b, 1), :]
    gate_mix = m[:, 2 * D_MODEL:3 * D_MODEL]
    shift_f = m[:, 3 * D_MODEL:4 * D_MODEL]
    scale_f = m[:, 4 * D_MODEL:5 * D_MODEL]

    o = of_ref[0].astype(F32) + ob_ref[0].astype(F32)
    hng = hng_ref[...]
    o_h = jnp.concatenate(
        [_rms(o[:, h * HGRN_DK:(h + 1) * HGRN_DK], hng) for h in range(HGRN_HEADS)], axis=1)
    g_raw = hg_ref[0].astype(F32)
    o_h = o_h * (g_raw * jax.nn.sigmoid(g_raw))
    y = (jax.nn.sigmoid(gh_ref[0].astype(F32)) * _dot(o_h.astype(BF16), wb0_ref[...])
         + jax.nn.sigmoid(ga_ref[0].astype(F32)) * _dot(oa_ref[0], wb1_ref[...]))
    y = _dot(y.astype(BF16), wo_ref[...])
    x1 = x_ref[0] + gate_mix * _rms(y, ng_ref[1:2, :])
    x1_ref[0] = x1
    t = _rms(x1, ng_ref[2:3, :]) * (1.0 + scale_f) + shift_f
    t_ref[...] = t

    t_hi = t.astype(BF16)
    t_lo = (t - t_hi.astype(F32)).astype(BF16)
    rw = rw_ref[...]
    rw_hi = rw.astype(BF16)
    rw_lo = (rw - rw_hi.astype(F32)).astype(BF16)
    logits = _dot_nt(rw_hi, t_hi) + _dot_nt(rw_hi, t_lo) + _dot_nt(rw_lo, t_hi) + rb_ref[...]
    eidx = lax.broadcasted_iota(I32, logits.shape, 0).astype(F32)
    vals = []
    for j in range(TOP_K):
        mx = jnp.max(logits, axis=0, keepdims=True)
        idx = jnp.min(jnp.where(logits == mx, eidx, float(N_EXPERTS)), axis=0, keepdims=True)
        ids_ref[j:j + 1, :] = idx.astype(I32)
        vals.append(mx)
        logits = jnp.where(eidx == idx, -jnp.inf, logits)
    ex = [jnp.exp(vj - vals[0]) for vj in vals]
    den = ex[0] + ex[1] + ex[2] + ex[3]
    for j in range(TOP_K):
        wts_ref[j:j + 1, :] = ex[j] / den


def _merge(o_f, o_b, p, o_a, x, mod, norm_g, hgrn_norm_g, wb0, wb1, wo, router_wt, router_b, n_ctx):
    nb, n_lat, d = x.shape
    tm = _largest_tile(n_lat, 256, 128)
    assert n_ctx % tm == 0
    off = n_ctx // tm
    nt = n_lat // tm
    row = (1, tm, d)

    def pcol(col):
        return pl.BlockSpec(row, lambda b, i: (b, off + i, col // d))

    def full(a):
        return pl.BlockSpec(a.shape, lambda b, i: (0,) * a.ndim)

    tok = nb * n_lat
    return pl.pallas_call(
        _merge_kernel,
        grid=(nb, nt),
        in_specs=[pl.BlockSpec(row, lambda b, i: (b, off + i, 0)),
                  pl.BlockSpec(row, lambda b, i: (b, off + i, 0)),
                  pcol(COL_HG), pcol(COL_GH), pcol(COL_GA),
                  pl.BlockSpec(row, lambda b, i: (b, i, 0)),
                  pl.BlockSpec(row, lambda b, i: (b, i, 0)),
                  full(mod), full(norm_g), full(hgrn_norm_g), full(wb0), full(wb1), full(wo),
                  full(router_wt), full(router_b)],
        out_specs=[pl.BlockSpec(row, lambda b, i: (b, i, 0)),
                   pl.BlockSpec((tm, d), lambda b, i: (b * nt + i, 0)),
                   pl.BlockSpec((TOP_K, tm), lambda b, i: (0, b * nt + i)),
                   pl.BlockSpec((TOP_K, tm), lambda b, i: (0, b * nt + i))],
        out_shape=[jax.ShapeDtypeStruct((nb, n_lat, d), F32),
                   jax.ShapeDtypeStruct((tok, d), F32),
                   jax.ShapeDtypeStruct((TOP_K, tok), I32),
                   jax.ShapeDtypeStruct((TOP_K, tok), F32)],
        compiler_params=pltpu.CompilerParams(
            dimension_semantics=("arbitrary", "arbitrary"),
            vmem_limit_bytes=_vmem_limit(40 * 1024 * 1024)),
        name="merge_router",
    )(o_f, o_b, p, p, p, o_a, x, mod, norm_g, hgrn_norm_g, wb0, wb1, wo, router_wt, router_b)


def _rank_kernel(ids_ref, rank_ref, cnt_ref, carry_ref, *, tt):
    @pl.when(pl.program_id(0) == 0)
    def _():
        carry_ref[...] = jnp.zeros(carry_ref.shape, F32)

    eidx = lax.broadcasted_iota(I32, (N_EXPERTS, tt), 0)
    si = lax.broadcasted_iota(I32, (tt, tt), 0)
    ti = lax.broadcasted_iota(I32, (tt, tt), 1)
    before = jnp.where(si < ti, 1.0, 0.0).astype(BF16)
    carry = carry_ref[...]
    for j in range(TOP_K):
        onehot = eidx == ids_ref[j:j + 1, :]
        oh = jnp.where(onehot, 1.0, 0.0)
        earlier = _dot(oh.astype(BF16), before)
        rank = jnp.sum(jnp.where(onehot, carry + earlier, 0.0), axis=0, keepdims=True)
        rank_ref[j:j + 1, :] = rank.astype(I32)
        carry = carry + jnp.sum(oh, axis=1, keepdims=True)
    carry_ref[...] = carry
    cnt_ref[...] = carry.astype(I32)


def _expert_rank(ids):
    _, tok = ids.shape
    tt = _largest_tile(tok, 512, 128)
    return pl.pallas_call(
        functools.partial(_rank_kernel, tt=tt),
        grid=(tok // tt,),
        in_specs=[pl.BlockSpec((TOP_K, tt), lambda i: (0, i))],
        out_specs=[pl.BlockSpec((TOP_K, tt), lambda i: (0, i)),
                   pl.BlockSpec((N_EXPERTS, 1), lambda i: (0, 0))],
        out_shape=[jax.ShapeDtypeStruct((TOP_K, tok), I32),
                   jax.ShapeDtypeStruct((N_EXPERTS, 1), I32)],
        scratch_shapes=[pltpu.VMEM((N_EXPERTS, 1), F32)],
        compiler_params=pltpu.CompilerParams(dimension_semantics=("arbitrary",)),
        name="expert_rank",
    )(ids)


def _row_copies_wait(src, dst, sem):
    pltpu.make_async_copy(src, dst, sem).wait()


def _scatter_kernel(zt_ref, zvalid_ref, pos_ref, t_ref, xs_ref, zero_buf, sem, zsem, *, tt):
    @pl.when(pl.program_id(0) == 0)
    def _():
        zero_buf[...] = jnp.zeros(zero_buf.shape, F32)

        def zero_copy(k):
            return pltpu.make_async_copy(zero_buf, xs_ref.at[pl.ds(zt_ref[k] * MOE_TILE, MOE_TILE)], zsem)

        def start(k, carry):
            @pl.when(zvalid_ref[k] == 1)
            def _():
                zero_copy(k).start()
            return carry

        def wait(k, carry):
            @pl.when(zvalid_ref[k] == 1)
            def _():
                zero_copy(k).wait()
            return carry

        lax.fori_loop(0, zt_ref.shape[0], start, 0)
        lax.fori_loop(0, zt_ref.shape[0], wait, 0)

    def body(r, carry):
        for j in range(TOP_K):
            pltpu.make_async_copy(t_ref.at[pl.ds(r, 1)], xs_ref.at[pl.ds(pos_ref[0, j, r], 1)], sem).start()
        return carry

    lax.fori_loop(0, tt, body, 0, unroll=8)
    for j in range(TOP_K):
        _row_copies_wait(t_ref, xs_ref.at[pl.ds(0, tt)], sem)


def _scatter_rows(zero_tiles, zero_valid, pos3, t, n_rows):
    tok, d = t.shape
    tt = pos3.shape[2]
    return pl.pallas_call(
        functools.partial(_scatter_kernel, tt=tt),
        grid_spec=pltpu.PrefetchScalarGridSpec(
            num_scalar_prefetch=2,
            grid=(tok // tt,),
            in_specs=[pl.BlockSpec((1, TOP_K, tt), lambda i, zt, zv: (i, 0, 0), memory_space=pltpu.SMEM),
                      pl.BlockSpec((tt, d), lambda i, zt, zv: (i, 0))],
            out_specs=pl.BlockSpec(memory_space=pl.ANY),
            scratch_shapes=[pltpu.VMEM((MOE_TILE, d), F32), pltpu.SemaphoreType.DMA(()),
                            pltpu.SemaphoreType.DMA(())]),
        out_shape=jax.ShapeDtypeStruct((n_rows, d), F32),
        compiler_params=pltpu.CompilerParams(dimension_semantics=("arbitrary",)),
        name="moe_scatter",
    )(zero_tiles, zero_valid, pos3, t)


def _expert_kernel(te_ref, first_ref, nused_ref, xs_ref, wu_ref, bu_ref, wd_ref, bd_ref, ys_ref,
                   wu_bf, wd_bf):
    i = pl.program_id(0)

    @pl.when(i >= nused_ref[0])
    def _():
        ys_ref[...] = jnp.zeros(ys_ref.shape, F32)

    @pl.when(i < nused_ref[0])
    def _():
        @pl.when(first_ref[i] == 1)
        def _():
            wu_bf[...] = wu_ref[0].astype(BF16)
            wd_bf[...] = wd_ref[0].astype(BF16)

        h = _dot(xs_ref[...].astype(BF16), wu_bf[...]) + bu_ref[0]
        glu = jnp.minimum(h[:, :D_EXPERT], SWIGLU_LIMIT)
        lin = jnp.clip(h[:, D_EXPERT:], -SWIGLU_LIMIT, SWIGLU_LIMIT)
        a = glu * jax.nn.sigmoid(SWIGLU_ALPHA * glu) * (lin + 1.0)
        ys_ref[...] = _dot(a.astype(BF16), wd_bf[...]) + bd_ref[0]


def _experts(xs, tile_expert, tile_first, n_used, w_up, b_up, w_down, b_down):
    n_rows, d = xs.shape
    tm = MOE_TILE
    n_tiles = n_rows // tm

    def tile(i, te, first, nused):
        return (jnp.minimum(i, nused[0] - 1), 0)

    def wsel(i, te, first, nused):
        return (te[i], 0, 0)

    vmem = 2 * (d * 2 * D_EXPERT * 4 + D_EXPERT * d * 4) + d * 2 * D_EXPERT * 2 + D_EXPERT * d * 2 \
        + 4 * tm * d * 4 + 6 * tm * 2 * D_EXPERT * 4
    return pl.pallas_call(
        _expert_kernel,
        grid_spec=pltpu.PrefetchScalarGridSpec(
            num_scalar_prefetch=3,
            grid=(n_tiles,),
            in_specs=[pl.BlockSpec((tm, d), tile),
                      pl.BlockSpec((1, d, 2 * D_EXPERT), wsel),
                      pl.BlockSpec((1, 1, 2 * D_EXPERT), wsel),
                      pl.BlockSpec((1, D_EXPERT, d), wsel),
                      pl.BlockSpec((1, 1, d), wsel)],
            out_specs=pl.BlockSpec((tm, d), lambda i, te, first, nused: (i, 0)),
            scratch_shapes=[pltpu.VMEM((d, 2 * D_EXPERT), BF16), pltpu.VMEM((D_EXPERT, d), BF16)]),
        out_shape=jax.ShapeDtypeStruct((n_rows, d), F32),
        compiler_params=pltpu.CompilerParams(
            dimension_semantics=("arbitrary",), vmem_limit_bytes=_vmem_limit(vmem)),
        name="moe_experts",
    )(tile_expert, tile_first, n_used, xs, w_up, b_up.reshape(N_EXPERTS, 1, -1), w_down,
      b_down.reshape(N_EXPERTS, 1, -1))


def _combine_kernel(pos_ref, ys_ref, w_ref, x1_ref, mod_ref, ng_ref, o_ref, buf, sem, *, tt, tiles_per_batch):
    def body(r, carry):
        for j in range(TOP_K):
            pltpu.make_async_copy(ys_ref.at[pl.ds(pos_ref[0, j, r], 1)], buf.at[j, pl.ds(r, 1)], sem).start()
        return carry

    lax.fori_loop(0, tt, body, 0, unroll=8)
    b = pl.program_id(0) // tiles_per_batch
    gate_f = mod_ref[pl.ds(b, 1), 5 * D_MODEL:6 * D_MODEL]
    w = w_ref[...]
    for j in range(TOP_K):
        _row_copies_wait(ys_ref.at[pl.ds(0, tt)], buf.at[j], sem)
    f = w[:, 0:1] * buf[0]
    for j in range(1, TOP_K):
        f = f + w[:, j:j + 1] * buf[j]
    o_ref[...] = x1_ref[...] + gate_f * _rms(f, ng_ref[3:4, :])


def _combine(pos3, ys, wts_t, x1, mod, norm_g, tiles_per_batch):
    tok, d = x1.shape
    tt = pos3.shape[2]
    kern = functools.partial(_combine_kernel, tt=tt, tiles_per_batch=tiles_per_batch)
    return pl.pallas_call(
        kern,
        grid=(tok // tt,),
        in_specs=[pl.BlockSpec((1, TOP_K, tt), lambda i: (i, 0, 0), memory_space=pltpu.SMEM),
                  pl.BlockSpec(memory_space=pl.ANY),
                  pl.BlockSpec((tt, TOP_K), lambda i: (i, 0)),
                  pl.BlockSpec((tt, d), lambda i: (i, 0)),
                  pl.BlockSpec(mod.shape, lambda i: (0, 0)),
                  pl.BlockSpec(norm_g.shape, lambda i: (0, 0))],
        out_specs=pl.BlockSpec((tt, d), lambda i: (i, 0)),
        out_shape=jax.ShapeDtypeStruct((tok, d), F32),
        scratch_shapes=[pltpu.VMEM((TOP_K, tt, d), F32), pltpu.SemaphoreType.DMA(())],
        compiler_params=pltpu.CompilerParams(dimension_semantics=("arbitrary",)),
        name="moe_combine",
    )(pos3, ys, wts_t, x1, mod, norm_g)


def _rope_tables(n_ctx, n_lat):
    inv = ROPE_THETA ** (-np.arange(0, ROPE_AXIS_DIM, 2, dtype=np.float32) / ROPE_AXIS_DIM)
    idx = jnp.arange(n_lat, dtype=I32)
    rows = (idx // GRID_W).astype(F32)
    cols = (idx % GRID_W).astype(F32)
    half = ROPE_AXIS_DIM // 2
    inv = jnp.asarray(inv, F32)
    ang_r = rows[:, None] * inv[None, :]
    ang_c = cols[:, None] * inv[None, :]
    ang = jnp.concatenate([ang_r, ang_r, ang_c, ang_c], axis=1)
    cos, sin = jnp.cos(ang), jnp.sin(ang)
    lane = np.arange(HEAD_DIM)
    first = jnp.asarray((lane % ROPE_AXIS_DIM) < half)
    sin_a = jnp.where(first, -sin, 0.0)
    sin_b = jnp.where(first, 0.0, sin)
    pad1 = jnp.ones((n_ctx, HEAD_DIM), F32)
    pad0 = jnp.zeros((n_ctx, HEAD_DIM), F32)
    return (jnp.concatenate([pad1, cos], axis=0), jnp.concatenate([pad0, sin_a], axis=0),
            jnp.concatenate([pad0, sin_b], axis=0))


def kernel(x, c, ctx, c_ctx, w_mod, b_mod, norm_g, w_in, hgrn_lb, hgrn_norm_g, qk_norm_g, w_branch, w_out,
           router_w, router_b, w_up, b_up, w_down, b_down):
    nb, n_lat, d = x.shape
    n_ctx = ctx.shape[1]
    assert d == D_MODEL and w_mod.shape[0] == 1, "single-layer kernel"
    tok = nb * n_lat

    cv = jnp.zeros((8, d), F32).at[:nb].set(c).at[nb].set(c_ctx)
    mod = _modulation(cv, w_mod[0], b_mod[0])

    xc = jnp.concatenate([ctx, x], axis=1)
    sizes = np.cumsum([0, HGRN_W, HGRN_W, HGRN_W, HGRN_W, HGRN_W, ATT_W, KV_W, KV_W, D_MODEL, D_MODEL])
    order = [0, 1, 2, 3, 4, 5, 8, 9, 6, 7]
    w_in_bf = jnp.concatenate([w_in[0][:, sizes[i]:sizes[i + 1]] for i in order], axis=1).astype(BF16)
    p = _in_projection(xc, mod, norm_g[0], w_in_bf, n_ctx)

    cos, sin_a, sin_b = _rope_tables(n_ctx, n_lat)
    qt, kn, vt = _qk_prep(p, cos, sin_a, sin_b, qk_norm_g[0])
    o_a = _attention(qt, kn, vt, n_ctx)
    o_f, o_b = _hgrn_scan(p, hgrn_lb, n_ctx)

    x1, t, ids, wts = _merge(
        o_f, o_b, p, o_a, x, mod, norm_g[0], hgrn_norm_g, w_branch[0, 0].astype(BF16),
        w_branch[0, 1].astype(BF16), w_out[0].astype(BF16), router_w[0].T, router_b[0].reshape(N_EXPERTS, 1),
        n_ctx)

    rank, counts = _expert_rank(ids)
    counts = counts[:, 0]
    tiles_e = (counts + MOE_TILE - 1) // MOE_TILE
    tile_end = jnp.cumsum(tiles_e)
    starts = (tile_end - tiles_e) * MOE_TILE
    n_tiles = (tok * TOP_K) // MOE_TILE + N_EXPERTS
    tile_idx = jnp.arange(n_tiles, dtype=I32)
    tile_expert = jnp.sum((tile_end[None, :] <= tile_idx[:, None]).astype(I32), axis=1)
    tile_expert = jnp.minimum(tile_expert, N_EXPERTS - 1)
    n_used = tile_end[-1:].astype(I32)
    tile_expert = jnp.where(tile_idx < n_used[0], tile_expert, tile_expert[jnp.maximum(n_used[0] - 1, 0)])
    tile_first = jnp.concatenate([jnp.ones((1,), I32), (tile_expert[1:] != tile_expert[:-1]).astype(I32)])
    onehot = ids[:, :, None] == jnp.arange(N_EXPERTS, dtype=I32)
    pos = jnp.sum(jnp.where(onehot, starts.astype(I32), 0), axis=-1) + rank
    tt = _largest_tile(n_lat, TOK_TILE, 8)
    pos3 = pos.reshape(TOP_K, tok // tt, tt).transpose(1, 0, 2)

    tail = n_used[0] + jnp.arange(N_EXPERTS, dtype=I32)
    zero_tiles = jnp.clip(jnp.concatenate([tile_end.astype(I32) - 1, tail]), 0, n_tiles - 1)
    zero_valid = jnp.concatenate([counts % MOE_TILE != 0, tail < n_tiles]).astype(I32)

    xs = _scatter_rows(zero_tiles, zero_valid, pos3, t, n_tiles * MOE_TILE)
    ys = _experts(xs, tile_expert, tile_first, n_used, w_up[0], b_up[0], w_down[0], b_down[0])
    out = _combine(pos3, ys, wts.T, x1.reshape(tok, d), mod, norm_g[0], n_lat // tt)
    return out.reshape(nb, n_lat, d)
```

```python
import functools

import jax
import jax.numpy as jnp
import numpy as np
from jax import lax
from jax.experimental import pallas as pl
from jax.experimental.pallas import tpu as pltpu

F32 = jnp.float32
BF16 = jnp.bfloat16
I32 = jnp.int32

D_MODEL = 1024
GRID_W = 64
N_MOD = 6
EPS = 1e-6
HGRN_HEADS = 8
HGRN_DK = 128
HGRN_W = HGRN_HEADS * HGRN_DK
ATT_HEADS = 8
ATT_KV_HEADS = 2
ATT_GROUPS = ATT_HEADS // ATT_KV_HEADS
HEAD_DIM = 128
ATT_W = ATT_HEADS * HEAD_DIM
KV_W = ATT_KV_HEADS * HEAD_DIM
ROPE_AXIS_DIM = HEAD_DIM // 2
ROPE_THETA = 10000.0
N_EXPERTS = 32
TOP_K = 4
D_EXPERT = 1024
SWIGLU_LIMIT = 7.0
SWIGLU_ALPHA = 1.702

COL_HQ, COL_HI, COL_FF, COL_FB, COL_HG, COL_AQ, COL_GH, COL_GA = (i * 1024 for i in range(8))
COL_K = 8 * 1024
COL_V = COL_K + KV_W
IN_COLS = COL_V + KV_W

V7X_VMEM_BYTES = 64 * 1024 * 1024
LANES = 128

HGRN_CHUNK = 64
HGRN_SUB = 16
MOE_TILE = 256
TOK_TILE = 256


def _vmem_limit(nbytes):
    return int(min(V7X_VMEM_BYTES - 6 * 1024 * 1024, max(nbytes, 32 * 1024 * 1024)))


def _largest_tile(n, cap, mult):
    best = None
    for t in range(mult, min(n, cap) + 1, mult):
        if n % t == 0:
            best = t
    assert best is not None, (n, cap, mult)
    return best


def _rms(x, g):
    return x * lax.rsqrt(jnp.mean(x * x, axis=-1, keepdims=True) + EPS) * g


def _dot(a, b):
    return jnp.dot(a, b, preferred_element_type=F32)


def _dot_nt(a, b):
    return lax.dot_general(a, b, (((1,), (1,)), ((), ())), preferred_element_type=F32)


def _dot_tn(a, b):
    return lax.dot_general(a, b, (((0,), (0,)), ((), ())), preferred_element_type=F32)


def _mod_kernel(cv_ref, w_ref, b_ref, o_ref):
    cv = cv_ref[...]
    s = cv * jax.nn.sigmoid(cv)
    o_ref[...] = jnp.dot(s, w_ref[...], preferred_element_type=F32,
                         precision=lax.Precision.HIGHEST) + b_ref[...]


def _modulation(cv, w_mod, b_mod):
    rows, d = cv.shape
    n = w_mod.shape[1]
    tn = 1024
    return pl.pallas_call(
        _mod_kernel,
        grid=(n // tn,),
        in_specs=[pl.BlockSpec((rows, d), lambda j: (0, 0)),
                  pl.BlockSpec((d, tn), lambda j: (0, j)),
                  pl.BlockSpec((1, tn), lambda j: (0, j))],
        out_specs=pl.BlockSpec((rows, tn), lambda j: (0, j)),
        out_shape=jax.ShapeDtypeStruct((rows, n), F32),
        name="modulation",
    )(cv, w_mod, b_mod.reshape(1, n))


def _inproj_kernel(x_ref, mod_ref, g_ref, w_ref, o_ref, *, tm, n_ctx, n_batch):
    b = pl.program_id(1)
    i = pl.program_id(2)
    x = x_ref[0]
    xn = _rms(x, g_ref[0:1, :])
    row = i * tm + lax.broadcasted_iota(I32, (tm, 1), 0)
    is_ctx = row < n_ctx
    m_lat = mod_ref[pl.ds(b, 1), :]
    m_ctx = mod_ref[n_batch:n_batch + 1, :]
    shift = jnp.where(is_ctx, m_ctx[:, 0:D_MODEL], m_lat[:, 0:D_MODEL])
    scale = jnp.where(is_ctx, m_ctx[:, D_MODEL:2 * D_MODEL], m_lat[:, D_MODEL:2 * D_MODEL])
    u = xn * (1.0 + scale) + shift
    o_ref[0] = _dot(u.astype(BF16), w_ref[...]).astype(BF16)


def _in_projection(xc, mod, norm_g, w_in_bf, n_ctx):
    nb, lc, d = xc.shape
    n = w_in_bf.shape[1]
    tm = _largest_tile(lc, 1056, 16)
    tn = 2176
    assert n % tn == 0
    kern = functools.partial(_inproj_kernel, tm=tm, n_ctx=n_ctx, n_batch=nb)
    vmem = 2 * (tm * d * 4 + d * tn * 2 + tm * tn * 2) + 6 * tm * d * 4
    return pl.pallas_call(
        kern,
        grid=(n // tn, nb, lc // tm),
        in_specs=[pl.BlockSpec((1, tm, d), lambda j, b, i: (b, i, 0)),
                  pl.BlockSpec(mod.shape, lambda j, b, i: (0, 0)),
                  pl.BlockSpec(norm_g.shape, lambda j, b, i: (0, 0)),
                  pl.BlockSpec((d, tn), lambda j, b, i: (0, j))],
        out_specs=pl.BlockSpec((1, tm, tn), lambda j, b, i: (b, i, j)),
        out_shape=jax.ShapeDtypeStruct((nb, lc, n), BF16),
        compiler_params=pltpu.CompilerParams(
            dimension_semantics=("arbitrary", "arbitrary", "arbitrary"),
            vmem_limit_bytes=_vmem_limit(vmem)),
        name="in_projection",
    )(xc, mod, norm_g, w_in_bf)


def _rope(xn, cos, sin_a, sin_b):
    return (xn * cos + pltpu.roll(xn, HEAD_DIM - ROPE_AXIS_DIM // 2, 1) * sin_a
            + pltpu.roll(xn, ROPE_AXIS_DIM // 2, 1) * sin_b)


def _qkprep_kernel(q_ref, k_ref, v_ref, cos_ref, sa_ref, sb_ref, g_ref, qt_ref, ko_ref, vt_ref):
    cos, sa, sb = cos_ref[...], sa_ref[...], sb_ref[...]
    gq, gk = g_ref[0:1, :], g_ref[1:2, :]
    qscale = HEAD_DIM ** -0.5 * np.log2(np.e)
    for h in range(ATT_HEADS):
        hs = slice(h * HEAD_DIM, (h + 1) * HEAD_DIM)
        xn = _rms(q_ref[0, :, hs].astype(F32), gq)
        qt_ref[0, hs, :] = (_rope(xn, cos, sa, sb) * qscale).T.astype(BF16)
    for h in range(ATT_KV_HEADS):
        hs = slice(h * HEAD_DIM, (h + 1) * HEAD_DIM)
        xn = _rms(k_ref[0, :, hs].astype(F32), gk)
        ko_ref[0, :, hs] = _rope(xn, cos, sa, sb).astype(BF16)
        vt_ref[0, hs, :] = v_ref[0, :, hs].astype(F32).T.astype(BF16)


def _qk_prep(p, cos, sin_a, sin_b, qk_norm_g):
    nb, lc, _ = p.shape
    tm = _largest_tile(lc, 512, 128)
    tab = pl.BlockSpec((tm, HEAD_DIM), lambda b, i: (i, 0))
    return pl.pallas_call(
        _qkprep_kernel,
        grid=(nb, lc // tm),
        in_specs=[pl.BlockSpec((1, tm, ATT_W), lambda b, i: (b, i, COL_AQ // ATT_W)),
                  pl.BlockSpec((1, tm, KV_W), lambda b, i: (b, i, COL_K // KV_W)),
                  pl.BlockSpec((1, tm, KV_W), lambda b, i: (b, i, COL_V // KV_W)),
                  tab, tab, tab,
                  pl.BlockSpec(qk_norm_g.shape, lambda b, i: (0, 0))],
        out_specs=[pl.BlockSpec((1, ATT_W, tm), lambda b, i: (b, 0, i)),
                   pl.BlockSpec((1, tm, KV_W), lambda b, i: (b, i, 0)),
                   pl.BlockSpec((1, KV_W, tm), lambda b, i: (b, 0, i))],
        out_shape=[jax.ShapeDtypeStruct((nb, ATT_W, lc), BF16),
                   jax.ShapeDtypeStruct((nb, lc, KV_W), BF16),
                   jax.ShapeDtypeStruct((nb, KV_W, lc), BF16)],
        name="qk_prep",
    )(p, p, p, cos, sin_a, sin_b, qk_norm_g)


def _attn_kernel(qt_ref, k_ref, vt_ref, o_ref, acc_ref, s_ref, *, tq, kc, n_kc):
    acc_ref[...] = jnp.zeros(acc_ref.shape, F32)
    qts = [qt_ref[0, g * HEAD_DIM:(g + 1) * HEAD_DIM, :] for g in range(ATT_GROUPS)]

    def put_scores(c, slot):
        kch = k_ref[0, pl.ds(pl.multiple_of(c * kc, kc), kc), :]
        for g in range(ATT_GROUPS):
            s_ref[slot, g] = _dot(kch, qts[g])

    def softmax_pv(c, slot, carry):
        vch = vt_ref[0, :, pl.ds(pl.multiple_of(c * kc, kc), kc)]
        out = []
        for g in range(ATT_GROUPS):
            m_old, l_old = carry[g]
            s = s_ref[slot, g]
            m_new = jnp.maximum(m_old, jnp.max(s, axis=0, keepdims=True))
            alpha = jnp.exp2(m_old - m_new)
            p = jnp.exp2(s - m_new)
            l_new = l_old * alpha + jnp.sum(p, axis=0, keepdims=True)
            acc_ref[g] = acc_ref[g] * alpha + _dot(vch, p.astype(BF16))
            out.append((m_new, l_new))
        return tuple(out)

    def body(i, carry):
        c = 2 * i
        put_scores(c + 1, 1)
        carry = softmax_pv(c, 0, carry)
        put_scores(c + 2, 0)
        return softmax_pv(c + 1, 1, carry)

    init = tuple((jnp.full((1, tq), -jnp.inf, F32), jnp.zeros((1, tq), F32)) for _ in range(ATT_GROUPS))
    put_scores(0, 0)
    fin = lax.fori_loop(0, (n_kc - 1) // 2, body, init)
    if n_kc % 2 == 0:
        put_scores(n_kc - 1, 1)
        fin = softmax_pv(n_kc - 2, 0, fin)
    fin = softmax_pv(n_kc - 1, (n_kc - 1) % 2, fin)
    for g in range(ATT_GROUPS):
        o_ref[0, :, g * HEAD_DIM:(g + 1) * HEAD_DIM] = (acc_ref[g] / fin[g][1]).T.astype(BF16)


def _attention(qt, kn, vt, n_ctx):
    nb, lc, _ = kn.shape
    n_lat = lc - n_ctx
    tq = _largest_tile(n_lat, 256, 128)
    assert n_ctx % tq == 0
    kc = _largest_tile(lc, 768, 256)
    gw = ATT_GROUPS * HEAD_DIM
    kern = functools.partial(_attn_kernel, tq=tq, kc=kc, n_kc=lc // kc)
    vmem = 8 * lc * HEAD_DIM * 2 + 4 * ATT_GROUPS * kc * tq * 4 + 8 * gw * tq * 4
    return pl.pallas_call(
        kern,
        grid=(nb, ATT_KV_HEADS, n_lat // tq),
        in_specs=[pl.BlockSpec((1, gw, tq), lambda b, h, i: (b, h, n_ctx // tq + i)),
                  pl.BlockSpec((1, lc, HEAD_DIM), lambda b, h, i: (b, 0, h)),
                  pl.BlockSpec((1, HEAD_DIM, lc), lambda b, h, i: (b, h, 0))],
        out_specs=pl.BlockSpec((1, tq, gw), lambda b, h, i: (b, i, h)),
        out_shape=jax.ShapeDtypeStruct((nb, n_lat, ATT_W), BF16),
        scratch_shapes=[pltpu.VMEM((ATT_GROUPS, HEAD_DIM, tq), F32),
                        pltpu.VMEM((2, ATT_GROUPS, kc, tq), F32)],
        compiler_params=pltpu.CompilerParams(
            dimension_semantics=("arbitrary", "arbitrary", "arbitrary"),
            vmem_limit_bytes=_vmem_limit(vmem)),
        name="attention",
    )(qt, kn, vt)


def _hgrn_chunk(q_ref, v_ref, r_ref, lb, reverse):
    c = HGRN_CHUNK
    q = q_ref[0].astype(F32)
    v = v_ref[0]
    r = r_ref[0].astype(F32)
    sig = jax.nn.sigmoid(r)
    f = lb + (1.0 - lb) * sig
    logf = jnp.log(f)
    k = (1.0 - lb) * (1.0 - sig)
    ti = lax.broadcasted_iota(I32, (c, c), 0)
    si = lax.broadcasted_iota(I32, (c, c), 1)
    tri = jnp.where((si >= ti) if reverse else (si <= ti), 1.0, 0.0).astype(BF16)
    hi = logf.astype(BF16)
    lo = (logf - hi.astype(F32)).astype(BF16)
    bcum = _dot(tri, hi) + _dot(tri, lo)
    b_end = bcum[0:1, :] if reverse else bcum[c - 1:c, :]
    qt = (q * jnp.exp(bcum)).astype(BF16)
    kt = (k * jnp.exp(b_end - bcum)).astype(BF16)
    dec = jnp.exp(b_end)

    nsub = c // HGRN_SUB
    qp, kp, cols = [], [], []
    for blk in range(nsub):
        rs = slice(blk * HGRN_SUB, (blk + 1) * HGRN_SUB)
        cs = slice(blk * HGRN_SUB, c) if reverse else slice(0, (blk + 1) * HGRN_SUB)
        mid = blk * HGRN_SUB + HGRN_SUB // 2
        ref = bcum[mid:mid + 1, :]
        qp.append((q[rs] * jnp.exp(bcum[rs] - ref)).astype(BF16))
        kp.append((k[cs] * jnp.exp(ref - bcum[cs])).astype(BF16))
        cols.append(cs)

    keep = []
    for blk in range(nsub):
        n_cols = cols[blk].stop - cols[blk].start
        trow = blk * HGRN_SUB + lax.broadcasted_iota(I32, (HGRN_SUB, n_cols), 0)
        scol = cols[blk].start + lax.broadcasted_iota(I32, (HGRN_SUB, n_cols), 1)
        keep.append((scol >= trow) if reverse else (scol <= trow))
    return dict(v=v, qt=qt, kt=kt, dec=dec, qp=qp, kp=kp, cols=cols, keep=keep)


def _hgrn_emit(dirs, s_refs, o_refs):
    nsub = HGRN_CHUNK // HGRN_SUB
    heads = [slice(h * HGRN_DK, (h + 1) * HGRN_DK) for h in range(HGRN_HEADS)]
    inter = []
    for d, s_ref in zip(dirs, s_refs):
        row = []
        for h, hs in enumerate(heads):
            st = s_ref[h]
            row.append(_dot_nt(d["qt"][:, hs], st.astype(BF16)))
            s_ref[h] = st * d["dec"][:, hs] + _dot_tn(d["v"][:, hs], d["kt"][:, hs])
        inter.append(row)
    scores = [[[_dot_nt(d["qp"][blk][:, hs], d["kp"][blk][:, hs]) for blk in range(nsub)] for hs in heads]
              for d in dirs]
    for d, sc, it, o_ref in zip(dirs, scores, inter, o_refs):
        outs = []
        for h, hs in enumerate(heads):
            parts = [_dot(jnp.where(d["keep"][blk], sc[h][blk], 0.0).astype(BF16), d["v"][d["cols"][blk], hs])
                     for blk in range(nsub)]
            outs.append(it[h] + jnp.concatenate(parts, axis=0))
        o_ref[0] = jnp.concatenate(outs, axis=1).astype(BF16)


def _hgrn_kernel(qf_ref, vf_ref, rf_ref, qb_ref, vb_ref, rb_ref, lb_ref, of_ref, ob_ref, sf_ref, sb_ref):
    @pl.when(pl.program_id(1) == 0)
    def _():
        sf_ref[...] = jnp.zeros(sf_ref.shape, F32)
        sb_ref[...] = jnp.zeros(sb_ref.shape, F32)

    n_layers = lb_ref.shape[0] // 2

    def lower_bound(direction):
        rows = [lb_ref[direction * n_layers + l:direction * n_layers + l + 1, :] for l in range(n_layers)]
        amax = functools.reduce(jnp.maximum, rows)
        e = [jnp.exp(a - amax) for a in rows]
        return e[0] / functools.reduce(lambda u, w: u + w, e)

    fwd = _hgrn_chunk(qf_ref, vf_ref, rf_ref, lower_bound(0), False)
    bwd = _hgrn_chunk(qb_ref, vb_ref, rb_ref, lower_bound(1), True)
    _hgrn_emit([fwd, bwd], [sf_ref, sb_ref], [of_ref, ob_ref])


def _hgrn_scan(p, hgrn_lb, n_ctx):
    nb, lc, _ = p.shape
    c = HGRN_CHUNK
    assert n_ctx % c == 0 and lc % c == 0
    n_chunks = lc // c
    ctx_chunks = n_ctx // c

    def fwd(col):
        return lambda b, s: (b, s, col)

    def bwd_chunk(s):
        return jnp.where(s < ctx_chunks, ctx_chunks - 1 - s, n_chunks - 1 + ctx_chunks - s)

    def bwd(col):
        return lambda b, s: (b, bwd_chunk(s), col)

    blk = (1, c, HGRN_W)
    lb2 = hgrn_lb.reshape(-1, HGRN_W)
    return pl.pallas_call(
        _hgrn_kernel,
        grid=(nb, n_chunks),
        in_specs=[pl.BlockSpec(blk, fwd(COL_HQ // HGRN_W)), pl.BlockSpec(blk, fwd(COL_HI // HGRN_W)),
                  pl.BlockSpec(blk, fwd(COL_FF // HGRN_W)),
                  pl.BlockSpec(blk, bwd(COL_HQ // HGRN_W)), pl.BlockSpec(blk, bwd(COL_HI // HGRN_W)),
                  pl.BlockSpec(blk, bwd(COL_FB // HGRN_W)),
                  pl.BlockSpec(lb2.shape, lambda b, s: (0, 0))],
        out_specs=[pl.BlockSpec(blk, fwd(0)), pl.BlockSpec(blk, bwd(0))],
        out_shape=[jax.ShapeDtypeStruct((nb, lc, HGRN_W), BF16)] * 2,
        scratch_shapes=[pltpu.VMEM((HGRN_HEADS, HGRN_DK, HGRN_DK), F32)] * 2,
        compiler_params=pltpu.CompilerParams(dimension_semantics=("arbitrary", "arbitrary")),
        name="hgrn_scan",
    )(p, p, p, p, p, p, lb2)


def _merge_kernel(of_ref, ob_ref, hg_ref, gh_ref, ga_ref, oa_ref, x_ref, mod_ref, ng_ref, hng_ref,
                  wb0_ref, wb1_ref, wo_ref, rw_ref, rb_ref,
                  x1_ref, t_ref, ids_ref, wts_ref):
    b = pl.program_id(0)
    m = mod_ref[pl.ds(b, 1), :]
    gate_mix = m[:, 2 * D_MODEL:3 * D_MODEL]
    shift_f = m[:, 3 * D_MODEL:4 * D_MODEL]
    scale_f = m[:, 4 * D_MODEL:5 * D_MODEL]

    o = of_ref[0].astype(F32) + ob_ref[0].astype(F32)
    hng = hng_ref[...]
    o_h = jnp.concatenate(
        [_rms(o[:, h * HGRN_DK:(h + 1) * HGRN_DK], hng) for h in range(HGRN_HEADS)], axis=1)
    g_raw = hg_ref[0].astype(F32)
    o_h = o_h * (g_raw * jax.nn.sigmoid(g_raw))
    y = (jax.nn.sigmoid(gh_ref[0].astype(F32)) * _dot(o_h.astype(BF16), wb0_ref[...])
         + jax.nn.sigmoid(ga_ref[0].astype(F32)) * _dot(oa_ref[0], wb1_ref[...]))
    y = _dot(y.astype(BF16), wo_ref[...])
    x1 = x_ref[0] + gate_mix * _rms(y, ng_ref[1:2, :])
    x1_ref[0] = x1
    t = _rms(x1, ng_ref[2:3, :]) * (1.0 + scale_f) + shift_f
    t_ref[...] = t

    t_hi = t.astype(BF16)
    t_lo = (t - t_hi.astype(F32)).astype(BF16)
    rw = rw_ref[...]
    rw_hi = rw.astype(BF16)
    rw_lo = (rw - rw_hi.astype(F32)).astype(BF16)
    logits = _dot_nt(rw_hi, t_hi) + _dot_nt(rw_hi, t_lo) + _dot_nt(rw_lo, t_hi) + rb_ref[...]
    eidx = lax.broadcasted_iota(I32, logits.shape, 0).astype(F32)
    vals = []
    for j in range(TOP_K):
        mx = jnp.max(logits, axis=0, keepdims=True)
        idx = jnp.min(jnp.where(logits == mx, eidx, float(N_EXPERTS)), axis=0, keepdims=True)
        ids_ref[j:j + 1, :] = idx.astype(I32)
        vals.append(mx)
        logits = jnp.where(eidx == idx, -jnp.inf, logits)
    ex = [jnp.exp(vj - vals[0]) for vj in vals]
    den = ex[0] + ex[1] + ex[2] + ex[3]
    for j in range(TOP_K):
        wts_ref[j:j + 1, :] = ex[j] / den


def _merge(o_f, o_b, p, o_a, x, mod, norm_g, hgrn_norm_g, wb0, wb1, wo, router_wt, router_b, n_ctx):
    nb, n_lat, d = x.shape
    tm = _largest_tile(n_lat, 256, 128)
    assert n_ctx % tm == 0
    off = n_ctx // tm
    nt = n_lat // tm
    row = (1, tm, d)

    def pcol(col):
        return pl.BlockSpec(row, lambda b, i: (b, off + i, col // d))

    def full(a):
        return pl.BlockSpec(a.shape, lambda b, i: (0,) * a.ndim)

    tok = nb * n_lat
    return pl.pallas_call(
        _merge_kernel,
        grid=(nb, nt),
        in_specs=[pl.BlockSpec(row, lambda b, i: (b, off + i, 0)),
                  pl.BlockSpec(row, lambda b, i: (b, off + i, 0)),
                  pcol(COL_HG), pcol(COL_GH), pcol(COL_GA),
                  pl.BlockSpec(row, lambda b, i: (b, i, 0)),
                  pl.BlockSpec(row, lambda b, i: (b, i, 0)),
                  full(mod), full(norm_g), full(hgrn_norm_g), full(wb0), full(wb1), full(wo),
                  full(router_wt), full(router_b)],
        out_specs=[pl.BlockSpec(row, lambda b, i: (b, i, 0)),
                   pl.BlockSpec((tm, d), lambda b, i: (b * nt + i, 0)),
                   pl.BlockSpec((TOP_K, tm), lambda b, i: (0, b * nt + i)),
                   pl.BlockSpec((TOP_K, tm), lambda b, i: (0, b * nt + i))],
        out_shape=[jax.ShapeDtypeStruct((nb, n_lat, d), F32),
                   jax.ShapeDtypeStruct((tok, d), F32),
                   jax.ShapeDtypeStruct((TOP_K, tok), I32),
                   jax.ShapeDtypeStruct((TOP_K, tok), F32)],
        compiler_params=pltpu.CompilerParams(
            dimension_semantics=("arbitrary", "arbitrary"),
            vmem_limit_bytes=_vmem_limit(40 * 1024 * 1024)),
        name="merge_router",
    )(o_f, o_b, p, p, p, o_a, x, mod, norm_g, hgrn_norm_g, wb0, wb1, wo, router_wt, router_b)


def _rank_kernel(ids_ref, rank_ref, cnt_ref, carry_ref, *, tt):
    @pl.when(pl.program_id(0) == 0)
    def _():
        carry_ref[...] = jnp.zeros(carry_ref.shape, F32)

    eidx = lax.broadcasted_iota(I32, (N_EXPERTS, tt), 0)
    si = lax.broadcasted_iota(I32, (tt, tt), 0)
    ti = lax.broadcasted_iota(I32, (tt, tt), 1)
    before = jnp.where(si < ti, 1.0, 0.0).astype(BF16)
    carry = carry_ref[...]
    for j in range(TOP_K):
        onehot = eidx == ids_ref[j:j + 1, :]
        oh = jnp.where(onehot, 1.0, 0.0)
        earlier = _dot(oh.astype(BF16), before)
        rank = jnp.sum(jnp.where(onehot, carry + earlier, 0.0), axis=0, keepdims=True)
        rank_ref[j:j + 1, :] = rank.astype(I32)
        carry = carry + jnp.sum(oh, axis=1, keepdims=True)
    carry_ref[...] = carry
    cnt_ref[...] = carry.astype(I32)


def _expert_rank(ids):
    _, tok = ids.shape
    tt = _largest_tile(tok, 512, 128)
    return pl.pallas_call(
        functools.partial(_rank_kernel, tt=tt),
        grid=(tok // tt,),
        in_specs=[pl.BlockSpec((TOP_K, tt), lambda i: (0, i))],
        out_specs=[pl.BlockSpec((TOP_K, tt), lambda i: (0, i)),
                   pl.BlockSpec((N_EXPERTS, 1), lambda i: (0, 0))],
        out_shape=[jax.ShapeDtypeStruct((TOP_K, tok), I32),
                   jax.ShapeDtypeStruct((N_EXPERTS, 1), I32)],
        scratch_shapes=[pltpu.VMEM((N_EXPERTS, 1), F32)],
        compiler_params=pltpu.CompilerParams(dimension_semantics=("arbitrary",)),
        name="expert_rank",
    )(ids)


def _row_copies_wait(src, dst, sem):
    pltpu.make_async_copy(src, dst, sem).wait()


def _scatter_kernel(zt_ref, zvalid_ref, pos_ref, t_ref, xs_ref, zero_buf, sem, zsem, *, tt):
    @pl.when(pl.program_id(0) == 0)
    def _():
        zero_buf[...] = jnp.zeros(zero_buf.shape, F32)

        def zero_copy(k):
            return pltpu.make_async_copy(zero_buf, xs_ref.at[pl.ds(zt_ref[k] * MOE_TILE, MOE_TILE)], zsem)

        def start(k, carry):
            @pl.when(zvalid_ref[k] == 1)
            def _():
                zero_copy(k).start()
            return carry

        def wait(k, carry):
            @pl.when(zvalid_ref[k] == 1)
            def _():
                zero_copy(k).wait()
            return carry

        lax.fori_loop(0, zt_ref.shape[0], start, 0)
        lax.fori_loop(0, zt_ref.shape[0], wait, 0)

    def body(r, carry):
        for j in range(TOP_K):
            pltpu.make_async_copy(t_ref.at[pl.ds(r, 1)], xs_ref.at[pl.ds(pos_ref[0, j, r], 1)], sem).start()
        return carry

    lax.fori_loop(0, tt, body, 0, unroll=8)
    for j in range(TOP_K):
        _row_copies_wait(t_ref, xs_ref.at[pl.ds(0, tt)], sem)


def _scatter_rows(zero_tiles, zero_valid, pos3, t, n_rows):
    tok, d = t.shape
    tt = pos3.shape[2]
    return pl.pallas_call(
        functools.partial(_scatter_kernel, tt=tt),
        grid_spec=pltpu.PrefetchScalarGridSpec(
            num_scalar_prefetch=2,
            grid=(tok // tt,),
            in_specs=[pl.BlockSpec((1, TOP_K, tt), lambda i, zt, zv: (i, 0, 0), memory_space=pltpu.SMEM),
                      pl.BlockSpec((tt, d), lambda i, zt, zv: (i, 0))],
            out_specs=pl.BlockSpec(memory_space=pl.ANY),
            scratch_shapes=[pltpu.VMEM((MOE_TILE, d), F32), pltpu.SemaphoreType.DMA(()),
                            pltpu.SemaphoreType.DMA(())]),
        out_shape=jax.ShapeDtypeStruct((n_rows, d), F32),
        compiler_params=pltpu.CompilerParams(dimension_semantics=("arbitrary",)),
        name="moe_scatter",
    )(zero_tiles, zero_valid, pos3, t)


def _expert_kernel(te_ref, first_ref, nused_ref, xs_ref, wu_ref, bu_ref, wd_ref, bd_ref, ys_ref,
                   wu_bf, wd_bf):
    i = pl.program_id(0)

    @pl.when(i >= nused_ref[0])
    def _():
        ys_ref[...] = jnp.zeros(ys_ref.shape, F32)

    @pl.when(i < nused_ref[0])
    def _():
        @pl.when(first_ref[i] == 1)
        def _():
            wu_bf[...] = wu_ref[0].astype(BF16)
            wd_bf[...] = wd_ref[0].astype(BF16)

        h = _dot(xs_ref[...].astype(BF16), wu_bf[...]) + bu_ref[0]
        glu = jnp.minimum(h[:, :D_EXPERT], SWIGLU_LIMIT)
        lin = jnp.clip(h[:, D_EXPERT:], -SWIGLU_LIMIT, SWIGLU_LIMIT)
        a = glu * jax.nn.sigmoid(SWIGLU_ALPHA * glu) * (lin + 1.0)
        ys_ref[...] = _dot(a.astype(BF16), wd_bf[...]) + bd_ref[0]


def _experts(xs, tile_expert, tile_first, n_used, w_up, b_up, w_down, b_down):
    n_rows, d = xs.shape
    tm = MOE_TILE
    n_tiles = n_rows // tm

    def tile(i, te, first, nused):
        return (jnp.minimum(i, nused[0] - 1), 0)

    def wsel(i, te, first, nused):
        return (te[i], 0, 0)

    vmem = 2 * (d * 2 * D_EXPERT * 4 + D_EXPERT * d * 4) + d * 2 * D_EXPERT * 2 + D_EXPERT * d * 2 \
        + 4 * tm * d * 4 + 6 * tm * 2 * D_EXPERT * 4
    return pl.pallas_call(
        _expert_kernel,
        grid_spec=pltpu.PrefetchScalarGridSpec(
            num_scalar_prefetch=3,
            grid=(n_tiles,),
            in_specs=[pl.BlockSpec((tm, d), tile),
                      pl.BlockSpec((1, d, 2 * D_EXPERT), wsel),
                      pl.BlockSpec((1, 1, 2 * D_EXPERT), wsel),
                      pl.BlockSpec((1, D_EXPERT, d), wsel),
                      pl.BlockSpec((1, 1, d), wsel)],
            out_specs=pl.BlockSpec((tm, d), lambda i, te, first, nused: (i, 0)),
            scratch_shapes=[pltpu.VMEM((d, 2 * D_EXPERT), BF16), pltpu.VMEM((D_EXPERT, d), BF16)]),
        out_shape=jax.ShapeDtypeStruct((n_rows, d), F32),
        compiler_params=pltpu.CompilerParams(
            dimension_semantics=("arbitrary",), vmem_limit_bytes=_vmem_limit(vmem)),
        name="moe_experts",
    )(tile_expert, tile_first, n_used, xs, w_up, b_up.reshape(N_EXPERTS, 1, -1), w_down,
      b_down.reshape(N_EXPERTS, 1, -1))


def _combine_kernel(pos_ref, ys_ref, w_ref, x1_ref, mod_ref, ng_ref, o_ref, buf, sem, *, tt, tiles_per_batch):
    def body(r, carry):
        for j in range(TOP_K):
            pltpu.make_async_copy(ys_ref.at[pl.ds(pos_ref[0, j, r], 1)], buf.at[j, pl.ds(r, 1)], sem).start()
        return carry

    lax.fori_loop(0, tt, body, 0, unroll=8)
    b = pl.program_id(0) // tiles_per_batch
    gate_f = mod_ref[pl.ds(b, 1), 5 * D_MODEL:6 * D_MODEL]
    w = w_ref[...]
    for j in range(TOP_K):
        _row_copies_wait(ys_ref.at[pl.ds(0, tt)], buf.at[j], sem)
    f = w[:, 0:1] * buf[0]
    for j in range(1, TOP_K):
        f = f + w[:, j:j + 1] * buf[j]
    o_ref[...] = x1_ref[...] + gate_f * _rms(f, ng_ref[3:4, :])


def _combine(pos3, ys, wts_t, x1, mod, norm_g, tiles_per_batch):
    tok, d = x1.shape
    tt = pos3.shape[2]
    kern = functools.partial(_combine_kernel, tt=tt, tiles_per_batch=tiles_per_batch)
    return pl.pallas_call(
        kern,
        grid=(tok // tt,),
        in_specs=[pl.BlockSpec((1, TOP_K, tt), lambda i: (i, 0, 0), memory_space=pltpu.SMEM),
                  pl.BlockSpec(memory_space=pl.ANY),
                  pl.BlockSpec((tt, TOP_K), lambda i: (i, 0)),
                  pl.BlockSpec((tt, d), lambda i: (i, 0)),
                  pl.BlockSpec(mod.shape, lambda i: (0, 0)),
                  pl.BlockSpec(norm_g.shape, lambda i: (0, 0))],
        out_specs=pl.BlockSpec((tt, d), lambda i: (i, 0)),
        out_shape=jax.ShapeDtypeStruct((tok, d), F32),
        scratch_shapes=[pltpu.VMEM((TOP_K, tt, d), F32), pltpu.SemaphoreType.DMA(())],
        compiler_params=pltpu.CompilerParams(dimension_semantics=("arbitrary",)),
        name="moe_combine",
    )(pos3, ys, wts_t, x1, mod, norm_g)


def _rope_tables(n_ctx, n_lat):
    inv = ROPE_THETA ** (-np.arange(0, ROPE_AXIS_DIM, 2, dtype=np.float32) / ROPE_AXIS_DIM)
    idx = jnp.arange(n_lat, dtype=I32)
    rows = (idx // GRID_W).astype(F32)
    cols = (idx % GRID_W).astype(F32)
    half = ROPE_AXIS_DIM // 2
    inv = jnp.asarray(inv, F32)
    ang_r = rows[:, None] * inv[None, :]
    ang_c = cols[:, None] * inv[None, :]
    ang = jnp.concatenate([ang_r, ang_r, ang_c, ang_c], axis=1)
    cos, sin = jnp.cos(ang), jnp.sin(ang)
    lane = np.arange(HEAD_DIM)
    first = jnp.asarray((lane % ROPE_AXIS_DIM) < half)
    sin_a = jnp.where(first, -sin, 0.0)
    sin_b = jnp.where(first, 0.0, sin)
    pad1 = jnp.ones((n_ctx, HEAD_DIM), F32)
    pad0 = jnp.zeros((n_ctx, HEAD_DIM), F32)
    return (jnp.concatenate([pad1, cos], axis=0), jnp.concatenate([pad0, sin_a], axis=0),
            jnp.concatenate([pad0, sin_b], axis=0))


def kernel(x, c, ctx, c_ctx, w_mod, b_mod, norm_g, w_in, hgrn_lb, hgrn_norm_g, qk_norm_g, w_branch, w_out,
           router_w, router_b, w_up, b_up, w_down, b_down):
    nb, n_lat, d = x.shape
    n_ctx = ctx.shape[1]
    assert d == D_MODEL and w_mod.shape[0] == 1, "single-layer kernel"
    tok = nb * n_lat

    cv = jnp.zeros((8, d), F32).at[:nb].set(c).at[nb].set(c_ctx)
    mod = _modulation(cv, w_mod[0], b_mod[0])

    xc = jnp.concatenate([ctx, x], axis=1)
    sizes = np.cumsum([0, HGRN_W, HGRN_W, HGRN_W, HGRN_W, HGRN_W, ATT_W, KV_W, KV_W, D_MODEL, D_MODEL])
    order = [0, 1, 2, 3, 4, 5, 8, 9, 6, 7]
    w_in_bf = jnp.concatenate([w_in[0][:, sizes[i]:sizes[i + 1]] for i in order], axis=1).astype(BF16)
    p = _in_projection(xc, mod, norm_g[0], w_in_bf, n_ctx)

    cos, sin_a, sin_b = _rope_tables(n_ctx, n_lat)
    qt, kn, vt = _qk_prep(p, cos, sin_a, sin_b, qk_norm_g[0])
    o_a = _attention(qt, kn, vt, n_ctx)
    o_f, o_b = _hgrn_scan(p, hgrn_lb, n_ctx)

    x1, t, ids, wts = _merge(
        o_f, o_b, p, o_a, x, mod, norm_g[0], hgrn_norm_g, w_branch[0, 0].astype(BF16),
        w_branch[0, 1].astype(BF16), w_out[0].astype(BF16), router_w[0].T, router_b[0].reshape(N_EXPERTS, 1),
        n_ctx)

    rank, counts = _expert_rank(ids)
    counts = counts[:, 0]
    tiles_e = (counts + MOE_TILE - 1) // MOE_TILE
    tile_end = jnp.cumsum(tiles_e)
    starts = (tile_end - tiles_e) * MOE_TILE
    n_tiles = (tok * TOP_K) // MOE_TILE + N_EXPERTS
    tile_idx = jnp.arange(n_tiles, dtype=I32)
    tile_expert = jnp.sum((tile_end[None, :] <= tile_idx[:, None]).astype(I32), axis=1)
    tile_expert = jnp.minimum(tile_expert, N_EXPERTS - 1)
    n_used = tile_end[-1:].astype(I32)
    tile_expert = jnp.where(tile_idx < n_used[0], tile_expert, tile_expert[jnp.maximum(n_used[0] - 1, 0)])
    tile_first = jnp.concatenate([jnp.ones((1,), I32), (tile_expert[1:] != tile_expert[:-1]).astype(I32)])
    onehot = ids[:, :, None] == jnp.arange(N_EXPERTS, dtype=I32)
    pos = jnp.sum(jnp.where(onehot, starts.astype(I32), 0), axis=-1) + rank
    tt = _largest_tile(n_lat, TOK_TILE, 8)
    pos3 = pos.reshape(TOP_K, tok // tt, tt).transpose(1, 0, 2)

    tail = n_used[0] + jnp.arange(N_EXPERTS, dtype=I32)
    zero_tiles = jnp.clip(jnp.concatenate([tile_end.astype(I32) - 1, tail]), 0, n_tiles - 1)
    zero_valid = jnp.concatenate([counts % MOE_TILE != 0, tail < n_tiles]).astype(I32)

    xs = _scatter_rows(zero_tiles, zero_valid, pos3, t, n_tiles * MOE_TILE)
    ys = _experts(xs, tile_expert, tile_first, n_used, w_up[0], b_up[0], w_down[0], b_down[0])
    out = _combine(pos3, ys, wts.T, x1.reshape(tok, d), mod, norm_g[0], n_lat // tt)
    return out.reshape(nb, n_lat, d)
```

```python
import functools

import jax
import jax.numpy as jnp
import numpy as np
from jax import lax
from jax.experimental import pallas as pl
from jax.experimental.pallas import tpu as pltpu

F32 = jnp.float32
BF16 = jnp.bfloat16
I32 = jnp.int32

D_MODEL = 1024
GRID_W = 64
N_MOD = 6
EPS = 1e-6
HGRN_HEADS = 8
HGRN_DK = 128
HGRN_W = HGRN_HEADS * HGRN_DK
ATT_HEADS = 8
ATT_KV_HEADS = 2
ATT_GROUPS = ATT_HEADS // ATT_KV_HEADS
HEAD_DIM = 128
ATT_W = ATT_HEADS * HEAD_DIM
KV_W = ATT_KV_HEADS * HEAD_DIM
ROPE_AXIS_DIM = HEAD_DIM // 2
ROPE_THETA = 10000.0
N_EXPERTS = 32
TOP_K = 4
D_EXPERT = 1024
SWIGLU_LIMIT = 7.0
SWIGLU_ALPHA = 1.702

COL_HQ, COL_HI, COL_FF, COL_FB, COL_HG, COL_AQ, COL_GH, COL_GA = (i * 1024 for i in range(8))
COL_K = 8 * 1024
COL_V = COL_K + KV_W
IN_COLS = COL_V + KV_W

V7X_VMEM_BYTES = 64 * 1024 * 1024
LANES = 128

HGRN_CHUNK = 64
HGRN_SUB = 16
MOE_TILE = 256
DISPATCH_TILE = 512
RUN_ALIGN = 8
PERM_ROWS = 256
WAIT_GROUP_LOG2 = 5


def _vmem_limit(nbytes):
    return int(min(V7X_VMEM_BYTES - 6 * 1024 * 1024, max(nbytes, 32 * 1024 * 1024)))


def _largest_tile(n, cap, mult):
    best = None
    for t in range(mult, min(n, cap) + 1, mult):
        if n % t == 0:
            best = t
    assert best is not None, (n, cap, mult)
    return best


def _rms(x, g):
    return x * lax.rsqrt(jnp.mean(x * x, axis=-1, keepdims=True) + EPS) * g


def _dot(a, b):
    return jnp.dot(a, b, preferred_element_type=F32)


def _dot_nt(a, b):
    return lax.dot_general(a, b, (((1,), (1,)), ((), ())), preferred_element_type=F32)


def _dot_tn(a, b):
    return lax.dot_general(a, b, (((0,), (0,)), ((), ())), preferred_element_type=F32)


def _mod_kernel(cv_ref, w_ref, b_ref, o_ref):
    cv = cv_ref[...]
    s = cv * jax.nn.sigmoid(cv)
    o_ref[...] = jnp.dot(s, w_ref[...], preferred_element_type=F32,
                         precision=lax.Precision.HIGHEST) + b_ref[...]


def _modulation(cv, w_mod, b_mod):
    rows, d = cv.shape
    n = w_mod.shape[1]
    tn = 1024
    return pl.pallas_call(
        _mod_kernel,
        grid=(n // tn,),
        in_specs=[pl.BlockSpec((rows, d), lambda j: (0, 0)),
                  pl.BlockSpec((d, tn), lambda j: (0, j)),
                  pl.BlockSpec((1, tn), lambda j: (0, j))],
        out_specs=pl.BlockSpec((rows, tn), lambda j: (0, j)),
        out_shape=jax.ShapeDtypeStruct((rows, n), F32),
        name="modulation",
    )(cv, w_mod, b_mod.reshape(1, n))


def _inproj_kernel(x_ref, mod_ref, g_ref, w_ref, o_ref, *, tm, n_ctx, n_batch):
    b = pl.program_id(1)
    i = pl.program_id(2)
    x = x_ref[0]
    xn = _rms(x, g_ref[0:1, :])
    row = i * tm + lax.broadcasted_iota(I32, (tm, 1), 0)
    is_ctx = row < n_ctx
    m_lat = mod_ref[pl.ds(b, 1), :]
    m_ctx = mod_ref[n_batch:n_batch + 1, :]
    shift = jnp.where(is_ctx, m_ctx[:, 0:D_MODEL], m_lat[:, 0:D_MODEL])
    scale = jnp.where(is_ctx, m_ctx[:, D_MODEL:2 * D_MODEL], m_lat[:, D_MODEL:2 * D_MODEL])
    u = xn * (1.0 + scale) + shift
    o_ref[0] = _dot(u.astype(BF16), w_ref[...]).astype(BF16)


def _in_projection(xc, mod, norm_g, w_in_bf, n_ctx):
    nb, lc, d = xc.shape
    n = w_in_bf.shape[1]
    tm = _largest_tile(lc, 1056, 16)
    tn = 2176
    assert n % tn == 0
    kern = functools.partial(_inproj_kernel, tm=tm, n_ctx=n_ctx, n_batch=nb)
    vmem = 2 * (tm * d * 4 + d * tn * 2 + tm * tn * 2) + 6 * tm * d * 4
    return pl.pallas_call(
        kern,
        grid=(n // tn, nb, lc // tm),
        in_specs=[pl.BlockSpec((1, tm, d), lambda j, b, i: (b, i, 0)),
                  pl.BlockSpec(mod.shape, lambda j, b, i: (0, 0)),
                  pl.BlockSpec(norm_g.shape, lambda j, b, i: (0, 0)),
                  pl.BlockSpec((d, tn), lambda j, b, i: (0, j))],
        out_specs=pl.BlockSpec((1, tm, tn), lambda j, b, i: (b, i, j)),
        out_shape=jax.ShapeDtypeStruct((nb, lc, n), BF16),
        compiler_params=pltpu.CompilerParams(
            dimension_semantics=("arbitrary", "arbitrary", "arbitrary"),
            vmem_limit_bytes=_vmem_limit(vmem)),
        name="in_projection",
    )(xc, mod, norm_g, w_in_bf)


def _rope(xn, cos, sin_a, sin_b):
    return (xn * cos + pltpu.roll(xn, HEAD_DIM - ROPE_AXIS_DIM // 2, 1) * sin_a
            + pltpu.roll(xn, ROPE_AXIS_DIM // 2, 1) * sin_b)


def _qkprep_kernel(q_ref, k_ref, v_ref, cos_ref, sa_ref, sb_ref, g_ref, qt_ref, ko_ref, vt_ref):
    cos, sa, sb = cos_ref[...], sa_ref[...], sb_ref[...]
    gq, gk = g_ref[0:1, :], g_ref[1:2, :]
    qscale = HEAD_DIM ** -0.5 * np.log2(np.e)
    for h in range(ATT_HEADS):
        hs = slice(h * HEAD_DIM, (h + 1) * HEAD_DIM)
        xn = _rms(q_ref[0, :, hs].astype(F32), gq)
        qt_ref[0, hs, :] = (_rope(xn, cos, sa, sb) * qscale).T.astype(BF16)
    for h in range(ATT_KV_HEADS):
        hs = slice(h * HEAD_DIM, (h + 1) * HEAD_DIM)
        xn = _rms(k_ref[0, :, hs].astype(F32), gk)
        ko_ref[0, :, hs] = _rope(xn, cos, sa, sb).astype(BF16)
        vt_ref[0, hs, :] = v_ref[0, :, hs].astype(F32).T.astype(BF16)


def _qk_prep(p, cos, sin_a, sin_b, qk_norm_g):
    nb, lc, _ = p.shape
    tm = _largest_tile(lc, 512, 128)
    tab = pl.BlockSpec((tm, HEAD_DIM), lambda b, i: (i, 0))
    return pl.pallas_call(
        _qkprep_kernel,
        grid=(nb, lc // tm),
        in_specs=[pl.BlockSpec((1, tm, ATT_W), lambda b, i: (b, i, COL_AQ // ATT_W)),
                  pl.BlockSpec((1, tm, KV_W), lambda b, i: (b, i, COL_K // KV_W)),
                  pl.BlockSpec((1, tm, KV_W), lambda b, i: (b, i, COL_V // KV_W)),
                  tab, tab, tab,
                  pl.BlockSpec(qk_norm_g.shape, lambda b, i: (0, 0))],
        out_specs=[pl.BlockSpec((1, ATT_W, tm), lambda b, i: (b, 0, i)),
                   pl.BlockSpec((1, tm, KV_W), lambda b, i: (b, i, 0)),
                   pl.BlockSpec((1, KV_W, tm), lambda b, i: (b, 0, i))],
        out_shape=[jax.ShapeDtypeStruct((nb, ATT_W, lc), BF16),
                   jax.ShapeDtypeStruct((nb, lc, KV_W), BF16),
                   jax.ShapeDtypeStruct((nb, KV_W, lc), BF16)],
        name="qk_prep",
    )(p, p, p, cos, sin_a, sin_b, qk_norm_g)


def _attn_kernel(qt_ref, k_ref, vt_ref, o_ref, acc_ref, s_ref, *, tq, kc, n_kc):
    acc_ref[...] = jnp.zeros(acc_ref.shape, F32)
    qts = [qt_ref[0, g * HEAD_DIM:(g + 1) * HEAD_DIM, :] for g in range(ATT_GROUPS)]

    def put_scores(c, slot):
        kch = k_ref[0, pl.ds(pl.multiple_of(c * kc, kc), kc), :]
        for g in range(ATT_GROUPS):
            s_ref[slot, g] = _dot(kch, qts[g])

    def softmax_pv(c, slot, carry):
        vch = vt_ref[0, :, pl.ds(pl.multiple_of(c * kc, kc), kc)]
        out = []
        for g in range(ATT_GROUPS):
            m_old, l_old = carry[g]
            s = s_ref[slot, g]
            m_new = jnp.maximum(m_old, jnp.max(s, axis=0, keepdims=True))
            alpha = jnp.exp2(m_old - m_new)
            p = jnp.exp2(s - m_new)
            l_new = l_old * alpha + jnp.sum(p, axis=0, keepdims=True)
            acc_ref[g] = acc_ref[g] * alpha + _dot(vch, p.astype(BF16))
            out.append((m_new, l_new))
        return tuple(out)

    def body(i, carry):
        c = 2 * i
        put_scores(c + 1, 1)
        carry = softmax_pv(c, 0, carry)
        put_scores(c + 2, 0)
        return softmax_pv(c + 1, 1, carry)

    init = tuple((jnp.full((1, tq), -jnp.inf, F32), jnp.zeros((1, tq), F32)) for _ in range(ATT_GROUPS))
    put_scores(0, 0)
    fin = lax.fori_loop(0, (n_kc - 1) // 2, body, init)
    if n_kc % 2 == 0:
        put_scores(n_kc - 1, 1)
        fin = softmax_pv(n_kc - 2, 0, fin)
    fin = softmax_pv(n_kc - 1, (n_kc - 1) % 2, fin)
    for g in range(ATT_GROUPS):
        o_ref[0, :, g * HEAD_DIM:(g + 1) * HEAD_DIM] = (acc_ref[g] / fin[g][1]).T.astype(BF16)


def _attention(qt, kn, vt, n_ctx):
    nb, lc, _ = kn.shape
    n_lat = lc - n_ctx
    tq = _largest_tile(n_lat, 256, 128)
    assert n_ctx % tq == 0
    kc = _largest_tile(lc, 768, 256)
    gw = ATT_GROUPS * HEAD_DIM
    kern = functools.partial(_attn_kernel, tq=tq, kc=kc, n_kc=lc // kc)
    vmem = 8 * lc * HEAD_DIM * 2 + 4 * ATT_GROUPS * kc * tq * 4 + 8 * gw * tq * 4
    return pl.pallas_call(
        kern,
        grid=(nb, ATT_KV_HEADS, n_lat // tq),
        in_specs=[pl.BlockSpec((1, gw, tq), lambda b, h, i: (b, h, n_ctx // tq + i)),
                  pl.BlockSpec((1, lc, HEAD_DIM), lambda b, h, i: (b, 0, h)),
                  pl.BlockSpec((1, HEAD_DIM, lc), lambda b, h, i: (b, h, 0))],
        out_specs=pl.BlockSpec((1, tq, gw), lambda b, h, i: (b, i, h)),
        out_shape=jax.ShapeDtypeStruct((nb, n_lat, ATT_W), BF16),
        scratch_shapes=[pltpu.VMEM((ATT_GROUPS, HEAD_DIM, tq), F32),
                        pltpu.VMEM((2, ATT_GROUPS, kc, tq), F32)],
        compiler_params=pltpu.CompilerParams(
            dimension_semantics=("arbitrary", "arbitrary", "arbitrary"),
            vmem_limit_bytes=_vmem_limit(vmem)),
        name="attention",
    )(qt, kn, vt)


def _hgrn_chunk(q_ref, v_ref, r_ref, lb, reverse):
    c = HGRN_CHUNK
    q = q_ref[0].astype(F32)
    v = v_ref[0]
    r = r_ref[0].astype(F32)
    sig = jax.nn.sigmoid(r)
    f = lb + (1.0 - lb) * sig
    logf = jnp.log(f)
    k = (1.0 - lb) * (1.0 - sig)
    ti = lax.broadcasted_iota(I32, (c, c), 0)
    si = lax.broadcasted_iota(I32, (c, c), 1)
    tri = jnp.where((si >= ti) if reverse else (si <= ti), 1.0, 0.0).astype(BF16)
    hi = logf.astype(BF16)
    lo = (logf - hi.astype(F32)).astype(BF16)
    bcum = _dot(tri, hi) + _dot(tri, lo)
    b_end = bcum[0:1, :] if reverse else bcum[c - 1:c, :]
    qt = (q * jnp.exp(bcum)).astype(BF16)
    kt = (k * jnp.exp(b_end - bcum)).astype(BF16)
    dec = jnp.exp(b_end)

    nsub = c // HGRN_SUB
    qp, kp, cols = [], [], []
    for blk in range(nsub):
        rs = slice(blk * HGRN_SUB, (blk + 1) * HGRN_SUB)
        cs = slice(blk * HGRN_SUB, c) if reverse else slice(0, (blk + 1) * HGRN_SUB)
        mid = blk * HGRN_SUB + HGRN_SUB // 2
        ref = bcum[mid:mid + 1, :]
        qp.append((q[rs] * jnp.exp(bcum[rs] - ref)).astype(BF16))
        kp.append((k[cs] * jnp.exp(ref - bcum[cs])).astype(BF16))
        cols.append(cs)

    keep = []
    for blk in range(nsub):
        n_cols = cols[blk].stop - cols[blk].start
        trow = blk * HGRN_SUB + lax.broadcasted_iota(I32, (HGRN_SUB, n_cols), 0)
        scol = cols[blk].start + lax.broadcasted_iota(I32, (HGRN_SUB, n_cols), 1)
        keep.append((scol >= trow) if reverse else (scol <= trow))
    return dict(v=v, qt=qt, kt=kt, dec=dec, qp=qp, kp=kp, cols=cols, keep=keep)


def _hgrn_emit(dirs, s_refs, o_refs):
    nsub = HGRN_CHUNK // HGRN_SUB
    heads = [slice(h * HGRN_DK, (h + 1) * HGRN_DK) for h in range(HGRN_HEADS)]
    inter = []
    for d, s_ref in zip(dirs, s_refs):
        row = []
        for h, hs in enumerate(heads):
            st = s_ref[h]
            row.append(_dot_nt(d["qt"][:, hs], st.astype(BF16)))
            s_ref[h] = st * d["dec"][:, hs] + _dot_tn(d["v"][:, hs], d["kt"][:, hs])
        inter.append(row)
    scores = [[[_dot_nt(d["qp"][blk][:, hs], d["kp"][blk][:, hs]) for blk in range(nsub)] for hs in heads]
              for d in dirs]
    for d, sc, it, o_ref in zip(dirs, scores, inter, o_refs):
        outs = []
        for h, hs in enumerate(heads):
            parts = [_dot(jnp.where(d["keep"][blk], sc[h][blk], 0.0).astype(BF16), d["v"][d["cols"][blk], hs])
                     for blk in range(nsub)]
            outs.append(it[h] + jnp.concatenate(parts, axis=0))
        o_ref[0] = jnp.concatenate(outs, axis=1).astype(BF16)


def _hgrn_kernel(qf_ref, vf_ref, rf_ref, qb_ref, vb_ref, rb_ref, lb_ref, of_ref, ob_ref, sf_ref, sb_ref):
    @pl.when(pl.program_id(1) == 0)
    def _():
        sf_ref[...] = jnp.zeros(sf_ref.shape, F32)
        sb_ref[...] = jnp.zeros(sb_ref.shape, F32)

    n_layers = lb_ref.shape[0] // 2

    def lower_bound(direction):
        rows = [lb_ref[direction * n_layers + l:direction * n_layers + l + 1, :] for l in range(n_layers)]
        amax = functools.reduce(jnp.maximum, rows)
        e = [jnp.exp(a - amax) for a in rows]
        return e[0] / functools.reduce(lambda u, w: u + w, e)

    fwd = _hgrn_chunk(qf_ref, vf_ref, rf_ref, lower_bound(0), False)
    bwd = _hgrn_chunk(qb_ref, vb_ref, rb_ref, lower_bound(1), True)
    _hgrn_emit([fwd, bwd], [sf_ref, sb_ref], [of_ref, ob_ref])


def _hgrn_scan(p, hgrn_lb, n_ctx):
    nb, lc, _ = p.shape
    c = HGRN_CHUNK
    assert n_ctx % c == 0 and lc % c == 0
    n_chunks = lc // c
    ctx_chunks = n_ctx // c

    def fwd(col):
        return lambda b, s: (b, s, col)

    def bwd_chunk(s):
        return jnp.where(s < ctx_chunks, ctx_chunks - 1 - s, n_chunks - 1 + ctx_chunks - s)

    def bwd(col):
        return lambda b, s: (b, bwd_chunk(s), col)

    blk = (1, c, HGRN_W)
    lb2 = hgrn_lb.reshape(-1, HGRN_W)
    return pl.pallas_call(
        _hgrn_kernel,
        grid=(nb, n_chunks),
        in_specs=[pl.BlockSpec(blk, fwd(COL_HQ // HGRN_W)), pl.BlockSpec(blk, fwd(COL_HI // HGRN_W)),
                  pl.BlockSpec(blk, fwd(COL_FF // HGRN_W)),
                  pl.BlockSpec(blk, bwd(COL_HQ // HGRN_W)), pl.BlockSpec(blk, bwd(COL_HI // HGRN_W)),
                  pl.BlockSpec(blk, bwd(COL_FB // HGRN_W)),
                  pl.BlockSpec(lb2.shape, lambda b, s: (0, 0))],
        out_specs=[pl.BlockSpec(blk, fwd(0)), pl.BlockSpec(blk, bwd(0))],
        out_shape=[jax.ShapeDtypeStruct((nb, lc, HGRN_W), BF16)] * 2,
        scratch_shapes=[pltpu.VMEM((HGRN_HEADS, HGRN_DK, HGRN_DK), F32)] * 2,
        compiler_params=pltpu.CompilerParams(dimension_semantics=("arbitrary", "arbitrary")),
        name="hgrn_scan",
    )(p, p, p, p, p, p, lb2)


def _merge_kernel(of_ref, ob_ref, hg_ref, gh_ref, ga_ref, oa_ref, x_ref, mod_ref, ng_ref, hng_ref,
                  wb0_ref, wb1_ref, wo_ref, rw_ref, rb_ref,
                  x1_ref, t_ref, ids_ref, wts_ref):
    b = pl.program_id(0)
    m = mod_ref[pl.ds(b, 1), :]
    gate_mix = m[:, 2 * D_MODEL:3 * D_MODEL]
    shift_f = m[:, 3 * D_MODEL:4 * D_MODEL]
    scale_f = m[:, 4 * D_MODEL:5 * D_MODEL]

    o = of_ref[0].astype(F32) + ob_ref[0].astype(F32)
    hng = hng_ref[...]
    o_h = jnp.concatenate(
        [_rms(o[:, h * HGRN_DK:(h + 1) * HGRN_DK], hng) for h in range(HGRN_HEADS)], axis=1)
    g_raw = hg_ref[0].astype(F32)
    o_h = o_h * (g_raw * jax.nn.sigmoid(g_raw))
    y = (jax.nn.sigmoid(gh_ref[0].astype(F32)) * _dot(o_h.astype(BF16), wb0_ref[...])
         + jax.nn.sigmoid(ga_ref[0].astype(F32)) * _dot(oa_ref[0], wb1_ref[...]))
    y = _dot(y.astype(BF16), wo_ref[...])
    x1 = x_ref[0] + gate_mix * _rms(y, ng_ref[1:2, :])
    x1_ref[0] = x1
    t = _rms(x1, ng_ref[2:3, :]) * (1.0 + scale_f) + shift_f
    t_ref[...] = t

    t_hi = t.astype(BF16)
    t_lo = (t - t_hi.astype(F32)).astype(BF16)
    rw = rw_ref[...]
    rw_hi = rw.astype(BF16)
    rw_lo = (rw - rw_hi.astype(F32)).astype(BF16)
    logits = _dot_nt(rw_hi, t_hi) + _dot_nt(rw_hi, t_lo) + _dot_nt(rw_lo, t_hi) + rb_ref[...]
    eidx = lax.broadcasted_iota(I32, logits.shape, 0).astype(F32)
    vals = []
    for j in range(TOP_K):
        mx = jnp.max(logits, axis=0, keepdims=True)
        idx = jnp.min(jnp.where(logits == mx, eidx, float(N_EXPERTS)), axis=0, keepdims=True)
        ids_ref[j:j + 1, :] = idx.astype(I32)
        vals.append(mx)
        logits = jnp.where(eidx == idx, -jnp.inf, logits)
    ex = [jnp.exp(vj - vals[0]) for vj in vals]
    den = ex[0] + ex[1] + ex[2] + ex[3]
    for j in range(TOP_K):
        wts_ref[j:j + 1, :] = ex[j] / den


def _merge(o_f, o_b, p, o_a, x, mod, norm_g, hgrn_norm_g, wb0, wb1, wo, router_wt, router_b, n_ctx):
    nb, n_lat, d = x.shape
    tm = _largest_tile(n_lat, 256, 128)
    assert n_ctx % tm == 0
    off = n_ctx // tm
    nt = n_lat // tm
    row = (1, tm, d)

    def pcol(col):
        return pl.BlockSpec(row, lambda b, i: (b, off + i, col // d))

    def full(a):
        return pl.BlockSpec(a.shape, lambda b, i: (0,) * a.ndim)

    tok = nb * n_lat
    return pl.pallas_call(
        _merge_kernel,
        grid=(nb, nt),
        in_specs=[pl.BlockSpec(row, lambda b, i: (b, off + i, 0)),
                  pl.BlockSpec(row, lambda b, i: (b, off + i, 0)),
                  pcol(COL_HG), pcol(COL_GH), pcol(COL_GA),
                  pl.BlockSpec(row, lambda b, i: (b, i, 0)),
                  pl.BlockSpec(row, lambda b, i: (b, i, 0)),
                  full(mod), full(norm_g), full(hgrn_norm_g), full(wb0), full(wb1), full(wo),
                  full(router_wt), full(router_b)],
        out_specs=[pl.BlockSpec(row, lambda b, i: (b, i, 0)),
                   pl.BlockSpec((tm, d), lambda b, i: (b * nt + i, 0)),
                   pl.BlockSpec((TOP_K, tm), lambda b, i: (0, b * nt + i)),
                   pl.BlockSpec((TOP_K, tm), lambda b, i: (0, b * nt + i))],
        out_shape=[jax.ShapeDtypeStruct((nb, n_lat, d), F32),
                   jax.ShapeDtypeStruct((tok, d), F32),
                   jax.ShapeDtypeStruct((TOP_K, tok), I32),
                   jax.ShapeDtypeStruct((TOP_K, tok), F32)],
        compiler_params=pltpu.CompilerParams(
            dimension_semantics=("arbitrary", "arbitrary"),
            vmem_limit_bytes=_vmem_limit(40 * 1024 * 1024)),
        name="merge_router",
    )(o_f, o_b, p, p, p, o_a, x, mod, norm_g, hgrn_norm_g, wb0, wb1, wo, router_wt, router_b)


def _rank_kernel(ids_ref, rank_ref, cnt_ref, *, tt):
    eidx = lax.broadcasted_iota(I32, (N_EXPERTS, tt), 0)
    si = lax.broadcasted_iota(I32, (tt, tt), 0)
    ti = lax.broadcasted_iota(I32, (tt, tt), 1)
    before = jnp.where(si < ti, 1.0, 0.0).astype(BF16)
    seen = jnp.zeros((N_EXPERTS, 1), F32)
    for j in range(TOP_K):
        onehot = eidx == ids_ref[j:j + 1, :]
        oh = jnp.where(onehot, 1.0, 0.0)
        earlier = _dot(oh.astype(BF16), before)
        rank = jnp.sum(jnp.where(onehot, seen + earlier, 0.0), axis=0, keepdims=True)
        rank_ref[j:j + 1, :] = rank.astype(I32)
        seen = seen + jnp.sum(oh, axis=1, keepdims=True)
    cnt_ref[0] = seen.astype(I32)


def _expert_rank(ids, tt):
    _, tok = ids.shape
    return pl.pallas_call(
        functools.partial(_rank_kernel, tt=tt),
        grid=(tok // tt,),
        in_specs=[pl.BlockSpec((TOP_K, tt), lambda i: (0, i))],
        out_specs=[pl.BlockSpec((TOP_K, tt), lambda i: (0, i)),
                   pl.BlockSpec((1, N_EXPERTS, 1), lambda i: (i, 0, 0))],
        out_shape=[jax.ShapeDtypeStruct((TOP_K, tok), I32),
                   jax.ShapeDtypeStruct((tok // tt, N_EXPERTS, 1), I32)],
        compiler_params=pltpu.CompilerParams(dimension_semantics=("arbitrary",)),
        name="expert_rank",
    )(ids)


def _block_copies(n_blocks, make_copy, make_group_copy):
    def start(b, carry):
        make_copy(b).start()
        return carry

    def wait_group(g, carry):
        make_group_copy().wait()
        return carry

    def wait_one(b, carry):
        make_copy(0).wait()
        return carry

    lax.fori_loop(0, n_blocks, start, 0)
    lax.fori_loop(0, lax.shift_right_logical(n_blocks, WAIT_GROUP_LOG2), wait_group, 0)
    lax.fori_loop(0, jnp.bitwise_and(n_blocks, (1 << WAIT_GROUP_LOG2) - 1), wait_one, 0)


def _scatter_kernel(zt_ref, zvalid_ref, nblk_ref, bdst_ref, lp_ref, t_ref, xs_ref, stage, zero_buf, sem, zsem,
                    *, tt, n_stage):
    @pl.when(pl.program_id(0) == 0)
    def _():
        zero_buf[...] = jnp.zeros(zero_buf.shape, F32)

        def zero_copy(k):
            return pltpu.make_async_copy(zero_buf, xs_ref.at[pl.ds(zt_ref[k] * MOE_TILE, MOE_TILE)], zsem)

        def start(k, carry):
            @pl.when(zvalid_ref[k] == 1)
            def _():
                zero_copy(k).start()
            return carry

        def wait(k, carry):
            @pl.when(zvalid_ref[k] == 1)
            def _():
                zero_copy(k).wait()
            return carry

        lax.fori_loop(0, zt_ref.shape[0], start, 0)
        lax.fori_loop(0, zt_ref.shape[0], wait, 0)

    t_bf = t_ref[...].astype(BF16)
    lps = [lp_ref[j:j + 1, :] for j in range(TOP_K)]
    for rc in range(n_stage // PERM_ROWS):
        row = rc * PERM_ROWS + lax.broadcasted_iota(I32, (PERM_ROWS, tt), 0)
        sel = jnp.where(row == lps[TOP_K - 1], 1.0, 0.0)
        for j in range(TOP_K - 2, -1, -1):
            sel = jnp.where(row == lps[j], 1.0, sel)
        stage[rc * PERM_ROWS:(rc + 1) * PERM_ROWS, :] = _dot(sel.astype(BF16), t_bf)

    def copy(b):
        src = pl.multiple_of(b * RUN_ALIGN, RUN_ALIGN)
        dst = pl.multiple_of(bdst_ref[0, 0, b], RUN_ALIGN)
        return pltpu.make_async_copy(stage.at[pl.ds(src, RUN_ALIGN)], xs_ref.at[pl.ds(dst, RUN_ALIGN)], sem)

    def group_copy():
        rows = RUN_ALIGN << WAIT_GROUP_LOG2
        return pltpu.make_async_copy(stage.at[pl.ds(0, rows)], xs_ref.at[pl.ds(0, rows)], sem)

    _block_copies(nblk_ref[pl.program_id(0)], copy, group_copy)


def _scatter_rows(zero_tiles, zero_valid, n_blocks, block_rows, lp, t, n_rows, tt, n_stage):
    tok, d = t.shape
    nb = block_rows.shape[2]
    return pl.pallas_call(
        functools.partial(_scatter_kernel, tt=tt, n_stage=n_stage),
        grid_spec=pltpu.PrefetchScalarGridSpec(
            num_scalar_prefetch=3,
            grid=(tok // tt,),
            in_specs=[pl.BlockSpec((1, 1, nb), lambda i, *_: (i, 0, 0), memory_space=pltpu.SMEM),
                      pl.BlockSpec((TOP_K, tt), lambda i, *_: (0, i)),
                      pl.BlockSpec((tt, d), lambda i, *_: (i, 0))],
            out_specs=pl.BlockSpec(memory_space=pl.ANY),
            scratch_shapes=[pltpu.VMEM((n_stage, d), F32), pltpu.VMEM((MOE_TILE, d), F32),
                            pltpu.SemaphoreType.DMA(()), pltpu.SemaphoreType.DMA(())]),
        out_shape=jax.ShapeDtypeStruct((n_rows, d), F32),
        compiler_params=pltpu.CompilerParams(
            dimension_semantics=("arbitrary",), vmem_limit_bytes=_vmem_limit(48 * 1024 * 1024)),
        name="moe_scatter",
    )(zero_tiles, zero_valid, n_blocks, block_rows, lp, t)


def _expert_kernel(te_ref, first_ref, nused_ref, xs_ref, wu_ref, bu_ref, wd_ref, bd_ref, ys_ref,
                   wu_bf, wd_bf):
    i = pl.program_id(0)

    @pl.when(i >= nused_ref[0])
    def _():
        ys_ref[...] = jnp.zeros(ys_ref.shape, F32)

    @pl.when(i < nused_ref[0])
    def _():
        @pl.when(first_ref[i] == 1)
        def _():
            wu_bf[...] = wu_ref[0].astype(BF16)
            wd_bf[...] = wd_ref[0].astype(BF16)

        h = _dot(xs_ref[...].astype(BF16), wu_bf[...]) + bu_ref[0]
        glu = jnp.minimum(h[:, :D_EXPERT], SWIGLU_LIMIT)
        lin = jnp.clip(h[:, D_EXPERT:], -SWIGLU_LIMIT, SWIGLU_LIMIT)
        a = glu * jax.nn.sigmoid(SWIGLU_ALPHA * glu) * (lin + 1.0)
        ys_ref[...] = _dot(a.astype(BF16), wd_bf[...]) + bd_ref[0]


def _experts(xs, tile_expert, tile_first, n_used, w_up, b_up, w_down, b_down):
    n_rows, d = xs.shape
    tm = MOE_TILE
    n_tiles = n_rows // tm

    def tile(i, te, first, nused):
        return (jnp.minimum(i, nused[0] - 1), 0)

    def wsel(i, te, first, nused):
        return (te[i], 0, 0)

    vmem = 2 * (d * 2 * D_EXPERT * 4 + D_EXPERT * d * 4) + d * 2 * D_EXPERT * 2 + D_EXPERT * d * 2 \
        + 4 * tm * d * 4 + 6 * tm * 2 * D_EXPERT * 4
    return pl.pallas_call(
        _expert_kernel,
        grid_spec=pltpu.PrefetchScalarGridSpec(
            num_scalar_prefetch=3,
            grid=(n_tiles,),
            in_specs=[pl.BlockSpec((tm, d), tile),
                      pl.BlockSpec((1, d, 2 * D_EXPERT), wsel),
                      pl.BlockSpec((1, 1, 2 * D_EXPERT), wsel),
                      pl.BlockSpec((1, D_EXPERT, d), wsel),
                      pl.BlockSpec((1, 1, d), wsel)],
            out_specs=pl.BlockSpec((tm, d), lambda i, te, first, nused: (i, 0)),
            scratch_shapes=[pltpu.VMEM((d, 2 * D_EXPERT), BF16), pltpu.VMEM((D_EXPERT, d), BF16)]),
        out_shape=jax.ShapeDtypeStruct((n_rows, d), F32),
        compiler_params=pltpu.CompilerParams(
            dimension_semantics=("arbitrary",), vmem_limit_bytes=_vmem_limit(vmem)),
        name="moe_experts",
    )(tile_expert, tile_first, n_used, xs, w_up, b_up.reshape(N_EXPERTS, 1, -1), w_down,
      b_down.reshape(N_EXPERTS, 1, -1))


def _combine_kernel(nblk_ref, bsrc_ref, ys_ref, lp_ref, w_ref, x1_ref, mod_ref, ng_ref, o_ref, stage, sem,
                    *, tt, n_stage, tiles_per_batch):
    i = pl.program_id(0)

    @pl.when(i == 0)
    def _():
        stage[...] = jnp.zeros(stage.shape, F32)

    def copy(b):
        src = pl.multiple_of(bsrc_ref[0, 0, b], RUN_ALIGN)
        dst = pl.multiple_of(b * RUN_ALIGN, RUN_ALIGN)
        return pltpu.make_async_copy(ys_ref.at[pl.ds(src, RUN_ALIGN)], stage.at[pl.ds(dst, RUN_ALIGN)], sem)

    def group_copy():
        rows = RUN_ALIGN << WAIT_GROUP_LOG2
        return pltpu.make_async_copy(ys_ref.at[pl.ds(0, rows)], stage.at[pl.ds(0, rows)], sem)

    _block_copies(nblk_ref[i], copy, group_copy)

    lp = [jnp.broadcast_to(lp_ref[:, j:j + 1], (tt, PERM_ROWS)) for j in range(TOP_K)]
    w = [jnp.broadcast_to(w_ref[:, j:j + 1], (tt, PERM_ROWS)) for j in range(TOP_K)]
    f = jnp.zeros((tt, D_MODEL), F32)
    for lc in range(n_stage // PERM_ROWS):
        col = lc * PERM_ROWS + lax.broadcasted_iota(I32, (tt, PERM_ROWS), 1)
        wp = jnp.where(col == lp[TOP_K - 1], w[TOP_K - 1], 0.0)
        for j in range(TOP_K - 2, -1, -1):
            wp = jnp.where(col == lp[j], w[j], wp)
        f = f + _dot(wp.astype(BF16), stage[lc * PERM_ROWS:(lc + 1) * PERM_ROWS, :].astype(BF16))
    b = i // tiles_per_batch
    gate_f = mod_ref[pl.ds(b, 1), 5 * D_MODEL:6 * D_MODEL]
    o_ref[...] = x1_ref[...] + gate_f * _rms(f, ng_ref[3:4, :])


def _combine(n_blocks, block_rows, ys, lp_t, wts_t, x1, mod, norm_g, tt, n_stage, tiles_per_batch):
    tok, d = x1.shape
    nb = block_rows.shape[2]
    kern = functools.partial(_combine_kernel, tt=tt, n_stage=n_stage, tiles_per_batch=tiles_per_batch)
    return pl.pallas_call(
        kern,
        grid_spec=pltpu.PrefetchScalarGridSpec(
            num_scalar_prefetch=1,
            grid=(tok // tt,),
            in_specs=[pl.BlockSpec((1, 1, nb), lambda i, *_: (i, 0, 0), memory_space=pltpu.SMEM),
                      pl.BlockSpec(memory_space=pl.ANY),
                      pl.BlockSpec((tt, TOP_K), lambda i, *_: (i, 0)),
                      pl.BlockSpec((tt, TOP_K), lambda i, *_: (i, 0)),
                      pl.BlockSpec((tt, d), lambda i, *_: (i, 0)),
                      pl.BlockSpec(mod.shape, lambda i, *_: (0, 0)),
                      pl.BlockSpec(norm_g.shape, lambda i, *_: (0, 0))],
            out_specs=pl.BlockSpec((tt, d), lambda i, *_: (i, 0)),
            scratch_shapes=[pltpu.VMEM((n_stage, d), F32), pltpu.SemaphoreType.DMA(())]),
        out_shape=jax.ShapeDtypeStruct((tok, d), F32),
        compiler_params=pltpu.CompilerParams(
            dimension_semantics=("arbitrary",), vmem_limit_bytes=_vmem_limit(48 * 1024 * 1024)),
        name="moe_combine",
    )(n_blocks, block_rows, ys, lp_t, wts_t, x1, mod, norm_g)


def _rope_tables(n_ctx, n_lat):
    inv = ROPE_THETA ** (-np.arange(0, ROPE_AXIS_DIM, 2, dtype=np.float32) / ROPE_AXIS_DIM)
    idx = jnp.arange(n_lat, dtype=I32)
    rows = (idx // GRID_W).astype(F32)
    cols = (idx % GRID_W).astype(F32)
    half = ROPE_AXIS_DIM // 2
    inv = jnp.asarray(inv, F32)
    ang_r = rows[:, None] * inv[None, :]
    ang_c = cols[:, None] * inv[None, :]
    ang = jnp.concatenate([ang_r, ang_r, ang_c, ang_c], axis=1)
    cos, sin = jnp.cos(ang), jnp.sin(ang)
    lane = np.arange(HEAD_DIM)
    first = jnp.asarray((lane % ROPE_AXIS_DIM) < half)
    sin_a = jnp.where(first, -sin, 0.0)
    sin_b = jnp.where(first, 0.0, sin)
    pad1 = jnp.ones((n_ctx, HEAD_DIM), F32)
    pad0 = jnp.zeros((n_ctx, HEAD_DIM), F32)
    return (jnp.concatenate([pad1, cos], axis=0), jnp.concatenate([pad0, sin_a], axis=0),
            jnp.concatenate([pad0, sin_b], axis=0))


def kernel(x, c, ctx, c_ctx, w_mod, b_mod, norm_g, w_in, hgrn_lb, hgrn_norm_g, qk_norm_g, w_branch, w_out,
           router_w, router_b, w_up, b_up, w_down, b_down):
    nb, n_lat, d = x.shape
    n_ctx = ctx.shape[1]
    assert d == D_MODEL and w_mod.shape[0] == 1, "single-layer kernel"
    tok = nb * n_lat

    cv = jnp.zeros((8, d), F32).at[:nb].set(c).at[nb].set(c_ctx)
    mod = _modulation(cv, w_mod[0], b_mod[0])

    xc = jnp.concatenate([ctx, x], axis=1)
    sizes = np.cumsum([0, HGRN_W, HGRN_W, HGRN_W, HGRN_W, HGRN_W, ATT_W, KV_W, KV_W, D_MODEL, D_MODEL])
    order = [0, 1, 2, 3, 4, 5, 8, 9, 6, 7]
    w_in_bf = jnp.concatenate([w_in[0][:, sizes[i]:sizes[i + 1]] for i in order], axis=1).astype(BF16)
    p = _in_projection(xc, mod, norm_g[0], w_in_bf, n_ctx)

    cos, sin_a, sin_b = _rope_tables(n_ctx, n_lat)
    qt, kn, vt = _qk_prep(p, cos, sin_a, sin_b, qk_norm_g[0])
    o_a = _attention(qt, kn, vt, n_ctx)
    o_f, o_b = _hgrn_scan(p, hgrn_lb, n_ctx)

    x1, t, ids, wts = _merge(
        o_f, o_b, p, o_a, x, mod, norm_g[0], hgrn_norm_g, w_branch[0, 0].astype(BF16),
        w_branch[0, 1].astype(BF16), w_out[0].astype(BF16), router_w[0].T, router_b[0].reshape(N_EXPERTS, 1),
        n_ctx)

    tt = _largest_tile(n_lat, DISPATCH_TILE, 128)
    n_tt = tok // tt
    rank, counts = _expert_rank(ids, tt)
    counts = counts[:, :, 0]
    run = (counts + RUN_ALIGN - 1) // RUN_ALIGN * RUN_ALIGN
    run_end = jnp.cumsum(run, axis=1)
    stage_off = run_end - run
    expert_rows = jnp.sum(run, axis=0)
    tiles_e = (expert_rows + MOE_TILE - 1) // MOE_TILE
    tile_end = jnp.cumsum(tiles_e)
    starts = (tile_end - tiles_e) * MOE_TILE
    run_dst = starts[None, :] + jnp.cumsum(run, axis=0) - run
    n_tiles = -(-(tok * TOP_K + N_EXPERTS * n_tt * (RUN_ALIGN - 1)) // MOE_TILE) + N_EXPERTS
    tile_idx = jnp.arange(n_tiles, dtype=I32)
    tile_expert = jnp.sum((tile_end[None, :] <= tile_idx[:, None]).astype(I32), axis=1)
    tile_expert = jnp.minimum(tile_expert, N_EXPERTS - 1)
    n_used = tile_end[-1:].astype(I32)
    tile_expert = jnp.where(tile_idx < n_used[0], tile_expert, tile_expert[jnp.maximum(n_used[0] - 1, 0)])
    tile_first = jnp.concatenate([jnp.ones((1,), I32), (tile_expert[1:] != tile_expert[:-1]).astype(I32)])

    n_stage = -(-(TOP_K * tt + N_EXPERTS * (RUN_ALIGN - 1)) // PERM_ROWS) * PERM_ROWS
    ids_t = ids.reshape(TOP_K, n_tt, tt)
    onehot = ids_t[..., None] == jnp.arange(N_EXPERTS, dtype=I32)
    lp = jnp.sum(jnp.where(onehot, stage_off[None, :, None, :], 0), axis=-1).reshape(TOP_K, tok) + rank
    blk_row = jnp.arange(n_stage // RUN_ALIGN, dtype=I32) * RUN_ALIGN
    blk_expert = jnp.minimum(jnp.sum((run_end[:, None, :] <= blk_row[None, :, None]).astype(I32), axis=-1),
                             N_EXPERTS - 1)
    blk_onehot = blk_expert[..., None] == jnp.arange(N_EXPERTS, dtype=I32)
    blk_dst = jnp.sum(jnp.where(blk_onehot, (run_dst - stage_off)[:, None, :], 0), axis=-1) + blk_row[None, :]
    n_blocks = (run_end[:, -1] // RUN_ALIGN).astype(I32)
    blk_dst = jnp.where(blk_row[None, :] < run_end[:, -1:], blk_dst, 0).astype(I32)[:, None, :]

    n_tail = n_tiles - (tok * TOP_K) // MOE_TILE
    tail = n_used[0] + jnp.arange(n_tail, dtype=I32)
    zero_tiles = jnp.clip(jnp.concatenate([tile_end.astype(I32) - 1, tail]), 0, n_tiles - 1)
    zero_valid = jnp.concatenate([expert_rows % MOE_TILE != 0, tail < n_tiles]).astype(I32)

    xs = _scatter_rows(zero_tiles, zero_valid, n_blocks, blk_dst, lp, t, n_tiles * MOE_TILE, tt, n_stage)
    ys = _experts(xs, tile_expert, tile_first, n_used, w_up[0], b_up[0], w_down[0], b_down[0])
    out = _combine(n_blocks, blk_dst, ys, lp.T, wts.T, x1.reshape(tok, d), mod, norm_g[0], tt, n_stage,
                   n_lat // tt)
    return out.reshape(nb, n_lat, d)
```

```python
import functools

import jax
import jax.numpy as jnp
import numpy as np
from jax import lax
from jax.experimental import pallas as pl
from jax.experimental.pallas import tpu as pltpu

F32 = jnp.float32
BF16 = jnp.bfloat16
I32 = jnp.int32

D_MODEL = 1024
GRID_W = 64
N_MOD = 6
EPS = 1e-6
HGRN_HEADS = 8
HGRN_DK = 128
HGRN_W = HGRN_HEADS * HGRN_DK
ATT_HEADS = 8
ATT_KV_HEADS = 2
ATT_GROUPS = ATT_HEADS // ATT_KV_HEADS
HEAD_DIM = 128
ATT_W = ATT_HEADS * HEAD_DIM
KV_W = ATT_KV_HEADS * HEAD_DIM
ROPE_AXIS_DIM = HEAD_DIM // 2
ROPE_THETA = 10000.0
N_EXPERTS = 32
TOP_K = 4
D_EXPERT = 1024
SWIGLU_LIMIT = 7.0
SWIGLU_ALPHA = 1.702

COL_HQ, COL_HI, COL_FF, COL_FB, COL_HG, COL_AQ, COL_GH, COL_GA = (i * 1024 for i in range(8))
COL_K = 8 * 1024
COL_V = COL_K + KV_W
IN_COLS = COL_V + KV_W

V7X_VMEM_BYTES = 64 * 1024 * 1024
LANES = 128

HGRN_CHUNK = 64
HGRN_SUB = 16
MOE_TILE = 256
DISPATCH_TILE = 512
RUN_ALIGN = 8
PERM_ROWS = 256
WAIT_GROUP_LOG2 = 5


def _vmem_limit(nbytes):
    return int(min(V7X_VMEM_BYTES - 6 * 1024 * 1024, max(nbytes, 32 * 1024 * 1024)))


def _largest_tile(n, cap, mult):
    best = None
    for t in range(mult, min(n, cap) + 1, mult):
        if n % t == 0:
            best = t
    assert best is not None, (n, cap, mult)
    return best


def _rms(x, g):
    return x * lax.rsqrt(jnp.mean(x * x, axis=-1, keepdims=True) + EPS) * g


def _dot(a, b):
    return jnp.dot(a, b, preferred_element_type=F32)


def _dot_nt(a, b):
    return lax.dot_general(a, b, (((1,), (1,)), ((), ())), preferred_element_type=F32)


def _dot_tn(a, b):
    return lax.dot_general(a, b, (((0,), (0,)), ((), ())), preferred_element_type=F32)


def _mod_kernel(cv_ref, w_ref, b_ref, o_ref):
    cv = cv_ref[...]
    s = cv * jax.nn.sigmoid(cv)
    o_ref[...] = jnp.dot(s, w_ref[...], preferred_element_type=F32,
                         precision=lax.Precision.HIGHEST) + b_ref[...]


def _modulation(cv, w_mod, b_mod):
    rows, d = cv.shape
    n = w_mod.shape[1]
    tn = 1024
    return pl.pallas_call(
        _mod_kernel,
        grid=(n // tn,),
        in_specs=[pl.BlockSpec((rows, d), lambda j: (0, 0)),
                  pl.BlockSpec((d, tn), lambda j: (0, j)),
                  pl.BlockSpec((1, tn), lambda j: (0, j))],
        out_specs=pl.BlockSpec((rows, tn), lambda j: (0, j)),
        out_shape=jax.ShapeDtypeStruct((rows, n), F32),
        name="modulation",
    )(cv, w_mod, b_mod.reshape(1, n))


def _inproj_kernel(x_ref, mod_ref, g_ref, w_ref, o_ref, *, tm, n_ctx, n_batch):
    b = pl.program_id(1)
    i = pl.program_id(2)
    x = x_ref[0]
    xn = _rms(x, g_ref[0:1, :])
    row = i * tm + lax.broadcasted_iota(I32, (tm, 1), 0)
    is_ctx = row < n_ctx
    m_lat = mod_ref[pl.ds(b, 1), :]
    m_ctx = mod_ref[n_batch:n_batch + 1, :]
    shift = jnp.where(is_ctx, m_ctx[:, 0:D_MODEL], m_lat[:, 0:D_MODEL])
    scale = jnp.where(is_ctx, m_ctx[:, D_MODEL:2 * D_MODEL], m_lat[:, D_MODEL:2 * D_MODEL])
    u = xn * (1.0 + scale) + shift
    o_ref[0] = _dot(u.astype(BF16), w_ref[...]).astype(BF16)


def _in_projection(xc, mod, norm_g, w_in_bf, n_ctx):
    nb, lc, d = xc.shape
    n = w_in_bf.shape[1]
    tm = _largest_tile(lc, 1056, 16)
    tn = 2176
    assert n % tn == 0
    kern = functools.partial(_inproj_kernel, tm=tm, n_ctx=n_ctx, n_batch=nb)
    vmem = 2 * (tm * d * 4 + d * tn * 2 + tm * tn * 2) + 6 * tm * d * 4
    return pl.pallas_call(
        kern,
        grid=(n // tn, nb, lc // tm),
        in_specs=[pl.BlockSpec((1, tm, d), lambda j, b, i: (b, i, 0)),
                  pl.BlockSpec(mod.shape, lambda j, b, i: (0, 0)),
                  pl.BlockSpec(norm_g.shape, lambda j, b, i: (0, 0)),
                  pl.BlockSpec((d, tn), lambda j, b, i: (0, j))],
        out_specs=pl.BlockSpec((1, tm, tn), lambda j, b, i: (b, i, j)),
        out_shape=jax.ShapeDtypeStruct((nb, lc, n), BF16),
        compiler_params=pltpu.CompilerParams(
            dimension_semantics=("arbitrary", "arbitrary", "arbitrary"),
            vmem_limit_bytes=_vmem_limit(vmem)),
        name="in_projection",
    )(xc, mod, norm_g, w_in_bf)


def _rope(xn, cos, sin_a, sin_b):
    return (xn * cos + pltpu.roll(xn, HEAD_DIM - ROPE_AXIS_DIM // 2, 1) * sin_a
            + pltpu.roll(xn, ROPE_AXIS_DIM // 2, 1) * sin_b)


def _qkprep_kernel(q_ref, k_ref, v_ref, cos_ref, sa_ref, sb_ref, g_ref, qt_ref, ko_ref, vt_ref):
    cos, sa, sb = cos_ref[...], sa_ref[...], sb_ref[...]
    gq, gk = g_ref[0:1, :], g_ref[1:2, :]
    qscale = HEAD_DIM ** -0.5 * np.log2(np.e)
    for h in range(ATT_HEADS):
        hs = slice(h * HEAD_DIM, (h + 1) * HEAD_DIM)
        xn = _rms(q_ref[0, :, hs].astype(F32), gq)
        qt_ref[0, hs, :] = (_rope(xn, cos, sa, sb) * qscale).T.astype(BF16)
    for h in range(ATT_KV_HEADS):
        hs = slice(h * HEAD_DIM, (h + 1) * HEAD_DIM)
        xn = _rms(k_ref[0, :, hs].astype(F32), gk)
        ko_ref[0, :, hs] = _rope(xn, cos, sa, sb).astype(BF16)
        vt_ref[0, hs, :] = v_ref[0, :, hs].astype(F32).T.astype(BF16)


def _qk_prep(p, cos, sin_a, sin_b, qk_norm_g):
    nb, lc, _ = p.shape
    tm = _largest_tile(lc, 512, 128)
    tab = pl.BlockSpec((tm, HEAD_DIM), lambda b, i: (i, 0))
    return pl.pallas_call(
        _qkprep_kernel,
        grid=(nb, lc // tm),
        in_specs=[pl.BlockSpec((1, tm, ATT_W), lambda b, i: (b, i, COL_AQ // ATT_W)),
                  pl.BlockSpec((1, tm, KV_W), lambda b, i: (b, i, COL_K // KV_W)),
                  pl.BlockSpec((1, tm, KV_W), lambda b, i: (b, i, COL_V // KV_W)),
                  tab, tab, tab,
                  pl.BlockSpec(qk_norm_g.shape, lambda b, i: (0, 0))],
        out_specs=[pl.BlockSpec((1, ATT_W, tm), lambda b, i: (b, 0, i)),
                   pl.BlockSpec((1, tm, KV_W), lambda b, i: (b, i, 0)),
                   pl.BlockSpec((1, KV_W, tm), lambda b, i: (b, 0, i))],
        out_shape=[jax.ShapeDtypeStruct((nb, ATT_W, lc), BF16),
                   jax.ShapeDtypeStruct((nb, lc, KV_W), BF16),
                   jax.ShapeDtypeStruct((nb, KV_W, lc), BF16)],
        name="qk_prep",
    )(p, p, p, cos, sin_a, sin_b, qk_norm_g)


def _attn_kernel(qt_ref, k_ref, vt_ref, o_ref, acc_ref, s_ref, *, tq, kc, n_kc):
    acc_ref[...] = jnp.zeros(acc_ref.shape, F32)
    qts = [qt_ref[0, g * HEAD_DIM:(g + 1) * HEAD_DIM, :] for g in range(ATT_GROUPS)]

    def put_scores(c, slot):
        kch = k_ref[0, pl.ds(pl.multiple_of(c * kc, kc), kc), :]
        for g in range(ATT_GROUPS):
            s_ref[slot, g] = _dot(kch, qts[g])

    def softmax_pv(c, slot, carry):
        vch = vt_ref[0, :, pl.ds(pl.multiple_of(c * kc, kc), kc)]
        out = []
        for g in range(ATT_GROUPS):
            m_old, l_old = carry[g]
            s = s_ref[slot, g]
            m_new = jnp.maximum(m_old, jnp.max(s, axis=0, keepdims=True))
            alpha = jnp.exp2(m_old - m_new)
            p = jnp.exp2(s - m_new)
            l_new = l_old * alpha + jnp.sum(p, axis=0, keepdims=True)
            acc_ref[g] = acc_ref[g] * alpha + _dot(vch, p.astype(BF16))
            out.append((m_new, l_new))
        return tuple(out)

    def body(i, carry):
        c = 2 * i
        put_scores(c + 1, 1)
        carry = softmax_pv(c, 0, carry)
        put_scores(c + 2, 0)
        return softmax_pv(c + 1, 1, carry)

    init = tuple((jnp.full((1, tq), -jnp.inf, F32), jnp.zeros((1, tq), F32)) for _ in range(ATT_GROUPS))
    put_scores(0, 0)
    fin = lax.fori_loop(0, (n_kc - 1) // 2, body, init)
    if n_kc % 2 == 0:
        put_scores(n_kc - 1, 1)
        fin = softmax_pv(n_kc - 2, 0, fin)
    fin = softmax_pv(n_kc - 1, (n_kc - 1) % 2, fin)
    for g in range(ATT_GROUPS):
        o_ref[0, :, g * HEAD_DIM:(g + 1) * HEAD_DIM] = (acc_ref[g] / fin[g][1]).T.astype(BF16)


def _attention(qt, kn, vt, n_ctx):
    nb, lc, _ = kn.shape
    n_lat = lc - n_ctx
    tq = _largest_tile(n_lat, 256, 128)
    assert n_ctx % tq == 0
    kc = _largest_tile(lc, 768, 256)
    gw = ATT_GROUPS * HEAD_DIM
    kern = functools.partial(_attn_kernel, tq=tq, kc=kc, n_kc=lc // kc)
    vmem = 8 * lc * HEAD_DIM * 2 + 4 * ATT_GROUPS * kc * tq * 4 + 8 * gw * tq * 4
    return pl.pallas_call(
        kern,
        grid=(nb, ATT_KV_HEADS, n_lat // tq),
        in_specs=[pl.BlockSpec((1, gw, tq), lambda b, h, i: (b, h, n_ctx // tq + i)),
                  pl.BlockSpec((1, lc, HEAD_DIM), lambda b, h, i: (b, 0, h)),
                  pl.BlockSpec((1, HEAD_DIM, lc), lambda b, h, i: (b, h, 0))],
        out_specs=pl.BlockSpec((1, tq, gw), lambda b, h, i: (b, i, h)),
        out_shape=jax.ShapeDtypeStruct((nb, n_lat, ATT_W), BF16),
        scratch_shapes=[pltpu.VMEM((ATT_GROUPS, HEAD_DIM, tq), F32),
                        pltpu.VMEM((2, ATT_GROUPS, kc, tq), F32)],
        compiler_params=pltpu.CompilerParams(
            dimension_semantics=("arbitrary", "arbitrary", "arbitrary"),
            vmem_limit_bytes=_vmem_limit(vmem)),
        name="attention",
    )(qt, kn, vt)


def _hgrn_chunk(q_ref, v_ref, r_ref, lb, reverse):
    c = HGRN_CHUNK
    q = q_ref[0].astype(F32)
    v = v_ref[0]
    r = r_ref[0].astype(F32)
    sig = jax.nn.sigmoid(r)
    f = lb + (1.0 - lb) * sig
    logf = jnp.log(f)
    k = (1.0 - lb) * (1.0 - sig)
    ti = lax.broadcasted_iota(I32, (c, c), 0)
    si = lax.broadcasted_iota(I32, (c, c), 1)
    tri = jnp.where((si >= ti) if reverse else (si <= ti), 1.0, 0.0).astype(BF16)
    hi = logf.astype(BF16)
    lo = (logf - hi.astype(F32)).astype(BF16)
    bcum = _dot(tri, hi) + _dot(tri, lo)
    b_end = bcum[0:1, :] if reverse else bcum[c - 1:c, :]
    qt = (q * jnp.exp(bcum)).astype(BF16)
    kt = (k * jnp.exp(b_end - bcum)).astype(BF16)
    dec = jnp.exp(b_end)

    nsub = c // HGRN_SUB
    qp, kp, cols = [], [], []
    for blk in range(nsub):
        rs = slice(blk * HGRN_SUB, (blk + 1) * HGRN_SUB)
        cs = slice(blk * HGRN_SUB, c) if reverse else slice(0, (blk + 1) * HGRN_SUB)
        mid = blk * HGRN_SUB + HGRN_SUB // 2
        ref = bcum[mid:mid + 1, :]
        qp.append((q[rs] * jnp.exp(bcum[rs] - ref)).astype(BF16))
        kp.append((k[cs] * jnp.exp(ref - bcum[cs])).astype(BF16))
        cols.append(cs)

    keep = []
    for blk in range(nsub):
        n_cols = cols[blk].stop - cols[blk].start
        trow = blk * HGRN_SUB + lax.broadcasted_iota(I32, (HGRN_SUB, n_cols), 0)
        scol = cols[blk].start + lax.broadcasted_iota(I32, (HGRN_SUB, n_cols), 1)
        keep.append((scol >= trow) if reverse else (scol <= trow))
    return dict(v=v, qt=qt, kt=kt, dec=dec, qp=qp, kp=kp, cols=cols, keep=keep)


def _hgrn_emit(dirs, s_refs, o_refs):
    nsub = HGRN_CHUNK // HGRN_SUB
    heads = [slice(h * HGRN_DK, (h + 1) * HGRN_DK) for h in range(HGRN_HEADS)]
    inter = []
    for d, s_ref in zip(dirs, s_refs):
        row = []
        for h, hs in enumerate(heads):
            st = s_ref[h]
            row.append(_dot_nt(d["qt"][:, hs], st.astype(BF16)))
            s_ref[h] = st * d["dec"][:, hs] + _dot_tn(d["v"][:, hs], d["kt"][:, hs])
        inter.append(row)
    scores = [[[_dot_nt(d["qp"][blk][:, hs], d["kp"][blk][:, hs]) for blk in range(nsub)] for hs in heads]
              for d in dirs]
    for d, sc, it, o_ref in zip(dirs, scores, inter, o_refs):
        outs = []
        for h, hs in enumerate(heads):
            parts = [_dot(jnp.where(d["keep"][blk], sc[h][blk], 0.0).astype(BF16), d["v"][d["cols"][blk], hs])
                     for blk in range(nsub)]
            outs.append(it[h] + jnp.concatenate(parts, axis=0))
        o_ref[0] = jnp.concatenate(outs, axis=1).astype(BF16)


def _hgrn_kernel(qf_ref, vf_ref, rf_ref, qb_ref, vb_ref, rb_ref, lb_ref, of_ref, ob_ref, sf_ref, sb_ref):
    @pl.when(pl.program_id(1) == 0)
    def _():
        sf_ref[...] = jnp.zeros(sf_ref.shape, F32)
        sb_ref[...] = jnp.zeros(sb_ref.shape, F32)

    n_layers = lb_ref.shape[0] // 2

    def lower_bound(direction):
        rows = [lb_ref[direction * n_layers + l:direction * n_layers + l + 1, :] for l in range(n_layers)]
        amax = functools.reduce(jnp.maximum, rows)
        e = [jnp.exp(a - amax) for a in rows]
        return e[0] / functools.reduce(lambda u, w: u + w, e)

    fwd = _hgrn_chunk(qf_ref, vf_ref, rf_ref, lower_bound(0), False)
    bwd = _hgrn_chunk(qb_ref, vb_ref, rb_ref, lower_bound(1), True)
    _hgrn_emit([fwd, bwd], [sf_ref, sb_ref], [of_ref, ob_ref])


def _hgrn_scan(p, hgrn_lb, n_ctx):
    nb, lc, _ = p.shape
    c = HGRN_CHUNK
    assert n_ctx % c == 0 and lc % c == 0
    n_chunks = lc // c
    ctx_chunks = n_ctx // c

    def fwd(col):
        return lambda b, s: (b, s, col)

    def bwd_chunk(s):
        return jnp.where(s < ctx_chunks, ctx_chunks - 1 - s, n_chunks - 1 + ctx_chunks - s)

    def bwd(col):
        return lambda b, s: (b, bwd_chunk(s), col)

    blk = (1, c, HGRN_W)
    lb2 = hgrn_lb.reshape(-1, HGRN_W)
    return pl.pallas_call(
        _hgrn_kernel,
        grid=(nb, n_chunks),
        in_specs=[pl.BlockSpec(blk, fwd(COL_HQ // HGRN_W)), pl.BlockSpec(blk, fwd(COL_HI // HGRN_W)),
                  pl.BlockSpec(blk, fwd(COL_FF // HGRN_W)),
                  pl.BlockSpec(blk, bwd(COL_HQ // HGRN_W)), pl.BlockSpec(blk, bwd(COL_HI // HGRN_W)),
                  pl.BlockSpec(blk, bwd(COL_FB // HGRN_W)),
                  pl.BlockSpec(lb2.shape, lambda b, s: (0, 0))],
        out_specs=[pl.BlockSpec(blk, fwd(0)), pl.BlockSpec(blk, bwd(0))],
        out_shape=[jax.ShapeDtypeStruct((nb, lc, HGRN_W), BF16)] * 2,
        scratch_shapes=[pltpu.VMEM((HGRN_HEADS, HGRN_DK, HGRN_DK), F32)] * 2,
        compiler_params=pltpu.CompilerParams(dimension_semantics=("arbitrary", "arbitrary")),
        name="hgrn_scan",
    )(p, p, p, p, p, p, lb2)


def _merge_kernel(of_ref, ob_ref, hg_ref, gh_ref, ga_ref, oa_ref, x_ref, mod_ref, ng_ref, hng_ref,
                  wb0_ref, wb1_ref, wo_ref, rw_ref, rb_ref,
                  x1_ref, t_ref, ids_ref, wts_ref):
    b = pl.program_id(0)
    m = mod_ref[pl.ds(b, 1), :]
    gate_mix = m[:, 2 * D_MODEL:3 * D_MODEL]
    shift_f = m[:, 3 * D_MODEL:4 * D_MODEL]
    scale_f = m[:, 4 * D_MODEL:5 * D_MODEL]

    o = of_ref[0].astype(F32) + ob_ref[0].astype(F32)
    hng = hng_ref[...]
    o_h = jnp.concatenate(
        [_rms(o[:, h * HGRN_DK:(h + 1) * HGRN_DK], hng) for h in range(HGRN_HEADS)], axis=1)
    g_raw = hg_ref[0].astype(F32)
    o_h = o_h * (g_raw * jax.nn.sigmoid(g_raw))
    y = (jax.nn.sigmoid(gh_ref[0].astype(F32)) * _dot(o_h.astype(BF16), wb0_ref[...])
         + jax.nn.sigmoid(ga_ref[0].astype(F32)) * _dot(oa_ref[0], wb1_ref[...]))
    y = _dot(y.astype(BF16), wo_ref[...])
    x1 = x_ref[0] + gate_mix * _rms(y, ng_ref[1:2, :])
    x1_ref[0] = x1
    t = _rms(x1, ng_ref[2:3, :]) * (1.0 + scale_f) + shift_f
    t_ref[...] = t

    t_hi = t.astype(BF16)
    t_lo = (t - t_hi.astype(F32)).astype(BF16)
    rw = rw_ref[...]
    rw_hi = rw.astype(BF16)
    rw_lo = (rw - rw_hi.astype(F32)).astype(BF16)
    logits = _dot_nt(rw_hi, t_hi) + _dot_nt(rw_hi, t_lo) + _dot_nt(rw_lo, t_hi) + rb_ref[...]
    eidx = lax.broadcasted_iota(I32, logits.shape, 0).astype(F32)
    vals = []
    for j in range(TOP_K):
        mx = jnp.max(logits, axis=0, keepdims=True)
        idx = jnp.min(jnp.where(logits == mx, eidx, float(N_EXPERTS)), axis=0, keepdims=True)
        ids_ref[j:j + 1, :] = idx.astype(I32)
        vals.append(mx)
        logits = jnp.where(eidx == idx, -jnp.inf, logits)
    ex = [jnp.exp(vj - vals[0]) for vj in vals]
    den = ex[0] + ex[1] + ex[2] + ex[3]
    for j in range(TOP_K):
        wts_ref[j:j + 1, :] = ex[j] / den


def _merge(o_f, o_b, p, o_a, x, mod, norm_g, hgrn_norm_g, wb0, wb1, wo, router_wt, router_b, n_ctx):
    nb, n_lat, d = x.shape
    tm = _largest_tile(n_lat, 256, 128)
    assert n_ctx % tm == 0
    off = n_ctx // tm
    nt = n_lat // tm
    row = (1, tm, d)

    def pcol(col):
        return pl.BlockSpec(row, lambda b, i: (b, off + i, col // d))

    def full(a):
        return pl.BlockSpec(a.shape, lambda b, i: (0,) * a.ndim)

    tok = nb * n_lat
    return pl.pallas_call(
        _merge_kernel,
        grid=(nb, nt),
        in_specs=[pl.BlockSpec(row, lambda b, i: (b, off + i, 0)),
                  pl.BlockSpec(row, lambda b, i: (b, off + i, 0)),
                  pcol(COL_HG), pcol(COL_GH), pcol(COL_GA),
                  pl.BlockSpec(row, lambda b, i: (b, i, 0)),
                  pl.BlockSpec(row, lambda b, i: (b, i, 0)),
                  full(mod), full(norm_g), full(hgrn_norm_g), full(wb0), full(wb1), full(wo),
                  full(router_wt), full(router_b)],
        out_specs=[pl.BlockSpec(row, lambda b, i: (b, i, 0)),
                   pl.BlockSpec((tm, d), lambda b, i: (b * nt + i, 0)),
                   pl.BlockSpec((TOP_K, tm), lambda b, i: (0, b * nt + i)),
                   pl.BlockSpec((TOP_K, tm), lambda b, i: (0, b * nt + i))],
        out_shape=[jax.ShapeDtypeStruct((nb, n_lat, d), F32),
                   jax.ShapeDtypeStruct((tok, d), F32),
                   jax.ShapeDtypeStruct((TOP_K, tok), I32),
                   jax.ShapeDtypeStruct((TOP_K, tok), F32)],
        compiler_params=pltpu.CompilerParams(
            dimension_semantics=("arbitrary", "arbitrary"),
            vmem_limit_bytes=_vmem_limit(40 * 1024 * 1024)),
        name="merge_router",
    )(o_f, o_b, p, p, p, o_a, x, mod, norm_g, hgrn_norm_g, wb0, wb1, wo, router_wt, router_b)


def _rank_kernel(ids_ref, rank_ref, cnt_ref, *, tt):
    eidx = lax.broadcasted_iota(I32, (N_EXPERTS, tt), 0)
    si = lax.broadcasted_iota(I32, (tt, tt), 0)
    ti = lax.broadcasted_iota(I32, (tt, tt), 1)
    before = jnp.where(si < ti, 1.0, 0.0).astype(BF16)
    seen = jnp.zeros((N_EXPERTS, 1), F32)
    for j in range(TOP_K):
        onehot = eidx == ids_ref[j:j + 1, :]
        oh = jnp.where(onehot, 1.0, 0.0)
        earlier = _dot(oh.astype(BF16), before)
        rank = jnp.sum(jnp.where(onehot, seen + earlier, 0.0), axis=0, keepdims=True)
        rank_ref[j:j + 1, :] = rank.astype(I32)
        seen = seen + jnp.sum(oh, axis=1, keepdims=True)
    cnt_ref[0] = seen.astype(I32)


def _expert_rank(ids, tt):
    _, tok = ids.shape
    return pl.pallas_call(
        functools.partial(_rank_kernel, tt=tt),
        grid=(tok // tt,),
        in_specs=[pl.BlockSpec((TOP_K, tt), lambda i: (0, i))],
        out_specs=[pl.BlockSpec((TOP_K, tt), lambda i: (0, i)),
                   pl.BlockSpec((1, N_EXPERTS, 1), lambda i: (i, 0, 0))],
        out_shape=[jax.ShapeDtypeStruct((TOP_K, tok), I32),
                   jax.ShapeDtypeStruct((tok // tt, N_EXPERTS, 1), I32)],
        compiler_params=pltpu.CompilerParams(dimension_semantics=("arbitrary",)),
        name="expert_rank",
    )(ids)


def _start_blocks(n_blocks, make_copy):
    def start(b, carry):
        make_copy(b).start()
        return carry

    lax.fori_loop(0, n_blocks, start, 0)


def _wait_blocks(n_blocks, make_copy, make_group_copy):
    def wait_group(g, carry):
        make_group_copy().wait()
        return carry

    def wait_one(b, carry):
        make_copy(0).wait()
        return carry

    lax.fori_loop(0, lax.shift_right_logical(n_blocks, WAIT_GROUP_LOG2), wait_group, 0)
    lax.fori_loop(0, jnp.bitwise_and(n_blocks, (1 << WAIT_GROUP_LOG2) - 1), wait_one, 0)


def _scatter_kernel(zt_ref, zvalid_ref, nblk_ref, bdst_ref, lp_ref, t_ref, xs_ref, stage, zero_buf, sem, zsem,
                    *, tt, n_stage):
    @pl.when(pl.program_id(0) == 0)
    def _():
        zero_buf[...] = jnp.zeros(zero_buf.shape, F32)

        def zero_copy(k):
            return pltpu.make_async_copy(zero_buf, xs_ref.at[pl.ds(zt_ref[k] * MOE_TILE, MOE_TILE)], zsem)

        def start(k, carry):
            @pl.when(zvalid_ref[k] == 1)
            def _():
                zero_copy(k).start()
            return carry

        def wait(k, carry):
            @pl.when(zvalid_ref[k] == 1)
            def _():
                zero_copy(k).wait()
            return carry

        lax.fori_loop(0, zt_ref.shape[0], start, 0)
        lax.fori_loop(0, zt_ref.shape[0], wait, 0)

    i = pl.program_id(0)
    slot = i % 2
    t_bf = t_ref[...].astype(BF16)
    lps = [lp_ref[j:j + 1, :] for j in range(TOP_K)]
    for rc in range(n_stage // PERM_ROWS):
        row = rc * PERM_ROWS + lax.broadcasted_iota(I32, (PERM_ROWS, tt), 0)
        sel = jnp.where(row == lps[TOP_K - 1], 1.0, 0.0)
        for j in range(TOP_K - 2, -1, -1):
            sel = jnp.where(row == lps[j], 1.0, sel)
        stage[slot, rc * PERM_ROWS:(rc + 1) * PERM_ROWS, :] = _dot(sel.astype(BF16), t_bf)

    def copy(b):
        src = pl.multiple_of(b * RUN_ALIGN, RUN_ALIGN)
        dst = pl.multiple_of(bdst_ref[0, 0, b], RUN_ALIGN)
        return pltpu.make_async_copy(stage.at[slot, pl.ds(src, RUN_ALIGN)], xs_ref.at[pl.ds(dst, RUN_ALIGN)],
                                     sem.at[slot])

    def waiter(s):
        rows = RUN_ALIGN << WAIT_GROUP_LOG2
        return (lambda b: pltpu.make_async_copy(stage.at[s, pl.ds(0, RUN_ALIGN)],
                                                xs_ref.at[pl.ds(0, RUN_ALIGN)], sem.at[s]),
                lambda: pltpu.make_async_copy(stage.at[s, pl.ds(0, rows)], xs_ref.at[pl.ds(0, rows)], sem.at[s]))

    _start_blocks(nblk_ref[i], copy)

    @pl.when(i > 0)
    def _():
        _wait_blocks(nblk_ref[i - 1], *waiter(1 - slot))

    @pl.when(i == pl.num_programs(0) - 1)
    def _():
        _wait_blocks(nblk_ref[i], *waiter(slot))


def _scatter_rows(zero_tiles, zero_valid, n_blocks, block_rows, lp, t, n_rows, tt, n_stage):
    tok, d = t.shape
    nb = block_rows.shape[2]
    return pl.pallas_call(
        functools.partial(_scatter_kernel, tt=tt, n_stage=n_stage),
        grid_spec=pltpu.PrefetchScalarGridSpec(
            num_scalar_prefetch=3,
            grid=(tok // tt,),
            in_specs=[pl.BlockSpec((1, 1, nb), lambda i, *_: (i, 0, 0), memory_space=pltpu.SMEM),
                      pl.BlockSpec((TOP_K, tt), lambda i, *_: (0, i)),
                      pl.BlockSpec((tt, d), lambda i, *_: (i, 0))],
            out_specs=pl.BlockSpec(memory_space=pl.ANY),
            scratch_shapes=[pltpu.VMEM((2, n_stage, d), F32), pltpu.VMEM((MOE_TILE, d), F32),
                            pltpu.SemaphoreType.DMA((2,)), pltpu.SemaphoreType.DMA(())]),
        out_shape=jax.ShapeDtypeStruct((n_rows, d), F32),
        compiler_params=pltpu.CompilerParams(
            dimension_semantics=("arbitrary",), vmem_limit_bytes=_vmem_limit(48 * 1024 * 1024)),
        name="moe_scatter",
    )(zero_tiles, zero_valid, n_blocks, block_rows, lp, t)


def _expert_kernel(te_ref, first_ref, nused_ref, xs_ref, wu_ref, bu_ref, wd_ref, bd_ref, ys_ref,
                   wu_bf, wd_bf):
    i = pl.program_id(0)

    @pl.when(i >= nused_ref[0])
    def _():
        ys_ref[...] = jnp.zeros(ys_ref.shape, F32)

    @pl.when(i < nused_ref[0])
    def _():
        @pl.when(first_ref[i] == 1)
        def _():
            wu_bf[...] = wu_ref[0].astype(BF16)
            wd_bf[...] = wd_ref[0].astype(BF16)

        h = _dot(xs_ref[...].astype(BF16), wu_bf[...]) + bu_ref[0]
        glu = jnp.minimum(h[:, :D_EXPERT], SWIGLU_LIMIT)
        lin = jnp.clip(h[:, D_EXPERT:], -SWIGLU_LIMIT, SWIGLU_LIMIT)
        a = glu * jax.nn.sigmoid(SWIGLU_ALPHA * glu) * (lin + 1.0)
        ys_ref[...] = _dot(a.astype(BF16), wd_bf[...]) + bd_ref[0]


def _experts(xs, tile_expert, tile_first, n_used, w_up, b_up, w_down, b_down):
    n_rows, d = xs.shape
    tm = MOE_TILE
    n_tiles = n_rows // tm

    def tile(i, te, first, nused):
        return (jnp.minimum(i, nused[0] - 1), 0)

    def wsel(i, te, first, nused):
        return (te[i], 0, 0)

    vmem = 2 * (d * 2 * D_EXPERT * 4 + D_EXPERT * d * 4) + d * 2 * D_EXPERT * 2 + D_EXPERT * d * 2 \
        + 4 * tm * d * 4 + 6 * tm * 2 * D_EXPERT * 4
    return pl.pallas_call(
        _expert_kernel,
        grid_spec=pltpu.PrefetchScalarGridSpec(
            num_scalar_prefetch=3,
            grid=(n_tiles,),
            in_specs=[pl.BlockSpec((tm, d), tile),
                      pl.BlockSpec((1, d, 2 * D_EXPERT), wsel),
                      pl.BlockSpec((1, 1, 2 * D_EXPERT), wsel),
                      pl.BlockSpec((1, D_EXPERT, d), wsel),
                      pl.BlockSpec((1, 1, d), wsel)],
            out_specs=pl.BlockSpec((tm, d), lambda i, te, first, nused: (i, 0)),
            scratch_shapes=[pltpu.VMEM((d, 2 * D_EXPERT), BF16), pltpu.VMEM((D_EXPERT, d), BF16)]),
        out_shape=jax.ShapeDtypeStruct((n_rows, d), F32),
        compiler_params=pltpu.CompilerParams(
            dimension_semantics=("arbitrary",), vmem_limit_bytes=_vmem_limit(vmem)),
        name="moe_experts",
    )(tile_expert, tile_first, n_used, xs, w_up, b_up.reshape(N_EXPERTS, 1, -1), w_down,
      b_down.reshape(N_EXPERTS, 1, -1))


def _combine_kernel(nblk_ref, bsrc_ref, ys_ref, lp_ref, w_ref, x1_ref, mod_ref, ng_ref, o_ref, stage, sem,
                    *, tt, n_stage, tiles_per_batch):
    i = pl.program_id(0)
    slot = i % 2

    def fetch(tile, s):
        def copy(b):
            src = pl.multiple_of(bsrc_ref[tile, b], RUN_ALIGN)
            dst = pl.multiple_of(b * RUN_ALIGN, RUN_ALIGN)
            return pltpu.make_async_copy(ys_ref.at[pl.ds(src, RUN_ALIGN)], stage.at[s, pl.ds(dst, RUN_ALIGN)],
                                         sem.at[s])
        return copy

    def group_copy():
        rows = RUN_ALIGN << WAIT_GROUP_LOG2
        return pltpu.make_async_copy(ys_ref.at[pl.ds(0, rows)], stage.at[slot, pl.ds(0, rows)], sem.at[slot])

    @pl.when(i == 0)
    def _():
        stage[...] = jnp.zeros(stage.shape, F32)
        _start_blocks(nblk_ref[0], fetch(0, 0))

    @pl.when(i + 1 < pl.num_programs(0))
    def _():
        _start_blocks(nblk_ref[i + 1], fetch(i + 1, 1 - slot))

    _wait_blocks(nblk_ref[i], fetch(i, slot), group_copy)

    lp = [jnp.broadcast_to(lp_ref[:, j:j + 1], (tt, PERM_ROWS)) for j in range(TOP_K)]
    w = [jnp.broadcast_to(w_ref[:, j:j + 1], (tt, PERM_ROWS)) for j in range(TOP_K)]
    f = jnp.zeros((tt, D_MODEL), F32)
    for lc in range(n_stage // PERM_ROWS):
        col = lc * PERM_ROWS + lax.broadcasted_iota(I32, (tt, PERM_ROWS), 1)
        wp = jnp.where(col == lp[TOP_K - 1], w[TOP_K - 1], 0.0)
        for j in range(TOP_K - 2, -1, -1):
            wp = jnp.where(col == lp[j], w[j], wp)
        f = f + _dot(wp.astype(BF16), stage[slot, lc * PERM_ROWS:(lc + 1) * PERM_ROWS, :].astype(BF16))
    b = i // tiles_per_batch
    gate_f = mod_ref[pl.ds(b, 1), 5 * D_MODEL:6 * D_MODEL]
    o_ref[...] = x1_ref[...] + gate_f * _rms(f, ng_ref[3:4, :])


def _combine(n_blocks, block_rows, ys, lp_t, wts_t, x1, mod, norm_g, tt, n_stage, tiles_per_batch):
    tok, d = x1.shape
    kern = functools.partial(_combine_kernel, tt=tt, n_stage=n_stage, tiles_per_batch=tiles_per_batch)
    return pl.pallas_call(
        kern,
        grid_spec=pltpu.PrefetchScalarGridSpec(
            num_scalar_prefetch=2,
            grid=(tok // tt,),
            in_specs=[pl.BlockSpec(memory_space=pl.ANY),
                      pl.BlockSpec((tt, TOP_K), lambda i, *_: (i, 0)),
                      pl.BlockSpec((tt, TOP_K), lambda i, *_: (i, 0)),
                      pl.BlockSpec((tt, d), lambda i, *_: (i, 0)),
                      pl.BlockSpec(mod.shape, lambda i, *_: (0, 0)),
                      pl.BlockSpec(norm_g.shape, lambda i, *_: (0, 0))],
            out_specs=pl.BlockSpec((tt, d), lambda i, *_: (i, 0)),
            scratch_shapes=[pltpu.VMEM((2, n_stage, d), F32), pltpu.SemaphoreType.DMA((2,))]),
        out_shape=jax.ShapeDtypeStruct((tok, d), F32),
        compiler_params=pltpu.CompilerParams(
            dimension_semantics=("arbitrary",), vmem_limit_bytes=_vmem_limit(48 * 1024 * 1024)),
        name="moe_combine",
    )(n_blocks, block_rows[:, 0, :], ys, lp_t, wts_t, x1, mod, norm_g)


def _rope_tables(n_ctx, n_lat):
    inv = ROPE_THETA ** (-np.arange(0, ROPE_AXIS_DIM, 2, dtype=np.float32) / ROPE_AXIS_DIM)
    idx = jnp.arange(n_lat, dtype=I32)
    rows = (idx // GRID_W).astype(F32)
    cols = (idx % GRID_W).astype(F32)
    half = ROPE_AXIS_DIM // 2
    inv = jnp.asarray(inv, F32)
    ang_r = rows[:, None] * inv[None, :]
    ang_c = cols[:, None] * inv[None, :]
    ang = jnp.concatenate([ang_r, ang_r, ang_c, ang_c], axis=1)
    cos, sin = jnp.cos(ang), jnp.sin(ang)
    lane = np.arange(HEAD_DIM)
    first = jnp.asarray((lane % ROPE_AXIS_DIM) < half)
    sin_a = jnp.where(first, -sin, 0.0)
    sin_b = jnp.where(first, 0.0, sin)
    pad1 = jnp.ones((n_ctx, HEAD_DIM), F32)
    pad0 = jnp.zeros((n_ctx, HEAD_DIM), F32)
    return (jnp.concatenate([pad1, cos], axis=0), jnp.concatenate([pad0, sin_a], axis=0),
            jnp.concatenate([pad0, sin_b], axis=0))


def kernel(x, c, ctx, c_ctx, w_mod, b_mod, norm_g, w_in, hgrn_lb, hgrn_norm_g, qk_norm_g, w_branch, w_out,
           router_w, router_b, w_up, b_up, w_down, b_down):
    nb, n_lat, d = x.shape
    n_ctx = ctx.shape[1]
    assert d == D_MODEL and w_mod.shape[0] == 1, "single-layer kernel"
    tok = nb * n_lat

    cv = jnp.zeros((8, d), F32).at[:nb].set(c).at[nb].set(c_ctx)
    mod = _modulation(cv, w_mod[0], b_mod[0])

    xc = jnp.concatenate([ctx, x], axis=1)
    sizes = np.cumsum([0, HGRN_W, HGRN_W, HGRN_W, HGRN_W, HGRN_W, ATT_W, KV_W, KV_W, D_MODEL, D_MODEL])
    order = [0, 1, 2, 3, 4, 5, 8, 9, 6, 7]
    w_in_bf = jnp.concatenate([w_in[0][:, sizes[i]:sizes[i + 1]] for i in order], axis=1).astype(BF16)
    p = _in_projection(xc, mod, norm_g[0], w_in_bf, n_ctx)

    cos, sin_a, sin_b = _rope_tables(n_ctx, n_lat)
    qt, kn, vt = _qk_prep(p, cos, sin_a, sin_b, qk_norm_g[0])
    o_a = _attention(qt, kn, vt, n_ctx)
    o_f, o_b = _hgrn_scan(p, hgrn_lb, n_ctx)

    x1, t, ids, wts = _merge(
        o_f, o_b, p, o_a, x, mod, norm_g[0], hgrn_norm_g, w_branch[0, 0].astype(BF16),
        w_branch[0, 1].astype(BF16), w_out[0].astype(BF16), router_w[0].T, router_b[0].reshape(N_EXPERTS, 1),
        n_ctx)

    tt = _largest_tile(n_lat, DISPATCH_TILE, 128)
    n_tt = tok // tt
    rank, counts = _expert_rank(ids, tt)
    counts = counts[:, :, 0]
    run = (counts + RUN_ALIGN - 1) // RUN_ALIGN * RUN_ALIGN
    run_end = jnp.cumsum(run, axis=1)
    stage_off = run_end - run
    expert_rows = jnp.sum(run, axis=0)
    tiles_e = (expert_rows + MOE_TILE - 1) // MOE_TILE
    tile_end = jnp.cumsum(tiles_e)
    starts = (tile_end - tiles_e) * MOE_TILE
    run_dst = starts[None, :] + jnp.cumsum(run, axis=0) - run
    n_tiles = -(-(tok * TOP_K + N_EXPERTS * n_tt * (RUN_ALIGN - 1)) // MOE_TILE) + N_EXPERTS
    tile_idx = jnp.arange(n_tiles, dtype=I32)
    tile_expert = jnp.sum((tile_end[None, :] <= tile_idx[:, None]).astype(I32), axis=1)
    tile_expert = jnp.minimum(tile_expert, N_EXPERTS - 1)
    n_used = tile_end[-1:].astype(I32)
    tile_expert = jnp.where(tile_idx < n_used[0], tile_expert, tile_expert[jnp.maximum(n_used[0] - 1, 0)])
    tile_first = jnp.concatenate([jnp.ones((1,), I32), (tile_expert[1:] != tile_expert[:-1]).astype(I32)])

    n_stage = -(-(TOP_K * tt + N_EXPERTS * (RUN_ALIGN - 1)) // PERM_ROWS) * PERM_ROWS
    ids_t = ids.reshape(TOP_K, n_tt, tt)
    onehot = ids_t[..., None] == jnp.arange(N_EXPERTS, dtype=I32)
    lp = jnp.sum(jnp.where(onehot, stage_off[None, :, None, :], 0), axis=-1).reshape(TOP_K, tok) + rank
    blk_row = jnp.arange(n_stage // RUN_ALIGN, dtype=I32) * RUN_ALIGN
    blk_expert = jnp.minimum(jnp.sum((run_end[:, None, :] <= blk_row[None, :, None]).astype(I32), axis=-1),
                             N_EXPERTS - 1)
    blk_onehot = blk_expert[..., None] == jnp.arange(N_EXPERTS, dtype=I32)
    blk_dst = jnp.sum(jnp.where(blk_onehot, (run_dst - stage_off)[:, None, :], 0), axis=-1) + blk_row[None, :]
    n_blocks = (run_end[:, -1] // RUN_ALIGN).astype(I32)
    blk_dst = jnp.where(blk_row[None, :] < run_end[:, -1:], blk_dst, 0).astype(I32)[:, None, :]

    n_tail = n_tiles - (tok * TOP_K) // MOE_TILE
    tail = n_used[0] + jnp.arange(n_tail, dtype=I32)
    zero_tiles = jnp.clip(jnp.concatenate([tile_end.astype(I32) - 1, tail]), 0, n_tiles - 1)
    zero_valid = jnp.concatenate([expert_rows % MOE_TILE != 0, tail < n_tiles]).astype(I32)

    xs = _scatter_rows(zero_tiles, zero_valid, n_blocks, blk_dst, lp, t, n_tiles * MOE_TILE, tt, n_stage)
    ys = _experts(xs, tile_expert, tile_first, n_used, w_up[0], b_up[0], w_down[0], b_down[0])
    out = _combine(n_blocks, blk_dst, ys, lp.T, wts.T, x1.reshape(tok, d), mod, norm_g[0], tt, n_stage,
                   n_lat // tt)
    return out.reshape(nb, n_lat, d)
```

```python
import functools

import jax
import jax.numpy as jnp
import numpy as np
from jax import lax
from jax.experimental import pallas as pl
from jax.experimental.pallas import tpu as pltpu

F32 = jnp.float32
BF16 = jnp.bfloat16
I32 = jnp.int32

D_MODEL = 1024
GRID_W = 64
N_MOD = 6
EPS = 1e-6
HGRN_HEADS = 8
HGRN_DK = 128
HGRN_W = HGRN_HEADS * HGRN_DK
ATT_HEADS = 8
ATT_KV_HEADS = 2
ATT_GROUPS = ATT_HEADS // ATT_KV_HEADS
HEAD_DIM = 128
ATT_W = ATT_HEADS * HEAD_DIM
KV_W = ATT_KV_HEADS * HEAD_DIM
ROPE_AXIS_DIM = HEAD_DIM // 2
ROPE_THETA = 10000.0
N_EXPERTS = 32
TOP_K = 4
D_EXPERT = 1024
SWIGLU_LIMIT = 7.0
SWIGLU_ALPHA = 1.702

COL_HQ, COL_HI, COL_FF, COL_FB, COL_HG, COL_AQ, COL_GH, COL_GA = (i * 1024 for i in range(8))
COL_K = 8 * 1024
COL_V = COL_K + KV_W
IN_COLS = COL_V + KV_W

V7X_VMEM_BYTES = 64 * 1024 * 1024
LANES = 128

HGRN_CHUNK = 64
HGRN_SUB = 16
MOE_TILE = 256
DISPATCH_TILE = 512
RUN_ALIGN = 8
PERM_ROWS = 256
WAIT_GROUP_LOG2 = 5


def _vmem_limit(nbytes):
    return int(min(V7X_VMEM_BYTES - 6 * 1024 * 1024, max(nbytes, 32 * 1024 * 1024)))


def _largest_tile(n, cap, mult):
    best = None
    for t in range(mult, min(n, cap) + 1, mult):
        if n % t == 0:
            best = t
    assert best is not None, (n, cap, mult)
    return best


def _rms(x, g):
    return x * lax.rsqrt(jnp.mean(x * x, axis=-1, keepdims=True) + EPS) * g


def _dot(a, b):
    return jnp.dot(a, b, preferred_element_type=F32)


def _dot_nt(a, b):
    return lax.dot_general(a, b, (((1,), (1,)), ((), ())), preferred_element_type=F32)


def _dot_tn(a, b):
    return lax.dot_general(a, b, (((0,), (0,)), ((), ())), preferred_element_type=F32)


def _mod_kernel(cv_ref, w_ref, b_ref, o_ref):
    cv = cv_ref[...]
    s = cv * jax.nn.sigmoid(cv)
    o_ref[...] = jnp.dot(s, w_ref[...], preferred_element_type=F32,
                         precision=lax.Precision.HIGHEST) + b_ref[...]


def _modulation(cv, w_mod, b_mod):
    rows, d = cv.shape
    n = w_mod.shape[1]
    tn = 1024
    return pl.pallas_call(
        _mod_kernel,
        grid=(n // tn,),
        in_specs=[pl.BlockSpec((rows, d), lambda j: (0, 0)),
                  pl.BlockSpec((d, tn), lambda j: (0, j)),
                  pl.BlockSpec((1, tn), lambda j: (0, j))],
        out_specs=pl.BlockSpec((rows, tn), lambda j: (0, j)),
        out_shape=jax.ShapeDtypeStruct((rows, n), F32),
        name="modulation",
    )(cv, w_mod, b_mod.reshape(1, n))


def _inproj_kernel(x_ref, mod_ref, g_ref, w_ref, o_ref, *, tm, n_ctx, n_batch):
    b = pl.program_id(1)
    i = pl.program_id(2)
    x = x_ref[0]
    xn = _rms(x, g_ref[0:1, :])
    row = i * tm + lax.broadcasted_iota(I32, (tm, 1), 0)
    is_ctx = row < n_ctx
    m_lat = mod_ref[pl.ds(b, 1), :]
    m_ctx = mod_ref[n_batch:n_batch + 1, :]
    shift = jnp.where(is_ctx, m_ctx[:, 0:D_MODEL], m_lat[:, 0:D_MODEL])
    scale = jnp.where(is_ctx, m_ctx[:, D_MODEL:2 * D_MODEL], m_lat[:, D_MODEL:2 * D_MODEL])
    u = xn * (1.0 + scale) + shift
    o_ref[0] = _dot(u.astype(BF16), w_ref[...]).astype(BF16)


def _in_projection(xc, mod, norm_g, w_in_bf, n_ctx):
    nb, lc, d = xc.shape
    n = w_in_bf.shape[1]
    tm = _largest_tile(lc, 1056, 16)
    tn = 2176
    assert n % tn == 0
    kern = functools.partial(_inproj_kernel, tm=tm, n_ctx=n_ctx, n_batch=nb)
    vmem = 2 * (tm * d * 4 + d * tn * 2 + tm * tn * 2) + 6 * tm * d * 4
    return pl.pallas_call(
        kern,
        grid=(n // tn, nb, lc // tm),
        in_specs=[pl.BlockSpec((1, tm, d), lambda j, b, i: (b, i, 0)),
                  pl.BlockSpec(mod.shape, lambda j, b, i: (0, 0)),
                  pl.BlockSpec(norm_g.shape, lambda j, b, i: (0, 0)),
                  pl.BlockSpec((d, tn), lambda j, b, i: (0, j))],
        out_specs=pl.BlockSpec((1, tm, tn), lambda j, b, i: (b, i, j)),
        out_shape=jax.ShapeDtypeStruct((nb, lc, n), BF16),
        compiler_params=pltpu.CompilerParams(
            dimension_semantics=("arbitrary", "arbitrary", "arbitrary"),
            vmem_limit_bytes=_vmem_limit(vmem)),
        name="in_projection",
    )(xc, mod, norm_g, w_in_bf)


def _rope(xn, cos, sin_a, sin_b):
    return (xn * cos + pltpu.roll(xn, HEAD_DIM - ROPE_AXIS_DIM // 2, 1) * sin_a
            + pltpu.roll(xn, ROPE_AXIS_DIM // 2, 1) * sin_b)


def _qkprep_kernel(q_ref, k_ref, v_ref, cos_ref, sa_ref, sb_ref, g_ref, qt_ref, ko_ref, vt_ref):
    cos, sa, sb = cos_ref[...], sa_ref[...], sb_ref[...]
    gq, gk = g_ref[0:1, :], g_ref[1:2, :]
    qscale = HEAD_DIM ** -0.5 * np.log2(np.e)
    for h in range(ATT_HEADS):
        hs = slice(h * HEAD_DIM, (h + 1) * HEAD_DIM)
        xn = _rms(q_ref[0, :, hs].astype(F32), gq)
        qt_ref[0, hs, :] = (_rope(xn, cos, sa, sb) * qscale).T.astype(BF16)
    for h in range(ATT_KV_HEADS):
        hs = slice(h * HEAD_DIM, (h + 1) * HEAD_DIM)
        xn = _rms(k_ref[0, :, hs].astype(F32), gk)
        ko_ref[0, :, hs] = _rope(xn, cos, sa, sb).astype(BF16)
        vt_ref[0, hs, :] = v_ref[0, :, hs].astype(F32).T.astype(BF16)


def _qk_prep(p, cos, sin_a, sin_b, qk_norm_g):
    nb, lc, _ = p.shape
    tm = _largest_tile(lc, 512, 128)
    tab = pl.BlockSpec((tm, HEAD_DIM), lambda b, i: (i, 0))
    return pl.pallas_call(
        _qkprep_kernel,
        grid=(nb, lc // tm),
        in_specs=[pl.BlockSpec((1, tm, ATT_W), lambda b, i: (b, i, COL_AQ // ATT_W)),
                  pl.BlockSpec((1, tm, KV_W), lambda b, i: (b, i, COL_K // KV_W)),
                  pl.BlockSpec((1, tm, KV_W), lambda b, i: (b, i, COL_V // KV_W)),
                  tab, tab, tab,
                  pl.BlockSpec(qk_norm_g.shape, lambda b, i: (0, 0))],
        out_specs=[pl.BlockSpec((1, ATT_W, tm), lambda b, i: (b, 0, i)),
                   pl.BlockSpec((1, tm, KV_W), lambda b, i: (b, i, 0)),
                   pl.BlockSpec((1, KV_W, tm), lambda b, i: (b, 0, i))],
        out_shape=[jax.ShapeDtypeStruct((nb, ATT_W, lc), BF16),
                   jax.ShapeDtypeStruct((nb, lc, KV_W), BF16),
                   jax.ShapeDtypeStruct((nb, KV_W, lc), BF16)],
        name="qk_prep",
    )(p, p, p, cos, sin_a, sin_b, qk_norm_g)


def _attn_kernel(qt_ref, k_ref, vt_ref, o_ref, acc_ref, s_ref, *, tq, kc, n_kc):
    acc_ref[...] = jnp.zeros(acc_ref.shape, F32)
    qts = [qt_ref[0, g * HEAD_DIM:(g + 1) * HEAD_DIM, :] for g in range(ATT_GROUPS)]

    def put_scores(c, slot):
        kch = k_ref[0, pl.ds(pl.multiple_of(c * kc, kc), kc), :]
        for g in range(ATT_GROUPS):
            s_ref[slot, g] = _dot(kch, qts[g])

    def softmax_pv(c, slot, carry):
        vch = vt_ref[0, :, pl.ds(pl.multiple_of(c * kc, kc), kc)]
        out = []
        for g in range(ATT_GROUPS):
            m_old, l_old = carry[g]
            s = s_ref[slot, g]
            m_new = jnp.maximum(m_old, jnp.max(s, axis=0, keepdims=True))
            alpha = jnp.exp2(m_old - m_new)
            p = jnp.exp2(s - m_new)
            l_new = l_old * alpha + jnp.sum(p, axis=0, keepdims=True)
            acc_ref[g] = acc_ref[g] * alpha + _dot(vch, p.astype(BF16))
            out.append((m_new, l_new))
        return tuple(out)

    def body(i, carry):
        c = 2 * i
        put_scores(c + 1, 1)
        carry = softmax_pv(c, 0, carry)
        put_scores(c + 2, 0)
        return softmax_pv(c + 1, 1, carry)

    init = tuple((jnp.full((1, tq), -jnp.inf, F32), jnp.zeros((1, tq), F32)) for _ in range(ATT_GROUPS))
    put_scores(0, 0)
    fin = lax.fori_loop(0, (n_kc - 1) // 2, body, init)
    if n_kc % 2 == 0:
        put_scores(n_kc - 1, 1)
        fin = softmax_pv(n_kc - 2, 0, fin)
    fin = softmax_pv(n_kc - 1, (n_kc - 1) % 2, fin)
    for g in range(ATT_GROUPS):
        o_ref[0, :, g * HEAD_DIM:(g + 1) * HEAD_DIM] = (acc_ref[g] / fin[g][1]).T.astype(BF16)


def _attention(qt, kn, vt, n_ctx):
    nb, lc, _ = kn.shape
    n_lat = lc - n_ctx
    tq = _largest_tile(n_lat, 256, 128)
    assert n_ctx % tq == 0
    kc = _largest_tile(lc, 768, 256)
    gw = ATT_GROUPS * HEAD_DIM
    kern = functools.partial(_attn_kernel, tq=tq, kc=kc, n_kc=lc // kc)
    vmem = 8 * lc * HEAD_DIM * 2 + 4 * ATT_GROUPS * kc * tq * 4 + 8 * gw * tq * 4
    return pl.pallas_call(
        kern,
        grid=(nb, ATT_KV_HEADS, n_lat // tq),
        in_specs=[pl.BlockSpec((1, gw, tq), lambda b, h, i: (b, h, n_ctx // tq + i)),
                  pl.BlockSpec((1, lc, HEAD_DIM), lambda b, h, i: (b, 0, h)),
                  pl.BlockSpec((1, HEAD_DIM, lc), lambda b, h, i: (b, h, 0))],
        out_specs=pl.BlockSpec((1, tq, gw), lambda b, h, i: (b, i, h)),
        out_shape=jax.ShapeDtypeStruct((nb, n_lat, ATT_W), BF16),
        scratch_shapes=[pltpu.VMEM((ATT_GROUPS, HEAD_DIM, tq), F32),
                        pltpu.VMEM((2, ATT_GROUPS, kc, tq), F32)],
        compiler_params=pltpu.CompilerParams(
            dimension_semantics=("arbitrary", "arbitrary", "arbitrary"),
            vmem_limit_bytes=_vmem_limit(vmem)),
        name="attention",
    )(qt, kn, vt)


def _hgrn_chunk(q_ref, v_ref, r_ref, lb, reverse):
    c = HGRN_CHUNK
    q = q_ref[0].astype(F32)
    v = v_ref[0]
    r = r_ref[0].astype(F32)
    sig = jax.nn.sigmoid(r)
    f = lb + (1.0 - lb) * sig
    logf = jnp.log(f)
    k = (1.0 - lb) * (1.0 - sig)
    ti = lax.broadcasted_iota(I32, (c, c), 0)
    si = lax.broadcasted_iota(I32, (c, c), 1)
    tri = jnp.where((si >= ti) if reverse else (si <= ti), 1.0, 0.0).astype(BF16)
    hi = logf.astype(BF16)
    lo = (logf - hi.astype(F32)).astype(BF16)
    bcum = _dot(tri, hi) + _dot(tri, lo)
    b_end = bcum[0:1, :] if reverse else bcum[c - 1:c, :]
    qt = (q * jnp.exp(bcum)).astype(BF16)
    kt = (k * jnp.exp(b_end - bcum)).astype(BF16)
    dec = jnp.exp(b_end)

    nsub = c // HGRN_SUB
    qp, kp, cols = [], [], []
    for blk in range(nsub):
        rs = slice(blk * HGRN_SUB, (blk + 1) * HGRN_SUB)
        cs = slice(blk * HGRN_SUB, c) if reverse else slice(0, (blk + 1) * HGRN_SUB)
        mid = blk * HGRN_SUB + HGRN_SUB // 2
        ref = bcum[mid:mid + 1, :]
        qp.append((q[rs] * jnp.exp(bcum[rs] - ref)).astype(BF16))
        kp.append((k[cs] * jnp.exp(ref - bcum[cs])).astype(BF16))
        cols.append(cs)

    keep = []
    for blk in range(nsub):
        n_cols = cols[blk].stop - cols[blk].start
        trow = blk * HGRN_SUB + lax.broadcasted_iota(I32, (HGRN_SUB, n_cols), 0)
        scol = cols[blk].start + lax.broadcasted_iota(I32, (HGRN_SUB, n_cols), 1)
        keep.append((scol >= trow) if reverse else (scol <= trow))
    return dict(v=v, qt=qt, kt=kt, dec=dec, qp=qp, kp=kp, cols=cols, keep=keep)


def _hgrn_emit(dirs, s_refs, o_refs):
    nsub = HGRN_CHUNK // HGRN_SUB
    heads = [slice(h * HGRN_DK, (h + 1) * HGRN_DK) for h in range(HGRN_HEADS)]
    inter = []
    for d, s_ref in zip(dirs, s_refs):
        row = []
        for h, hs in enumerate(heads):
            st = s_ref[h]
            row.append(_dot_nt(d["qt"][:, hs], st.astype(BF16)))
            s_ref[h] = st * d["dec"][:, hs] + _dot_tn(d["v"][:, hs], d["kt"][:, hs])
        inter.append(row)
    scores = [[[_dot_nt(d["qp"][blk][:, hs], d["kp"][blk][:, hs]) for blk in range(nsub)] for hs in heads]
              for d in dirs]
    for d, sc, it, o_ref in zip(dirs, scores, inter, o_refs):
        outs = []
        for h, hs in enumerate(heads):
            parts = [_dot(jnp.where(d["keep"][blk], sc[h][blk], 0.0).astype(BF16), d["v"][d["cols"][blk], hs])
                     for blk in range(nsub)]
            outs.append(it[h] + jnp.concatenate(parts, axis=0))
        o_ref[0] = jnp.concatenate(outs, axis=1).astype(BF16)


def _hgrn_kernel(qf_ref, vf_ref, rf_ref, qb_ref, vb_ref, rb_ref, lb_ref, of_ref, ob_ref, sf_ref, sb_ref):
    @pl.when(pl.program_id(1) == 0)
    def _():
        sf_ref[...] = jnp.zeros(sf_ref.shape, F32)
        sb_ref[...] = jnp.zeros(sb_ref.shape, F32)

    n_layers = lb_ref.shape[0] // 2

    def lower_bound(direction):
        rows = [lb_ref[direction * n_layers + l:direction * n_layers + l + 1, :] for l in range(n_layers)]
        amax = functools.reduce(jnp.maximum, rows)
        e = [jnp.exp(a - amax) for a in rows]
        return e[0] / functools.reduce(lambda u, w: u + w, e)

    fwd = _hgrn_chunk(qf_ref, vf_ref, rf_ref, lower_bound(0), False)
    bwd = _hgrn_chunk(qb_ref, vb_ref, rb_ref, lower_bound(1), True)
    _hgrn_emit([fwd, bwd], [sf_ref, sb_ref], [of_ref, ob_ref])


def _hgrn_scan(p, hgrn_lb, n_ctx):
    nb, lc, _ = p.shape
    c = HGRN_CHUNK
    assert n_ctx % c == 0 and lc % c == 0
    n_chunks = lc // c
    ctx_chunks = n_ctx // c

    def fwd(col):
        return lambda b, s: (b, s, col)

    def bwd_chunk(s):
        return jnp.where(s < ctx_chunks, ctx_chunks - 1 - s, n_chunks - 1 + ctx_chunks - s)

    def bwd(col):
        return lambda b, s: (b, bwd_chunk(s), col)

    blk = (1, c, HGRN_W)
    lb2 = hgrn_lb.reshape(-1, HGRN_W)
    return pl.pallas_call(
        _hgrn_kernel,
        grid=(nb, n_chunks),
        in_specs=[pl.BlockSpec(blk, fwd(COL_HQ // HGRN_W)), pl.BlockSpec(blk, fwd(COL_HI // HGRN_W)),
                  pl.BlockSpec(blk, fwd(COL_FF // HGRN_W)),
                  pl.BlockSpec(blk, bwd(COL_HQ // HGRN_W)), pl.BlockSpec(blk, bwd(COL_HI // HGRN_W)),
                  pl.BlockSpec(blk, bwd(COL_FB // HGRN_W)),
                  pl.BlockSpec(lb2.shape, lambda b, s: (0, 0))],
        out_specs=[pl.BlockSpec(blk, fwd(0)), pl.BlockSpec(blk, bwd(0))],
        out_shape=[jax.ShapeDtypeStruct((nb, lc, HGRN_W), BF16)] * 2,
        scratch_shapes=[pltpu.VMEM((HGRN_HEADS, HGRN_DK, HGRN_DK), F32)] * 2,
        compiler_params=pltpu.CompilerParams(dimension_semantics=("arbitrary", "arbitrary")),
        name="hgrn_scan",
    )(p, p, p, p, p, p, lb2)


def _merge_kernel(of_ref, ob_ref, hg_ref, gh_ref, ga_ref, oa_ref, x_ref, mod_ref, ng_ref, hng_ref,
                  wb0_ref, wb1_ref, wo_ref, rw_ref, rb_ref,
                  x1_ref, t_ref, ids_ref, wts_ref):
    b = pl.program_id(0)
    m = mod_ref[pl.ds(b, 1), :]
    gate_mix = m[:, 2 * D_MODEL:3 * D_MODEL]
    shift_f = m[:, 3 * D_MODEL:4 * D_MODEL]
    scale_f = m[:, 4 * D_MODEL:5 * D_MODEL]

    o = of_ref[0].astype(F32) + ob_ref[0].astype(F32)
    hng = hng_ref[...]
    o_h = jnp.concatenate(
        [_rms(o[:, h * HGRN_DK:(h + 1) * HGRN_DK], hng) for h in range(HGRN_HEADS)], axis=1)
    g_raw = hg_ref[0].astype(F32)
    o_h = o_h * (g_raw * jax.nn.sigmoid(g_raw))
    y = (jax.nn.sigmoid(gh_ref[0].astype(F32)) * _dot(o_h.astype(BF16), wb0_ref[...])
         + jax.nn.sigmoid(ga_ref[0].astype(F32)) * _dot(oa_ref[0], wb1_ref[...]))
    y = _dot(y.astype(BF16), wo_ref[...])
    x1 = x_ref[0] + gate_mix * _rms(y, ng_ref[1:2, :])
    x1_ref[0] = x1
    t = _rms(x1, ng_ref[2:3, :]) * (1.0 + scale_f) + shift_f
    t_ref[...] = t

    t_hi = t.astype(BF16)
    t_lo = (t - t_hi.astype(F32)).astype(BF16)
    rw = rw_ref[...]
    rw_hi = rw.astype(BF16)
    rw_lo = (rw - rw_hi.astype(F32)).astype(BF16)
    logits = _dot_nt(rw_hi, t_hi) + _dot_nt(rw_hi, t_lo) + _dot_nt(rw_lo, t_hi) + rb_ref[...]
    eidx = lax.broadcasted_iota(I32, logits.shape, 0).astype(F32)
    vals = []
    for j in range(TOP_K):
        mx = jnp.max(logits, axis=0, keepdims=True)
        idx = jnp.min(jnp.where(logits == mx, eidx, float(N_EXPERTS)), axis=0, keepdims=True)
        ids_ref[j:j + 1, :] = idx.astype(I32)
        vals.append(mx)
        logits = jnp.where(eidx == idx, -jnp.inf, logits)
    ex = [jnp.exp(vj - vals[0]) for vj in vals]
    den = ex[0] + ex[1] + ex[2] + ex[3]
    for j in range(TOP_K):
        wts_ref[j:j + 1, :] = ex[j] / den


def _merge(o_f, o_b, p, o_a, x, mod, norm_g, hgrn_norm_g, wb0, wb1, wo, router_wt, router_b, n_ctx):
    nb, n_lat, d = x.shape
    tm = _largest_tile(n_lat, 256, 128)
    assert n_ctx % tm == 0
    off = n_ctx // tm
    nt = n_lat // tm
    row = (1, tm, d)

    def pcol(col):
        return pl.BlockSpec(row, lambda b, i: (b, off + i, col // d))

    def full(a):
        return pl.BlockSpec(a.shape, lambda b, i: (0,) * a.ndim)

    tok = nb * n_lat
    return pl.pallas_call(
        _merge_kernel,
        grid=(nb, nt),
        in_specs=[pl.BlockSpec(row, lambda b, i: (b, off + i, 0)),
                  pl.BlockSpec(row, lambda b, i: (b, off + i, 0)),
                  pcol(COL_HG), pcol(COL_GH), pcol(COL_GA),
                  pl.BlockSpec(row, lambda b, i: (b, i, 0)),
                  pl.BlockSpec(row, lambda b, i: (b, i, 0)),
                  full(mod), full(norm_g), full(hgrn_norm_g), full(wb0), full(wb1), full(wo),
                  full(router_wt), full(router_b)],
        out_specs=[pl.BlockSpec(row, lambda b, i: (b, i, 0)),
                   pl.BlockSpec((tm, d), lambda b, i: (b * nt + i, 0)),
                   pl.BlockSpec((TOP_K, tm), lambda b, i: (0, b * nt + i)),
                   pl.BlockSpec((TOP_K, tm), lambda b, i: (0, b * nt + i))],
        out_shape=[jax.ShapeDtypeStruct((nb, n_lat, d), F32),
                   jax.ShapeDtypeStruct((tok, d), F32),
                   jax.ShapeDtypeStruct((TOP_K, tok), I32),
                   jax.ShapeDtypeStruct((TOP_K, tok), F32)],
        compiler_params=pltpu.CompilerParams(
            dimension_semantics=("arbitrary", "arbitrary"),
            vmem_limit_bytes=_vmem_limit(40 * 1024 * 1024)),
        name="merge_router",
    )(o_f, o_b, p, p, p, o_a, x, mod, norm_g, hgrn_norm_g, wb0, wb1, wo, router_wt, router_b)


def _rank_kernel(ids_ref, rank_ref, cnt_ref, *, tt):
    eidx = lax.broadcasted_iota(I32, (N_EXPERTS, tt), 0)
    si = lax.broadcasted_iota(I32, (tt, tt), 0)
    ti = lax.broadcasted_iota(I32, (tt, tt), 1)
    before = jnp.where(si < ti, 1.0, 0.0).astype(BF16)
    seen = jnp.zeros((N_EXPERTS, 1), F32)
    for j in range(TOP_K):
        onehot = eidx == ids_ref[j:j + 1, :]
        oh = jnp.where(onehot, 1.0, 0.0)
        earlier = _dot(oh.astype(BF16), before)
        rank = jnp.sum(jnp.where(onehot, seen + earlier, 0.0), axis=0, keepdims=True)
        rank_ref[j:j + 1, :] = rank.astype(I32)
        seen = seen + jnp.sum(oh, axis=1, keepdims=True)
    cnt_ref[0] = seen.astype(I32)


def _expert_rank(ids, tt):
    _, tok = ids.shape
    return pl.pallas_call(
        functools.partial(_rank_kernel, tt=tt),
        grid=(tok // tt,),
        in_specs=[pl.BlockSpec((TOP_K, tt), lambda i: (0, i))],
        out_specs=[pl.BlockSpec((TOP_K, tt), lambda i: (0, i)),
                   pl.BlockSpec((1, N_EXPERTS, 1), lambda i: (i, 0, 0))],
        out_shape=[jax.ShapeDtypeStruct((TOP_K, tok), I32),
                   jax.ShapeDtypeStruct((tok // tt, N_EXPERTS, 1), I32)],
        compiler_params=pltpu.CompilerParams(dimension_semantics=("arbitrary",)),
        name="expert_rank",
    )(ids)


def _start_blocks(n_blocks, make_copy):
    def start(b, carry):
        make_copy(b).start()
        return carry

    lax.fori_loop(0, n_blocks, start, 0)


def _wait_blocks(n_blocks, make_copy, make_group_copy):
    def wait_group(g, carry):
        make_group_copy().wait()
        return carry

    def wait_one(b, carry):
        make_copy(0).wait()
        return carry

    lax.fori_loop(0, lax.shift_right_logical(n_blocks, WAIT_GROUP_LOG2), wait_group, 0)
    lax.fori_loop(0, jnp.bitwise_and(n_blocks, (1 << WAIT_GROUP_LOG2) - 1), wait_one, 0)


def _scatter_kernel(zt_ref, zvalid_ref, nblk_ref, bdst_ref, lp_ref, t_ref, xs_ref, stage, zero_buf, sem, zsem,
                    *, tt, n_stage):
    @pl.when(pl.program_id(0) == 0)
    def _():
        zero_buf[...] = jnp.zeros(zero_buf.shape, F32)

        def zero_copy(k):
            return pltpu.make_async_copy(zero_buf, xs_ref.at[pl.ds(zt_ref[k] * MOE_TILE, MOE_TILE)], zsem)

        def start(k, carry):
            @pl.when(zvalid_ref[k] == 1)
            def _():
                zero_copy(k).start()
            return carry

        def wait(k, carry):
            @pl.when(zvalid_ref[k] == 1)
            def _():
                zero_copy(k).wait()
            return carry

        lax.fori_loop(0, zt_ref.shape[0], start, 0)
        lax.fori_loop(0, zt_ref.shape[0], wait, 0)

    i = pl.program_id(0)
    slot = i % 2
    t_bf = t_ref[...].astype(BF16)
    lps = [lp_ref[j:j + 1, :] for j in range(TOP_K)]
    for rc in range(n_stage // PERM_ROWS):
        row = rc * PERM_ROWS + lax.broadcasted_iota(I32, (PERM_ROWS, tt), 0)
        sel = jnp.where(row == lps[TOP_K - 1], 1.0, 0.0)
        for j in range(TOP_K - 2, -1, -1):
            sel = jnp.where(row == lps[j], 1.0, sel)
        stage[slot, rc * PERM_ROWS:(rc + 1) * PERM_ROWS, :] = _dot(sel.astype(BF16), t_bf)

    def copy(b):
        src = pl.multiple_of(b * RUN_ALIGN, RUN_ALIGN)
        dst = pl.multiple_of(bdst_ref[0, 0, b], RUN_ALIGN)
        return pltpu.make_async_copy(stage.at[slot, pl.ds(src, RUN_ALIGN)], xs_ref.at[pl.ds(dst, RUN_ALIGN)],
                                     sem.at[slot])

    def waiter(s):
        rows = RUN_ALIGN << WAIT_GROUP_LOG2
        return (lambda b: pltpu.make_async_copy(stage.at[s, pl.ds(0, RUN_ALIGN)],
                                                xs_ref.at[pl.ds(0, RUN_ALIGN)], sem.at[s]),
                lambda: pltpu.make_async_copy(stage.at[s, pl.ds(0, rows)], xs_ref.at[pl.ds(0, rows)], sem.at[s]))

    _start_blocks(nblk_ref[i], copy)

    @pl.when(i > 0)
    def _():
        _wait_blocks(nblk_ref[i - 1], *waiter(1 - slot))

    @pl.when(i == pl.num_programs(0) - 1)
    def _():
        _wait_blocks(nblk_ref[i], *waiter(slot))


def _scatter_rows(zero_tiles, zero_valid, n_blocks, block_rows, lp, t, n_rows, tt, n_stage):
    tok, d = t.shape
    nb = block_rows.shape[2]
    return pl.pallas_call(
        functools.partial(_scatter_kernel, tt=tt, n_stage=n_stage),
        grid_spec=pltpu.PrefetchScalarGridSpec(
            num_scalar_prefetch=3,
            grid=(tok // tt,),
            in_specs=[pl.BlockSpec((1, 1, nb), lambda i, *_: (i, 0, 0), memory_space=pltpu.SMEM),
                      pl.BlockSpec((TOP_K, tt), lambda i, *_: (0, i)),
                      pl.BlockSpec((tt, d), lambda i, *_: (i, 0))],
            out_specs=pl.BlockSpec(memory_space=pl.ANY),
            scratch_shapes=[pltpu.VMEM((2, n_stage, d), F32), pltpu.VMEM((MOE_TILE, d), F32),
                            pltpu.SemaphoreType.DMA((2,)), pltpu.SemaphoreType.DMA(())]),
        out_shape=jax.ShapeDtypeStruct((n_rows, d), F32),
        compiler_params=pltpu.CompilerParams(
            dimension_semantics=("arbitrary",), vmem_limit_bytes=_vmem_limit(48 * 1024 * 1024)),
        name="moe_scatter",
    )(zero_tiles, zero_valid, n_blocks, block_rows, lp, t)


def _expert_kernel(te_ref, first_ref, nused_ref, slot_ref, next_ref, xs_ref, wu_hbm, bu_ref, wd_hbm, bd_ref,
                   ys_ref, wu32, wd32, wu_bf, wd_bf, sem_u, sem_d):
    i = pl.program_id(0)

    def weight_copies(e, s):
        return (pltpu.make_async_copy(wu_hbm.at[e], wu32.at[s], sem_u.at[s]),
                pltpu.make_async_copy(wd_hbm.at[e], wd32.at[s], sem_d.at[s]))

    @pl.when(i >= nused_ref[0])
    def _():
        ys_ref[...] = jnp.zeros(ys_ref.shape, F32)

    @pl.when(i < nused_ref[0])
    def _():
        @pl.when(first_ref[i] == 1)
        def _():
            s = slot_ref[i]

            @pl.when(i == 0)
            def _():
                for cp in weight_copies(te_ref[0], 0):
                    cp.start()

            for cp in weight_copies(te_ref[i], s):
                cp.wait()
            wu_bf[...] = wu32[s].astype(BF16)
            wd_bf[...] = wd32[s].astype(BF16)

            @pl.when(next_ref[i] >= 0)
            def _():
                for cp in weight_copies(next_ref[i], 1 - s):
                    cp.start()

        h = _dot(xs_ref[...].astype(BF16), wu_bf[...]) + bu_ref[0]
        glu = jnp.minimum(h[:, :D_EXPERT], SWIGLU_LIMIT)
        lin = jnp.clip(h[:, D_EXPERT:], -SWIGLU_LIMIT, SWIGLU_LIMIT)
        a = glu * jax.nn.sigmoid(SWIGLU_ALPHA * glu) * (lin + 1.0)
        ys_ref[...] = _dot(a.astype(BF16), wd_bf[...]) + bd_ref[0]


def _experts(xs, tile_expert, tile_first, n_used, tile_slot, next_expert, w_up, b_up, w_down, b_down):
    n_rows, d = xs.shape
    tm = MOE_TILE
    n_tiles = n_rows // tm

    def tile(i, te, first, nused, *_):
        return (jnp.minimum(i, nused[0] - 1), 0)

    def bsel(i, te, *_):
        return (te[i], 0, 0)

    vmem = 2 * (d * 2 * D_EXPERT * 4 + D_EXPERT * d * 4) + d * 2 * D_EXPERT * 2 + D_EXPERT * d * 2 \
        + 4 * tm * d * 4 + 6 * tm * 2 * D_EXPERT * 4
    return pl.pallas_call(
        _expert_kernel,
        grid_spec=pltpu.PrefetchScalarGridSpec(
            num_scalar_prefetch=5,
            grid=(n_tiles,),
            in_specs=[pl.BlockSpec((tm, d), tile),
                      pl.BlockSpec(memory_space=pl.ANY),
                      pl.BlockSpec((1, 1, 2 * D_EXPERT), bsel),
                      pl.BlockSpec(memory_space=pl.ANY),
                      pl.BlockSpec((1, 1, d), bsel)],
            out_specs=pl.BlockSpec((tm, d), lambda i, *_: (i, 0)),
            scratch_shapes=[pltpu.VMEM((2, d, 2 * D_EXPERT), F32), pltpu.VMEM((2, D_EXPERT, d), F32),
                            pltpu.VMEM((d, 2 * D_EXPERT), BF16), pltpu.VMEM((D_EXPERT, d), BF16),
                            pltpu.SemaphoreType.DMA((2,)), pltpu.SemaphoreType.DMA((2,))]),
        out_shape=jax.ShapeDtypeStruct((n_rows, d), F32),
        compiler_params=pltpu.CompilerParams(
            dimension_semantics=("arbitrary",), vmem_limit_bytes=_vmem_limit(vmem)),
        name="moe_experts",
    )(tile_expert, tile_first, n_used, tile_slot, next_expert, xs, w_up, b_up.reshape(N_EXPERTS, 1, -1), w_down,
      b_down.reshape(N_EXPERTS, 1, -1))


def _combine_kernel(nblk_ref, bsrc_ref, ys_ref, lp_ref, w_ref, x1_ref, mod_ref, ng_ref, o_ref, stage, sem,
                    *, tt, n_stage, tiles_per_batch):
    i = pl.program_id(0)
    slot = i % 2

    def fetch(tile, s):
        def copy(b):
            src = pl.multiple_of(bsrc_ref[tile, b], RUN_ALIGN)
            dst = pl.multiple_of(b * RUN_ALIGN, RUN_ALIGN)
            return pltpu.make_async_copy(ys_ref.at[pl.ds(src, RUN_ALIGN)], stage.at[s, pl.ds(dst, RUN_ALIGN)],
                                         sem.at[s])
        return copy

    def group_copy():
        rows = RUN_ALIGN << WAIT_GROUP_LOG2
        return pltpu.make_async_copy(ys_ref.at[pl.ds(0, rows)], stage.at[slot, pl.ds(0, rows)], sem.at[slot])

    @pl.when(i == 0)
    def _():
        stage[...] = jnp.zeros(stage.shape, F32)
        _start_blocks(nblk_ref[0], fetch(0, 0))

    @pl.when(i + 1 < pl.num_programs(0))
    def _():
        _start_blocks(nblk_ref[i + 1], fetch(i + 1, 1 - slot))

    _wait_blocks(nblk_ref[i], fetch(i, slot), group_copy)

    lp = [jnp.broadcast_to(lp_ref[:, j:j + 1], (tt, PERM_ROWS)) for j in range(TOP_K)]
    w = [jnp.broadcast_to(w_ref[:, j:j + 1], (tt, PERM_ROWS)) for j in range(TOP_K)]
    f = jnp.zeros((tt, D_MODEL), F32)
    for lc in range(n_stage // PERM_ROWS):
        col = lc * PERM_ROWS + lax.broadcasted_iota(I32, (tt, PERM_ROWS), 1)
        wp = jnp.where(col == lp[TOP_K - 1], w[TOP_K - 1], 0.0)
        for j in range(TOP_K - 2, -1, -1):
            wp = jnp.where(col == lp[j], w[j], wp)
        f = f + _dot(wp.astype(BF16), stage[slot, lc * PERM_ROWS:(lc + 1) * PERM_ROWS, :].astype(BF16))
    b = i // tiles_per_batch
    gate_f = mod_ref[pl.ds(b, 1), 5 * D_MODEL:6 * D_MODEL]
    o_ref[...] = x1_ref[...] + gate_f * _rms(f, ng_ref[3:4, :])


def _combine(n_blocks, block_rows, ys, lp_t, wts_t, x1, mod, norm_g, tt, n_stage, tiles_per_batch):
    tok, d = x1.shape
    kern = functools.partial(_combine_kernel, tt=tt, n_stage=n_stage, tiles_per_batch=tiles_per_batch)
    return pl.pallas_call(
        kern,
        grid_spec=pltpu.PrefetchScalarGridSpec(
            num_scalar_prefetch=2,
            grid=(tok // tt,),
            in_specs=[pl.BlockSpec(memory_space=pl.ANY),
                      pl.BlockSpec((tt, TOP_K), lambda i, *_: (i, 0)),
                      pl.BlockSpec((tt, TOP_K), lambda i, *_: (i, 0)),
                      pl.BlockSpec((tt, d), lambda i, *_: (i, 0)),
                      pl.BlockSpec(mod.shape, lambda i, *_: (0, 0)),
                      pl.BlockSpec(norm_g.shape, lambda i, *_: (0, 0))],
            out_specs=pl.BlockSpec((tt, d), lambda i, *_: (i, 0)),
            scratch_shapes=[pltpu.VMEM((2, n_stage, d), F32), pltpu.SemaphoreType.DMA((2,))]),
        out_shape=jax.ShapeDtypeStruct((tok, d), F32),
        compiler_params=pltpu.CompilerParams(
            dimension_semantics=("arbitrary",), vmem_limit_bytes=_vmem_limit(48 * 1024 * 1024)),
        name="moe_combine",
    )(n_blocks, block_rows[:, 0, :], ys, lp_t, wts_t, x1, mod, norm_g)


def _rope_tables(n_ctx, n_lat):
    inv = ROPE_THETA ** (-np.arange(0, ROPE_AXIS_DIM, 2, dtype=np.float32) / ROPE_AXIS_DIM)
    idx = jnp.arange(n_lat, dtype=I32)
    rows = (idx // GRID_W).astype(F32)
    cols = (idx % GRID_W).astype(F32)
    half = ROPE_AXIS_DIM // 2
    inv = jnp.asarray(inv, F32)
    ang_r = rows[:, None] * inv[None, :]
    ang_c = cols[:, None] * inv[None, :]
    ang = jnp.concatenate([ang_r, ang_r, ang_c, ang_c], axis=1)
    cos, sin = jnp.cos(ang), jnp.sin(ang)
    lane = np.arange(HEAD_DIM)
    first = jnp.asarray((lane % ROPE_AXIS_DIM) < half)
    sin_a = jnp.where(first, -sin, 0.0)
    sin_b = jnp.where(first, 0.0, sin)
    pad1 = jnp.ones((n_ctx, HEAD_DIM), F32)
    pad0 = jnp.zeros((n_ctx, HEAD_DIM), F32)
    return (jnp.concatenate([pad1, cos], axis=0), jnp.concatenate([pad0, sin_a], axis=0),
            jnp.concatenate([pad0, sin_b], axis=0))


def kernel(x, c, ctx, c_ctx, w_mod, b_mod, norm_g, w_in, hgrn_lb, hgrn_norm_g, qk_norm_g, w_branch, w_out,
           router_w, router_b, w_up, b_up, w_down, b_down):
    nb, n_lat, d = x.shape
    n_ctx = ctx.shape[1]
    assert d == D_MODEL and w_mod.shape[0] == 1, "single-layer kernel"
    tok = nb * n_lat

    cv = jnp.zeros((8, d), F32).at[:nb].set(c).at[nb].set(c_ctx)
    mod = _modulation(cv, w_mod[0], b_mod[0])

    xc = jnp.concatenate([ctx, x], axis=1)
    sizes = np.cumsum([0, HGRN_W, HGRN_W, HGRN_W, HGRN_W, HGRN_W, ATT_W, KV_W, KV_W, D_MODEL, D_MODEL])
    order = [0, 1, 2, 3, 4, 5, 8, 9, 6, 7]
    w_in_bf = jnp.concatenate([w_in[0][:, sizes[i]:sizes[i + 1]] for i in order], axis=1).astype(BF16)
    p = _in_projection(xc, mod, norm_g[0], w_in_bf, n_ctx)

    cos, sin_a, sin_b = _rope_tables(n_ctx, n_lat)
    qt, kn, vt = _qk_prep(p, cos, sin_a, sin_b, qk_norm_g[0])
    o_a = _attention(qt, kn, vt, n_ctx)
    o_f, o_b = _hgrn_scan(p, hgrn_lb, n_ctx)

    x1, t, ids, wts = _merge(
        o_f, o_b, p, o_a, x, mod, norm_g[0], hgrn_norm_g, w_branch[0, 0].astype(BF16),
        w_branch[0, 1].astype(BF16), w_out[0].astype(BF16), router_w[0].T, router_b[0].reshape(N_EXPERTS, 1),
        n_ctx)

    tt = _largest_tile(n_lat, DISPATCH_TILE, 128)
    n_tt = tok // tt
    rank, counts = _expert_rank(ids, tt)
    counts = counts[:, :, 0]
    run = (counts + RUN_ALIGN - 1) // RUN_ALIGN * RUN_ALIGN
    run_end = jnp.cumsum(run, axis=1)
    stage_off = run_end - run
    expert_rows = jnp.sum(run, axis=0)
    tiles_e = (expert_rows + MOE_TILE - 1) // MOE_TILE
    tile_end = jnp.cumsum(tiles_e)
    starts = (tile_end - tiles_e) * MOE_TILE
    run_dst = starts[None, :] + jnp.cumsum(run, axis=0) - run
    n_tiles = -(-(tok * TOP_K + N_EXPERTS * n_tt * (RUN_ALIGN - 1)) // MOE_TILE) + N_EXPERTS
    tile_idx = jnp.arange(n_tiles, dtype=I32)
    tile_expert = jnp.sum((tile_end[None, :] <= tile_idx[:, None]).astype(I32), axis=1)
    tile_expert = jnp.minimum(tile_expert, N_EXPERTS - 1)
    n_used = tile_end[-1:].astype(I32)
    tile_expert = jnp.where(tile_idx < n_used[0], tile_expert, tile_expert[jnp.maximum(n_used[0] - 1, 0)])
    tile_first = jnp.concatenate([jnp.ones((1,), I32), (tile_expert[1:] != tile_expert[:-1]).astype(I32)])
    e_idx = jnp.arange(N_EXPERTS, dtype=I32)
    used = tiles_e > 0
    e_slot = (jnp.cumsum(used.astype(I32)) - 1) % 2
    later = used[None, :] & (e_idx[None, :] > e_idx[:, None])
    e_next = jnp.min(jnp.where(later, e_idx[None, :], N_EXPERTS), axis=1)
    e_next = jnp.where(e_next == N_EXPERTS, -1, e_next)
    tile_onehot = tile_expert[:, None] == e_idx[None, :]
    tile_slot = jnp.sum(jnp.where(tile_onehot, e_slot[None, :], 0), axis=1).astype(I32)
    next_expert = jnp.sum(jnp.where(tile_onehot, e_next[None, :], 0), axis=1).astype(I32)

    n_stage = -(-(TOP_K * tt + N_EXPERTS * (RUN_ALIGN - 1)) // PERM_ROWS) * PERM_ROWS
    ids_t = ids.reshape(TOP_K, n_tt, tt)
    onehot = ids_t[..., None] == jnp.arange(N_EXPERTS, dtype=I32)
    lp = jnp.sum(jnp.where(onehot, stage_off[None, :, None, :], 0), axis=-1).reshape(TOP_K, tok) + rank
    blk_row = jnp.arange(n_stage // RUN_ALIGN, dtype=I32) * RUN_ALIGN
    blk_expert = jnp.minimum(jnp.sum((run_end[:, None, :] <= blk_row[None, :, None]).astype(I32), axis=-1),
                             N_EXPERTS - 1)
    blk_onehot = blk_expert[..., None] == jnp.arange(N_EXPERTS, dtype=I32)
    blk_dst = jnp.sum(jnp.where(blk_onehot, (run_dst - stage_off)[:, None, :], 0), axis=-1) + blk_row[None, :]
    n_blocks = (run_end[:, -1] // RUN_ALIGN).astype(I32)
    blk_dst = jnp.where(blk_row[None, :] < run_end[:, -1:], blk_dst, 0).astype(I32)[:, None, :]

    n_tail = n_tiles - (tok * TOP_K) // MOE_TILE
    tail = n_used[0] + jnp.arange(n_tail, dtype=I32)
    zero_tiles = jnp.clip(jnp.concatenate([tile_end.astype(I32) - 1, tail]), 0, n_tiles - 1)
    zero_valid = jnp.concatenate([expert_rows % MOE_TILE != 0, tail < n_tiles]).astype(I32)

    xs = _scatter_rows(zero_tiles, zero_valid, n_blocks, blk_dst, lp, t, n_tiles * MOE_TILE, tt, n_stage)
    ys = _experts(xs, tile_expert, tile_first, n_used, tile_slot, next_expert, w_up[0], b_up[0], w_down[0],
                  b_down[0])
    out = _combine(n_blocks, blk_dst, ys, lp.T, wts.T, x1.reshape(tok, d), mod, norm_g[0], tt, n_stage,
                   n_lat // tt)
    return out.reshape(nb, n_lat, d)
```

```python
import functools

import jax
import jax.numpy as jnp
import numpy as np
from jax import lax
from jax.experimental import pallas as pl
from jax.experimental.pallas import tpu as pltpu

F32 = jnp.float32
BF16 = jnp.bfloat16
I32 = jnp.int32

D_MODEL = 1024
GRID_W = 64
N_MOD = 6
EPS = 1e-6
HGRN_HEADS = 8
HGRN_DK = 128
HGRN_W = HGRN_HEADS * HGRN_DK
ATT_HEADS = 8
ATT_KV_HEADS = 2
ATT_GROUPS = ATT_HEADS // ATT_KV_HEADS
HEAD_DIM = 128
ATT_W = ATT_HEADS * HEAD_DIM
KV_W = ATT_KV_HEADS * HEAD_DIM
ROPE_AXIS_DIM = HEAD_DIM // 2
ROPE_THETA = 10000.0
N_EXPERTS = 32
TOP_K = 4
D_EXPERT = 1024
SWIGLU_LIMIT = 7.0
SWIGLU_ALPHA = 1.702

COL_HQ, COL_HI, COL_FF, COL_FB, COL_HG, COL_AQ, COL_GH, COL_GA = (i * 1024 for i in range(8))
COL_K = 8 * 1024
COL_V = COL_K + KV_W
IN_COLS = COL_V + KV_W

V7X_VMEM_BYTES = 64 * 1024 * 1024
LANES = 128

HGRN_CHUNK = 64
HGRN_SUB = 16
MOE_TILE = 256
MERGE_ROWS = 128
DISPATCH_TILE = 512
RUN_ALIGN = 8
PERM_ROWS = 256
WAIT_GROUP_LOG2 = 5


def _vmem_limit(nbytes):
    return int(min(V7X_VMEM_BYTES - 6 * 1024 * 1024, max(nbytes, 32 * 1024 * 1024)))


def _largest_tile(n, cap, mult):
    best = None
    for t in range(mult, min(n, cap) + 1, mult):
        if n % t == 0:
            best = t
    assert best is not None, (n, cap, mult)
    return best


def _rms(x, g):
    return x * lax.rsqrt(jnp.mean(x * x, axis=-1, keepdims=True) + EPS) * g


def _dot(a, b):
    return jnp.dot(a, b, preferred_element_type=F32)


def _dot_nt(a, b):
    return lax.dot_general(a, b, (((1,), (1,)), ((), ())), preferred_element_type=F32)


def _dot_tn(a, b):
    return lax.dot_general(a, b, (((0,), (0,)), ((), ())), preferred_element_type=F32)


def _mod_kernel(cv_ref, w_ref, b_ref, o_ref):
    cv = cv_ref[...]
    s = cv * jax.nn.sigmoid(cv)
    o_ref[...] = jnp.dot(s, w_ref[...], preferred_element_type=F32,
                         precision=lax.Precision.HIGHEST) + b_ref[...]


def _modulation(cv, w_mod, b_mod):
    rows, d = cv.shape
    n = w_mod.shape[1]
    tn = 1024
    return pl.pallas_call(
        _mod_kernel,
        grid=(n // tn,),
        in_specs=[pl.BlockSpec((rows, d), lambda j: (0, 0)),
                  pl.BlockSpec((d, tn), lambda j: (0, j)),
                  pl.BlockSpec((1, tn), lambda j: (0, j))],
        out_specs=pl.BlockSpec((rows, tn), lambda j: (0, j)),
        out_shape=jax.ShapeDtypeStruct((rows, n), F32),
        name="modulation",
    )(cv, w_mod, b_mod.reshape(1, n))


def _inproj_kernel(x_ref, mod_ref, g_ref, w_ref, o_ref, *, tm, n_ctx, n_batch):
    b = pl.program_id(1)
    i = pl.program_id(2)
    x = x_ref[0]
    xn = _rms(x, g_ref[0:1, :])
    row = i * tm + lax.broadcasted_iota(I32, (tm, 1), 0)
    is_ctx = row < n_ctx
    m_lat = mod_ref[pl.ds(b, 1), :]
    m_ctx = mod_ref[n_batch:n_batch + 1, :]
    shift = jnp.where(is_ctx, m_ctx[:, 0:D_MODEL], m_lat[:, 0:D_MODEL])
    scale = jnp.where(is_ctx, m_ctx[:, D_MODEL:2 * D_MODEL], m_lat[:, D_MODEL:2 * D_MODEL])
    u = xn * (1.0 + scale) + shift
    o_ref[0] = _dot(u.astype(BF16), w_ref[...]).astype(BF16)


def _in_projection(xc, mod, norm_g, w_in_bf, n_ctx):
    nb, lc, d = xc.shape
    n = w_in_bf.shape[1]
    tm = _largest_tile(lc, 1056, 16)
    tn = 2176
    assert n % tn == 0
    kern = functools.partial(_inproj_kernel, tm=tm, n_ctx=n_ctx, n_batch=nb)
    vmem = 2 * (tm * d * 4 + d * tn * 2 + tm * tn * 2) + 6 * tm * d * 4
    return pl.pallas_call(
        kern,
        grid=(n // tn, nb, lc // tm),
        in_specs=[pl.BlockSpec((1, tm, d), lambda j, b, i: (b, i, 0)),
                  pl.BlockSpec(mod.shape, lambda j, b, i: (0, 0)),
                  pl.BlockSpec(norm_g.shape, lambda j, b, i: (0, 0)),
                  pl.BlockSpec((d, tn), lambda j, b, i: (0, j))],
        out_specs=pl.BlockSpec((1, tm, tn), lambda j, b, i: (b, i, j)),
        out_shape=jax.ShapeDtypeStruct((nb, lc, n), BF16),
        compiler_params=pltpu.CompilerParams(
            dimension_semantics=("arbitrary", "arbitrary", "arbitrary"),
            vmem_limit_bytes=_vmem_limit(vmem)),
        name="in_projection",
    )(xc, mod, norm_g, w_in_bf)


def _rope(xn, cos, sin_a, sin_b):
    return (xn * cos + pltpu.roll(xn, HEAD_DIM - ROPE_AXIS_DIM // 2, 1) * sin_a
            + pltpu.roll(xn, ROPE_AXIS_DIM // 2, 1) * sin_b)


def _qkprep_kernel(q_ref, k_ref, v_ref, cos_ref, sa_ref, sb_ref, g_ref, qt_ref, ko_ref, vt_ref):
    cos, sa, sb = cos_ref[...], sa_ref[...], sb_ref[...]
    gq, gk = g_ref[0:1, :], g_ref[1:2, :]
    qscale = HEAD_DIM ** -0.5 * np.log2(np.e)
    for h in range(ATT_HEADS):
        hs = slice(h * HEAD_DIM, (h + 1) * HEAD_DIM)
        xn = _rms(q_ref[0, :, hs].astype(F32), gq)
        qt_ref[0, hs, :] = (_rope(xn, cos, sa, sb) * qscale).T.astype(BF16)
    for h in range(ATT_KV_HEADS):
        hs = slice(h * HEAD_DIM, (h + 1) * HEAD_DIM)
        xn = _rms(k_ref[0, :, hs].astype(F32), gk)
        ko_ref[0, :, hs] = _rope(xn, cos, sa, sb).astype(BF16)
        vt_ref[0, hs, :] = v_ref[0, :, hs].astype(F32).T.astype(BF16)


def _qk_prep(p, cos, sin_a, sin_b, qk_norm_g):
    nb, lc, _ = p.shape
    tm = _largest_tile(lc, 512, 128)
    tab = pl.BlockSpec((tm, HEAD_DIM), lambda b, i: (i, 0))
    return pl.pallas_call(
        _qkprep_kernel,
        grid=(nb, lc // tm),
        in_specs=[pl.BlockSpec((1, tm, ATT_W), lambda b, i: (b, i, COL_AQ // ATT_W)),
                  pl.BlockSpec((1, tm, KV_W), lambda b, i: (b, i, COL_K // KV_W)),
                  pl.BlockSpec((1, tm, KV_W), lambda b, i: (b, i, COL_V // KV_W)),
                  tab, tab, tab,
                  pl.BlockSpec(qk_norm_g.shape, lambda b, i: (0, 0))],
        out_specs=[pl.BlockSpec((1, ATT_W, tm), lambda b, i: (b, 0, i)),
                   pl.BlockSpec((1, tm, KV_W), lambda b, i: (b, i, 0)),
                   pl.BlockSpec((1, KV_W, tm), lambda b, i: (b, 0, i))],
        out_shape=[jax.ShapeDtypeStruct((nb, ATT_W, lc), BF16),
                   jax.ShapeDtypeStruct((nb, lc, KV_W), BF16),
                   jax.ShapeDtypeStruct((nb, KV_W, lc), BF16)],
        name="qk_prep",
    )(p, p, p, cos, sin_a, sin_b, qk_norm_g)


def _attn_kernel(qt_ref, k_ref, vt_ref, o_ref, acc_ref, s_ref, *, tq, kc, n_kc):
    acc_ref[...] = jnp.zeros(acc_ref.shape, F32)
    qts = [qt_ref[0, g * HEAD_DIM:(g + 1) * HEAD_DIM, :] for g in range(ATT_GROUPS)]

    def put_scores(c, slot):
        kch = k_ref[0, pl.ds(pl.multiple_of(c * kc, kc), kc), :]
        for g in range(ATT_GROUPS):
            s_ref[slot, g] = _dot(kch, qts[g])

    def softmax_pv(c, slot, carry):
        vch = vt_ref[0, :, pl.ds(pl.multiple_of(c * kc, kc), kc)]
        out = []
        for g in range(ATT_GROUPS):
            m_old, l_old = carry[g]
            s = s_ref[slot, g]
            m_new = jnp.maximum(m_old, jnp.max(s, axis=0, keepdims=True))
            alpha = jnp.exp2(m_old - m_new)
            p = jnp.exp2(s - m_new)
            l_new = l_old * alpha + jnp.sum(p, axis=0, keepdims=True)
            acc_ref[g] = acc_ref[g] * alpha + _dot(vch, p.astype(BF16))
            out.append((m_new, l_new))
        return tuple(out)

    def body(i, carry):
        c = 2 * i
        put_scores(c + 1, 1)
        carry = softmax_pv(c, 0, carry)
        put_scores(c + 2, 0)
        return softmax_pv(c + 1, 1, carry)

    init = tuple((jnp.full((1, tq), -jnp.inf, F32), jnp.zeros((1, tq), F32)) for _ in range(ATT_GROUPS))
    put_scores(0, 0)
    fin = lax.fori_loop(0, (n_kc - 1) // 2, body, init)
    if n_kc % 2 == 0:
        put_scores(n_kc - 1, 1)
        fin = softmax_pv(n_kc - 2, 0, fin)
    fin = softmax_pv(n_kc - 1, (n_kc - 1) % 2, fin)
    for g in range(ATT_GROUPS):
        o_ref[0, :, g * HEAD_DIM:(g + 1) * HEAD_DIM] = (acc_ref[g] / fin[g][1]).T.astype(BF16)


def _attention(qt, kn, vt, n_ctx):
    nb, lc, _ = kn.shape
    n_lat = lc - n_ctx
    tq = _largest_tile(n_lat, 256, 128)
    assert n_ctx % tq == 0
    kc = _largest_tile(lc, 768, 256)
    gw = ATT_GROUPS * HEAD_DIM
    kern = functools.partial(_attn_kernel, tq=tq, kc=kc, n_kc=lc // kc)
    vmem = 8 * lc * HEAD_DIM * 2 + 4 * ATT_GROUPS * kc * tq * 4 + 8 * gw * tq * 4
    return pl.pallas_call(
        kern,
        grid=(nb, ATT_KV_HEADS, n_lat // tq),
        in_specs=[pl.BlockSpec((1, gw, tq), lambda b, h, i: (b, h, n_ctx // tq + i)),
                  pl.BlockSpec((1, lc, HEAD_DIM), lambda b, h, i: (b, 0, h)),
                  pl.BlockSpec((1, HEAD_DIM, lc), lambda b, h, i: (b, h, 0))],
        out_specs=pl.BlockSpec((1, tq, gw), lambda b, h, i: (b, i, h)),
        out_shape=jax.ShapeDtypeStruct((nb, n_lat, ATT_W), BF16),
        scratch_shapes=[pltpu.VMEM((ATT_GROUPS, HEAD_DIM, tq), F32),
                        pltpu.VMEM((2, ATT_GROUPS, kc, tq), F32)],
        compiler_params=pltpu.CompilerParams(
            dimension_semantics=("arbitrary", "arbitrary", "arbitrary"),
            vmem_limit_bytes=_vmem_limit(vmem)),
        name="attention",
    )(qt, kn, vt)


def _hgrn_chunk(q_ref, v_ref, r_ref, lb, reverse):
    c = HGRN_CHUNK
    q = q_ref[0].astype(F32)
    v = v_ref[0]
    r = r_ref[0].astype(F32)
    sig = jax.nn.sigmoid(r)
    f = lb + (1.0 - lb) * sig
    logf = jnp.log(f)
    k = (1.0 - lb) * (1.0 - sig)
    ti = lax.broadcasted_iota(I32, (c, c), 0)
    si = lax.broadcasted_iota(I32, (c, c), 1)
    tri = jnp.where((si >= ti) if reverse else (si <= ti), 1.0, 0.0).astype(BF16)
    hi = logf.astype(BF16)
    lo = (logf - hi.astype(F32)).astype(BF16)
    bcum = _dot(tri, hi) + _dot(tri, lo)
    b_end = bcum[0:1, :] if reverse else bcum[c - 1:c, :]
    qt = (q * jnp.exp(bcum)).astype(BF16)
    kt = (k * jnp.exp(b_end - bcum)).astype(BF16)
    dec = jnp.exp(b_end)

    nsub = c // HGRN_SUB
    qp, kp, cols = [], [], []
    for blk in range(nsub):
        rs = slice(blk * HGRN_SUB, (blk + 1) * HGRN_SUB)
        cs = slice(blk * HGRN_SUB, c) if reverse else slice(0, (blk + 1) * HGRN_SUB)
        mid = blk * HGRN_SUB + HGRN_SUB // 2
        ref = bcum[mid:mid + 1, :]
        qp.append((q[rs] * jnp.exp(bcum[rs] - ref)).astype(BF16))
        kp.append((k[cs] * jnp.exp(ref - bcum[cs])).astype(BF16))
        cols.append(cs)

    keep = []
    for blk in range(nsub):
        n_cols = cols[blk].stop - cols[blk].start
        trow = blk * HGRN_SUB + lax.broadcasted_iota(I32, (HGRN_SUB, n_cols), 0)
        scol = cols[blk].start + lax.broadcasted_iota(I32, (HGRN_SUB, n_cols), 1)
        keep.append((scol >= trow) if reverse else (scol <= trow))
    return dict(v=v, qt=qt, kt=kt, dec=dec, qp=qp, kp=kp, cols=cols, keep=keep)


def _hgrn_emit(dirs, s_refs, o_refs):
    nsub = HGRN_CHUNK // HGRN_SUB
    heads = [slice(h * HGRN_DK, (h + 1) * HGRN_DK) for h in range(HGRN_HEADS)]
    inter = []
    for d, s_ref in zip(dirs, s_refs):
        row = []
        for h, hs in enumerate(heads):
            st = s_ref[h]
            row.append(_dot_nt(d["qt"][:, hs], st.astype(BF16)))
            s_ref[h] = st * d["dec"][:, hs] + _dot_tn(d["v"][:, hs], d["kt"][:, hs])
        inter.append(row)
    scores = [[[_dot_nt(d["qp"][blk][:, hs], d["kp"][blk][:, hs]) for blk in range(nsub)] for hs in heads]
              for d in dirs]
    for d, sc, it, o_ref in zip(dirs, scores, inter, o_refs):
        outs = []
        for h, hs in enumerate(heads):
            parts = [_dot(jnp.where(d["keep"][blk], sc[h][blk], 0.0).astype(BF16), d["v"][d["cols"][blk], hs])
                     for blk in range(nsub)]
            outs.append(it[h] + jnp.concatenate(parts, axis=0))
        o_ref[0] = jnp.concatenate(outs, axis=1).astype(BF16)


def _hgrn_kernel(qf_ref, vf_ref, rf_ref, qb_ref, vb_ref, rb_ref, lb_ref, of_ref, ob_ref, sf_ref, sb_ref):
    @pl.when(pl.program_id(1) == 0)
    def _():
        sf_ref[...] = jnp.zeros(sf_ref.shape, F32)
        sb_ref[...] = jnp.zeros(sb_ref.shape, F32)

    n_layers = lb_ref.shape[0] // 2

    def lower_bound(direction):
        rows = [lb_ref[direction * n_layers + l:direction * n_layers + l + 1, :] for l in range(n_layers)]
        amax = functools.reduce(jnp.maximum, rows)
        e = [jnp.exp(a - amax) for a in rows]
        return e[0] / functools.reduce(lambda u, w: u + w, e)

    fwd = _hgrn_chunk(qf_ref, vf_ref, rf_ref, lower_bound(0), False)
    bwd = _hgrn_chunk(qb_ref, vb_ref, rb_ref, lower_bound(1), True)
    _hgrn_emit([fwd, bwd], [sf_ref, sb_ref], [of_ref, ob_ref])


def _hgrn_scan(p, hgrn_lb, n_ctx):
    nb, lc, _ = p.shape
    c = HGRN_CHUNK
    assert n_ctx % c == 0 and lc % c == 0
    n_chunks = lc // c
    ctx_chunks = n_ctx // c

    def fwd(col):
        return lambda b, s: (b, s, col)

    def bwd_chunk(s):
        return jnp.where(s < ctx_chunks, ctx_chunks - 1 - s, n_chunks - 1 + ctx_chunks - s)

    def bwd(col):
        return lambda b, s: (b, bwd_chunk(s), col)

    blk = (1, c, HGRN_W)
    lb2 = hgrn_lb.reshape(-1, HGRN_W)
    return pl.pallas_call(
        _hgrn_kernel,
        grid=(nb, n_chunks),
        in_specs=[pl.BlockSpec(blk, fwd(COL_HQ // HGRN_W)), pl.BlockSpec(blk, fwd(COL_HI // HGRN_W)),
                  pl.BlockSpec(blk, fwd(COL_FF // HGRN_W)),
                  pl.BlockSpec(blk, bwd(COL_HQ // HGRN_W)), pl.BlockSpec(blk, bwd(COL_HI // HGRN_W)),
                  pl.BlockSpec(blk, bwd(COL_FB // HGRN_W)),
                  pl.BlockSpec(lb2.shape, lambda b, s: (0, 0))],
        out_specs=[pl.BlockSpec(blk, fwd(0)), pl.BlockSpec(blk, bwd(0))],
        out_shape=[jax.ShapeDtypeStruct((nb, lc, HGRN_W), BF16)] * 2,
        scratch_shapes=[pltpu.VMEM((HGRN_HEADS, HGRN_DK, HGRN_DK), F32)] * 2,
        compiler_params=pltpu.CompilerParams(dimension_semantics=("arbitrary", "arbitrary")),
        name="hgrn_scan",
    )(p, p, p, p, p, p, lb2)


def _merge_kernel(of_ref, ob_ref, hg_ref, gh_ref, ga_ref, oa_ref, x_ref, mod_ref, ng_ref, hng_ref,
                  wb0_ref, wb1_ref, wo_ref, rw_ref, rb_ref,
                  x1_ref, t_ref, ids_ref, wts_ref):
    b = pl.program_id(0)
    m = mod_ref[pl.ds(b, 1), :]
    gate_mix = m[:, 2 * D_MODEL:3 * D_MODEL]
    shift_f = m[:, 3 * D_MODEL:4 * D_MODEL]
    scale_f = m[:, 4 * D_MODEL:5 * D_MODEL]
    hng = hng_ref[...]
    rw = rw_ref[...]
    rw_hi = rw.astype(BF16)
    rw_lo = (rw - rw_hi.astype(F32)).astype(BF16)
    groups = [slice(r, r + MERGE_ROWS) for r in range(0, of_ref.shape[1], MERGE_ROWS)]

    branch = []
    for rs in groups:
        o = of_ref[0, rs, :].astype(F32) + ob_ref[0, rs, :].astype(F32)
        o_h = jnp.concatenate(
            [_rms(o[:, h * HGRN_DK:(h + 1) * HGRN_DK], hng) for h in range(HGRN_HEADS)], axis=1)
        g_raw = hg_ref[0, rs, :].astype(F32)
        o_h = o_h * (g_raw * jax.nn.sigmoid(g_raw))
        branch.append((_dot(o_h.astype(BF16), wb0_ref[...]), _dot(oa_ref[0, rs, :], wb1_ref[...])))
    mixed = []
    for rs, (y_h, y_a) in zip(groups, branch):
        y = (jax.nn.sigmoid(gh_ref[0, rs, :].astype(F32)) * y_h
             + jax.nn.sigmoid(ga_ref[0, rs, :].astype(F32)) * y_a)
        mixed.append(_dot(y.astype(BF16), wo_ref[...]))
    logits_all = []
    for rs, y in zip(groups, mixed):
        x1 = x_ref[0, rs, :] + gate_mix * _rms(y, ng_ref[1:2, :])
        x1_ref[0, rs, :] = x1
        t = _rms(x1, ng_ref[2:3, :]) * (1.0 + scale_f) + shift_f
        t_ref[rs, :] = t
        t_hi = t.astype(BF16)
        t_lo = (t - t_hi.astype(F32)).astype(BF16)
        logits_all.append(_dot_nt(rw_hi, t_hi) + _dot_nt(rw_hi, t_lo) + _dot_nt(rw_lo, t_hi) + rb_ref[...])
    for rs, logits in zip(groups, logits_all):
        eidx = lax.broadcasted_iota(I32, logits.shape, 0).astype(F32)
        vals = []
        for j in range(TOP_K):
            mx = jnp.max(logits, axis=0, keepdims=True)
            idx = jnp.min(jnp.where(logits == mx, eidx, float(N_EXPERTS)), axis=0, keepdims=True)
            ids_ref[j:j + 1, rs] = idx.astype(I32)
            vals.append(mx)
            logits = jnp.where(eidx == idx, -jnp.inf, logits)
        ex = [jnp.exp(vj - vals[0]) for vj in vals]
        den = ex[0] + ex[1] + ex[2] + ex[3]
        for j in range(TOP_K):
            wts_ref[j:j + 1, rs] = ex[j] / den


def _merge(o_f, o_b, p, o_a, x, mod, norm_g, hgrn_norm_g, wb0, wb1, wo, router_wt, router_b, n_ctx):
    nb, n_lat, d = x.shape
    tm = _largest_tile(n_lat, 256, 128)
    assert n_ctx % tm == 0
    off = n_ctx // tm
    nt = n_lat // tm
    row = (1, tm, d)

    def pcol(col):
        return pl.BlockSpec(row, lambda b, i: (b, off + i, col // d))

    def full(a):
        return pl.BlockSpec(a.shape, lambda b, i: (0,) * a.ndim)

    tok = nb * n_lat
    return pl.pallas_call(
        _merge_kernel,
        grid=(nb, nt),
        in_specs=[pl.BlockSpec(row, lambda b, i: (b, off + i, 0)),
                  pl.BlockSpec(row, lambda b, i: (b, off + i, 0)),
                  pcol(COL_HG), pcol(COL_GH), pcol(COL_GA),
                  pl.BlockSpec(row, lambda b, i: (b, i, 0)),
                  pl.BlockSpec(row, lambda b, i: (b, i, 0)),
                  full(mod), full(norm_g), full(hgrn_norm_g), full(wb0), full(wb1), full(wo),
                  full(router_wt), full(router_b)],
        out_specs=[pl.BlockSpec(row, lambda b, i: (b, i, 0)),
                   pl.BlockSpec((tm, d), lambda b, i: (b * nt + i, 0)),
                   pl.BlockSpec((TOP_K, tm), lambda b, i: (0, b * nt + i)),
                   pl.BlockSpec((TOP_K, tm), lambda b, i: (0, b * nt + i))],
        out_shape=[jax.ShapeDtypeStruct((nb, n_lat, d), F32),
                   jax.ShapeDtypeStruct((tok, d), F32),
                   jax.ShapeDtypeStruct((TOP_K, tok), I32),
                   jax.ShapeDtypeStruct((TOP_K, tok), F32)],
        compiler_params=pltpu.CompilerParams(
            dimension_semantics=("arbitrary", "arbitrary"),
            vmem_limit_bytes=_vmem_limit(40 * 1024 * 1024)),
        name="merge_router",
    )(o_f, o_b, p, p, p, o_a, x, mod, norm_g, hgrn_norm_g, wb0, wb1, wo, router_wt, router_b)


def _rank_kernel(ids_ref, rank_ref, cnt_ref, *, tt):
    eidx = lax.broadcasted_iota(I32, (N_EXPERTS, tt), 0)
    si = lax.broadcasted_iota(I32, (tt, tt), 0)
    ti = lax.broadcasted_iota(I32, (tt, tt), 1)
    before = jnp.where(si < ti, 1.0, 0.0).astype(BF16)
    seen = jnp.zeros((N_EXPERTS, 1), F32)
    for j in range(TOP_K):
        onehot = eidx == ids_ref[j:j + 1, :]
        oh = jnp.where(onehot, 1.0, 0.0)
        earlier = _dot(oh.astype(BF16), before)
        rank = jnp.sum(jnp.where(onehot, seen + earlier, 0.0), axis=0, keepdims=True)
        rank_ref[j:j + 1, :] = rank.astype(I32)
        seen = seen + jnp.sum(oh, axis=1, keepdims=True)
    cnt_ref[0] = seen.astype(I32)


def _expert_rank(ids, tt):
    _, tok = ids.shape
    return pl.pallas_call(
        functools.partial(_rank_kernel, tt=tt),
        grid=(tok // tt,),
        in_specs=[pl.BlockSpec((TOP_K, tt), lambda i: (0, i))],
        out_specs=[pl.BlockSpec((TOP_K, tt), lambda i: (0, i)),
                   pl.BlockSpec((1, N_EXPERTS, 1), lambda i: (i, 0, 0))],
        out_shape=[jax.ShapeDtypeStruct((TOP_K, tok), I32),
                   jax.ShapeDtypeStruct((tok // tt, N_EXPERTS, 1), I32)],
        compiler_params=pltpu.CompilerParams(dimension_semantics=("arbitrary",)),
        name="expert_rank",
    )(ids)


def _start_blocks(n_blocks, make_copy):
    def start(b, carry):
        make_copy(b).start()
        return carry

    lax.fori_loop(0, n_blocks, start, 0)


def _wait_blocks(n_blocks, make_copy, make_group_copy):
    def wait_group(g, carry):
        make_group_copy().wait()
        return carry

    def wait_one(b, carry):
        make_copy(0).wait()
        return carry

    lax.fori_loop(0, lax.shift_right_logical(n_blocks, WAIT_GROUP_LOG2), wait_group, 0)
    lax.fori_loop(0, jnp.bitwise_and(n_blocks, (1 << WAIT_GROUP_LOG2) - 1), wait_one, 0)


def _scatter_kernel(zt_ref, zvalid_ref, nblk_ref, bdst_ref, lp_ref, t_ref, xs_ref, stage, zero_buf, sem, zsem,
                    *, tt, n_stage):
    @pl.when(pl.program_id(0) == 0)
    def _():
        zero_buf[...] = jnp.zeros(zero_buf.shape, F32)

        def zero_copy(k):
            return pltpu.make_async_copy(zero_buf, xs_ref.at[pl.ds(zt_ref[k] * MOE_TILE, MOE_TILE)], zsem)

        def start(k, carry):
            @pl.when(zvalid_ref[k] == 1)
            def _():
                zero_copy(k).start()
            return carry

        def wait(k, carry):
            @pl.when(zvalid_ref[k] == 1)
            def _():
                zero_copy(k).wait()
            return carry

        lax.fori_loop(0, zt_ref.shape[0], start, 0)
        lax.fori_loop(0, zt_ref.shape[0], wait, 0)

    i = pl.program_id(0)
    slot = i % 2
    t_bf = t_ref[...].astype(BF16)
    lps = [lp_ref[j:j + 1, :] for j in range(TOP_K)]
    for rc in range(n_stage // PERM_ROWS):
        row = rc * PERM_ROWS + lax.broadcasted_iota(I32, (PERM_ROWS, tt), 0)
        sel = jnp.where(row == lps[TOP_K - 1], 1.0, 0.0)
        for j in range(TOP_K - 2, -1, -1):
            sel = jnp.where(row == lps[j], 1.0, sel)
        stage[slot, rc * PERM_ROWS:(rc + 1) * PERM_ROWS, :] = _dot(sel.astype(BF16), t_bf)

    def copy(b):
        src = pl.multiple_of(b * RUN_ALIGN, RUN_ALIGN)
        dst = pl.multiple_of(bdst_ref[0, 0, b], RUN_ALIGN)
        return pltpu.make_async_copy(stage.at[slot, pl.ds(src, RUN_ALIGN)], xs_ref.at[pl.ds(dst, RUN_ALIGN)],
                                     sem.at[slot])

    def waiter(s):
        rows = RUN_ALIGN << WAIT_GROUP_LOG2
        return (lambda b: pltpu.make_async_copy(stage.at[s, pl.ds(0, RUN_ALIGN)],
                                                xs_ref.at[pl.ds(0, RUN_ALIGN)], sem.at[s]),
                lambda: pltpu.make_async_copy(stage.at[s, pl.ds(0, rows)], xs_ref.at[pl.ds(0, rows)], sem.at[s]))

    _start_blocks(nblk_ref[i], copy)

    @pl.when(i > 0)
    def _():
        _wait_blocks(nblk_ref[i - 1], *waiter(1 - slot))

    @pl.when(i == pl.num_programs(0) - 1)
    def _():
        _wait_blocks(nblk_ref[i], *waiter(slot))


def _scatter_rows(zero_tiles, zero_valid, n_blocks, block_rows, lp, t, n_rows, tt, n_stage):
    tok, d = t.shape
    nb = block_rows.shape[2]
    return pl.pallas_call(
        functools.partial(_scatter_kernel, tt=tt, n_stage=n_stage),
        grid_spec=pltpu.PrefetchScalarGridSpec(
            num_scalar_prefetch=3,
            grid=(tok // tt,),
            in_specs=[pl.BlockSpec((1, 1, nb), lambda i, *_: (i, 0, 0), memory_space=pltpu.SMEM),
                      pl.BlockSpec((TOP_K, tt), lambda i, *_: (0, i)),
                      pl.BlockSpec((tt, d), lambda i, *_: (i, 0))],
            out_specs=pl.BlockSpec(memory_space=pl.ANY),
            scratch_shapes=[pltpu.VMEM((2, n_stage, d), F32), pltpu.VMEM((MOE_TILE, d), F32),
                            pltpu.SemaphoreType.DMA((2,)), pltpu.SemaphoreType.DMA(())]),
        out_shape=jax.ShapeDtypeStruct((n_rows, d), F32),
        compiler_params=pltpu.CompilerParams(
            dimension_semantics=("arbitrary",), vmem_limit_bytes=_vmem_limit(48 * 1024 * 1024)),
        name="moe_scatter",
    )(zero_tiles, zero_valid, n_blocks, block_rows, lp, t)


def _expert_kernel(te_ref, first_ref, nused_ref, slot_ref, next_ref, xs_ref, wu_hbm, bu_ref, wd_hbm, bd_ref,
                   ys_ref, wu32, wd32, wu_bf, wd_bf, sem_u, sem_d):
    i = pl.program_id(0)

    def weight_copies(e, s):
        return (pltpu.make_async_copy(wu_hbm.at[e], wu32.at[s], sem_u.at[s]),
                pltpu.make_async_copy(wd_hbm.at[e], wd32.at[s], sem_d.at[s]))

    @pl.when(i >= nused_ref[0])
    def _():
        ys_ref[...] = jnp.zeros(ys_ref.shape, F32)

    @pl.when(i < nused_ref[0])
    def _():
        @pl.when(first_ref[i] == 1)
        def _():
            s = slot_ref[i]

            @pl.when(i == 0)
            def _():
                for cp in weight_copies(te_ref[0], 0):
                    cp.start()

            for cp in weight_copies(te_ref[i], s):
                cp.wait()
            wu_bf[...] = wu32[s].astype(BF16)
            wd_bf[...] = wd32[s].astype(BF16)

            @pl.when(next_ref[i] >= 0)
            def _():
                for cp in weight_copies(next_ref[i], 1 - s):
                    cp.start()

        h = _dot(xs_ref[...].astype(BF16), wu_bf[...]) + bu_ref[0]
        glu = jnp.minimum(h[:, :D_EXPERT], SWIGLU_LIMIT)
        lin = jnp.clip(h[:, D_EXPERT:], -SWIGLU_LIMIT, SWIGLU_LIMIT)
        a = glu * jax.nn.sigmoid(SWIGLU_ALPHA * glu) * (lin + 1.0)
        ys_ref[...] = _dot(a.astype(BF16), wd_bf[...]) + bd_ref[0]


def _experts(xs, tile_expert, tile_first, n_used, tile_slot, next_expert, w_up, b_up, w_down, b_down):
    n_rows, d = xs.shape
    tm = MOE_TILE
    n_tiles = n_rows // tm

    def tile(i, te, first, nused, *_):
        return (jnp.minimum(i, nused[0] - 1), 0)

    def bsel(i, te, *_):
        return (te[i], 0, 0)

    vmem = 2 * (d * 2 * D_EXPERT * 4 + D_EXPERT * d * 4) + d * 2 * D_EXPERT * 2 + D_EXPERT * d * 2 \
        + 4 * tm * d * 4 + 6 * tm * 2 * D_EXPERT * 4
    return pl.pallas_call(
        _expert_kernel,
        grid_spec=pltpu.PrefetchScalarGridSpec(
            num_scalar_prefetch=5,
            grid=(n_tiles,),
            in_specs=[pl.BlockSpec((tm, d), tile),
                      pl.BlockSpec(memory_space=pl.ANY),
                      pl.BlockSpec((1, 1, 2 * D_EXPERT), bsel),
                      pl.BlockSpec(memory_space=pl.ANY),
                      pl.BlockSpec((1, 1, d), bsel)],
            out_specs=pl.BlockSpec((tm, d), lambda i, *_: (i, 0)),
            scratch_shapes=[pltpu.VMEM((2, d, 2 * D_EXPERT), F32), pltpu.VMEM((2, D_EXPERT, d), F32),
                            pltpu.VMEM((d, 2 * D_EXPERT), BF16), pltpu.VMEM((D_EXPERT, d), BF16),
                            pltpu.SemaphoreType.DMA((2,)), pltpu.SemaphoreType.DMA((2,))]),
        out_shape=jax.ShapeDtypeStruct((n_rows, d), F32),
        compiler_params=pltpu.CompilerParams(
            dimension_semantics=("arbitrary",), vmem_limit_bytes=_vmem_limit(vmem)),
        name="moe_experts",
    )(tile_expert, tile_first, n_used, tile_slot, next_expert, xs, w_up, b_up.reshape(N_EXPERTS, 1, -1), w_down,
      b_down.reshape(N_EXPERTS, 1, -1))


def _combine_kernel(nblk_ref, bsrc_ref, ys_ref, lp_ref, w_ref, x1_ref, mod_ref, ng_ref, o_ref, stage, sem,
                    *, tt, n_stage, tiles_per_batch):
    i = pl.program_id(0)
    slot = i % 2

    def fetch(tile, s):
        def copy(b):
            src = pl.multiple_of(bsrc_ref[tile, b], RUN_ALIGN)
            dst = pl.multiple_of(b * RUN_ALIGN, RUN_ALIGN)
            return pltpu.make_async_copy(ys_ref.at[pl.ds(src, RUN_ALIGN)], stage.at[s, pl.ds(dst, RUN_ALIGN)],
                                         sem.at[s])
        return copy

    def group_copy():
        rows = RUN_ALIGN << WAIT_GROUP_LOG2
        return pltpu.make_async_copy(ys_ref.at[pl.ds(0, rows)], stage.at[slot, pl.ds(0, rows)], sem.at[slot])

    @pl.when(i == 0)
    def _():
        stage[...] = jnp.zeros(stage.shape, F32)
        _start_blocks(nblk_ref[0], fetch(0, 0))

    @pl.when(i + 1 < pl.num_programs(0))
    def _():
        _start_blocks(nblk_ref[i + 1], fetch(i + 1, 1 - slot))

    _wait_blocks(nblk_ref[i], fetch(i, slot), group_copy)

    lp = [jnp.broadcast_to(lp_ref[:, j:j + 1], (tt, PERM_ROWS)) for j in range(TOP_K)]
    w = [jnp.broadcast_to(w_ref[:, j:j + 1], (tt, PERM_ROWS)) for j in range(TOP_K)]
    f = jnp.zeros((tt, D_MODEL), F32)
    for lc in range(n_stage // PERM_ROWS):
        col = lc * PERM_ROWS + lax.broadcasted_iota(I32, (tt, PERM_ROWS), 1)
        wp = jnp.where(col == lp[TOP_K - 1], w[TOP_K - 1], 0.0)
        for j in range(TOP_K - 2, -1, -1):
            wp = jnp.where(col == lp[j], w[j], wp)
        f = f + _dot(wp.astype(BF16), stage[slot, lc * PERM_ROWS:(lc + 1) * PERM_ROWS, :].astype(BF16))
    b = i // tiles_per_batch
    gate_f = mod_ref[pl.ds(b, 1), 5 * D_MODEL:6 * D_MODEL]
    o_ref[...] = x1_ref[...] + gate_f * _rms(f, ng_ref[3:4, :])


def _combine(n_blocks, block_rows, ys, lp_t, wts_t, x1, mod, norm_g, tt, n_stage, tiles_per_batch):
    tok, d = x1.shape
    kern = functools.partial(_combine_kernel, tt=tt, n_stage=n_stage, tiles_per_batch=tiles_per_batch)
    return pl.pallas_call(
        kern,
        grid_spec=pltpu.PrefetchScalarGridSpec(
            num_scalar_prefetch=2,
            grid=(tok // tt,),
            in_specs=[pl.BlockSpec(memory_space=pl.ANY),
                      pl.BlockSpec((tt, TOP_K), lambda i, *_: (i, 0)),
                      pl.BlockSpec((tt, TOP_K), lambda i, *_: (i, 0)),
                      pl.BlockSpec((tt, d), lambda i, *_: (i, 0)),
                      pl.BlockSpec(mod.shape, lambda i, *_: (0, 0)),
                      pl.BlockSpec(norm_g.shape, lambda i, *_: (0, 0))],
            out_specs=pl.BlockSpec((tt, d), lambda i, *_: (i, 0)),
            scratch_shapes=[pltpu.VMEM((2, n_stage, d), F32), pltpu.SemaphoreType.DMA((2,))]),
        out_shape=jax.ShapeDtypeStruct((tok, d), F32),
        compiler_params=pltpu.CompilerParams(
            dimension_semantics=("arbitrary",), vmem_limit_bytes=_vmem_limit(48 * 1024 * 1024)),
        name="moe_combine",
    )(n_blocks, block_rows[:, 0, :], ys, lp_t, wts_t, x1, mod, norm_g)


def _rope_tables(n_ctx, n_lat):
    inv = ROPE_THETA ** (-np.arange(0, ROPE_AXIS_DIM, 2, dtype=np.float32) / ROPE_AXIS_DIM)
    idx = jnp.arange(n_lat, dtype=I32)
    rows = (idx // GRID_W).astype(F32)
    cols = (idx % GRID_W).astype(F32)
    half = ROPE_AXIS_DIM // 2
    inv = jnp.asarray(inv, F32)
    ang_r = rows[:, None] * inv[None, :]
    ang_c = cols[:, None] * inv[None, :]
    ang = jnp.concatenate([ang_r, ang_r, ang_c, ang_c], axis=1)
    cos, sin = jnp.cos(ang), jnp.sin(ang)
    lane = np.arange(HEAD_DIM)
    first = jnp.asarray((lane % ROPE_AXIS_DIM) < half)
    sin_a = jnp.where(first, -sin, 0.0)
    sin_b = jnp.where(first, 0.0, sin)
    pad1 = jnp.ones((n_ctx, HEAD_DIM), F32)
    pad0 = jnp.zeros((n_ctx, HEAD_DIM), F32)
    return (jnp.concatenate([pad1, cos], axis=0), jnp.concatenate([pad0, sin_a], axis=0),
            jnp.concatenate([pad0, sin_b], axis=0))


def kernel(x, c, ctx, c_ctx, w_mod, b_mod, norm_g, w_in, hgrn_lb, hgrn_norm_g, qk_norm_g, w_branch, w_out,
           router_w, router_b, w_up, b_up, w_down, b_down):
    nb, n_lat, d = x.shape
    n_ctx = ctx.shape[1]
    assert d == D_MODEL and w_mod.shape[0] == 1, "single-layer kernel"
    tok = nb * n_lat

    cv = jnp.zeros((8, d), F32).at[:nb].set(c).at[nb].set(c_ctx)
    mod = _modulation(cv, w_mod[0], b_mod[0])

    xc = jnp.concatenate([ctx, x], axis=1)
    sizes = np.cumsum([0, HGRN_W, HGRN_W, HGRN_W, HGRN_W, HGRN_W, ATT_W, KV_W, KV_W, D_MODEL, D_MODEL])
    order = [0, 1, 2, 3, 4, 5, 8, 9, 6, 7]
    w_in_bf = jnp.concatenate([w_in[0][:, sizes[i]:sizes[i + 1]] for i in order], axis=1).astype(BF16)
    p = _in_projection(xc, mod, norm_g[0], w_in_bf, n_ctx)

    cos, sin_a, sin_b = _rope_tables(n_ctx, n_lat)
    qt, kn, vt = _qk_prep(p, cos, sin_a, sin_b, qk_norm_g[0])
    o_a = _attention(qt, kn, vt, n_ctx)
    o_f, o_b = _hgrn_scan(p, hgrn_lb, n_ctx)

    x1, t, ids, wts = _merge(
        o_f, o_b, p, o_a, x, mod, norm_g[0], hgrn_norm_g, w_branch[0, 0].astype(BF16),
        w_branch[0, 1].astype(BF16), w_out[0].astype(BF16), router_w[0].T, router_b[0].reshape(N_EXPERTS, 1),
        n_ctx)

    tt = _largest_tile(n_lat, DISPATCH_TILE, 128)
    n_tt = tok // tt
    rank, counts = _expert_rank(ids, tt)
    counts = counts[:, :, 0]
    run = (counts + RUN_ALIGN - 1) // RUN_ALIGN * RUN_ALIGN
    run_end = jnp.cumsum(run, axis=1)
    stage_off = run_end - run
    expert_rows = jnp.sum(run, axis=0)
    tiles_e = (expert_rows + MOE_TILE - 1) // MOE_TILE
    tile_end = jnp.cumsum(tiles_e)
    starts = (tile_end - tiles_e) * MOE_TILE
    run_dst = starts[None, :] + jnp.cumsum(run, axis=0) - run
    n_tiles = -(-(tok * TOP_K + N_EXPERTS * n_tt * (RUN_ALIGN - 1)) // MOE_TILE) + N_EXPERTS
    tile_idx = jnp.arange(n_tiles, dtype=I32)
    tile_expert = jnp.sum((tile_end[None, :] <= tile_idx[:, None]).astype(I32), axis=1)
    tile_expert = jnp.minimum(tile_expert, N_EXPERTS - 1)
    n_used = tile_end[-1:].astype(I32)
    tile_expert = jnp.where(tile_idx < n_used[0], tile_expert, tile_expert[jnp.maximum(n_used[0] - 1, 0)])
    tile_first = jnp.concatenate([jnp.ones((1,), I32), (tile_expert[1:] != tile_expert[:-1]).astype(I32)])
    e_idx = jnp.arange(N_EXPERTS, dtype=I32)
    used = tiles_e > 0
    e_slot = (jnp.cumsum(used.astype(I32)) - 1) % 2
    later = used[None, :] & (e_idx[None, :] > e_idx[:, None])
    e_next = jnp.min(jnp.where(later, e_idx[None, :], N_EXPERTS), axis=1)
    e_next = jnp.where(e_next == N_EXPERTS, -1, e_next)
    tile_onehot = tile_expert[:, None] == e_idx[None, :]
    tile_slot = jnp.sum(jnp.where(tile_onehot, e_slot[None, :], 0), axis=1).astype(I32)
    next_expert = jnp.sum(jnp.where(tile_onehot, e_next[None, :], 0), axis=1).astype(I32)

    n_stage = -(-(TOP_K * tt + N_EXPERTS * (RUN_ALIGN - 1)) // PERM_ROWS) * PERM_ROWS
    ids_t = ids.reshape(TOP_K, n_tt, tt)
    onehot = ids_t[..., None] == jnp.arange(N_EXPERTS, dtype=I32)
    lp = jnp.sum(jnp.where(onehot, stage_off[None, :, None, :], 0), axis=-1).reshape(TOP_K, tok) + rank
    blk_row = jnp.arange(n_stage // RUN_ALIGN, dtype=I32) * RUN_ALIGN
    blk_expert = jnp.minimum(jnp.sum((run_end[:, None, :] <= blk_row[None, :, None]).astype(I32), axis=-1),
                             N_EXPERTS - 1)
    blk_onehot = blk_expert[..., None] == jnp.arange(N_EXPERTS, dtype=I32)
    blk_dst = jnp.sum(jnp.where(blk_onehot, (run_dst - stage_off)[:, None, :], 0), axis=-1) + blk_row[None, :]
    n_blocks = (run_end[:, -1] // RUN_ALIGN).astype(I32)
    blk_dst = jnp.where(blk_row[None, :] < run_end[:, -1:], blk_dst, 0).astype(I32)[:, None, :]

    n_tail = n_tiles - (tok * TOP_K) // MOE_TILE
    tail = n_used[0] + jnp.arange(n_tail, dtype=I32)
    zero_tiles = jnp.clip(jnp.concatenate([tile_end.astype(I32) - 1, tail]), 0, n_tiles - 1)
    zero_valid = jnp.concatenate([expert_rows % MOE_TILE != 0, tail < n_tiles]).astype(I32)

    xs = _scatter_rows(zero_tiles, zero_valid, n_blocks, blk_dst, lp, t, n_tiles * MOE_TILE, tt, n_stage)
    ys = _experts(xs, tile_expert, tile_first, n_used, tile_slot, next_expert, w_up[0], b_up[0], w_down[0],
                  b_down[0])
    out = _combine(n_blocks, blk_dst, ys, lp.T, wts.T, x1.reshape(tok, d), mod, norm_g[0], tt, n_stage,
                   n_lat // tt)
    return out.reshape(nb, n_lat, d)
```

```python
import functools

import jax
import jax.numpy as jnp
import numpy as np
from jax import lax
from jax.experimental import pallas as pl
from jax.experimental.pallas import tpu as pltpu

F32 = jnp.float32
BF16 = jnp.bfloat16
I32 = jnp.int32

D_MODEL = 1024
GRID_W = 64
N_MOD = 6
EPS = 1e-6
HGRN_HEADS = 8
HGRN_DK = 128
HGRN_W = HGRN_HEADS * HGRN_DK
ATT_HEADS = 8
ATT_KV_HEADS = 2
ATT_GROUPS = ATT_HEADS // ATT_KV_HEADS
HEAD_DIM = 128
ATT_W = ATT_HEADS * HEAD_DIM
KV_W = ATT_KV_HEADS * HEAD_DIM
ROPE_AXIS_DIM = HEAD_DIM // 2
ROPE_THETA = 10000.0
N_EXPERTS = 32
TOP_K = 4
D_EXPERT = 1024
SWIGLU_LIMIT = 7.0
SWIGLU_ALPHA = 1.702

COL_HQ, COL_HI, COL_FF, COL_FB, COL_HG, COL_AQ, COL_GH, COL_GA = (i * 1024 for i in range(8))
COL_K = 8 * 1024
COL_V = COL_K + KV_W
IN_COLS = COL_V + KV_W

V7X_VMEM_BYTES = 64 * 1024 * 1024
LANES = 128

HGRN_CHUNK = 64
HGRN_SUB = 16
MOE_TILE = 256
MERGE_ROWS = 128
INPROJ_CHAINS = 2
DISPATCH_TILE = 512
RUN_ALIGN = 8
PERM_ROWS = 256
WAIT_GROUP_LOG2 = 5


def _vmem_limit(nbytes):
    return int(min(V7X_VMEM_BYTES - 6 * 1024 * 1024, max(nbytes, 32 * 1024 * 1024)))


def _largest_tile(n, cap, mult):
    best = None
    for t in range(mult, min(n, cap) + 1, mult):
        if n % t == 0:
            best = t
    assert best is not None, (n, cap, mult)
    return best


def _rms(x, g):
    return x * lax.rsqrt(jnp.mean(x * x, axis=-1, keepdims=True) + EPS) * g


def _dot(a, b):
    return jnp.dot(a, b, preferred_element_type=F32)


def _dot_nt(a, b):
    return lax.dot_general(a, b, (((1,), (1,)), ((), ())), preferred_element_type=F32)


def _dot_tn(a, b):
    return lax.dot_general(a, b, (((0,), (0,)), ((), ())), preferred_element_type=F32)


def _mod_kernel(cv_ref, w_ref, b_ref, o_ref):
    cv = cv_ref[...]
    s = cv * jax.nn.sigmoid(cv)
    o_ref[...] = jnp.dot(s, w_ref[...], preferred_element_type=F32,
                         precision=lax.Precision.HIGHEST) + b_ref[...]


def _modulation(cv, w_mod, b_mod):
    rows, d = cv.shape
    n = w_mod.shape[1]
    tn = 1024
    return pl.pallas_call(
        _mod_kernel,
        grid=(n // tn,),
        in_specs=[pl.BlockSpec((rows, d), lambda j: (0, 0)),
                  pl.BlockSpec((d, tn), lambda j: (0, j)),
                  pl.BlockSpec((1, tn), lambda j: (0, j))],
        out_specs=pl.BlockSpec((rows, tn), lambda j: (0, j)),
        out_shape=jax.ShapeDtypeStruct((rows, n), F32),
        name="modulation",
    )(cv, w_mod, b_mod.reshape(1, n))


def _inproj_kernel(x_ref, mod_ref, g_ref, w_ref, o_ref, *, tm, n_ctx, n_batch):
    b = pl.program_id(1)
    i = pl.program_id(2)
    m_lat = mod_ref[pl.ds(b, 1), :]
    m_ctx = mod_ref[n_batch:n_batch + 1, :]
    rows = tm // INPROJ_CHAINS
    us = []
    for c in range(INPROJ_CHAINS):
        xn = _rms(x_ref[0, c * rows:(c + 1) * rows, :], g_ref[0:1, :])
        row = i * tm + c * rows + lax.broadcasted_iota(I32, (rows, 1), 0)
        is_ctx = row < n_ctx
        shift = jnp.where(is_ctx, m_ctx[:, 0:D_MODEL], m_lat[:, 0:D_MODEL])
        scale = jnp.where(is_ctx, m_ctx[:, D_MODEL:2 * D_MODEL], m_lat[:, D_MODEL:2 * D_MODEL])
        us.append((xn * (1.0 + scale) + shift).astype(BF16))
    for c in range(INPROJ_CHAINS):
        o_ref[0, c * rows:(c + 1) * rows, :] = _dot(us[c], w_ref[...]).astype(BF16)


def _in_projection(xc, mod, norm_g, w_in_bf, n_ctx):
    nb, lc, d = xc.shape
    n = w_in_bf.shape[1]
    tm = _largest_tile(lc, 1056, 16)
    tn = 2176
    assert n % tn == 0
    kern = functools.partial(_inproj_kernel, tm=tm, n_ctx=n_ctx, n_batch=nb)
    vmem = 2 * (tm * d * 4 + d * tn * 2 + tm * tn * 2) + 6 * tm * d * 4
    return pl.pallas_call(
        kern,
        grid=(n // tn, nb, lc // tm),
        in_specs=[pl.BlockSpec((1, tm, d), lambda j, b, i: (b, i, 0)),
                  pl.BlockSpec(mod.shape, lambda j, b, i: (0, 0)),
                  pl.BlockSpec(norm_g.shape, lambda j, b, i: (0, 0)),
                  pl.BlockSpec((d, tn), lambda j, b, i: (0, j))],
        out_specs=pl.BlockSpec((1, tm, tn), lambda j, b, i: (b, i, j)),
        out_shape=jax.ShapeDtypeStruct((nb, lc, n), BF16),
        compiler_params=pltpu.CompilerParams(
            dimension_semantics=("arbitrary", "arbitrary", "arbitrary"),
            vmem_limit_bytes=_vmem_limit(vmem)),
        name="in_projection",
    )(xc, mod, norm_g, w_in_bf)


def _rope(xn, cos, sin_a, sin_b):
    return (xn * cos + pltpu.roll(xn, HEAD_DIM - ROPE_AXIS_DIM // 2, 1) * sin_a
            + pltpu.roll(xn, ROPE_AXIS_DIM // 2, 1) * sin_b)


def _qkprep_kernel(q_ref, k_ref, v_ref, cos_ref, sa_ref, sb_ref, g_ref, qt_ref, ko_ref, vt_ref):
    cos, sa, sb = cos_ref[...], sa_ref[...], sb_ref[...]
    gq, gk = g_ref[0:1, :], g_ref[1:2, :]
    qscale = HEAD_DIM ** -0.5 * np.log2(np.e)
    for h in range(ATT_HEADS):
        hs = slice(h * HEAD_DIM, (h + 1) * HEAD_DIM)
        xn = _rms(q_ref[0, :, hs].astype(F32), gq)
        qt_ref[0, hs, :] = (_rope(xn, cos, sa, sb) * qscale).T.astype(BF16)
    for h in range(ATT_KV_HEADS):
        hs = slice(h * HEAD_DIM, (h + 1) * HEAD_DIM)
        xn = _rms(k_ref[0, :, hs].astype(F32), gk)
        ko_ref[0, :, hs] = _rope(xn, cos, sa, sb).astype(BF16)
        vt_ref[0, hs, :] = v_ref[0, :, hs].astype(F32).T.astype(BF16)


def _qk_prep(p, cos, sin_a, sin_b, qk_norm_g):
    nb, lc, _ = p.shape
    tm = _largest_tile(lc, 512, 128)
    tab = pl.BlockSpec((tm, HEAD_DIM), lambda b, i: (i, 0))
    return pl.pallas_call(
        _qkprep_kernel,
        grid=(nb, lc // tm),
        in_specs=[pl.BlockSpec((1, tm, ATT_W), lambda b, i: (b, i, COL_AQ // ATT_W)),
                  pl.BlockSpec((1, tm, KV_W), lambda b, i: (b, i, COL_K // KV_W)),
                  pl.BlockSpec((1, tm, KV_W), lambda b, i: (b, i, COL_V // KV_W)),
                  tab, tab, tab,
                  pl.BlockSpec(qk_norm_g.shape, lambda b, i: (0, 0))],
        out_specs=[pl.BlockSpec((1, ATT_W, tm), lambda b, i: (b, 0, i)),
                   pl.BlockSpec((1, tm, KV_W), lambda b, i: (b, i, 0)),
                   pl.BlockSpec((1, KV_W, tm), lambda b, i: (b, 0, i))],
        out_shape=[jax.ShapeDtypeStruct((nb, ATT_W, lc), BF16),
                   jax.ShapeDtypeStruct((nb, lc, KV_W), BF16),
                   jax.ShapeDtypeStruct((nb, KV_W, lc), BF16)],
        name="qk_prep",
    )(p, p, p, cos, sin_a, sin_b, qk_norm_g)


def _attn_kernel(qt_ref, k_ref, vt_ref, o_ref, acc_ref, s_ref, *, tq, kc, n_kc):
    acc_ref[...] = jnp.zeros(acc_ref.shape, F32)
    qts = [qt_ref[0, g * HEAD_DIM:(g + 1) * HEAD_DIM, :] for g in range(ATT_GROUPS)]

    def put_scores(c, slot):
        kch = k_ref[0, pl.ds(pl.multiple_of(c * kc, kc), kc), :]
        for g in range(ATT_GROUPS):
            s_ref[slot, g] = _dot(kch, qts[g])

    def softmax_pv(c, slot, carry):
        vch = vt_ref[0, :, pl.ds(pl.multiple_of(c * kc, kc), kc)]
        out = []
        for g in range(ATT_GROUPS):
            m_old, l_old = carry[g]
            s = s_ref[slot, g]
            m_new = jnp.maximum(m_old, jnp.max(s, axis=0, keepdims=True))
            alpha = jnp.exp2(m_old - m_new)
            p = jnp.exp2(s - m_new)
            l_new = l_old * alpha + jnp.sum(p, axis=0, keepdims=True)
            acc_ref[g] = acc_ref[g] * alpha + _dot(vch, p.astype(BF16))
            out.append((m_new, l_new))
        return tuple(out)

    def body(i, carry):
        c = 2 * i
        put_scores(c + 1, 1)
        carry = softmax_pv(c, 0, carry)
        put_scores(c + 2, 0)
        return softmax_pv(c + 1, 1, carry)

    init = tuple((jnp.full((1, tq), -jnp.inf, F32), jnp.zeros((1, tq), F32)) for _ in range(ATT_GROUPS))
    put_scores(0, 0)
    fin = lax.fori_loop(0, (n_kc - 1) // 2, body, init)
    if n_kc % 2 == 0:
        put_scores(n_kc - 1, 1)
        fin = softmax_pv(n_kc - 2, 0, fin)
    fin = softmax_pv(n_kc - 1, (n_kc - 1) % 2, fin)
    for g in range(ATT_GROUPS):
        o_ref[0, :, g * HEAD_DIM:(g + 1) * HEAD_DIM] = (acc_ref[g] / fin[g][1]).T.astype(BF16)


def _attention(qt, kn, vt, n_ctx):
    nb, lc, _ = kn.shape
    n_lat = lc - n_ctx
    tq = _largest_tile(n_lat, 256, 128)
    assert n_ctx % tq == 0
    kc = _largest_tile(lc, 768, 256)
    gw = ATT_GROUPS * HEAD_DIM
    kern = functools.partial(_attn_kernel, tq=tq, kc=kc, n_kc=lc // kc)
    vmem = 8 * lc * HEAD_DIM * 2 + 4 * ATT_GROUPS * kc * tq * 4 + 8 * gw * tq * 4
    return pl.pallas_call(
        kern,
        grid=(nb, ATT_KV_HEADS, n_lat // tq),
        in_specs=[pl.BlockSpec((1, gw, tq), lambda b, h, i: (b, h, n_ctx // tq + i)),
                  pl.BlockSpec((1, lc, HEAD_DIM), lambda b, h, i: (b, 0, h)),
                  pl.BlockSpec((1, HEAD_DIM, lc), lambda b, h, i: (b, h, 0))],
        out_specs=pl.BlockSpec((1, tq, gw), lambda b, h, i: (b, i, h)),
        out_shape=jax.ShapeDtypeStruct((nb, n_lat, ATT_W), BF16),
        scratch_shapes=[pltpu.VMEM((ATT_GROUPS, HEAD_DIM, tq), F32),
                        pltpu.VMEM((2, ATT_GROUPS, kc, tq), F32)],
        compiler_params=pltpu.CompilerParams(
            dimension_semantics=("arbitrary", "arbitrary", "arbitrary"),
            vmem_limit_bytes=_vmem_limit(vmem)),
        name="attention",
    )(qt, kn, vt)


def _hgrn_chunk(q_ref, v_ref, r_ref, lb, reverse):
    c = HGRN_CHUNK
    q = q_ref[0].astype(F32)
    v = v_ref[0]
    r = r_ref[0].astype(F32)
    sig = jax.nn.sigmoid(r)
    f = lb + (1.0 - lb) * sig
    logf = jnp.log(f)
    k = (1.0 - lb) * (1.0 - sig)
    ti = lax.broadcasted_iota(I32, (c, c), 0)
    si = lax.broadcasted_iota(I32, (c, c), 1)
    tri = jnp.where((si >= ti) if reverse else (si <= ti), 1.0, 0.0).astype(BF16)
    hi = logf.astype(BF16)
    lo = (logf - hi.astype(F32)).astype(BF16)
    bcum = _dot(tri, hi) + _dot(tri, lo)
    b_end = bcum[0:1, :] if reverse else bcum[c - 1:c, :]
    qt = (q * jnp.exp(bcum)).astype(BF16)
    kt = (k * jnp.exp(b_end - bcum)).astype(BF16)
    dec = jnp.exp(b_end)

    nsub = c // HGRN_SUB
    qp, kp, cols = [], [], []
    for blk in range(nsub):
        rs = slice(blk * HGRN_SUB, (blk + 1) * HGRN_SUB)
        cs = slice(blk * HGRN_SUB, c) if reverse else slice(0, (blk + 1) * HGRN_SUB)
        mid = blk * HGRN_SUB + HGRN_SUB // 2
        ref = bcum[mid:mid + 1, :]
        qp.append((q[rs] * jnp.exp(bcum[rs] - ref)).astype(BF16))
        kp.append((k[cs] * jnp.exp(ref - bcum[cs])).astype(BF16))
        cols.append(cs)

    keep = []
    for blk in range(nsub):
        n_cols = cols[blk].stop - cols[blk].start
        trow = blk * HGRN_SUB + lax.broadcasted_iota(I32, (HGRN_SUB, n_cols), 0)
        scol = cols[blk].start + lax.broadcasted_iota(I32, (HGRN_SUB, n_cols), 1)
        keep.append((scol >= trow) if reverse else (scol <= trow))
    return dict(v=v, qt=qt, kt=kt, dec=dec, qp=qp, kp=kp, cols=cols, keep=keep)


def _hgrn_emit(dirs, s_refs, o_refs):
    nsub = HGRN_CHUNK // HGRN_SUB
    heads = [slice(h * HGRN_DK, (h + 1) * HGRN_DK) for h in range(HGRN_HEADS)]
    inter = []
    for d, s_ref in zip(dirs, s_refs):
        row = []
        for h, hs in enumerate(heads):
            st = s_ref[h]
            row.append(_dot_nt(d["qt"][:, hs], st.astype(BF16)))
            s_ref[h] = st * d["dec"][:, hs] + _dot_tn(d["v"][:, hs], d["kt"][:, hs])
        inter.append(row)
    scores = [[[_dot_nt(d["qp"][blk][:, hs], d["kp"][blk][:, hs]) for blk in range(nsub)] for hs in heads]
              for d in dirs]
    for d, sc, it, o_ref in zip(dirs, scores, inter, o_refs):
        outs = []
        for h, hs in enumerate(heads):
            parts = [_dot(jnp.where(d["keep"][blk], sc[h][blk], 0.0).astype(BF16), d["v"][d["cols"][blk], hs])
                     for blk in range(nsub)]
            outs.append(it[h] + jnp.concatenate(parts, axis=0))
        o_ref[0] = jnp.concatenate(outs, axis=1).astype(BF16)


def _hgrn_kernel(qf_ref, vf_ref, rf_ref, qb_ref, vb_ref, rb_ref, lb_ref, of_ref, ob_ref, sf_ref, sb_ref):
    @pl.when(pl.program_id(1) == 0)
    def _():
        sf_ref[...] = jnp.zeros(sf_ref.shape, F32)
        sb_ref[...] = jnp.zeros(sb_ref.shape, F32)

    n_layers = lb_ref.shape[0] // 2

    def lower_bound(direction):
        rows = [lb_ref[direction * n_layers + l:direction * n_layers + l + 1, :] for l in range(n_layers)]
        amax = functools.reduce(jnp.maximum, rows)
        e = [jnp.exp(a - amax) for a in rows]
        return e[0] / functools.reduce(lambda u, w: u + w, e)

    fwd = _hgrn_chunk(qf_ref, vf_ref, rf_ref, lower_bound(0), False)
    bwd = _hgrn_chunk(qb_ref, vb_ref, rb_ref, lower_bound(1), True)
    _hgrn_emit([fwd, bwd], [sf_ref, sb_ref], [of_ref, ob_ref])


def _hgrn_scan(p, hgrn_lb, n_ctx):
    nb, lc, _ = p.shape
    c = HGRN_CHUNK
    assert n_ctx % c == 0 and lc % c == 0
    n_chunks = lc // c
    ctx_chunks = n_ctx // c

    def fwd(col):
        return lambda b, s: (b, s, col)

    def bwd_chunk(s):
        return jnp.where(s < ctx_chunks, ctx_chunks - 1 - s, n_chunks - 1 + ctx_chunks - s)

    def bwd(col):
        return lambda b, s: (b, bwd_chunk(s), col)

    blk = (1, c, HGRN_W)
    lb2 = hgrn_lb.reshape(-1, HGRN_W)
    return pl.pallas_call(
        _hgrn_kernel,
        grid=(nb, n_chunks),
        in_specs=[pl.BlockSpec(blk, fwd(COL_HQ // HGRN_W)), pl.BlockSpec(blk, fwd(COL_HI // HGRN_W)),
                  pl.BlockSpec(blk, fwd(COL_FF // HGRN_W)),
                  pl.BlockSpec(blk, bwd(COL_HQ // HGRN_W)), pl.BlockSpec(blk, bwd(COL_HI // HGRN_W)),
                  pl.BlockSpec(blk, bwd(COL_FB // HGRN_W)),
                  pl.BlockSpec(lb2.shape, lambda b, s: (0, 0))],
        out_specs=[pl.BlockSpec(blk, fwd(0)), pl.BlockSpec(blk, bwd(0))],
        out_shape=[jax.ShapeDtypeStruct((nb, lc, HGRN_W), BF16)] * 2,
        scratch_shapes=[pltpu.VMEM((HGRN_HEADS, HGRN_DK, HGRN_DK), F32)] * 2,
        compiler_params=pltpu.CompilerParams(dimension_semantics=("arbitrary", "arbitrary")),
        name="hgrn_scan",
    )(p, p, p, p, p, p, lb2)


def _merge_kernel(of_ref, ob_ref, hg_ref, gh_ref, ga_ref, oa_ref, x_ref, mod_ref, ng_ref, hng_ref,
                  wb0_ref, wb1_ref, wo_ref, rw_ref, rb_ref,
                  x1_ref, t_ref, ids_ref, wts_ref):
    b = pl.program_id(0)
    m = mod_ref[pl.ds(b, 1), :]
    gate_mix = m[:, 2 * D_MODEL:3 * D_MODEL]
    shift_f = m[:, 3 * D_MODEL:4 * D_MODEL]
    scale_f = m[:, 4 * D_MODEL:5 * D_MODEL]
    hng = hng_ref[...]
    rw = rw_ref[...]
    rw_hi = rw.astype(BF16)
    rw_lo = (rw - rw_hi.astype(F32)).astype(BF16)
    groups = [slice(r, r + MERGE_ROWS) for r in range(0, of_ref.shape[1], MERGE_ROWS)]

    branch = []
    for rs in groups:
        o = of_ref[0, rs, :].astype(F32) + ob_ref[0, rs, :].astype(F32)
        o_h = jnp.concatenate(
            [_rms(o[:, h * HGRN_DK:(h + 1) * HGRN_DK], hng) for h in range(HGRN_HEADS)], axis=1)
        g_raw = hg_ref[0, rs, :].astype(F32)
        o_h = o_h * (g_raw * jax.nn.sigmoid(g_raw))
        branch.append((_dot(o_h.astype(BF16), wb0_ref[...]), _dot(oa_ref[0, rs, :], wb1_ref[...])))
    mixed = []
    for rs, (y_h, y_a) in zip(groups, branch):
        y = (jax.nn.sigmoid(gh_ref[0, rs, :].astype(F32)) * y_h
             + jax.nn.sigmoid(ga_ref[0, rs, :].astype(F32)) * y_a)
        mixed.append(_dot(y.astype(BF16), wo_ref[...]))
    logits_all = []
    for rs, y in zip(groups, mixed):
        x1 = x_ref[0, rs, :] + gate_mix * _rms(y, ng_ref[1:2, :])
        x1_ref[0, rs, :] = x1
        t = _rms(x1, ng_ref[2:3, :]) * (1.0 + scale_f) + shift_f
        t_ref[rs, :] = t
        t_hi = t.astype(BF16)
        t_lo = (t - t_hi.astype(F32)).astype(BF16)
        logits_all.append(_dot_nt(rw_hi, t_hi) + _dot_nt(rw_hi, t_lo) + _dot_nt(rw_lo, t_hi) + rb_ref[...])
    for rs, logits in zip(groups, logits_all):
        eidx = lax.broadcasted_iota(I32, logits.shape, 0).astype(F32)
        vals = []
        for j in range(TOP_K):
            mx = jnp.max(logits, axis=0, keepdims=True)
            idx = jnp.min(jnp.where(logits == mx, eidx, float(N_EXPERTS)), axis=0, keepdims=True)
            ids_ref[j:j + 1, rs] = idx.astype(I32)
            vals.append(mx)
            logits = jnp.where(eidx == idx, -jnp.inf, logits)
        ex = [jnp.exp(vj - vals[0]) for vj in vals]
        den = ex[0] + ex[1] + ex[2] + ex[3]
        for j in range(TOP_K):
            wts_ref[j:j + 1, rs] = ex[j] / den


def _merge(o_f, o_b, p, o_a, x, mod, norm_g, hgrn_norm_g, wb0, wb1, wo, router_wt, router_b, n_ctx):
    nb, n_lat, d = x.shape
    tm = _largest_tile(n_lat, 256, 128)
    assert n_ctx % tm == 0
    off = n_ctx // tm
    nt = n_lat // tm
    row = (1, tm, d)

    def pcol(col):
        return pl.BlockSpec(row, lambda b, i: (b, off + i, col // d))

    def full(a):
        return pl.BlockSpec(a.shape, lambda b, i: (0,) * a.ndim)

    tok = nb * n_lat
    return pl.pallas_call(
        _merge_kernel,
        grid=(nb, nt),
        in_specs=[pl.BlockSpec(row, lambda b, i: (b, off + i, 0)),
                  pl.BlockSpec(row, lambda b, i: (b, off + i, 0)),
                  pcol(COL_HG), pcol(COL_GH), pcol(COL_GA),
                  pl.BlockSpec(row, lambda b, i: (b, i, 0)),
                  pl.BlockSpec(row, lambda b, i: (b, i, 0)),
                  full(mod), full(norm_g), full(hgrn_norm_g), full(wb0), full(wb1), full(wo),
                  full(router_wt), full(router_b)],
        out_specs=[pl.BlockSpec(row, lambda b, i: (b, i, 0)),
                   pl.BlockSpec((tm, d), lambda b, i: (b * nt + i, 0)),
                   pl.BlockSpec((TOP_K, tm), lambda b, i: (0, b * nt + i)),
                   pl.BlockSpec((TOP_K, tm), lambda b, i: (0, b * nt + i))],
        out_shape=[jax.ShapeDtypeStruct((nb, n_lat, d), F32),
                   jax.ShapeDtypeStruct((tok, d), F32),
                   jax.ShapeDtypeStruct((TOP_K, tok), I32),
                   jax.ShapeDtypeStruct((TOP_K, tok), F32)],
        compiler_params=pltpu.CompilerParams(
            dimension_semantics=("arbitrary", "arbitrary"),
            vmem_limit_bytes=_vmem_limit(40 * 1024 * 1024)),
        name="merge_router",
    )(o_f, o_b, p, p, p, o_a, x, mod, norm_g, hgrn_norm_g, wb0, wb1, wo, router_wt, router_b)


def _rank_kernel(ids_ref, rank_ref, cnt_ref, *, tt):
    eidx = lax.broadcasted_iota(I32, (N_EXPERTS, tt), 0)
    si = lax.broadcasted_iota(I32, (tt, tt), 0)
    ti = lax.broadcasted_iota(I32, (tt, tt), 1)
    before = jnp.where(si < ti, 1.0, 0.0).astype(BF16)
    seen = jnp.zeros((N_EXPERTS, 1), F32)
    for j in range(TOP_K):
        onehot = eidx == ids_ref[j:j + 1, :]
        oh = jnp.where(onehot, 1.0, 0.0)
        earlier = _dot(oh.astype(BF16), before)
        rank = jnp.sum(jnp.where(onehot, seen + earlier, 0.0), axis=0, keepdims=True)
        rank_ref[j:j + 1, :] = rank.astype(I32)
        seen = seen + jnp.sum(oh, axis=1, keepdims=True)
    cnt_ref[0] = seen.astype(I32)


def _expert_rank(ids, tt):
    _, tok = ids.shape
    return pl.pallas_call(
        functools.partial(_rank_kernel, tt=tt),
        grid=(tok // tt,),
        in_specs=[pl.BlockSpec((TOP_K, tt), lambda i: (0, i))],
        out_specs=[pl.BlockSpec((TOP_K, tt), lambda i: (0, i)),
                   pl.BlockSpec((1, N_EXPERTS, 1), lambda i: (i, 0, 0))],
        out_shape=[jax.ShapeDtypeStruct((TOP_K, tok), I32),
                   jax.ShapeDtypeStruct((tok // tt, N_EXPERTS, 1), I32)],
        compiler_params=pltpu.CompilerParams(dimension_semantics=("arbitrary",)),
        name="expert_rank",
    )(ids)


def _start_blocks(n_blocks, make_copy):
    def start(b, carry):
        make_copy(b).start()
        return carry

    lax.fori_loop(0, n_blocks, start, 0)


def _wait_blocks(n_blocks, make_copy, make_group_copy):
    def wait_group(g, carry):
        make_group_copy().wait()
        return carry

    def wait_one(b, carry):
        make_copy(0).wait()
        return carry

    lax.fori_loop(0, lax.shift_right_logical(n_blocks, WAIT_GROUP_LOG2), wait_group, 0)
    lax.fori_loop(0, jnp.bitwise_and(n_blocks, (1 << WAIT_GROUP_LOG2) - 1), wait_one, 0)


def _scatter_kernel(zt_ref, zvalid_ref, nblk_ref, bdst_ref, lp_ref, t_ref, xs_ref, stage, zero_buf, sem, zsem,
                    *, tt, n_stage):
    @pl.when(pl.program_id(0) == 0)
    def _():
        zero_buf[...] = jnp.zeros(zero_buf.shape, F32)

        def zero_copy(k):
            return pltpu.make_async_copy(zero_buf, xs_ref.at[pl.ds(zt_ref[k] * MOE_TILE, MOE_TILE)], zsem)

        def start(k, carry):
            @pl.when(zvalid_ref[k] == 1)
            def _():
                zero_copy(k).start()
            return carry

        def wait(k, carry):
            @pl.when(zvalid_ref[k] == 1)
            def _():
                zero_copy(k).wait()
            return carry

        lax.fori_loop(0, zt_ref.shape[0], start, 0)
        lax.fori_loop(0, zt_ref.shape[0], wait, 0)

    i = pl.program_id(0)
    slot = i % 2
    t_bf = t_ref[...].astype(BF16)
    lps = [lp_ref[j:j + 1, :] for j in range(TOP_K)]
    for rc in range(n_stage // PERM_ROWS):
        row = rc * PERM_ROWS + lax.broadcasted_iota(I32, (PERM_ROWS, tt), 0)
        sel = jnp.where(row == lps[TOP_K - 1], 1.0, 0.0)
        for j in range(TOP_K - 2, -1, -1):
            sel = jnp.where(row == lps[j], 1.0, sel)
        stage[slot, rc * PERM_ROWS:(rc + 1) * PERM_ROWS, :] = _dot(sel.astype(BF16), t_bf)

    def copy(b):
        src = pl.multiple_of(b * RUN_ALIGN, RUN_ALIGN)
        dst = pl.multiple_of(bdst_ref[0, 0, b], RUN_ALIGN)
        return pltpu.make_async_copy(stage.at[slot, pl.ds(src, RUN_ALIGN)], xs_ref.at[pl.ds(dst, RUN_ALIGN)],
                                     sem.at[slot])

    def waiter(s):
        rows = RUN_ALIGN << WAIT_GROUP_LOG2
        return (lambda b: pltpu.make_async_copy(stage.at[s, pl.ds(0, RUN_ALIGN)],
                                                xs_ref.at[pl.ds(0, RUN_ALIGN)], sem.at[s]),
                lambda: pltpu.make_async_copy(stage.at[s, pl.ds(0, rows)], xs_ref.at[pl.ds(0, rows)], sem.at[s]))

    _start_blocks(nblk_ref[i], copy)

    @pl.when(i > 0)
    def _():
        _wait_blocks(nblk_ref[i - 1], *waiter(1 - slot))

    @pl.when(i == pl.num_programs(0) - 1)
    def _():
        _wait_blocks(nblk_ref[i], *waiter(slot))


def _scatter_rows(zero_tiles, zero_valid, n_blocks, block_rows, lp, t, n_rows, tt, n_stage):
    tok, d = t.shape
    nb = block_rows.shape[2]
    return pl.pallas_call(
        functools.partial(_scatter_kernel, tt=tt, n_stage=n_stage),
        grid_spec=pltpu.PrefetchScalarGridSpec(
            num_scalar_prefetch=3,
            grid=(tok // tt,),
            in_specs=[pl.BlockSpec((1, 1, nb), lambda i, *_: (i, 0, 0), memory_space=pltpu.SMEM),
                      pl.BlockSpec((TOP_K, tt), lambda i, *_: (0, i)),
                      pl.BlockSpec((tt, d), lambda i, *_: (i, 0))],
            out_specs=pl.BlockSpec(memory_space=pl.ANY),
            scratch_shapes=[pltpu.VMEM((2, n_stage, d), F32), pltpu.VMEM((MOE_TILE, d), F32),
                            pltpu.SemaphoreType.DMA((2,)), pltpu.SemaphoreType.DMA(())]),
        out_shape=jax.ShapeDtypeStruct((n_rows, d), F32),
        compiler_params=pltpu.CompilerParams(
            dimension_semantics=("arbitrary",), vmem_limit_bytes=_vmem_limit(48 * 1024 * 1024)),
        name="moe_scatter",
    )(zero_tiles, zero_valid, n_blocks, block_rows, lp, t)


def _expert_kernel(te_ref, first_ref, nused_ref, slot_ref, next_ref, xs_ref, wu_hbm, bu_ref, wd_hbm, bd_ref,
                   ys_ref, wu32, wd32, wu_bf, wd_bf, sem_u, sem_d):
    i = pl.program_id(0)

    def weight_copies(e, s):
        return (pltpu.make_async_copy(wu_hbm.at[e], wu32.at[s], sem_u.at[s]),
                pltpu.make_async_copy(wd_hbm.at[e], wd32.at[s], sem_d.at[s]))

    @pl.when(i >= nused_ref[0])
    def _():
        ys_ref[...] = jnp.zeros(ys_ref.shape, F32)

    @pl.when(i < nused_ref[0])
    def _():
        @pl.when(first_ref[i] == 1)
        def _():
            s = slot_ref[i]

            @pl.when(i == 0)
            def _():
                for cp in weight_copies(te_ref[0], 0):
                    cp.start()

            for cp in weight_copies(te_ref[i], s):
                cp.wait()
            wu_bf[...] = wu32[s].astype(BF16)
            wd_bf[...] = wd32[s].astype(BF16)

            @pl.when(next_ref[i] >= 0)
            def _():
                for cp in weight_copies(next_ref[i], 1 - s):
                    cp.start()

        h = _dot(xs_ref[...].astype(BF16), wu_bf[...]) + bu_ref[0]
        glu = jnp.minimum(h[:, :D_EXPERT], SWIGLU_LIMIT)
        lin = jnp.clip(h[:, D_EXPERT:], -SWIGLU_LIMIT, SWIGLU_LIMIT)
        a = glu * jax.nn.sigmoid(SWIGLU_ALPHA * glu) * (lin + 1.0)
        ys_ref[...] = _dot(a.astype(BF16), wd_bf[...]) + bd_ref[0]


def _experts(xs, tile_expert, tile_first, n_used, tile_slot, next_expert, w_up, b_up, w_down, b_down):
    n_rows, d = xs.shape
    tm = MOE_TILE
    n_tiles = n_rows // tm

    def tile(i, te, first, nused, *_):
        return (jnp.minimum(i, nused[0] - 1), 0)

    def bsel(i, te, *_):
        return (te[i], 0, 0)

    vmem = 2 * (d * 2 * D_EXPERT * 4 + D_EXPERT * d * 4) + d * 2 * D_EXPERT * 2 + D_EXPERT * d * 2 \
        + 4 * tm * d * 4 + 6 * tm * 2 * D_EXPERT * 4
    return pl.pallas_call(
        _expert_kernel,
        grid_spec=pltpu.PrefetchScalarGridSpec(
            num_scalar_prefetch=5,
            grid=(n_tiles,),
            in_specs=[pl.BlockSpec((tm, d), tile),
                      pl.BlockSpec(memory_space=pl.ANY),
                      pl.BlockSpec((1, 1, 2 * D_EXPERT), bsel),
                      pl.BlockSpec(memory_space=pl.ANY),
                      pl.BlockSpec((1, 1, d), bsel)],
            out_specs=pl.BlockSpec((tm, d), lambda i, *_: (i, 0)),
            scratch_shapes=[pltpu.VMEM((2, d, 2 * D_EXPERT), F32), pltpu.VMEM((2, D_EXPERT, d), F32),
                            pltpu.VMEM((d, 2 * D_EXPERT), BF16), pltpu.VMEM((D_EXPERT, d), BF16),
                            pltpu.SemaphoreType.DMA((2,)), pltpu.SemaphoreType.DMA((2,))]),
        out_shape=jax.ShapeDtypeStruct((n_rows, d), F32),
        compiler_params=pltpu.CompilerParams(
            dimension_semantics=("arbitrary",), vmem_limit_bytes=_vmem_limit(vmem)),
        name="moe_experts",
    )(tile_expert, tile_first, n_used, tile_slot, next_expert, xs, w_up, b_up.reshape(N_EXPERTS, 1, -1), w_down,
      b_down.reshape(N_EXPERTS, 1, -1))


def _combine_kernel(nblk_ref, bsrc_ref, ys_ref, lp_ref, w_ref, x1_ref, mod_ref, ng_ref, o_ref, stage, sem,
                    *, tt, n_stage, tiles_per_batch):
    i = pl.program_id(0)
    slot = i % 2

    def fetch(tile, s):
        def copy(b):
            src = pl.multiple_of(bsrc_ref[tile, b], RUN_ALIGN)
            dst = pl.multiple_of(b * RUN_ALIGN, RUN_ALIGN)
            return pltpu.make_async_copy(ys_ref.at[pl.ds(src, RUN_ALIGN)], stage.at[s, pl.ds(dst, RUN_ALIGN)],
                                         sem.at[s])
        return copy

    def group_copy():
        rows = RUN_ALIGN << WAIT_GROUP_LOG2
        return pltpu.make_async_copy(ys_ref.at[pl.ds(0, rows)], stage.at[slot, pl.ds(0, rows)], sem.at[slot])

    @pl.when(i == 0)
    def _():
        stage[...] = jnp.zeros(stage.shape, F32)
        _start_blocks(nblk_ref[0], fetch(0, 0))

    @pl.when(i + 1 < pl.num_programs(0))
    def _():
        _start_blocks(nblk_ref[i + 1], fetch(i + 1, 1 - slot))

    _wait_blocks(nblk_ref[i], fetch(i, slot), group_copy)

    lp = [jnp.broadcast_to(lp_ref[:, j:j + 1], (tt, PERM_ROWS)) for j in range(TOP_K)]
    w = [jnp.broadcast_to(w_ref[:, j:j + 1], (tt, PERM_ROWS)) for j in range(TOP_K)]
    f = jnp.zeros((tt, D_MODEL), F32)
    for lc in range(n_stage // PERM_ROWS):
        col = lc * PERM_ROWS + lax.broadcasted_iota(I32, (tt, PERM_ROWS), 1)
        wp = jnp.where(col == lp[TOP_K - 1], w[TOP_K - 1], 0.0)
        for j in range(TOP_K - 2, -1, -1):
            wp = jnp.where(col == lp[j], w[j], wp)
        f = f + _dot(wp.astype(BF16), stage[slot, lc * PERM_ROWS:(lc + 1) * PERM_ROWS, :].astype(BF16))
    b = i // tiles_per_batch
    gate_f = mod_ref[pl.ds(b, 1), 5 * D_MODEL:6 * D_MODEL]
    o_ref[...] = x1_ref[...] + gate_f * _rms(f, ng_ref[3:4, :])


def _combine(n_blocks, block_rows, ys, lp_t, wts_t, x1, mod, norm_g, tt, n_stage, tiles_per_batch):
    tok, d = x1.shape
    kern = functools.partial(_combine_kernel, tt=tt, n_stage=n_stage, tiles_per_batch=tiles_per_batch)
    return pl.pallas_call(
        kern,
        grid_spec=pltpu.PrefetchScalarGridSpec(
            num_scalar_prefetch=2,
            grid=(tok // tt,),
            in_specs=[pl.BlockSpec(memory_space=pl.ANY),
                      pl.BlockSpec((tt, TOP_K), lambda i, *_: (i, 0)),
                      pl.BlockSpec((tt, TOP_K), lambda i, *_: (i, 0)),
                      pl.BlockSpec((tt, d), lambda i, *_: (i, 0)),
                      pl.BlockSpec(mod.shape, lambda i, *_: (0, 0)),
                      pl.BlockSpec(norm_g.shape, lambda i, *_: (0, 0))],
            out_specs=pl.BlockSpec((tt, d), lambda i, *_: (i, 0)),
            scratch_shapes=[pltpu.VMEM((2, n_stage, d), F32), pltpu.SemaphoreType.DMA((2,))]),
        out_shape=jax.ShapeDtypeStruct((tok, d), F32),
        compiler_params=pltpu.CompilerParams(
            dimension_semantics=("arbitrary",), vmem_limit_bytes=_vmem_limit(48 * 1024 * 1024)),
        name="moe_combine",
    )(n_blocks, block_rows[:, 0, :], ys, lp_t, wts_t, x1, mod, norm_g)


def _rope_tables(n_ctx, n_lat):
    inv = ROPE_THETA ** (-np.arange(0, ROPE_AXIS_DIM, 2, dtype=np.float32) / ROPE_AXIS_DIM)
    idx = jnp.arange(n_lat, dtype=I32)
    rows = (idx // GRID_W).astype(F32)
    cols = (idx % GRID_W).astype(F32)
    half = ROPE_AXIS_DIM // 2
    inv = jnp.asarray(inv, F32)
    ang_r = rows[:, None] * inv[None, :]
    ang_c = cols[:, None] * inv[None, :]
    ang = jnp.concatenate([ang_r, ang_r, ang_c, ang_c], axis=1)
    cos, sin = jnp.cos(ang), jnp.sin(ang)
    lane = np.arange(HEAD_DIM)
    first = jnp.asarray((lane % ROPE_AXIS_DIM) < half)
    sin_a = jnp.where(first, -sin, 0.0)
    sin_b = jnp.where(first, 0.0, sin)
    pad1 = jnp.ones((n_ctx, HEAD_DIM), F32)
    pad0 = jnp.zeros((n_ctx, HEAD_DIM), F32)
    return (jnp.concatenate([pad1, cos], axis=0), jnp.concatenate([pad0, sin_a], axis=0),
            jnp.concatenate([pad0, sin_b], axis=0))


def kernel(x, c, ctx, c_ctx, w_mod, b_mod, norm_g, w_in, hgrn_lb, hgrn_norm_g, qk_norm_g, w_branch, w_out,
           router_w, router_b, w_up, b_up, w_down, b_down):
    nb, n_lat, d = x.shape
    n_ctx = ctx.shape[1]
    assert d == D_MODEL and w_mod.shape[0] == 1, "single-layer kernel"
    tok = nb * n_lat

    cv = jnp.zeros((8, d), F32).at[:nb].set(c).at[nb].set(c_ctx)
    mod = _modulation(cv, w_mod[0], b_mod[0])

    xc = jnp.concatenate([ctx, x], axis=1)
    sizes = np.cumsum([0, HGRN_W, HGRN_W, HGRN_W, HGRN_W, HGRN_W, ATT_W, KV_W, KV_W, D_MODEL, D_MODEL])
    order = [0, 1, 2, 3, 4, 5, 8, 9, 6, 7]
    w_in_bf = jnp.concatenate([w_in[0][:, sizes[i]:sizes[i + 1]] for i in order], axis=1).astype(BF16)
    p = _in_projection(xc, mod, norm_g[0], w_in_bf, n_ctx)

    cos, sin_a, sin_b = _rope_tables(n_ctx, n_lat)
    qt, kn, vt = _qk_prep(p, cos, sin_a, sin_b, qk_norm_g[0])
    o_a = _attention(qt, kn, vt, n_ctx)
    o_f, o_b = _hgrn_scan(p, hgrn_lb, n_ctx)

    x1, t, ids, wts = _merge(
        o_f, o_b, p, o_a, x, mod, norm_g[0], hgrn_norm_g, w_branch[0, 0].astype(BF16),
        w_branch[0, 1].astype(BF16), w_out[0].astype(BF16), router_w[0].T, router_b[0].reshape(N_EXPERTS, 1),
        n_ctx)

    tt = _largest_tile(n_lat, DISPATCH_TILE, 128)
    n_tt = tok // tt
    rank, counts = _expert_rank(ids, tt)
    counts = counts[:, :, 0]
    run = (counts + RUN_ALIGN - 1) // RUN_ALIGN * RUN_ALIGN
    run_end = jnp.cumsum(run, axis=1)
    stage_off = run_end - run
    expert_rows = jnp.sum(run, axis=0)
    tiles_e = (expert_rows + MOE_TILE - 1) // MOE_TILE
    tile_end = jnp.cumsum(tiles_e)
    starts = (tile_end - tiles_e) * MOE_TILE
    run_dst = starts[None, :] + jnp.cumsum(run, axis=0) - run
    n_tiles = -(-(tok * TOP_K + N_EXPERTS * n_tt * (RUN_ALIGN - 1)) // MOE_TILE) + N_EXPERTS
    tile_idx = jnp.arange(n_tiles, dtype=I32)
    tile_expert = jnp.sum((tile_end[None, :] <= tile_idx[:, None]).astype(I32), axis=1)
    tile_expert = jnp.minimum(tile_expert, N_EXPERTS - 1)
    n_used = tile_end[-1:].astype(I32)
    tile_expert = jnp.where(tile_idx < n_used[0], tile_expert, tile_expert[jnp.maximum(n_used[0] - 1, 0)])
    tile_first = jnp.concatenate([jnp.ones((1,), I32), (tile_expert[1:] != tile_expert[:-1]).astype(I32)])
    e_idx = jnp.arange(N_EXPERTS, dtype=I32)
    used = tiles_e > 0
    e_slot = (jnp.cumsum(used.astype(I32)) - 1) % 2
    later = used[None, :] & (e_idx[None, :] > e_idx[:, None])
    e_next = jnp.min(jnp.where(later, e_idx[None, :], N_EXPERTS), axis=1)
    e_next = jnp.where(e_next == N_EXPERTS, -1, e_next)
    tile_onehot = tile_expert[:, None] == e_idx[None, :]
    tile_slot = jnp.sum(jnp.where(tile_onehot, e_slot[None, :], 0), axis=1).astype(I32)
    next_expert = jnp.sum(jnp.where(tile_onehot, e_next[None, :], 0), axis=1).astype(I32)

    n_stage = -(-(TOP_K * tt + N_EXPERTS * (RUN_ALIGN - 1)) // PERM_ROWS) * PERM_ROWS
    ids_t = ids.reshape(TOP_K, n_tt, tt)
    onehot = ids_t[..., None] == jnp.arange(N_EXPERTS, dtype=I32)
    lp = jnp.sum(jnp.where(onehot, stage_off[None, :, None, :], 0), axis=-1).reshape(TOP_K, tok) + rank
    blk_row = jnp.arange(n_stage // RUN_ALIGN, dtype=I32) * RUN_ALIGN
    blk_expert = jnp.minimum(jnp.sum((run_end[:, None, :] <= blk_row[None, :, None]).astype(I32), axis=-1),
                             N_EXPERTS - 1)
    blk_onehot = blk_expert[..., None] == jnp.arange(N_EXPERTS, dtype=I32)
    blk_dst = jnp.sum(jnp.where(blk_onehot, (run_dst - stage_off)[:, None, :], 0), axis=-1) + blk_row[None, :]
    n_blocks = (run_end[:, -1] // RUN_ALIGN).astype(I32)
    blk_dst = jnp.where(blk_row[None, :] < run_end[:, -1:], blk_dst, 0).astype(I32)[:, None, :]

    n_tail = n_tiles - (tok * TOP_K) // MOE_TILE
    tail = n_used[0] + jnp.arange(n_tail, dtype=I32)
    zero_tiles = jnp.clip(jnp.concatenate([tile_end.astype(I32) - 1, tail]), 0, n_tiles - 1)
    zero_valid = jnp.concatenate([expert_rows % MOE_TILE != 0, tail < n_tiles]).astype(I32)

    xs = _scatter_rows(zero_tiles, zero_valid, n_blocks, blk_dst, lp, t, n_tiles * MOE_TILE, tt, n_stage)
    ys = _experts(xs, tile_expert, tile_first, n_used, tile_slot, next_expert, w_up[0], b_up[0], w_down[0],
                  b_down[0])
    out = _combine(n_blocks, blk_dst, ys, lp.T, wts.T, x1.reshape(tok, d), mod, norm_g[0], tt, n_stage,
                   n_lat // tt)
    return out.reshape(nb, n_lat, d)
```

```python
import functools

import jax
import jax.numpy as jnp
import numpy as np
from jax import lax
from jax.experimental import pallas as pl
from jax.experimental.pallas import tpu as pltpu

F32 = jnp.float32
BF16 = jnp.bfloat16
I32 = jnp.int32

D_MODEL = 1024
GRID_W = 64
N_MOD = 6
EPS = 1e-6
HGRN_HEADS = 8
HGRN_DK = 128
HGRN_W = HGRN_HEADS * HGRN_DK
ATT_HEADS = 8
ATT_KV_HEADS = 2
ATT_GROUPS = ATT_HEADS // ATT_KV_HEADS
HEAD_DIM = 128
ATT_W = ATT_HEADS * HEAD_DIM
KV_W = ATT_KV_HEADS * HEAD_DIM
ROPE_AXIS_DIM = HEAD_DIM // 2
ROPE_THETA = 10000.0
N_EXPERTS = 32
TOP_K = 4
D_EXPERT = 1024
SWIGLU_LIMIT = 7.0
SWIGLU_ALPHA = 1.702

COL_HQ, COL_HI, COL_FF, COL_FB, COL_HG, COL_AQ, COL_GH, COL_GA = (i * 1024 for i in range(8))
COL_K = 8 * 1024
COL_V = COL_K + KV_W
IN_COLS = COL_V + KV_W

V7X_VMEM_BYTES = 64 * 1024 * 1024
LANES = 128

HGRN_CHUNK = 64
HGRN_SUB = 16
MOE_TILE = 256
MERGE_ROWS = 128
INPROJ_CHAINS = 2
DISPATCH_TILE = 512
RUN_ALIGN = 8
PERM_ROWS = 256
WAIT_GROUP_LOG2 = 5


def _vmem_limit(nbytes):
    return int(min(V7X_VMEM_BYTES - 6 * 1024 * 1024, max(nbytes, 32 * 1024 * 1024)))


def _largest_tile(n, cap, mult):
    best = None
    for t in range(mult, min(n, cap) + 1, mult):
        if n % t == 0:
            best = t
    assert best is not None, (n, cap, mult)
    return best


def _rms(x, g):
    return x * lax.rsqrt(jnp.mean(x * x, axis=-1, keepdims=True) + EPS) * g


def _dot(a, b):
    return jnp.dot(a, b, preferred_element_type=F32)


def _dot_nt(a, b):
    return lax.dot_general(a, b, (((1,), (1,)), ((), ())), preferred_element_type=F32)


def _dot_tn(a, b):
    return lax.dot_general(a, b, (((0,), (0,)), ((), ())), preferred_element_type=F32)


def _mod_kernel(cv_ref, w_ref, b_ref, o_ref):
    cv = cv_ref[...]
    s = cv * jax.nn.sigmoid(cv)
    o_ref[...] = jnp.dot(s, w_ref[...], preferred_element_type=F32,
                         precision=lax.Precision.HIGHEST) + b_ref[...]


def _modulation(cv, w_mod, b_mod):
    rows, d = cv.shape
    n = w_mod.shape[1]
    tn = 1024
    return pl.pallas_call(
        _mod_kernel,
        grid=(n // tn,),
        in_specs=[pl.BlockSpec((rows, d), lambda j: (0, 0)),
                  pl.BlockSpec((d, tn), lambda j: (0, j)),
                  pl.BlockSpec((1, tn), lambda j: (0, j))],
        out_specs=pl.BlockSpec((rows, tn), lambda j: (0, j)),
        out_shape=jax.ShapeDtypeStruct((rows, n), F32),
        name="modulation",
    )(cv, w_mod, b_mod.reshape(1, n))


def _inproj_kernel(x_ref, mod_ref, g_ref, w_ref, o_ref, *, tm, n_ctx, n_batch):
    b = pl.program_id(1)
    i = pl.program_id(2)
    m_lat = mod_ref[pl.ds(b, 1), :]
    m_ctx = mod_ref[n_batch:n_batch + 1, :]
    rows = tm // INPROJ_CHAINS
    us = []
    for c in range(INPROJ_CHAINS):
        xn = _rms(x_ref[0, c * rows:(c + 1) * rows, :], g_ref[0:1, :])
        row = i * tm + c * rows + lax.broadcasted_iota(I32, (rows, 1), 0)
        is_ctx = row < n_ctx
        shift = jnp.where(is_ctx, m_ctx[:, 0:D_MODEL], m_lat[:, 0:D_MODEL])
        scale = jnp.where(is_ctx, m_ctx[:, D_MODEL:2 * D_MODEL], m_lat[:, D_MODEL:2 * D_MODEL])
        us.append((xn * (1.0 + scale) + shift).astype(BF16))
    for c in range(INPROJ_CHAINS):
        o_ref[0, c * rows:(c + 1) * rows, :] = _dot(us[c], w_ref[...]).astype(BF16)


def _in_projection(xc, mod, norm_g, w_in_bf, n_ctx):
    nb, lc, d = xc.shape
    n = w_in_bf.shape[1]
    tm = _largest_tile(lc, 1056, 16)
    tn = 2176
    assert n % tn == 0
    kern = functools.partial(_inproj_kernel, tm=tm, n_ctx=n_ctx, n_batch=nb)
    vmem = 2 * (tm * d * 4 + d * tn * 2 + tm * tn * 2) + 6 * tm * d * 4
    return pl.pallas_call(
        kern,
        grid=(n // tn, nb, lc // tm),
        in_specs=[pl.BlockSpec((1, tm, d), lambda j, b, i: (b, i, 0)),
                  pl.BlockSpec(mod.shape, lambda j, b, i: (0, 0)),
                  pl.BlockSpec(norm_g.shape, lambda j, b, i: (0, 0)),
                  pl.BlockSpec((d, tn), lambda j, b, i: (0, j))],
        out_specs=pl.BlockSpec((1, tm, tn), lambda j, b, i: (b, i, j)),
        out_shape=jax.ShapeDtypeStruct((nb, lc, n), BF16),
        compiler_params=pltpu.CompilerParams(
            dimension_semantics=("arbitrary", "arbitrary", "arbitrary"),
            vmem_limit_bytes=_vmem_limit(vmem)),
        name="in_projection",
    )(xc, mod, norm_g, w_in_bf)


def _rotate_half_matrix():
    half = ROPE_AXIS_DIM // 2
    i = lax.broadcasted_iota(I32, (HEAD_DIM, HEAD_DIM), 0)
    j = lax.broadcasted_iota(I32, (HEAD_DIM, HEAD_DIM), 1)
    first = jnp.bitwise_and(j, ROPE_AXIS_DIM - 1) < half
    r = jnp.where(i == j + half, jnp.where(first, -1.0, 0.0), jnp.where(i == j - half, jnp.where(first, 0.0, 1.0), 0.0))
    return r.astype(BF16)


def _rope(xn, cos, sin, rot):
    hi = xn.astype(BF16)
    lo = (xn - hi.astype(F32)).astype(BF16)
    return xn * cos + (_dot(hi, rot) + _dot(lo, rot)) * sin


def _qkprep_kernel(q_ref, k_ref, v_ref, cos_ref, sa_ref, sb_ref, g_ref, qt_ref, ko_ref, vt_ref):
    cos, sn = cos_ref[...], sb_ref[...] - sa_ref[...]
    rot = _rotate_half_matrix()
    gq, gk = g_ref[0:1, :], g_ref[1:2, :]
    qscale = HEAD_DIM ** -0.5 * np.log2(np.e)
    for h in range(ATT_HEADS):
        hs = slice(h * HEAD_DIM, (h + 1) * HEAD_DIM)
        xn = _rms(q_ref[0, :, hs].astype(F32), gq)
        qt_ref[0, hs, :] = (_rope(xn, cos, sn, rot) * qscale).T.astype(BF16)
    for h in range(ATT_KV_HEADS):
        hs = slice(h * HEAD_DIM, (h + 1) * HEAD_DIM)
        xn = _rms(k_ref[0, :, hs].astype(F32), gk)
        ko_ref[0, :, hs] = _rope(xn, cos, sn, rot).astype(BF16)
        vt_ref[0, hs, :] = v_ref[0, :, hs].astype(F32).T.astype(BF16)


def _qk_prep(p, cos, sin_a, sin_b, qk_norm_g):
    nb, lc, _ = p.shape
    tm = _largest_tile(lc, 512, 128)
    tab = pl.BlockSpec((tm, HEAD_DIM), lambda b, i: (i, 0))
    return pl.pallas_call(
        _qkprep_kernel,
        grid=(nb, lc // tm),
        in_specs=[pl.BlockSpec((1, tm, ATT_W), lambda b, i: (b, i, COL_AQ // ATT_W)),
                  pl.BlockSpec((1, tm, KV_W), lambda b, i: (b, i, COL_K // KV_W)),
                  pl.BlockSpec((1, tm, KV_W), lambda b, i: (b, i, COL_V // KV_W)),
                  tab, tab, tab,
                  pl.BlockSpec(qk_norm_g.shape, lambda b, i: (0, 0))],
        out_specs=[pl.BlockSpec((1, ATT_W, tm), lambda b, i: (b, 0, i)),
                   pl.BlockSpec((1, tm, KV_W), lambda b, i: (b, i, 0)),
                   pl.BlockSpec((1, KV_W, tm), lambda b, i: (b, 0, i))],
        out_shape=[jax.ShapeDtypeStruct((nb, ATT_W, lc), BF16),
                   jax.ShapeDtypeStruct((nb, lc, KV_W), BF16),
                   jax.ShapeDtypeStruct((nb, KV_W, lc), BF16)],
        name="qk_prep",
    )(p, p, p, cos, sin_a, sin_b, qk_norm_g)


def _attn_kernel(qt_ref, k_ref, vt_ref, o_ref, acc_ref, s_ref, *, tq, kc, n_kc):
    acc_ref[...] = jnp.zeros(acc_ref.shape, F32)
    qts = [qt_ref[0, g * HEAD_DIM:(g + 1) * HEAD_DIM, :] for g in range(ATT_GROUPS)]

    def put_scores(c, slot):
        kch = k_ref[0, pl.ds(pl.multiple_of(c * kc, kc), kc), :]
        for g in range(ATT_GROUPS):
            s_ref[slot, g] = _dot(kch, qts[g])

    def softmax_pv(c, slot, carry):
        vch = vt_ref[0, :, pl.ds(pl.multiple_of(c * kc, kc), kc)]
        out = []
        for g in range(ATT_GROUPS):
            m_old, l_old = carry[g]
            s = s_ref[slot, g]
            m_new = jnp.maximum(m_old, jnp.max(s, axis=0, keepdims=True))
            alpha = jnp.exp2(m_old - m_new)
            p = jnp.exp2(s - m_new)
            l_new = l_old * alpha + jnp.sum(p, axis=0, keepdims=True)
            acc_ref[g] = acc_ref[g] * alpha + _dot(vch, p.astype(BF16))
            out.append((m_new, l_new))
        return tuple(out)

    def body(i, carry):
        c = 2 * i
        put_scores(c + 1, 1)
        carry = softmax_pv(c, 0, carry)
        put_scores(c + 2, 0)
        return softmax_pv(c + 1, 1, carry)

    init = tuple((jnp.full((1, tq), -jnp.inf, F32), jnp.zeros((1, tq), F32)) for _ in range(ATT_GROUPS))
    put_scores(0, 0)
    fin = lax.fori_loop(0, (n_kc - 1) // 2, body, init)
    if n_kc % 2 == 0:
        put_scores(n_kc - 1, 1)
        fin = softmax_pv(n_kc - 2, 0, fin)
    fin = softmax_pv(n_kc - 1, (n_kc - 1) % 2, fin)
    for g in range(ATT_GROUPS):
        o_ref[0, :, g * HEAD_DIM:(g + 1) * HEAD_DIM] = (acc_ref[g] / fin[g][1]).T.astype(BF16)


def _attention(qt, kn, vt, n_ctx):
    nb, lc, _ = kn.shape
    n_lat = lc - n_ctx
    tq = _largest_tile(n_lat, 256, 128)
    assert n_ctx % tq == 0
    kc = _largest_tile(lc, 768, 256)
    gw = ATT_GROUPS * HEAD_DIM
    kern = functools.partial(_attn_kernel, tq=tq, kc=kc, n_kc=lc // kc)
    vmem = 8 * lc * HEAD_DIM * 2 + 4 * ATT_GROUPS * kc * tq * 4 + 8 * gw * tq * 4
    return pl.pallas_call(
        kern,
        grid=(nb, ATT_KV_HEADS, n_lat // tq),
        in_specs=[pl.BlockSpec((1, gw, tq), lambda b, h, i: (b, h, n_ctx // tq + i)),
                  pl.BlockSpec((1, lc, HEAD_DIM), lambda b, h, i: (b, 0, h)),
                  pl.BlockSpec((1, HEAD_DIM, lc), lambda b, h, i: (b, h, 0))],
        out_specs=pl.BlockSpec((1, tq, gw), lambda b, h, i: (b, i, h)),
        out_shape=jax.ShapeDtypeStruct((nb, n_lat, ATT_W), BF16),
        scratch_shapes=[pltpu.VMEM((ATT_GROUPS, HEAD_DIM, tq), F32),
                        pltpu.VMEM((2, ATT_GROUPS, kc, tq), F32)],
        compiler_params=pltpu.CompilerParams(
            dimension_semantics=("arbitrary", "arbitrary", "arbitrary"),
            vmem_limit_bytes=_vmem_limit(vmem)),
        name="attention",
    )(qt, kn, vt)


def _hgrn_chunk(q_ref, v_ref, r_ref, lb, reverse):
    c = HGRN_CHUNK
    q = q_ref[0].astype(F32)
    v = v_ref[0]
    r = r_ref[0].astype(F32)
    sig = jax.nn.sigmoid(r)
    f = lb + (1.0 - lb) * sig
    logf = jnp.log(f)
    k = (1.0 - lb) * (1.0 - sig)
    ti = lax.broadcasted_iota(I32, (c, c), 0)
    si = lax.broadcasted_iota(I32, (c, c), 1)
    tri = jnp.where((si >= ti) if reverse else (si <= ti), 1.0, 0.0).astype(BF16)
    hi = logf.astype(BF16)
    lo = (logf - hi.astype(F32)).astype(BF16)
    bcum = _dot(tri, hi) + _dot(tri, lo)
    b_end = bcum[0:1, :] if reverse else bcum[c - 1:c, :]
    qt = (q * jnp.exp(bcum)).astype(BF16)
    kt = (k * jnp.exp(b_end - bcum)).astype(BF16)
    dec = jnp.exp(b_end)

    nsub = c // HGRN_SUB
    qp, kp, cols = [], [], []
    for blk in range(nsub):
        rs = slice(blk * HGRN_SUB, (blk + 1) * HGRN_SUB)
        cs = slice(blk * HGRN_SUB, c) if reverse else slice(0, (blk + 1) * HGRN_SUB)
        mid = blk * HGRN_SUB + HGRN_SUB // 2
        ref = bcum[mid:mid + 1, :]
        qp.append((q[rs] * jnp.exp(bcum[rs] - ref)).astype(BF16))
        kp.append((k[cs] * jnp.exp(ref - bcum[cs])).astype(BF16))
        cols.append(cs)

    keep = []
    for blk in range(nsub):
        n_cols = cols[blk].stop - cols[blk].start
        trow = blk * HGRN_SUB + lax.broadcasted_iota(I32, (HGRN_SUB, n_cols), 0)
        scol = cols[blk].start + lax.broadcasted_iota(I32, (HGRN_SUB, n_cols), 1)
        keep.append((scol >= trow) if reverse else (scol <= trow))
    return dict(v=v, qt=qt, kt=kt, dec=dec, qp=qp, kp=kp, cols=cols, keep=keep)


def _hgrn_emit(dirs, s_refs, o_refs):
    nsub = HGRN_CHUNK // HGRN_SUB
    heads = [slice(h * HGRN_DK, (h + 1) * HGRN_DK) for h in range(HGRN_HEADS)]
    inter = []
    for d, s_ref in zip(dirs, s_refs):
        row = []
        for h, hs in enumerate(heads):
            st = s_ref[h]
            row.append(_dot_nt(d["qt"][:, hs], st.astype(BF16)))
            s_ref[h] = st * d["dec"][:, hs] + _dot_tn(d["v"][:, hs], d["kt"][:, hs])
        inter.append(row)
    scores = [[[_dot_nt(d["qp"][blk][:, hs], d["kp"][blk][:, hs]) for blk in range(nsub)] for hs in heads]
              for d in dirs]
    for d, sc, it, o_ref in zip(dirs, scores, inter, o_refs):
        outs = []
        for h, hs in enumerate(heads):
            parts = [_dot(jnp.where(d["keep"][blk], sc[h][blk], 0.0).astype(BF16), d["v"][d["cols"][blk], hs])
                     for blk in range(nsub)]
            outs.append(it[h] + jnp.concatenate(parts, axis=0))
        o_ref[0] = jnp.concatenate(outs, axis=1).astype(BF16)


def _hgrn_kernel(qf_ref, vf_ref, rf_ref, qb_ref, vb_ref, rb_ref, lb_ref, of_ref, ob_ref, sf_ref, sb_ref):
    @pl.when(pl.program_id(1) == 0)
    def _():
        sf_ref[...] = jnp.zeros(sf_ref.shape, F32)
        sb_ref[...] = jnp.zeros(sb_ref.shape, F32)

    n_layers = lb_ref.shape[0] // 2

    def lower_bound(direction):
        rows = [lb_ref[direction * n_layers + l:direction * n_layers + l + 1, :] for l in range(n_layers)]
        amax = functools.reduce(jnp.maximum, rows)
        e = [jnp.exp(a - amax) for a in rows]
        return e[0] / functools.reduce(lambda u, w: u + w, e)

    fwd = _hgrn_chunk(qf_ref, vf_ref, rf_ref, lower_bound(0), False)
    bwd = _hgrn_chunk(qb_ref, vb_ref, rb_ref, lower_bound(1), True)
    _hgrn_emit([fwd, bwd], [sf_ref, sb_ref], [of_ref, ob_ref])


def _hgrn_scan(p, hgrn_lb, n_ctx):
    nb, lc, _ = p.shape
    c = HGRN_CHUNK
    assert n_ctx % c == 0 and lc % c == 0
    n_chunks = lc // c
    ctx_chunks = n_ctx // c

    def fwd(col):
        return lambda b, s: (b, s, col)

    def bwd_chunk(s):
        return jnp.where(s < ctx_chunks, ctx_chunks - 1 - s, n_chunks - 1 + ctx_chunks - s)

    def bwd(col):
        return lambda b, s: (b, bwd_chunk(s), col)

    blk = (1, c, HGRN_W)
    lb2 = hgrn_lb.reshape(-1, HGRN_W)
    return pl.pallas_call(
        _hgrn_kernel,
        grid=(nb, n_chunks),
        in_specs=[pl.BlockSpec(blk, fwd(COL_HQ // HGRN_W)), pl.BlockSpec(blk, fwd(COL_HI // HGRN_W)),
                  pl.BlockSpec(blk, fwd(COL_FF // HGRN_W)),
                  pl.BlockSpec(blk, bwd(COL_HQ // HGRN_W)), pl.BlockSpec(blk, bwd(COL_HI // HGRN_W)),
                  pl.BlockSpec(blk, bwd(COL_FB // HGRN_W)),
                  pl.BlockSpec(lb2.shape, lambda b, s: (0, 0))],
        out_specs=[pl.BlockSpec(blk, fwd(0)), pl.BlockSpec(blk, bwd(0))],
        out_shape=[jax.ShapeDtypeStruct((nb, lc, HGRN_W), BF16)] * 2,
        scratch_shapes=[pltpu.VMEM((HGRN_HEADS, HGRN_DK, HGRN_DK), F32)] * 2,
        compiler_params=pltpu.CompilerParams(dimension_semantics=("arbitrary", "arbitrary")),
        name="hgrn_scan",
    )(p, p, p, p, p, p, lb2)


def _merge_kernel(of_ref, ob_ref, hg_ref, gh_ref, ga_ref, oa_ref, x_ref, mod_ref, ng_ref, hng_ref,
                  wb0_ref, wb1_ref, wo_ref, rw_ref, rb_ref,
                  x1_ref, t_ref, ids_ref, wts_ref):
    b = pl.program_id(0)
    m = mod_ref[pl.ds(b, 1), :]
    gate_mix = m[:, 2 * D_MODEL:3 * D_MODEL]
    shift_f = m[:, 3 * D_MODEL:4 * D_MODEL]
    scale_f = m[:, 4 * D_MODEL:5 * D_MODEL]
    hng = hng_ref[...]
    rw = rw_ref[...]
    rw_hi = rw.astype(BF16)
    rw_lo = (rw - rw_hi.astype(F32)).astype(BF16)
    groups = [slice(r, r + MERGE_ROWS) for r in range(0, of_ref.shape[1], MERGE_ROWS)]

    branch = []
    for rs in groups:
        o = of_ref[0, rs, :].astype(F32) + ob_ref[0, rs, :].astype(F32)
        o_h = jnp.concatenate(
            [_rms(o[:, h * HGRN_DK:(h + 1) * HGRN_DK], hng) for h in range(HGRN_HEADS)], axis=1)
        g_raw = hg_ref[0, rs, :].astype(F32)
        o_h = o_h * (g_raw * jax.nn.sigmoid(g_raw))
        branch.append((_dot(o_h.astype(BF16), wb0_ref[...]), _dot(oa_ref[0, rs, :], wb1_ref[...])))
    mixed = []
    for rs, (y_h, y_a) in zip(groups, branch):
        y = (jax.nn.sigmoid(gh_ref[0, rs, :].astype(F32)) * y_h
             + jax.nn.sigmoid(ga_ref[0, rs, :].astype(F32)) * y_a)
        mixed.append(_dot(y.astype(BF16), wo_ref[...]))
    logits_all = []
    for rs, y in zip(groups, mixed):
        x1 = x_ref[0, rs, :] + gate_mix * _rms(y, ng_ref[1:2, :])
        x1_ref[0, rs, :] = x1
        t = _rms(x1, ng_ref[2:3, :]) * (1.0 + scale_f) + shift_f
        t_ref[rs, :] = t
        t_hi = t.astype(BF16)
        t_lo = (t - t_hi.astype(F32)).astype(BF16)
        logits_all.append(_dot_nt(rw_hi, t_hi) + _dot_nt(rw_hi, t_lo) + _dot_nt(rw_lo, t_hi) + rb_ref[...])
    for rs, logits in zip(groups, logits_all):
        eidx = lax.broadcasted_iota(I32, logits.shape, 0).astype(F32)
        vals = []
        for j in range(TOP_K):
            mx = jnp.max(logits, axis=0, keepdims=True)
            idx = jnp.min(jnp.where(logits == mx, eidx, float(N_EXPERTS)), axis=0, keepdims=True)
            ids_ref[j:j + 1, rs] = idx.astype(I32)
            vals.append(mx)
            logits = jnp.where(eidx == idx, -jnp.inf, logits)
        ex = [jnp.exp(vj - vals[0]) for vj in vals]
        den = ex[0] + ex[1] + ex[2] + ex[3]
        for j in range(TOP_K):
            wts_ref[j:j + 1, rs] = ex[j] / den


def _merge(o_f, o_b, p, o_a, x, mod, norm_g, hgrn_norm_g, wb0, wb1, wo, router_wt, router_b, n_ctx):
    nb, n_lat, d = x.shape
    tm = _largest_tile(n_lat, 256, 128)
    assert n_ctx % tm == 0
    off = n_ctx // tm
    nt = n_lat // tm
    row = (1, tm, d)

    def pcol(col):
        return pl.BlockSpec(row, lambda b, i: (b, off + i, col // d))

    def full(a):
        return pl.BlockSpec(a.shape, lambda b, i: (0,) * a.ndim)

    tok = nb * n_lat
    return pl.pallas_call(
        _merge_kernel,
        grid=(nb, nt),
        in_specs=[pl.BlockSpec(row, lambda b, i: (b, off + i, 0)),
                  pl.BlockSpec(row, lambda b, i: (b, off + i, 0)),
                  pcol(COL_HG), pcol(COL_GH), pcol(COL_GA),
                  pl.BlockSpec(row, lambda b, i: (b, i, 0)),
                  pl.BlockSpec(row, lambda b, i: (b, i, 0)),
                  full(mod), full(norm_g), full(hgrn_norm_g), full(wb0), full(wb1), full(wo),
                  full(router_wt), full(router_b)],
        out_specs=[pl.BlockSpec(row, lambda b, i: (b, i, 0)),
                   pl.BlockSpec((tm, d), lambda b, i: (b * nt + i, 0)),
                   pl.BlockSpec((TOP_K, tm), lambda b, i: (0, b * nt + i)),
                   pl.BlockSpec((TOP_K, tm), lambda b, i: (0, b * nt + i))],
        out_shape=[jax.ShapeDtypeStruct((nb, n_lat, d), F32),
                   jax.ShapeDtypeStruct((tok, d), F32),
                   jax.ShapeDtypeStruct((TOP_K, tok), I32),
                   jax.ShapeDtypeStruct((TOP_K, tok), F32)],
        compiler_params=pltpu.CompilerParams(
            dimension_semantics=("arbitrary", "arbitrary"),
            vmem_limit_bytes=_vmem_limit(40 * 1024 * 1024)),
        name="merge_router",
    )(o_f, o_b, p, p, p, o_a, x, mod, norm_g, hgrn_norm_g, wb0, wb1, wo, router_wt, router_b)


def _rank_kernel(ids_ref, rank_ref, cnt_ref, *, tt):
    eidx = lax.broadcasted_iota(I32, (N_EXPERTS, tt), 0)
    si = lax.broadcasted_iota(I32, (tt, tt), 0)
    ti = lax.broadcasted_iota(I32, (tt, tt), 1)
    before = jnp.where(si < ti, 1.0, 0.0).astype(BF16)
    seen = jnp.zeros((N_EXPERTS, 1), F32)
    for j in range(TOP_K):
        onehot = eidx == ids_ref[j:j + 1, :]
        oh = jnp.where(onehot, 1.0, 0.0)
        earlier = _dot(oh.astype(BF16), before)
        rank = jnp.sum(jnp.where(onehot, seen + earlier, 0.0), axis=0, keepdims=True)
        rank_ref[j:j + 1, :] = rank.astype(I32)
        seen = seen + jnp.sum(oh, axis=1, keepdims=True)
    cnt_ref[0] = seen.astype(I32)


def _expert_rank(ids, tt):
    _, tok = ids.shape
    return pl.pallas_call(
        functools.partial(_rank_kernel, tt=tt),
        grid=(tok // tt,),
        in_specs=[pl.BlockSpec((TOP_K, tt), lambda i: (0, i))],
        out_specs=[pl.BlockSpec((TOP_K, tt), lambda i: (0, i)),
                   pl.BlockSpec((1, N_EXPERTS, 1), lambda i: (i, 0, 0))],
        out_shape=[jax.ShapeDtypeStruct((TOP_K, tok), I32),
                   jax.ShapeDtypeStruct((tok // tt, N_EXPERTS, 1), I32)],
        compiler_params=pltpu.CompilerParams(dimension_semantics=("arbitrary",)),
        name="expert_rank",
    )(ids)


def _start_blocks(n_blocks, make_copy):
    def start(b, carry):
        make_copy(b).start()
        return carry

    lax.fori_loop(0, n_blocks, start, 0)


def _wait_blocks(n_blocks, make_copy, make_group_copy):
    def wait_group(g, carry):
        make_group_copy().wait()
        return carry

    def wait_one(b, carry):
        make_copy(0).wait()
        return carry

    lax.fori_loop(0, lax.shift_right_logical(n_blocks, WAIT_GROUP_LOG2), wait_group, 0)
    lax.fori_loop(0, jnp.bitwise_and(n_blocks, (1 << WAIT_GROUP_LOG2) - 1), wait_one, 0)


def _scatter_kernel(zt_ref, zvalid_ref, nblk_ref, bdst_ref, lp_ref, t_ref, xs_ref, stage, zero_buf, sem, zsem,
                    *, tt, n_stage):
    @pl.when(pl.program_id(0) == 0)
    def _():
        zero_buf[...] = jnp.zeros(zero_buf.shape, F32)

        def zero_copy(k):
            return pltpu.make_async_copy(zero_buf, xs_ref.at[pl.ds(zt_ref[k] * MOE_TILE, MOE_TILE)], zsem)

        def start(k, carry):
            @pl.when(zvalid_ref[k] == 1)
            def _():
                zero_copy(k).start()
            return carry

        def wait(k, carry):
            @pl.when(zvalid_ref[k] == 1)
            def _():
                zero_copy(k).wait()
            return carry

        lax.fori_loop(0, zt_ref.shape[0], start, 0)
        lax.fori_loop(0, zt_ref.shape[0], wait, 0)

    i = pl.program_id(0)
    slot = i % 2
    t_bf = t_ref[...].astype(BF16)
    lps = [lp_ref[j:j + 1, :] for j in range(TOP_K)]
    for rc in range(n_stage // PERM_ROWS):
        row = rc * PERM_ROWS + lax.broadcasted_iota(I32, (PERM_ROWS, tt), 0)
        sel = jnp.where(row == lps[TOP_K - 1], 1.0, 0.0)
        for j in range(TOP_K - 2, -1, -1):
            sel = jnp.where(row == lps[j], 1.0, sel)
        stage[slot, rc * PERM_ROWS:(rc + 1) * PERM_ROWS, :] = _dot(sel.astype(BF16), t_bf)

    def copy(b):
        src = pl.multiple_of(b * RUN_ALIGN, RUN_ALIGN)
        dst = pl.multiple_of(bdst_ref[0, 0, b], RUN_ALIGN)
        return pltpu.make_async_copy(stage.at[slot, pl.ds(src, RUN_ALIGN)], xs_ref.at[pl.ds(dst, RUN_ALIGN)],
                                     sem.at[slot])

    def waiter(s):
        rows = RUN_ALIGN << WAIT_GROUP_LOG2
        return (lambda b: pltpu.make_async_copy(stage.at[s, pl.ds(0, RUN_ALIGN)],
                                                xs_ref.at[pl.ds(0, RUN_ALIGN)], sem.at[s]),
                lambda: pltpu.make_async_copy(stage.at[s, pl.ds(0, rows)], xs_ref.at[pl.ds(0, rows)], sem.at[s]))

    _start_blocks(nblk_ref[i], copy)

    @pl.when(i > 0)
    def _():
        _wait_blocks(nblk_ref[i - 1], *waiter(1 - slot))

    @pl.when(i == pl.num_programs(0) - 1)
    def _():
        _wait_blocks(nblk_ref[i], *waiter(slot))


def _scatter_rows(zero_tiles, zero_valid, n_blocks, block_rows, lp, t, n_rows, tt, n_stage):
    tok, d = t.shape
    nb = block_rows.shape[2]
    return pl.pallas_call(
        functools.partial(_scatter_kernel, tt=tt, n_stage=n_stage),
        grid_spec=pltpu.PrefetchScalarGridSpec(
            num_scalar_prefetch=3,
            grid=(tok // tt,),
            in_specs=[pl.BlockSpec((1, 1, nb), lambda i, *_: (i, 0, 0), memory_space=pltpu.SMEM),
                      pl.BlockSpec((TOP_K, tt), lambda i, *_: (0, i)),
                      pl.BlockSpec((tt, d), lambda i, *_: (i, 0))],
            out_specs=pl.BlockSpec(memory_space=pl.ANY),
            scratch_shapes=[pltpu.VMEM((2, n_stage, d), F32), pltpu.VMEM((MOE_TILE, d), F32),
                            pltpu.SemaphoreType.DMA((2,)), pltpu.SemaphoreType.DMA(())]),
        out_shape=jax.ShapeDtypeStruct((n_rows, d), F32),
        compiler_params=pltpu.CompilerParams(
            dimension_semantics=("arbitrary",), vmem_limit_bytes=_vmem_limit(48 * 1024 * 1024)),
        name="moe_scatter",
    )(zero_tiles, zero_valid, n_blocks, block_rows, lp, t)


def _expert_kernel(te_ref, first_ref, nused_ref, slot_ref, next_ref, xs_ref, wu_hbm, bu_ref, wd_hbm, bd_ref,
                   ys_ref, wu32, wd32, wu_bf, wd_bf, sem_u, sem_d):
    i = pl.program_id(0)

    def weight_copies(e, s):
        return (pltpu.make_async_copy(wu_hbm.at[e], wu32.at[s], sem_u.at[s]),
                pltpu.make_async_copy(wd_hbm.at[e], wd32.at[s], sem_d.at[s]))

    @pl.when(i >= nused_ref[0])
    def _():
        ys_ref[...] = jnp.zeros(ys_ref.shape, F32)

    @pl.when(i < nused_ref[0])
    def _():
        @pl.when(first_ref[i] == 1)
        def _():
            s = slot_ref[i]

            @pl.when(i == 0)
            def _():
                for cp in weight_copies(te_ref[0], 0):
                    cp.start()

            for cp in weight_copies(te_ref[i], s):
                cp.wait()
            wu_bf[...] = wu32[s].astype(BF16)
            wd_bf[...] = wd32[s].astype(BF16)

            @pl.when(next_ref[i] >= 0)
            def _():
                for cp in weight_copies(next_ref[i], 1 - s):
                    cp.start()

        h = _dot(xs_ref[...].astype(BF16), wu_bf[...]) + bu_ref[0]
        glu = jnp.minimum(h[:, :D_EXPERT], SWIGLU_LIMIT)
        lin = jnp.clip(h[:, D_EXPERT:], -SWIGLU_LIMIT, SWIGLU_LIMIT)
        a = glu * jax.nn.sigmoid(SWIGLU_ALPHA * glu) * (lin + 1.0)
        ys_ref[...] = _dot(a.astype(BF16), wd_bf[...]) + bd_ref[0]


def _experts(xs, tile_expert, tile_first, n_used, tile_slot, next_expert, w_up, b_up, w_down, b_down):
    n_rows, d = xs.shape
    tm = MOE_TILE
    n_tiles = n_rows // tm

    def tile(i, te, first, nused, *_):
        return (jnp.minimum(i, nused[0] - 1), 0)

    def bsel(i, te, *_):
        return (te[i], 0, 0)

    vmem = 2 * (d * 2 * D_EXPERT * 4 + D_EXPERT * d * 4) + d * 2 * D_EXPERT * 2 + D_EXPERT * d * 2 \
        + 4 * tm * d * 4 + 6 * tm * 2 * D_EXPERT * 4
    return pl.pallas_call(
        _expert_kernel,
        grid_spec=pltpu.PrefetchScalarGridSpec(
            num_scalar_prefetch=5,
            grid=(n_tiles,),
            in_specs=[pl.BlockSpec((tm, d), tile),
                      pl.BlockSpec(memory_space=pl.ANY),
                      pl.BlockSpec((1, 1, 2 * D_EXPERT), bsel),
                      pl.BlockSpec(memory_space=pl.ANY),
                      pl.BlockSpec((1, 1, d), bsel)],
            out_specs=pl.BlockSpec((tm, d), lambda i, *_: (i, 0)),
            scratch_shapes=[pltpu.VMEM((2, d, 2 * D_EXPERT), F32), pltpu.VMEM((2, D_EXPERT, d), F32),
                            pltpu.VMEM((d, 2 * D_EXPERT), BF16), pltpu.VMEM((D_EXPERT, d), BF16),
                            pltpu.SemaphoreType.DMA((2,)), pltpu.SemaphoreType.DMA((2,))]),
        out_shape=jax.ShapeDtypeStruct((n_rows, d), F32),
        compiler_params=pltpu.CompilerParams(
            dimension_semantics=("arbitrary",), vmem_limit_bytes=_vmem_limit(vmem)),
        name="moe_experts",
    )(tile_expert, tile_first, n_used, tile_slot, next_expert, xs, w_up, b_up.reshape(N_EXPERTS, 1, -1), w_down,
      b_down.reshape(N_EXPERTS, 1, -1))


def _combine_kernel(nblk_ref, bsrc_ref, ys_ref, lp_ref, w_ref, x1_ref, mod_ref, ng_ref, o_ref, stage, sem,
                    *, tt, n_stage, tiles_per_batch):
    i = pl.program_id(0)
    slot = i % 2

    def fetch(tile, s):
        def copy(b):
            src = pl.multiple_of(bsrc_ref[tile, b], RUN_ALIGN)
            dst = pl.multiple_of(b * RUN_ALIGN, RUN_ALIGN)
            return pltpu.make_async_copy(ys_ref.at[pl.ds(src, RUN_ALIGN)], stage.at[s, pl.ds(dst, RUN_ALIGN)],
                                         sem.at[s])
        return copy

    def group_copy():
        rows = RUN_ALIGN << WAIT_GROUP_LOG2
        return pltpu.make_async_copy(ys_ref.at[pl.ds(0, rows)], stage.at[slot, pl.ds(0, rows)], sem.at[slot])

    @pl.when(i == 0)
    def _():
        stage[...] = jnp.zeros(stage.shape, F32)
        _start_blocks(nblk_ref[0], fetch(0, 0))

    @pl.when(i + 1 < pl.num_programs(0))
    def _():
        _start_blocks(nblk_ref[i + 1], fetch(i + 1, 1 - slot))

    _wait_blocks(nblk_ref[i], fetch(i, slot), group_copy)

    lp = [jnp.broadcast_to(lp_ref[:, j:j + 1], (tt, PERM_ROWS)) for j in range(TOP_K)]
    w = [jnp.broadcast_to(w_ref[:, j:j + 1], (tt, PERM_ROWS)) for j in range(TOP_K)]
    f = jnp.zeros((tt, D_MODEL), F32)
    for lc in range(n_stage // PERM_ROWS):
        col = lc * PERM_ROWS + lax.broadcasted_iota(I32, (tt, PERM_ROWS), 1)
        wp = jnp.where(col == lp[TOP_K - 1], w[TOP_K - 1], 0.0)
        for j in range(TOP_K - 2, -1, -1):
            wp = jnp.where(col == lp[j], w[j], wp)
        f = f + _dot(wp.astype(BF16), stage[slot, lc * PERM_ROWS:(lc + 1) * PERM_ROWS, :].astype(BF16))
    b = i // tiles_per_batch
    gate_f = mod_ref[pl.ds(b, 1), 5 * D_MODEL:6 * D_MODEL]
    o_ref[...] = x1_ref[...] + gate_f * _rms(f, ng_ref[3:4, :])


def _combine(n_blocks, block_rows, ys, lp_t, wts_t, x1, mod, norm_g, tt, n_stage, tiles_per_batch):
    tok, d = x1.shape
    kern = functools.partial(_combine_kernel, tt=tt, n_stage=n_stage, tiles_per_batch=tiles_per_batch)
    return pl.pallas_call(
        kern,
        grid_spec=pltpu.PrefetchScalarGridSpec(
            num_scalar_prefetch=2,
            grid=(tok // tt,),
            in_specs=[pl.BlockSpec(memory_space=pl.ANY),
                      pl.BlockSpec((tt, TOP_K), lambda i, *_: (i, 0)),
                      pl.BlockSpec((tt, TOP_K), lambda i, *_: (i, 0)),
                      pl.BlockSpec((tt, d), lambda i, *_: (i, 0)),
                      pl.BlockSpec(mod.shape, lambda i, *_: (0, 0)),
                      pl.BlockSpec(norm_g.shape, lambda i, *_: (0, 0))],
            out_specs=pl.BlockSpec((tt, d), lambda i, *_: (i, 0)),
            scratch_shapes=[pltpu.VMEM((2, n_stage, d), F32), pltpu.SemaphoreType.DMA((2,))]),
        out_shape=jax.ShapeDtypeStruct((tok, d), F32),
        compiler_params=pltpu.CompilerParams(
            dimension_semantics=("arbitrary",), vmem_limit_bytes=_vmem_limit(48 * 1024 * 1024)),
        name="moe_combine",
    )(n_blocks, block_rows[:, 0, :], ys, lp_t, wts_t, x1, mod, norm_g)


def _rope_tables(n_ctx, n_lat):
    inv = ROPE_THETA ** (-np.arange(0, ROPE_AXIS_DIM, 2, dtype=np.float32) / ROPE_AXIS_DIM)
    idx = jnp.arange(n_lat, dtype=I32)
    rows = (idx // GRID_W).astype(F32)
    cols = (idx % GRID_W).astype(F32)
    half = ROPE_AXIS_DIM // 2
    inv = jnp.asarray(inv, F32)
    ang_r = rows[:, None] * inv[None, :]
    ang_c = cols[:, None] * inv[None, :]
    ang = jnp.concatenate([ang_r, ang_r, ang_c, ang_c], axis=1)
    cos, sin = jnp.cos(ang), jnp.sin(ang)
    lane = np.arange(HEAD_DIM)
    first = jnp.asarray((lane % ROPE_AXIS_DIM) < half)
    sin_a = jnp.where(first, -sin, 0.0)
    sin_b = jnp.where(first, 0.0, sin)
    pad1 = jnp.ones((n_ctx, HEAD_DIM), F32)
    pad0 = jnp.zeros((n_ctx, HEAD_DIM), F32)
    return (jnp.concatenate([pad1, cos], axis=0), jnp.concatenate([pad0, sin_a], axis=0),
            jnp.concatenate([pad0, sin_b], axis=0))


def kernel(x, c, ctx, c_ctx, w_mod, b_mod, norm_g, w_in, hgrn_lb, hgrn_norm_g, qk_norm_g, w_branch, w_out,
           router_w, router_b, w_up, b_up, w_down, b_down):
    nb, n_lat, d = x.shape
    n_ctx = ctx.shape[1]
    assert d == D_MODEL and w_mod.shape[0] == 1, "single-layer kernel"
    tok = nb * n_lat

    cv = jnp.zeros((8, d), F32).at[:nb].set(c).at[nb].set(c_ctx)
    mod = _modulation(cv, w_mod[0], b_mod[0])

    xc = jnp.concatenate([ctx, x], axis=1)
    sizes = np.cumsum([0, HGRN_W, HGRN_W, HGRN_W, HGRN_W, HGRN_W, ATT_W, KV_W, KV_W, D_MODEL, D_MODEL])
    order = [0, 1, 2, 3, 4, 5, 8, 9, 6, 7]
    w_in_bf = jnp.concatenate([w_in[0][:, sizes[i]:sizes[i + 1]] for i in order], axis=1).astype(BF16)
    p = _in_projection(xc, mod, norm_g[0], w_in_bf, n_ctx)

    cos, sin_a, sin_b = _rope_tables(n_ctx, n_lat)
    qt, kn, vt = _qk_prep(p, cos, sin_a, sin_b, qk_norm_g[0])
    o_a = _attention(qt, kn, vt, n_ctx)
    o_f, o_b = _hgrn_scan(p, hgrn_lb, n_ctx)

    x1, t, ids, wts = _merge(
        o_f, o_b, p, o_a, x, mod, norm_g[0], hgrn_norm_g, w_branch[0, 0].astype(BF16),
        w_branch[0, 1].astype(BF16), w_out[0].astype(BF16), router_w[0].T, router_b[0].reshape(N_EXPERTS, 1),
        n_ctx)

    tt = _largest_tile(n_lat, DISPATCH_TILE, 128)
    n_tt = tok // tt
    rank, counts = _expert_rank(ids, tt)
    counts = counts[:, :, 0]
    run = (counts + RUN_ALIGN - 1) // RUN_ALIGN * RUN_ALIGN
    run_end = jnp.cumsum(run, axis=1)
    stage_off = run_end - run
    expert_rows = jnp.sum(run, axis=0)
    tiles_e = (expert_rows + MOE_TILE - 1) // MOE_TILE
    tile_end = jnp.cumsum(tiles_e)
    starts = (tile_end - tiles_e) * MOE_TILE
    run_dst = starts[None, :] + jnp.cumsum(run, axis=0) - run
    n_tiles = -(-(tok * TOP_K + N_EXPERTS * n_tt * (RUN_ALIGN - 1)) // MOE_TILE) + N_EXPERTS
    tile_idx = jnp.arange(n_tiles, dtype=I32)
    tile_expert = jnp.sum((tile_end[None, :] <= tile_idx[:, None]).astype(I32), axis=1)
    tile_expert = jnp.minimum(tile_expert, N_EXPERTS - 1)
    n_used = tile_end[-1:].astype(I32)
    tile_expert = jnp.where(tile_idx < n_used[0], tile_expert, tile_expert[jnp.maximum(n_used[0] - 1, 0)])
    tile_first = jnp.concatenate([jnp.ones((1,), I32), (tile_expert[1:] != tile_expert[:-1]).astype(I32)])
    e_idx = jnp.arange(N_EXPERTS, dtype=I32)
    used = tiles_e > 0
    e_slot = (jnp.cumsum(used.astype(I32)) - 1) % 2
    later = used[None, :] & (e_idx[None, :] > e_idx[:, None])
    e_next = jnp.min(jnp.where(later, e_idx[None, :], N_EXPERTS), axis=1)
    e_next = jnp.where(e_next == N_EXPERTS, -1, e_next)
    tile_onehot = tile_expert[:, None] == e_idx[None, :]
    tile_slot = jnp.sum(jnp.where(tile_onehot, e_slot[None, :], 0), axis=1).astype(I32)
    next_expert = jnp.sum(jnp.where(tile_onehot, e_next[None, :], 0), axis=1).astype(I32)

    n_stage = -(-(TOP_K * tt + N_EXPERTS * (RUN_ALIGN - 1)) // PERM_ROWS) * PERM_ROWS
    ids_t = ids.reshape(TOP_K, n_tt, tt)
    onehot = ids_t[..., None] == jnp.arange(N_EXPERTS, dtype=I32)
    lp = jnp.sum(jnp.where(onehot, stage_off[None, :, None, :], 0), axis=-1).reshape(TOP_K, tok) + rank
    blk_row = jnp.arange(n_stage // RUN_ALIGN, dtype=I32) * RUN_ALIGN
    blk_expert = jnp.minimum(jnp.sum((run_end[:, None, :] <= blk_row[None, :, None]).astype(I32), axis=-1),
                             N_EXPERTS - 1)
    blk_onehot = blk_expert[..., None] == jnp.arange(N_EXPERTS, dtype=I32)
    blk_dst = jnp.sum(jnp.where(blk_onehot, (run_dst - stage_off)[:, None, :], 0), axis=-1) + blk_row[None, :]
    n_blocks = (run_end[:, -1] // RUN_ALIGN).astype(I32)
    blk_dst = jnp.where(blk_row[None, :] < run_end[:, -1:], blk_dst, 0).astype(I32)[:, None, :]

    n_tail = n_tiles - (tok * TOP_K) // MOE_TILE
    tail = n_used[0] + jnp.arange(n_tail, dtype=I32)
    zero_tiles = jnp.clip(jnp.concatenate([tile_end.astype(I32) - 1, tail]), 0, n_tiles - 1)
    zero_valid = jnp.concatenate([expert_rows % MOE_TILE != 0, tail < n_tiles]).astype(I32)

    xs = _scatter_rows(zero_tiles, zero_valid, n_blocks, blk_dst, lp, t, n_tiles * MOE_TILE, tt, n_stage)
    ys = _experts(xs, tile_expert, tile_first, n_used, tile_slot, next_expert, w_up[0], b_up[0], w_down[0],
                  b_down[0])
    out = _combine(n_blocks, blk_dst, ys, lp.T, wts.T, x1.reshape(tok, d), mod, norm_g[0], tt, n_stage,
                   n_lat // tt)
    return out.reshape(nb, n_lat, d)
```

```python
import functools

import jax
import jax.numpy as jnp
import numpy as np
from jax import lax
from jax.experimental import pallas as pl
from jax.experimental.pallas import tpu as pltpu

F32 = jnp.float32
BF16 = jnp.bfloat16
I32 = jnp.int32

D_MODEL = 1024
GRID_W = 64
N_MOD = 6
EPS = 1e-6
HGRN_HEADS = 8
HGRN_DK = 128
HGRN_W = HGRN_HEADS * HGRN_DK
ATT_HEADS = 8
ATT_KV_HEADS = 2
ATT_GROUPS = ATT_HEADS // ATT_KV_HEADS
HEAD_DIM = 128
ATT_W = ATT_HEADS * HEAD_DIM
KV_W = ATT_KV_HEADS * HEAD_DIM
ROPE_AXIS_DIM = HEAD_DIM // 2
ROPE_THETA = 10000.0
N_EXPERTS = 32
TOP_K = 4
D_EXPERT = 1024
SWIGLU_LIMIT = 7.0
SWIGLU_ALPHA = 1.702

COL_HQ, COL_HI, COL_FF, COL_FB, COL_HG, COL_AQ, COL_GH, COL_GA = (i * 1024 for i in range(8))
COL_K = 8 * 1024
COL_V = COL_K + KV_W
IN_COLS = COL_V + KV_W

V7X_VMEM_BYTES = 64 * 1024 * 1024
LANES = 128

HGRN_CHUNK = 64
HGRN_STEP_CHUNKS = 4
HGRN_SUB = 16
MOE_TILE = 256
MERGE_ROWS = 128
INPROJ_CHAINS = 2
DISPATCH_TILE = 512
RUN_ALIGN = 8
PERM_ROWS = 256
WAIT_GROUP_LOG2 = 5


def _vmem_limit(nbytes):
    return int(min(V7X_VMEM_BYTES - 6 * 1024 * 1024, max(nbytes, 32 * 1024 * 1024)))


def _largest_tile(n, cap, mult):
    best = None
    for t in range(mult, min(n, cap) + 1, mult):
        if n % t == 0:
            best = t
    assert best is not None, (n, cap, mult)
    return best


def _rms(x, g):
    return x * lax.rsqrt(jnp.mean(x * x, axis=-1, keepdims=True) + EPS) * g


def _dot(a, b):
    return jnp.dot(a, b, preferred_element_type=F32)


def _dot_nt(a, b):
    return lax.dot_general(a, b, (((1,), (1,)), ((), ())), preferred_element_type=F32)


def _dot_tn(a, b):
    return lax.dot_general(a, b, (((0,), (0,)), ((), ())), preferred_element_type=F32)


def _mod_kernel(cv_ref, w_ref, b_ref, o_ref):
    cv = cv_ref[...]
    s = cv * jax.nn.sigmoid(cv)
    o_ref[...] = jnp.dot(s, w_ref[...], preferred_element_type=F32,
                         precision=lax.Precision.HIGHEST) + b_ref[...]


def _modulation(cv, w_mod, b_mod):
    rows, d = cv.shape
    n = w_mod.shape[1]
    tn = 1024
    return pl.pallas_call(
        _mod_kernel,
        grid=(n // tn,),
        in_specs=[pl.BlockSpec((rows, d), lambda j: (0, 0)),
                  pl.BlockSpec((d, tn), lambda j: (0, j)),
                  pl.BlockSpec((1, tn), lambda j: (0, j))],
        out_specs=pl.BlockSpec((rows, tn), lambda j: (0, j)),
        out_shape=jax.ShapeDtypeStruct((rows, n), F32),
        name="modulation",
    )(cv, w_mod, b_mod.reshape(1, n))


def _inproj_kernel(x_ref, mod_ref, g_ref, w_ref, o_ref, *, tm, n_ctx, n_batch):
    b = pl.program_id(1)
    i = pl.program_id(2)
    m_lat = mod_ref[pl.ds(b, 1), :]
    m_ctx = mod_ref[n_batch:n_batch + 1, :]
    rows = tm // INPROJ_CHAINS
    us = []
    for c in range(INPROJ_CHAINS):
        xn = _rms(x_ref[0, c * rows:(c + 1) * rows, :], g_ref[0:1, :])
        row = i * tm + c * rows + lax.broadcasted_iota(I32, (rows, 1), 0)
        is_ctx = row < n_ctx
        shift = jnp.where(is_ctx, m_ctx[:, 0:D_MODEL], m_lat[:, 0:D_MODEL])
        scale = jnp.where(is_ctx, m_ctx[:, D_MODEL:2 * D_MODEL], m_lat[:, D_MODEL:2 * D_MODEL])
        us.append((xn * (1.0 + scale) + shift).astype(BF16))
    for c in range(INPROJ_CHAINS):
        o_ref[0, c * rows:(c + 1) * rows, :] = _dot(us[c], w_ref[...]).astype(BF16)


def _in_projection(xc, mod, norm_g, w_in_bf, n_ctx):
    nb, lc, d = xc.shape
    n = w_in_bf.shape[1]
    tm = _largest_tile(lc, 1056, 16)
    tn = 2176
    assert n % tn == 0
    kern = functools.partial(_inproj_kernel, tm=tm, n_ctx=n_ctx, n_batch=nb)
    vmem = 2 * (tm * d * 4 + d * tn * 2 + tm * tn * 2) + 6 * tm * d * 4
    return pl.pallas_call(
        kern,
        grid=(n // tn, nb, lc // tm),
        in_specs=[pl.BlockSpec((1, tm, d), lambda j, b, i: (b, i, 0)),
                  pl.BlockSpec(mod.shape, lambda j, b, i: (0, 0)),
                  pl.BlockSpec(norm_g.shape, lambda j, b, i: (0, 0)),
                  pl.BlockSpec((d, tn), lambda j, b, i: (0, j))],
        out_specs=pl.BlockSpec((1, tm, tn), lambda j, b, i: (b, i, j)),
        out_shape=jax.ShapeDtypeStruct((nb, lc, n), BF16),
        compiler_params=pltpu.CompilerParams(
            dimension_semantics=("arbitrary", "arbitrary", "arbitrary"),
            vmem_limit_bytes=_vmem_limit(vmem)),
        name="in_projection",
    )(xc, mod, norm_g, w_in_bf)


def _rotate_half_matrix():
    half = ROPE_AXIS_DIM // 2
    i = lax.broadcasted_iota(I32, (HEAD_DIM, HEAD_DIM), 0)
    j = lax.broadcasted_iota(I32, (HEAD_DIM, HEAD_DIM), 1)
    first = jnp.bitwise_and(j, ROPE_AXIS_DIM - 1) < half
    r = jnp.where(i == j + half, jnp.where(first, -1.0, 0.0), jnp.where(i == j - half, jnp.where(first, 0.0, 1.0), 0.0))
    return r.astype(BF16)


def _rope(xn, cos, sin, rot):
    hi = xn.astype(BF16)
    lo = (xn - hi.astype(F32)).astype(BF16)
    return xn * cos + (_dot(hi, rot) + _dot(lo, rot)) * sin


def _qkprep_kernel(q_ref, k_ref, v_ref, cos_ref, sa_ref, sb_ref, g_ref, qt_ref, ko_ref, vt_ref):
    cos, sn = cos_ref[...], sb_ref[...] - sa_ref[...]
    rot = _rotate_half_matrix()
    gq, gk = g_ref[0:1, :], g_ref[1:2, :]
    qscale = HEAD_DIM ** -0.5 * np.log2(np.e)
    for h in range(ATT_HEADS):
        hs = slice(h * HEAD_DIM, (h + 1) * HEAD_DIM)
        xn = _rms(q_ref[0, :, hs].astype(F32), gq)
        qt_ref[0, hs, :] = (_rope(xn, cos, sn, rot) * qscale).T.astype(BF16)
    for h in range(ATT_KV_HEADS):
        hs = slice(h * HEAD_DIM, (h + 1) * HEAD_DIM)
        xn = _rms(k_ref[0, :, hs].astype(F32), gk)
        ko_ref[0, :, hs] = _rope(xn, cos, sn, rot).astype(BF16)
        vt_ref[0, hs, :] = v_ref[0, :, hs].astype(F32).T.astype(BF16)


def _qk_prep(p, cos, sin_a, sin_b, qk_norm_g):
    nb, lc, _ = p.shape
    tm = _largest_tile(lc, 512, 128)
    tab = pl.BlockSpec((tm, HEAD_DIM), lambda b, i: (i, 0))
    return pl.pallas_call(
        _qkprep_kernel,
        grid=(nb, lc // tm),
        in_specs=[pl.BlockSpec((1, tm, ATT_W), lambda b, i: (b, i, COL_AQ // ATT_W)),
                  pl.BlockSpec((1, tm, KV_W), lambda b, i: (b, i, COL_K // KV_W)),
                  pl.BlockSpec((1, tm, KV_W), lambda b, i: (b, i, COL_V // KV_W)),
                  tab, tab, tab,
                  pl.BlockSpec(qk_norm_g.shape, lambda b, i: (0, 0))],
        out_specs=[pl.BlockSpec((1, ATT_W, tm), lambda b, i: (b, 0, i)),
                   pl.BlockSpec((1, tm, KV_W), lambda b, i: (b, i, 0)),
                   pl.BlockSpec((1, KV_W, tm), lambda b, i: (b, 0, i))],
        out_shape=[jax.ShapeDtypeStruct((nb, ATT_W, lc), BF16),
                   jax.ShapeDtypeStruct((nb, lc, KV_W), BF16),
                   jax.ShapeDtypeStruct((nb, KV_W, lc), BF16)],
        name="qk_prep",
    )(p, p, p, cos, sin_a, sin_b, qk_norm_g)


def _attn_kernel(qt_ref, k_ref, vt_ref, o_ref, acc_ref, s_ref, *, tq, kc, n_kc):
    acc_ref[...] = jnp.zeros(acc_ref.shape, F32)
    qts = [qt_ref[0, g * HEAD_DIM:(g + 1) * HEAD_DIM, :] for g in range(ATT_GROUPS)]

    def put_scores(c, slot):
        kch = k_ref[0, pl.ds(pl.multiple_of(c * kc, kc), kc), :]
        for g in range(ATT_GROUPS):
            s_ref[slot, g] = _dot(kch, qts[g])

    def softmax_pv(c, slot, carry):
        vch = vt_ref[0, :, pl.ds(pl.multiple_of(c * kc, kc), kc)]
        out = []
        for g in range(ATT_GROUPS):
            m_old, l_old = carry[g]
            s = s_ref[slot, g]
            m_new = jnp.maximum(m_old, jnp.max(s, axis=0, keepdims=True))
            alpha = jnp.exp2(m_old - m_new)
            p = jnp.exp2(s - m_new)
            l_new = l_old * alpha + jnp.sum(p, axis=0, keepdims=True)
            acc_ref[g] = acc_ref[g] * alpha + _dot(vch, p.astype(BF16))
            out.append((m_new, l_new))
        return tuple(out)

    def body(i, carry):
        c = 2 * i
        put_scores(c + 1, 1)
        carry = softmax_pv(c, 0, carry)
        put_scores(c + 2, 0)
        return softmax_pv(c + 1, 1, carry)

    init = tuple((jnp.full((1, tq), -jnp.inf, F32), jnp.zeros((1, tq), F32)) for _ in range(ATT_GROUPS))
    put_scores(0, 0)
    fin = lax.fori_loop(0, (n_kc - 1) // 2, body, init)
    if n_kc % 2 == 0:
        put_scores(n_kc - 1, 1)
        fin = softmax_pv(n_kc - 2, 0, fin)
    fin = softmax_pv(n_kc - 1, (n_kc - 1) % 2, fin)
    for g in range(ATT_GROUPS):
        o_ref[0, :, g * HEAD_DIM:(g + 1) * HEAD_DIM] = (acc_ref[g] / fin[g][1]).T.astype(BF16)


def _attention(qt, kn, vt, n_ctx):
    nb, lc, _ = kn.shape
    n_lat = lc - n_ctx
    tq = _largest_tile(n_lat, 256, 128)
    assert n_ctx % tq == 0
    kc = _largest_tile(lc, 768, 256)
    gw = ATT_GROUPS * HEAD_DIM
    kern = functools.partial(_attn_kernel, tq=tq, kc=kc, n_kc=lc // kc)
    vmem = 8 * lc * HEAD_DIM * 2 + 4 * ATT_GROUPS * kc * tq * 4 + 8 * gw * tq * 4
    return pl.pallas_call(
        kern,
        grid=(nb, ATT_KV_HEADS, n_lat // tq),
        in_specs=[pl.BlockSpec((1, gw, tq), lambda b, h, i: (b, h, n_ctx // tq + i)),
                  pl.BlockSpec((1, lc, HEAD_DIM), lambda b, h, i: (b, 0, h)),
                  pl.BlockSpec((1, HEAD_DIM, lc), lambda b, h, i: (b, h, 0))],
        out_specs=pl.BlockSpec((1, tq, gw), lambda b, h, i: (b, i, h)),
        out_shape=jax.ShapeDtypeStruct((nb, n_lat, ATT_W), BF16),
        scratch_shapes=[pltpu.VMEM((ATT_GROUPS, HEAD_DIM, tq), F32),
                        pltpu.VMEM((2, ATT_GROUPS, kc, tq), F32)],
        compiler_params=pltpu.CompilerParams(
            dimension_semantics=("arbitrary", "arbitrary", "arbitrary"),
            vmem_limit_bytes=_vmem_limit(vmem)),
        name="attention",
    )(qt, kn, vt)


def _hgrn_chunk(q_ref, v_ref, r_ref, lb, reverse, rows):
    c = HGRN_CHUNK
    q = q_ref[0, rows, :].astype(F32)
    v = v_ref[0, rows, :]
    r = r_ref[0, rows, :].astype(F32)
    sig = jax.nn.sigmoid(r)
    f = lb + (1.0 - lb) * sig
    logf = jnp.log(f)
    k = (1.0 - lb) * (1.0 - sig)
    ti = lax.broadcasted_iota(I32, (c, c), 0)
    si = lax.broadcasted_iota(I32, (c, c), 1)
    tri = jnp.where((si >= ti) if reverse else (si <= ti), 1.0, 0.0).astype(BF16)
    hi = logf.astype(BF16)
    lo = (logf - hi.astype(F32)).astype(BF16)
    bcum = _dot(tri, hi) + _dot(tri, lo)
    b_end = bcum[0:1, :] if reverse else bcum[c - 1:c, :]
    qt = (q * jnp.exp(bcum)).astype(BF16)
    kt = (k * jnp.exp(b_end - bcum)).astype(BF16)
    dec = jnp.exp(b_end)

    nsub = c // HGRN_SUB
    qp, kp, cols = [], [], []
    for blk in range(nsub):
        rs = slice(blk * HGRN_SUB, (blk + 1) * HGRN_SUB)
        cs = slice(blk * HGRN_SUB, c) if reverse else slice(0, (blk + 1) * HGRN_SUB)
        mid = blk * HGRN_SUB + HGRN_SUB // 2
        ref = bcum[mid:mid + 1, :]
        qp.append((q[rs] * jnp.exp(bcum[rs] - ref)).astype(BF16))
        kp.append((k[cs] * jnp.exp(ref - bcum[cs])).astype(BF16))
        cols.append(cs)

    keep = []
    for blk in range(nsub):
        n_cols = cols[blk].stop - cols[blk].start
        trow = blk * HGRN_SUB + lax.broadcasted_iota(I32, (HGRN_SUB, n_cols), 0)
        scol = cols[blk].start + lax.broadcasted_iota(I32, (HGRN_SUB, n_cols), 1)
        keep.append((scol >= trow) if reverse else (scol <= trow))
    return dict(v=v, qt=qt, kt=kt, dec=dec, qp=qp, kp=kp, cols=cols, keep=keep, rows=rows)


def _hgrn_emit(dirs, s_refs, o_refs):
    nsub = HGRN_CHUNK // HGRN_SUB
    heads = [slice(h * HGRN_DK, (h + 1) * HGRN_DK) for h in range(HGRN_HEADS)]
    inter = [[None] * len(chunks) for chunks in dirs]
    for k in range(len(dirs[0])):
        for di, (chunks, s_ref) in enumerate(zip(dirs, s_refs)):
            d = chunks[k]
            row = []
            for h, hs in enumerate(heads):
                st = s_ref[h]
                row.append(_dot_nt(d["qt"][:, hs], st.astype(BF16)))
                s_ref[h] = st * d["dec"][:, hs] + _dot_tn(d["v"][:, hs], d["kt"][:, hs])
            inter[di][k] = row
    scores = [[[[_dot_nt(d["qp"][blk][:, hs], d["kp"][blk][:, hs]) for blk in range(nsub)] for hs in heads]
               for d in chunks] for chunks in dirs]
    for chunks, sc_d, it_d, o_ref in zip(dirs, scores, inter, o_refs):
        for d, sc, it in zip(chunks, sc_d, it_d):
            outs = []
            for h, hs in enumerate(heads):
                parts = [_dot(jnp.where(d["keep"][blk], sc[h][blk], 0.0).astype(BF16), d["v"][d["cols"][blk], hs])
                         for blk in range(nsub)]
                outs.append(it[h] + jnp.concatenate(parts, axis=0))
            o_ref[0, d["rows"], :] = jnp.concatenate(outs, axis=1).astype(BF16)


def _hgrn_kernel(qf_ref, vf_ref, rf_ref, qb_ref, vb_ref, rb_ref, lb_ref, of_ref, ob_ref, sf_ref, sb_ref):
    @pl.when(pl.program_id(1) == 0)
    def _():
        sf_ref[...] = jnp.zeros(sf_ref.shape, F32)
        sb_ref[...] = jnp.zeros(sb_ref.shape, F32)

    n_layers = lb_ref.shape[0] // 2

    def lower_bound(direction):
        rows = [lb_ref[direction * n_layers + l:direction * n_layers + l + 1, :] for l in range(n_layers)]
        amax = functools.reduce(jnp.maximum, rows)
        e = [jnp.exp(a - amax) for a in rows]
        return e[0] / functools.reduce(lambda u, w: u + w, e)

    rows = [slice(k * HGRN_CHUNK, (k + 1) * HGRN_CHUNK) for k in range(HGRN_STEP_CHUNKS)]
    fwd = [_hgrn_chunk(qf_ref, vf_ref, rf_ref, lower_bound(0), False, rs) for rs in rows]
    bwd = [_hgrn_chunk(qb_ref, vb_ref, rb_ref, lower_bound(1), True, rs) for rs in reversed(rows)]
    _hgrn_emit([fwd, bwd], [sf_ref, sb_ref], [of_ref, ob_ref])


def _hgrn_scan(p, hgrn_lb, n_ctx):
    nb, lc, _ = p.shape
    c = HGRN_CHUNK * HGRN_STEP_CHUNKS
    assert n_ctx % c == 0 and lc % c == 0
    n_chunks = lc // c
    ctx_chunks = n_ctx // c

    def fwd(col):
        return lambda b, s: (b, s, col)

    def bwd_chunk(s):
        return jnp.where(s < ctx_chunks, ctx_chunks - 1 - s, n_chunks - 1 + ctx_chunks - s)

    def bwd(col):
        return lambda b, s: (b, bwd_chunk(s), col)

    blk = (1, c, HGRN_W)
    lb2 = hgrn_lb.reshape(-1, HGRN_W)
    return pl.pallas_call(
        _hgrn_kernel,
        grid=(nb, n_chunks),
        in_specs=[pl.BlockSpec(blk, fwd(COL_HQ // HGRN_W)), pl.BlockSpec(blk, fwd(COL_HI // HGRN_W)),
                  pl.BlockSpec(blk, fwd(COL_FF // HGRN_W)),
                  pl.BlockSpec(blk, bwd(COL_HQ // HGRN_W)), pl.BlockSpec(blk, bwd(COL_HI // HGRN_W)),
                  pl.BlockSpec(blk, bwd(COL_FB // HGRN_W)),
                  pl.BlockSpec(lb2.shape, lambda b, s: (0, 0))],
        out_specs=[pl.BlockSpec(blk, fwd(0)), pl.BlockSpec(blk, bwd(0))],
        out_shape=[jax.ShapeDtypeStruct((nb, lc, HGRN_W), BF16)] * 2,
        scratch_shapes=[pltpu.VMEM((HGRN_HEADS, HGRN_DK, HGRN_DK), F32)] * 2,
        compiler_params=pltpu.CompilerParams(dimension_semantics=("arbitrary", "arbitrary")),
        name="hgrn_scan",
    )(p, p, p, p, p, p, lb2)


def _merge_kernel(of_ref, ob_ref, hg_ref, gh_ref, ga_ref, oa_ref, x_ref, mod_ref, ng_ref, hng_ref,
                  wb0_ref, wb1_ref, wo_ref, rw_ref, rb_ref,
                  x1_ref, t_ref, ids_ref, wts_ref):
    b = pl.program_id(0)
    m = mod_ref[pl.ds(b, 1), :]
    gate_mix = m[:, 2 * D_MODEL:3 * D_MODEL]
    shift_f = m[:, 3 * D_MODEL:4 * D_MODEL]
    scale_f = m[:, 4 * D_MODEL:5 * D_MODEL]
    hng = hng_ref[...]
    rw = rw_ref[...]
    rw_hi = rw.astype(BF16)
    rw_lo = (rw - rw_hi.astype(F32)).astype(BF16)
    groups = [slice(r, r + MERGE_ROWS) for r in range(0, of_ref.shape[1], MERGE_ROWS)]

    branch = []
    for rs in groups:
        o = of_ref[0, rs, :].astype(F32) + ob_ref[0, rs, :].astype(F32)
        o_h = jnp.concatenate(
            [_rms(o[:, h * HGRN_DK:(h + 1) * HGRN_DK], hng) for h in range(HGRN_HEADS)], axis=1)
        g_raw = hg_ref[0, rs, :].astype(F32)
        o_h = o_h * (g_raw * jax.nn.sigmoid(g_raw))
        branch.append((_dot(o_h.astype(BF16), wb0_ref[...]), _dot(oa_ref[0, rs, :], wb1_ref[...])))
    mixed = []
    for rs, (y_h, y_a) in zip(groups, branch):
        y = (jax.nn.sigmoid(gh_ref[0, rs, :].astype(F32)) * y_h
             + jax.nn.sigmoid(ga_ref[0, rs, :].astype(F32)) * y_a)
        mixed.append(_dot(y.astype(BF16), wo_ref[...]))
    logits_all = []
    for rs, y in zip(groups, mixed):
        x1 = x_ref[0, rs, :] + gate_mix * _rms(y, ng_ref[1:2, :])
        x1_ref[0, rs, :] = x1
        t = _rms(x1, ng_ref[2:3, :]) * (1.0 + scale_f) + shift_f
        t_ref[rs, :] = t
        t_hi = t.astype(BF16)
        t_lo = (t - t_hi.astype(F32)).astype(BF16)
        logits_all.append(_dot_nt(rw_hi, t_hi) + _dot_nt(rw_hi, t_lo) + _dot_nt(rw_lo, t_hi) + rb_ref[...])
    for rs, logits in zip(groups, logits_all):
        eidx = lax.broadcasted_iota(I32, logits.shape, 0).astype(F32)
        vals = []
        for j in range(TOP_K):
            mx = jnp.max(logits, axis=0, keepdims=True)
            idx = jnp.min(jnp.where(logits == mx, eidx, float(N_EXPERTS)), axis=0, keepdims=True)
            ids_ref[j:j + 1, rs] = idx.astype(I32)
            vals.append(mx)
            logits = jnp.where(eidx == idx, -jnp.inf, logits)
        ex = [jnp.exp(vj - vals[0]) for vj in vals]
        den = ex[0] + ex[1] + ex[2] + ex[3]
        for j in range(TOP_K):
            wts_ref[j:j + 1, rs] = ex[j] / den


def _merge(o_f, o_b, p, o_a, x, mod, norm_g, hgrn_norm_g, wb0, wb1, wo, router_wt, router_b, n_ctx):
    nb, n_lat, d = x.shape
    tm = _largest_tile(n_lat, 256, 128)
    assert n_ctx % tm == 0
    off = n_ctx // tm
    nt = n_lat // tm
    row = (1, tm, d)

    def pcol(col):
        return pl.BlockSpec(row, lambda b, i: (b, off + i, col // d))

    def full(a):
        return pl.BlockSpec(a.shape, lambda b, i: (0,) * a.ndim)

    tok = nb * n_lat
    return pl.pallas_call(
        _merge_kernel,
        grid=(nb, nt),
        in_specs=[pl.BlockSpec(row, lambda b, i: (b, off + i, 0)),
                  pl.BlockSpec(row, lambda b, i: (b, off + i, 0)),
                  pcol(COL_HG), pcol(COL_GH), pcol(COL_GA),
                  pl.BlockSpec(row, lambda b, i: (b, i, 0)),
                  pl.BlockSpec(row, lambda b, i: (b, i, 0)),
                  full(mod), full(norm_g), full(hgrn_norm_g), full(wb0), full(wb1), full(wo),
                  full(router_wt), full(router_b)],
        out_specs=[pl.BlockSpec(row, lambda b, i: (b, i, 0)),
                   pl.BlockSpec((tm, d), lambda b, i: (b * nt + i, 0)),
                   pl.BlockSpec((TOP_K, tm), lambda b, i: (0, b * nt + i)),
                   pl.BlockSpec((TOP_K, tm), lambda b, i: (0, b * nt + i))],
        out_shape=[jax.ShapeDtypeStruct((nb, n_lat, d), F32),
                   jax.ShapeDtypeStruct((tok, d), F32),
                   jax.ShapeDtypeStruct((TOP_K, tok), I32),
                   jax.ShapeDtypeStruct((TOP_K, tok), F32)],
        compiler_params=pltpu.CompilerParams(
            dimension_semantics=("arbitrary", "arbitrary"),
            vmem_limit_bytes=_vmem_limit(40 * 1024 * 1024)),
        name="merge_router",
    )(o_f, o_b, p, p, p, o_a, x, mod, norm_g, hgrn_norm_g, wb0, wb1, wo, router_wt, router_b)


def _rank_kernel(ids_ref, rank_ref, cnt_ref, *, tt):
    eidx = lax.broadcasted_iota(I32, (N_EXPERTS, tt), 0)
    si = lax.broadcasted_iota(I32, (tt, tt), 0)
    ti = lax.broadcasted_iota(I32, (tt, tt), 1)
    before = jnp.where(si < ti, 1.0, 0.0).astype(BF16)
    seen = jnp.zeros((N_EXPERTS, 1), F32)
    for j in range(TOP_K):
        onehot = eidx == ids_ref[j:j + 1, :]
        oh = jnp.where(onehot, 1.0, 0.0)
        earlier = _dot(oh.astype(BF16), before)
        rank = jnp.sum(jnp.where(onehot, seen + earlier, 0.0), axis=0, keepdims=True)
        rank_ref[j:j + 1, :] = rank.astype(I32)
        seen = seen + jnp.sum(oh, axis=1, keepdims=True)
    cnt_ref[0] = seen.astype(I32)


def _expert_rank(ids, tt):
    _, tok = ids.shape
    return pl.pallas_call(
        functools.partial(_rank_kernel, tt=tt),
        grid=(tok // tt,),
        in_specs=[pl.BlockSpec((TOP_K, tt), lambda i: (0, i))],
        out_specs=[pl.BlockSpec((TOP_K, tt), lambda i: (0, i)),
                   pl.BlockSpec((1, N_EXPERTS, 1), lambda i: (i, 0, 0))],
        out_shape=[jax.ShapeDtypeStruct((TOP_K, tok), I32),
                   jax.ShapeDtypeStruct((tok // tt, N_EXPERTS, 1), I32)],
        compiler_params=pltpu.CompilerParams(dimension_semantics=("arbitrary",)),
        name="expert_rank",
    )(ids)


def _start_blocks(n_blocks, make_copy):
    def start(b, carry):
        make_copy(b).start()
        return carry

    lax.fori_loop(0, n_blocks, start, 0)


def _wait_blocks(n_blocks, make_copy, make_group_copy):
    def wait_group(g, carry):
        make_group_copy().wait()
        return carry

    def wait_one(b, carry):
        make_copy(0).wait()
        return carry

    lax.fori_loop(0, lax.shift_right_logical(n_blocks, WAIT_GROUP_LOG2), wait_group, 0)
    lax.fori_loop(0, jnp.bitwise_and(n_blocks, (1 << WAIT_GROUP_LOG2) - 1), wait_one, 0)


def _scatter_kernel(zt_ref, zvalid_ref, nblk_ref, bdst_ref, lp_ref, t_ref, xs_ref, stage, zero_buf, sem, zsem,
                    *, tt, n_stage):
    @pl.when(pl.program_id(0) == 0)
    def _():
        zero_buf[...] = jnp.zeros(zero_buf.shape, F32)

        def zero_copy(k):
            return pltpu.make_async_copy(zero_buf, xs_ref.at[pl.ds(zt_ref[k] * MOE_TILE, MOE_TILE)], zsem)

        def start(k, carry):
            @pl.when(zvalid_ref[k] == 1)
            def _():
                zero_copy(k).start()
            return carry

        def wait(k, carry):
            @pl.when(zvalid_ref[k] == 1)
            def _():
                zero_copy(k).wait()
            return carry

        lax.fori_loop(0, zt_ref.shape[0], start, 0)
        lax.fori_loop(0, zt_ref.shape[0], wait, 0)

    i = pl.program_id(0)
    slot = i % 2
    t_bf = t_ref[...].astype(BF16)
    lps = [lp_ref[j:j + 1, :] for j in range(TOP_K)]
    for rc in range(n_stage // PERM_ROWS):
        row = rc * PERM_ROWS + lax.broadcasted_iota(I32, (PERM_ROWS, tt), 0)
        sel = jnp.where(row == lps[TOP_K - 1], 1.0, 0.0)
        for j in range(TOP_K - 2, -1, -1):
            sel = jnp.where(row == lps[j], 1.0, sel)
        stage[slot, rc * PERM_ROWS:(rc + 1) * PERM_ROWS, :] = _dot(sel.astype(BF16), t_bf)

    def copy(b):
        src = pl.multiple_of(b * RUN_ALIGN, RUN_ALIGN)
        dst = pl.multiple_of(bdst_ref[0, 0, b], RUN_ALIGN)
        return pltpu.make_async_copy(stage.at[slot, pl.ds(src, RUN_ALIGN)], xs_ref.at[pl.ds(dst, RUN_ALIGN)],
                                     sem.at[slot])

    def waiter(s):
        rows = RUN_ALIGN << WAIT_GROUP_LOG2
        return (lambda b: pltpu.make_async_copy(stage.at[s, pl.ds(0, RUN_ALIGN)],
                                                xs_ref.at[pl.ds(0, RUN_ALIGN)], sem.at[s]),
                lambda: pltpu.make_async_copy(stage.at[s, pl.ds(0, rows)], xs_ref.at[pl.ds(0, rows)], sem.at[s]))

    _start_blocks(nblk_ref[i], copy)

    @pl.when(i > 0)
    def _():
        _wait_blocks(nblk_ref[i - 1], *waiter(1 - slot))

    @pl.when(i == pl.num_programs(0) - 1)
    def _():
        _wait_blocks(nblk_ref[i], *waiter(slot))


def _scatter_rows(zero_tiles, zero_valid, n_blocks, block_rows, lp, t, n_rows, tt, n_stage):
    tok, d = t.shape
    nb = block_rows.shape[2]
    return pl.pallas_call(
        functools.partial(_scatter_kernel, tt=tt, n_stage=n_stage),
        grid_spec=pltpu.PrefetchScalarGridSpec(
            num_scalar_prefetch=3,
            grid=(tok // tt,),
            in_specs=[pl.BlockSpec((1, 1, nb), lambda i, *_: (i, 0, 0), memory_space=pltpu.SMEM),
                      pl.BlockSpec((TOP_K, tt), lambda i, *_: (0, i)),
                      pl.BlockSpec((tt, d), lambda i, *_: (i, 0))],
            out_specs=pl.BlockSpec(memory_space=pl.ANY),
            scratch_shapes=[pltpu.VMEM((2, n_stage, d), F32), pltpu.VMEM((MOE_TILE, d), F32),
                            pltpu.SemaphoreType.DMA((2,)), pltpu.SemaphoreType.DMA(())]),
        out_shape=jax.ShapeDtypeStruct((n_rows, d), F32),
        compiler_params=pltpu.CompilerParams(
            dimension_semantics=("arbitrary",), vmem_limit_bytes=_vmem_limit(48 * 1024 * 1024)),
        name="moe_scatter",
    )(zero_tiles, zero_valid, n_blocks, block_rows, lp, t)


def _expert_kernel(te_ref, first_ref, nused_ref, slot_ref, next_ref, xs_ref, wu_hbm, bu_ref, wd_hbm, bd_ref,
                   ys_ref, wu32, wd32, wu_bf, wd_bf, sem_u, sem_d):
    i = pl.program_id(0)

    def weight_copies(e, s):
        return (pltpu.make_async_copy(wu_hbm.at[e], wu32.at[s], sem_u.at[s]),
                pltpu.make_async_copy(wd_hbm.at[e], wd32.at[s], sem_d.at[s]))

    @pl.when(i >= nused_ref[0])
    def _():
        ys_ref[...] = jnp.zeros(ys_ref.shape, F32)

    @pl.when(i < nused_ref[0])
    def _():
        @pl.when(first_ref[i] == 1)
        def _():
            s = slot_ref[i]

            @pl.when(i == 0)
            def _():
                for cp in weight_copies(te_ref[0], 0):
                    cp.start()

            for cp in weight_copies(te_ref[i], s):
                cp.wait()
            wu_bf[...] = wu32[s].astype(BF16)
            wd_bf[...] = wd32[s].astype(BF16)

            @pl.when(next_ref[i] >= 0)
            def _():
                for cp in weight_copies(next_ref[i], 1 - s):
                    cp.start()

        h = _dot(xs_ref[...].astype(BF16), wu_bf[...]) + bu_ref[0]
        glu = jnp.minimum(h[:, :D_EXPERT], SWIGLU_LIMIT)
        lin = jnp.clip(h[:, D_EXPERT:], -SWIGLU_LIMIT, SWIGLU_LIMIT)
        a = glu * jax.nn.sigmoid(SWIGLU_ALPHA * glu) * (lin + 1.0)
        ys_ref[...] = _dot(a.astype(BF16), wd_bf[...]) + bd_ref[0]


def _experts(xs, tile_expert, tile_first, n_used, tile_slot, next_expert, w_up, b_up, w_down, b_down):
    n_rows, d = xs.shape
    tm = MOE_TILE
    n_tiles = n_rows // tm

    def tile(i, te, first, nused, *_):
        return (jnp.minimum(i, nused[0] - 1), 0)

    def bsel(i, te, *_):
        return (te[i], 0, 0)

    vmem = 2 * (d * 2 * D_EXPERT * 4 + D_EXPERT * d * 4) + d * 2 * D_EXPERT * 2 + D_EXPERT * d * 2 \
        + 4 * tm * d * 4 + 6 * tm * 2 * D_EXPERT * 4
    return pl.pallas_call(
        _expert_kernel,
        grid_spec=pltpu.PrefetchScalarGridSpec(
            num_scalar_prefetch=5,
            grid=(n_tiles,),
            in_specs=[pl.BlockSpec((tm, d), tile),
                      pl.BlockSpec(memory_space=pl.ANY),
                      pl.BlockSpec((1, 1, 2 * D_EXPERT), bsel),
                      pl.BlockSpec(memory_space=pl.ANY),
                      pl.BlockSpec((1, 1, d), bsel)],
            out_specs=pl.BlockSpec((tm, d), lambda i, *_: (i, 0)),
            scratch_shapes=[pltpu.VMEM((2, d, 2 * D_EXPERT), F32), pltpu.VMEM((2, D_EXPERT, d), F32),
                            pltpu.VMEM((d, 2 * D_EXPERT), BF16), pltpu.VMEM((D_EXPERT, d), BF16),
                            pltpu.SemaphoreType.DMA((2,)), pltpu.SemaphoreType.DMA((2,))]),
        out_shape=jax.ShapeDtypeStruct((n_rows, d), F32),
        compiler_params=pltpu.CompilerParams(
            dimension_semantics=("arbitrary",), vmem_limit_bytes=_vmem_limit(vmem)),
        name="moe_experts",
    )(tile_expert, tile_first, n_used, tile_slot, next_expert, xs, w_up, b_up.reshape(N_EXPERTS, 1, -1), w_down,
      b_down.reshape(N_EXPERTS, 1, -1))


def _combine_kernel(nblk_ref, bsrc_ref, ys_ref, lp_ref, w_ref, x1_ref, mod_ref, ng_ref, o_ref, stage, sem,
                    *, tt, n_stage, tiles_per_batch):
    i = pl.program_id(0)
    slot = i % 2

    def fetch(tile, s):
        def copy(b):
            src = pl.multiple_of(bsrc_ref[tile, b], RUN_ALIGN)
            dst = pl.multiple_of(b * RUN_ALIGN, RUN_ALIGN)
            return pltpu.make_async_copy(ys_ref.at[pl.ds(src, RUN_ALIGN)], stage.at[s, pl.ds(dst, RUN_ALIGN)],
                                         sem.at[s])
        return copy

    def group_copy():
        rows = RUN_ALIGN << WAIT_GROUP_LOG2
        return pltpu.make_async_copy(ys_ref.at[pl.ds(0, rows)], stage.at[slot, pl.ds(0, rows)], sem.at[slot])

    @pl.when(i == 0)
    def _():
        stage[...] = jnp.zeros(stage.shape, F32)
        _start_blocks(nblk_ref[0], fetch(0, 0))

    @pl.when(i + 1 < pl.num_programs(0))
    def _():
        _start_blocks(nblk_ref[i + 1], fetch(i + 1, 1 - slot))

    _wait_blocks(nblk_ref[i], fetch(i, slot), group_copy)

    lp = [jnp.broadcast_to(lp_ref[:, j:j + 1], (tt, PERM_ROWS)) for j in range(TOP_K)]
    w = [jnp.broadcast_to(w_ref[:, j:j + 1], (tt, PERM_ROWS)) for j in range(TOP_K)]
    f = jnp.zeros((tt, D_MODEL), F32)
    for lc in range(n_stage // PERM_ROWS):
        col = lc * PERM_ROWS + lax.broadcasted_iota(I32, (tt, PERM_ROWS), 1)
        wp = jnp.where(col == lp[TOP_K - 1], w[TOP_K - 1], 0.0)
        for j in range(TOP_K - 2, -1, -1):
            wp = jnp.where(col == lp[j], w[j], wp)
        f = f + _dot(wp.astype(BF16), stage[slot, lc * PERM_ROWS:(lc + 1) * PERM_ROWS, :].astype(BF16))
    b = i // tiles_per_batch
    gate_f = mod_ref[pl.ds(b, 1), 5 * D_MODEL:6 * D_MODEL]
    o_ref[...] = x1_ref[...] + gate_f * _rms(f, ng_ref[3:4, :])


def _combine(n_blocks, block_rows, ys, lp_t, wts_t, x1, mod, norm_g, tt, n_stage, tiles_per_batch):
    tok, d = x1.shape
    kern = functools.partial(_combine_kernel, tt=tt, n_stage=n_stage, tiles_per_batch=tiles_per_batch)
    return pl.pallas_call(
        kern,
        grid_spec=pltpu.PrefetchScalarGridSpec(
            num_scalar_prefetch=2,
            grid=(tok // tt,),
            in_specs=[pl.BlockSpec(memory_space=pl.ANY),
                      pl.BlockSpec((tt, TOP_K), lambda i, *_: (i, 0)),
                      pl.BlockSpec((tt, TOP_K), lambda i, *_: (i, 0)),
                      pl.BlockSpec((tt, d), lambda i, *_: (i, 0)),
                      pl.BlockSpec(mod.shape, lambda i, *_: (0, 0)),
                      pl.BlockSpec(norm_g.shape, lambda i, *_: (0, 0))],
            out_specs=pl.BlockSpec((tt, d), lambda i, *_: (i, 0)),
            scratch_shapes=[pltpu.VMEM((2, n_stage, d), F32), pltpu.SemaphoreType.DMA((2,))]),
        out_shape=jax.ShapeDtypeStruct((tok, d), F32),
        compiler_params=pltpu.CompilerParams(
            dimension_semantics=("arbitrary",), vmem_limit_bytes=_vmem_limit(48 * 1024 * 1024)),
        name="moe_combine",
    )(n_blocks, block_rows[:, 0, :], ys, lp_t, wts_t, x1, mod, norm_g)


def _rope_tables(n_ctx, n_lat):
    inv = ROPE_THETA ** (-np.arange(0, ROPE_AXIS_DIM, 2, dtype=np.float32) / ROPE_AXIS_DIM)
    idx = jnp.arange(n_lat, dtype=I32)
    rows = (idx // GRID_W).astype(F32)
    cols = (idx % GRID_W).astype(F32)
    half = ROPE_AXIS_DIM // 2
    inv = jnp.asarray(inv, F32)
    ang_r = rows[:, None] * inv[None, :]
    ang_c = cols[:, None] * inv[None, :]
    ang = jnp.concatenate([ang_r, ang_r, ang_c, ang_c], axis=1)
    cos, sin = jnp.cos(ang), jnp.sin(ang)
    lane = np.arange(HEAD_DIM)
    first = jnp.asarray((lane % ROPE_AXIS_DIM) < half)
    sin_a = jnp.where(first, -sin, 0.0)
    sin_b = jnp.where(first, 0.0, sin)
    pad1 = jnp.ones((n_ctx, HEAD_DIM), F32)
    pad0 = jnp.zeros((n_ctx, HEAD_DIM), F32)
    return (jnp.concatenate([pad1, cos], axis=0), jnp.concatenate([pad0, sin_a], axis=0),
            jnp.concatenate([pad0, sin_b], axis=0))


def kernel(x, c, ctx, c_ctx, w_mod, b_mod, norm_g, w_in, hgrn_lb, hgrn_norm_g, qk_norm_g, w_branch, w_out,
           router_w, router_b, w_up, b_up, w_down, b_down):
    nb, n_lat, d = x.shape
    n_ctx = ctx.shape[1]
    assert d == D_MODEL and w_mod.shape[0] == 1, "single-layer kernel"
    tok = nb * n_lat

    cv = jnp.zeros((8, d), F32).at[:nb].set(c).at[nb].set(c_ctx)
    mod = _modulation(cv, w_mod[0], b_mod[0])

    xc = jnp.concatenate([ctx, x], axis=1)
    sizes = np.cumsum([0, HGRN_W, HGRN_W, HGRN_W, HGRN_W, HGRN_W, ATT_W, KV_W, KV_W, D_MODEL, D_MODEL])
    order = [0, 1, 2, 3, 4, 5, 8, 9, 6, 7]
    w_in_bf = jnp.concatenate([w_in[0][:, sizes[i]:sizes[i + 1]] for i in order], axis=1).astype(BF16)
    p = _in_projection(xc, mod, norm_g[0], w_in_bf, n_ctx)

    cos, sin_a, sin_b = _rope_tables(n_ctx, n_lat)
    qt, kn, vt = _qk_prep(p, cos, sin_a, sin_b, qk_norm_g[0])
    o_a = _attention(qt, kn, vt, n_ctx)
    o_f, o_b = _hgrn_scan(p, hgrn_lb, n_ctx)

    x1, t, ids, wts = _merge(
        o_f, o_b, p, o_a, x, mod, norm_g[0], hgrn_norm_g, w_branch[0, 0].astype(BF16),
        w_branch[0, 1].astype(BF16), w_out[0].astype(BF16), router_w[0].T, router_b[0].reshape(N_EXPERTS, 1),
        n_ctx)

    tt = _largest_tile(n_lat, DISPATCH_TILE, 128)
    n_tt = tok // tt
    rank, counts = _expert_rank(ids, tt)
    counts = counts[:, :, 0]
    run = (counts + RUN_ALIGN - 1) // RUN_ALIGN * RUN_ALIGN
    run_end = jnp.cumsum(run, axis=1)
    stage_off = run_end - run
    expert_rows = jnp.sum(run, axis=0)
    tiles_e = (expert_rows + MOE_TILE - 1) // MOE_TILE
    tile_end = jnp.cumsum(tiles_e)
    starts = (tile_end - tiles_e) * MOE_TILE
    run_dst = starts[None, :] + jnp.cumsum(run, axis=0) - run
    n_tiles = -(-(tok * TOP_K + N_EXPERTS * n_tt * (RUN_ALIGN - 1)) // MOE_TILE) + N_EXPERTS
    tile_idx = jnp.arange(n_tiles, dtype=I32)
    tile_expert = jnp.sum((tile_end[None, :] <= tile_idx[:, None]).astype(I32), axis=1)
    tile_expert = jnp.minimum(tile_expert, N_EXPERTS - 1)
    n_used = tile_end[-1:].astype(I32)
    tile_expert = jnp.where(tile_idx < n_used[0], tile_expert, tile_expert[jnp.maximum(n_used[0] - 1, 0)])
    tile_first = jnp.concatenate([jnp.ones((1,), I32), (tile_expert[1:] != tile_expert[:-1]).astype(I32)])
    e_idx = jnp.arange(N_EXPERTS, dtype=I32)
    used = tiles_e > 0
    e_slot = (jnp.cumsum(used.astype(I32)) - 1) % 2
    later = used[None, :] & (e_idx[None, :] > e_idx[:, None])
    e_next = jnp.min(jnp.where(later, e_idx[None, :], N_EXPERTS), axis=1)
    e_next = jnp.where(e_next == N_EXPERTS, -1, e_next)
    tile_onehot = tile_expert[:, None] == e_idx[None, :]
    tile_slot = jnp.sum(jnp.where(tile_onehot, e_slot[None, :], 0), axis=1).astype(I32)
    next_expert = jnp.sum(jnp.where(tile_onehot, e_next[None, :], 0), axis=1).astype(I32)

    n_stage = -(-(TOP_K * tt + N_EXPERTS * (RUN_ALIGN - 1)) // PERM_ROWS) * PERM_ROWS
    ids_t = ids.reshape(TOP_K, n_tt, tt)
    onehot = ids_t[..., None] == jnp.arange(N_EXPERTS, dtype=I32)
    lp = jnp.sum(jnp.where(onehot, stage_off[None, :, None, :], 0), axis=-1).reshape(TOP_K, tok) + rank
    blk_row = jnp.arange(n_stage // RUN_ALIGN, dtype=I32) * RUN_ALIGN
    blk_expert = jnp.minimum(jnp.sum((run_end[:, None, :] <= blk_row[None, :, None]).astype(I32), axis=-1),
                             N_EXPERTS - 1)
    blk_onehot = blk_expert[..., None] == jnp.arange(N_EXPERTS, dtype=I32)
    blk_dst = jnp.sum(jnp.where(blk_onehot, (run_dst - stage_off)[:, None, :], 0), axis=-1) + blk_row[None, :]
    n_blocks = (run_end[:, -1] // RUN_ALIGN).astype(I32)
    blk_dst = jnp.where(blk_row[None, :] < run_end[:, -1:], blk_dst, 0).astype(I32)[:, None, :]

    n_tail = n_tiles - (tok * TOP_K) // MOE_TILE
    tail = n_used[0] + jnp.arange(n_tail, dtype=I32)
    zero_tiles = jnp.clip(jnp.concatenate([tile_end.astype(I32) - 1, tail]), 0, n_tiles - 1)
    zero_valid = jnp.concatenate([expert_rows % MOE_TILE != 0, tail < n_tiles]).astype(I32)

    xs = _scatter_rows(zero_tiles, zero_valid, n_blocks, blk_dst, lp, t, n_tiles * MOE_TILE, tt, n_stage)
    ys = _experts(xs, tile_expert, tile_first, n_used, tile_slot, next_expert, w_up[0], b_up[0], w_down[0],
                  b_down[0])
    out = _combine(n_blocks, blk_dst, ys, lp.T, wts.T, x1.reshape(tok, d), mod, norm_g[0], tt, n_stage,
                   n_lat // tt)
    return out.reshape(nb, n_lat, d)
```

```python
import functools

import jax
import jax.numpy as jnp
import numpy as np
from jax import lax
from jax.experimental import pallas as pl
from jax.experimental.pallas import tpu as pltpu

F32 = jnp.float32
BF16 = jnp.bfloat16
I32 = jnp.int32

D_MODEL = 1024
GRID_W = 64
N_MOD = 6
EPS = 1e-6
HGRN_HEADS = 8
HGRN_DK = 128
HGRN_W = HGRN_HEADS * HGRN_DK
ATT_HEADS = 8
ATT_KV_HEADS = 2
ATT_GROUPS = ATT_HEADS // ATT_KV_HEADS
HEAD_DIM = 128
ATT_W = ATT_HEADS * HEAD_DIM
KV_W = ATT_KV_HEADS * HEAD_DIM
ROPE_AXIS_DIM = HEAD_DIM // 2
ROPE_THETA = 10000.0
N_EXPERTS = 32
TOP_K = 4
D_EXPERT = 1024
SWIGLU_LIMIT = 7.0
SWIGLU_ALPHA = 1.702

COL_HQ, COL_HI, COL_FF, COL_FB, COL_HG, COL_AQ, COL_GH, COL_GA = (i * 1024 for i in range(8))
COL_K = 8 * 1024
COL_V = COL_K + KV_W
IN_COLS = COL_V + KV_W

V7X_VMEM_BYTES = 64 * 1024 * 1024
LANES = 128

HGRN_CHUNK = 64
HGRN_STEP_CHUNKS = 4
HGRN_SUB = 16
MOE_TILE = 256
MERGE_ROWS = 128
INPROJ_CHAINS = 2
DISPATCH_TILE = 512
RUN_ALIGN = 8
PERM_ROWS = 256
WAIT_GROUP_LOG2 = 5


def _vmem_limit(nbytes):
    return int(min(V7X_VMEM_BYTES - 6 * 1024 * 1024, max(nbytes, 32 * 1024 * 1024)))


def _largest_tile(n, cap, mult):
    best = None
    for t in range(mult, min(n, cap) + 1, mult):
        if n % t == 0:
            best = t
    assert best is not None, (n, cap, mult)
    return best


def _rms(x, g):
    return x * lax.rsqrt(jnp.mean(x * x, axis=-1, keepdims=True) + EPS) * g


def _dot(a, b):
    return jnp.dot(a, b, preferred_element_type=F32)


def _dot_nt(a, b):
    return lax.dot_general(a, b, (((1,), (1,)), ((), ())), preferred_element_type=F32)


def _dot_tn(a, b):
    return lax.dot_general(a, b, (((0,), (0,)), ((), ())), preferred_element_type=F32)


def _mod_kernel(cv_ref, w_ref, b_ref, o_ref):
    cv = cv_ref[...]
    s = cv * jax.nn.sigmoid(cv)
    o_ref[...] = jnp.dot(s, w_ref[...], preferred_element_type=F32,
                         precision=lax.Precision.HIGHEST) + b_ref[...]


def _modulation(cv, w_mod, b_mod):
    rows, d = cv.shape
    n = w_mod.shape[1]
    tn = 1024
    return pl.pallas_call(
        _mod_kernel,
        grid=(n // tn,),
        in_specs=[pl.BlockSpec((rows, d), lambda j: (0, 0)),
                  pl.BlockSpec((d, tn), lambda j: (0, j)),
                  pl.BlockSpec((1, tn), lambda j: (0, j))],
        out_specs=pl.BlockSpec((rows, tn), lambda j: (0, j)),
        out_shape=jax.ShapeDtypeStruct((rows, n), F32),
        name="modulation",
    )(cv, w_mod, b_mod.reshape(1, n))


def _inproj_kernel(x_ref, mod_ref, g_ref, w_ref, o_ref, *, tm, n_ctx, n_batch):
    b = pl.program_id(1)
    i = pl.program_id(2)
    m_lat = mod_ref[pl.ds(b, 1), :]
    m_ctx = mod_ref[n_batch:n_batch + 1, :]
    rows = tm // INPROJ_CHAINS
    us = []
    for c in range(INPROJ_CHAINS):
        xn = _rms(x_ref[0, c * rows:(c + 1) * rows, :], g_ref[0:1, :])
        row = i * tm + c * rows + lax.broadcasted_iota(I32, (rows, 1), 0)
        is_ctx = row < n_ctx
        shift = jnp.where(is_ctx, m_ctx[:, 0:D_MODEL], m_lat[:, 0:D_MODEL])
        scale = jnp.where(is_ctx, m_ctx[:, D_MODEL:2 * D_MODEL], m_lat[:, D_MODEL:2 * D_MODEL])
        us.append((xn * (1.0 + scale) + shift).astype(BF16))
    for c in range(INPROJ_CHAINS):
        o_ref[0, c * rows:(c + 1) * rows, :] = _dot(us[c], w_ref[...]).astype(BF16)


def _in_projection(xc, mod, norm_g, w_in_bf, n_ctx):
    nb, lc, d = xc.shape
    n = w_in_bf.shape[1]
    tm = _largest_tile(lc, 1056, 16)
    tn = 2176
    assert n % tn == 0
    kern = functools.partial(_inproj_kernel, tm=tm, n_ctx=n_ctx, n_batch=nb)
    vmem = 2 * (tm * d * 4 + d * tn * 2 + tm * tn * 2) + 6 * tm * d * 4
    return pl.pallas_call(
        kern,
        grid=(n // tn, nb, lc // tm),
        in_specs=[pl.BlockSpec((1, tm, d), lambda j, b, i: (b, i, 0)),
                  pl.BlockSpec(mod.shape, lambda j, b, i: (0, 0)),
                  pl.BlockSpec(norm_g.shape, lambda j, b, i: (0, 0)),
                  pl.BlockSpec((d, tn), lambda j, b, i: (0, j))],
        out_specs=pl.BlockSpec((1, tm, tn), lambda j, b, i: (b, i, j)),
        out_shape=jax.ShapeDtypeStruct((nb, lc, n), BF16),
        compiler_params=pltpu.CompilerParams(
            dimension_semantics=("arbitrary", "arbitrary", "arbitrary"),
            vmem_limit_bytes=_vmem_limit(vmem)),
        name="in_projection",
    )(xc, mod, norm_g, w_in_bf)


def _rotate_half_matrix():
    half = ROPE_AXIS_DIM // 2
    i = lax.broadcasted_iota(I32, (HEAD_DIM, HEAD_DIM), 0)
    j = lax.broadcasted_iota(I32, (HEAD_DIM, HEAD_DIM), 1)
    first = jnp.bitwise_and(j, ROPE_AXIS_DIM - 1) < half
    r = jnp.where(i == j + half, jnp.where(first, -1.0, 0.0), jnp.where(i == j - half, jnp.where(first, 0.0, 1.0), 0.0))
    return r.astype(BF16)


def _rope(xn, cos, sin, rot):
    hi = xn.astype(BF16)
    lo = (xn - hi.astype(F32)).astype(BF16)
    return xn * cos + (_dot(hi, rot) + _dot(lo, rot)) * sin


def _qkprep_kernel(q_ref, k_ref, v_ref, cos_ref, sa_ref, sb_ref, g_ref, qt_ref, ko_ref, vt_ref):
    cos, sn = cos_ref[...], sb_ref[...] - sa_ref[...]
    rot = _rotate_half_matrix()
    gq, gk = g_ref[0:1, :], g_ref[1:2, :]
    qscale = HEAD_DIM ** -0.5 * np.log2(np.e)
    for h in range(ATT_HEADS):
        hs = slice(h * HEAD_DIM, (h + 1) * HEAD_DIM)
        xn = _rms(q_ref[0, :, hs].astype(F32), gq)
        qt_ref[0, hs, :] = (_rope(xn, cos, sn, rot) * qscale).T.astype(BF16)
    for h in range(ATT_KV_HEADS):
        hs = slice(h * HEAD_DIM, (h + 1) * HEAD_DIM)
        xn = _rms(k_ref[0, :, hs].astype(F32), gk)
        ko_ref[0, :, hs] = _rope(xn, cos, sn, rot).astype(BF16)
        vt_ref[0, hs, :] = v_ref[0, :, hs].astype(F32).T.astype(BF16)


def _qk_prep(p, cos, sin_a, sin_b, qk_norm_g):
    nb, lc, _ = p.shape
    tm = _largest_tile(lc, 512, 128)
    tab = pl.BlockSpec((tm, HEAD_DIM), lambda b, i: (i, 0))
    return pl.pallas_call(
        _qkprep_kernel,
        grid=(nb, lc // tm),
        in_specs=[pl.BlockSpec((1, tm, ATT_W), lambda b, i: (b, i, COL_AQ // ATT_W)),
                  pl.BlockSpec((1, tm, KV_W), lambda b, i: (b, i, COL_K // KV_W)),
                  pl.BlockSpec((1, tm, KV_W), lambda b, i: (b, i, COL_V // KV_W)),
                  tab, tab, tab,
                  pl.BlockSpec(qk_norm_g.shape, lambda b, i: (0, 0))],
        out_specs=[pl.BlockSpec((1, ATT_W, tm), lambda b, i: (b, 0, i)),
                   pl.BlockSpec((1, tm, KV_W), lambda b, i: (b, i, 0)),
                   pl.BlockSpec((1, KV_W, tm), lambda b, i: (b, 0, i))],
        out_shape=[jax.ShapeDtypeStruct((nb, ATT_W, lc), BF16),
                   jax.ShapeDtypeStruct((nb, lc, KV_W), BF16),
                   jax.ShapeDtypeStruct((nb, KV_W, lc), BF16)],
        name="qk_prep",
    )(p, p, p, cos, sin_a, sin_b, qk_norm_g)


def _attn_kernel(qt_ref, k_ref, vt_ref, o_ref, acc_ref, s_ref, *, tq, kc, n_kc):
    acc_ref[...] = jnp.zeros(acc_ref.shape, F32)
    qts = [qt_ref[0, g * HEAD_DIM:(g + 1) * HEAD_DIM, :] for g in range(ATT_GROUPS)]

    def put_scores(c, slot):
        kch = k_ref[0, pl.ds(pl.multiple_of(c * kc, kc), kc), :]
        for g in range(ATT_GROUPS):
            s_ref[slot, g] = _dot(kch, qts[g])

    def softmax_pv(c, slot, carry):
        vch = vt_ref[0, :, pl.ds(pl.multiple_of(c * kc, kc), kc)]
        out = []
        for g in range(ATT_GROUPS):
            m_old, l_old = carry[g]
            s = s_ref[slot, g]
            m_new = jnp.maximum(m_old, jnp.max(s, axis=0, keepdims=True))
            alpha = jnp.exp2(m_old - m_new)
            p = jnp.exp2(s - m_new)
            l_new = l_old * alpha + jnp.sum(p, axis=0, keepdims=True)
            acc_ref[g] = acc_ref[g] * alpha + _dot(vch, p.astype(BF16))
            out.append((m_new, l_new))
        return tuple(out)

    def body(i, carry):
        c = 2 * i
        put_scores(c + 1, 1)
        carry = softmax_pv(c, 0, carry)
        put_scores(c + 2, 0)
        return softmax_pv(c + 1, 1, carry)

    init = tuple((jnp.full((1, tq), -jnp.inf, F32), jnp.zeros((1, tq), F32)) for _ in range(ATT_GROUPS))
    put_scores(0, 0)
    fin = lax.fori_loop(0, (n_kc - 1) // 2, body, init)
    if n_kc % 2 == 0:
        put_scores(n_kc - 1, 1)
        fin = softmax_pv(n_kc - 2, 0, fin)
    fin = softmax_pv(n_kc - 1, (n_kc - 1) % 2, fin)
    for g in range(ATT_GROUPS):
        o_ref[0, :, g * HEAD_DIM:(g + 1) * HEAD_DIM] = (acc_ref[g] / fin[g][1]).T.astype(BF16)


def _attention(qt, kn, vt, n_ctx):
    nb, lc, _ = kn.shape
    n_lat = lc - n_ctx
    tq = _largest_tile(n_lat, 256, 128)
    assert n_ctx % tq == 0
    kc = _largest_tile(lc, 768, 256)
    gw = ATT_GROUPS * HEAD_DIM
    kern = functools.partial(_attn_kernel, tq=tq, kc=kc, n_kc=lc // kc)
    vmem = 8 * lc * HEAD_DIM * 2 + 4 * ATT_GROUPS * kc * tq * 4 + 8 * gw * tq * 4
    return pl.pallas_call(
        kern,
        grid=(nb, ATT_KV_HEADS, n_lat // tq),
        in_specs=[pl.BlockSpec((1, gw, tq), lambda b, h, i: (b, h, n_ctx // tq + i)),
                  pl.BlockSpec((1, lc, HEAD_DIM), lambda b, h, i: (b, 0, h)),
                  pl.BlockSpec((1, HEAD_DIM, lc), lambda b, h, i: (b, h, 0))],
        out_specs=pl.BlockSpec((1, tq, gw), lambda b, h, i: (b, i, h)),
        out_shape=jax.ShapeDtypeStruct((nb, n_lat, ATT_W), BF16),
        scratch_shapes=[pltpu.VMEM((ATT_GROUPS, HEAD_DIM, tq), F32),
                        pltpu.VMEM((2, ATT_GROUPS, kc, tq), F32)],
        compiler_params=pltpu.CompilerParams(
            dimension_semantics=("arbitrary", "arbitrary", "arbitrary"),
            vmem_limit_bytes=_vmem_limit(vmem)),
        name="attention",
    )(qt, kn, vt)


def _hgrn_chunk(q_ref, v_ref, r_ref, lb, reverse, rows):
    c = HGRN_CHUNK
    q = q_ref[0, rows, :].astype(F32)
    v = v_ref[0, rows, :]
    r = r_ref[0, rows, :].astype(F32)
    sig = jax.nn.sigmoid(r)
    f = lb + (1.0 - lb) * sig
    logf = jnp.log(f)
    k = (1.0 - lb) * (1.0 - sig)
    ti = lax.broadcasted_iota(I32, (c, c), 0)
    si = lax.broadcasted_iota(I32, (c, c), 1)
    tri = jnp.where((si >= ti) if reverse else (si <= ti), 1.0, 0.0).astype(BF16)
    hi = logf.astype(BF16)
    lo = (logf - hi.astype(F32)).astype(BF16)
    bcum = _dot(tri, hi) + _dot(tri, lo)
    b_end = bcum[0:1, :] if reverse else bcum[c - 1:c, :]
    qt = (q * jnp.exp(bcum)).astype(BF16)
    kt = (k * jnp.exp(b_end - bcum)).astype(BF16)
    dec = jnp.exp(b_end)

    nsub = c // HGRN_SUB
    qp, kp, cols = [], [], []
    for blk in range(nsub):
        rs = slice(blk * HGRN_SUB, (blk + 1) * HGRN_SUB)
        cs = slice(blk * HGRN_SUB, c) if reverse else slice(0, (blk + 1) * HGRN_SUB)
        mid = blk * HGRN_SUB + HGRN_SUB // 2
        ref = bcum[mid:mid + 1, :]
        qp.append((q[rs] * jnp.exp(bcum[rs] - ref)).astype(BF16))
        kp.append((k[cs] * jnp.exp(ref - bcum[cs])).astype(BF16))
        cols.append(cs)

    keep = []
    for blk in range(nsub):
        n_cols = cols[blk].stop - cols[blk].start
        trow = blk * HGRN_SUB + lax.broadcasted_iota(I32, (HGRN_SUB, n_cols), 0)
        scol = cols[blk].start + lax.broadcasted_iota(I32, (HGRN_SUB, n_cols), 1)
        keep.append((scol >= trow) if reverse else (scol <= trow))
    return dict(v=v, qt=qt, kt=kt, dec=dec, qp=qp, kp=kp, cols=cols, keep=keep, rows=rows)


def _hgrn_emit(dirs, s_refs, o_refs):
    nsub = HGRN_CHUNK // HGRN_SUB
    heads = [slice(h * HGRN_DK, (h + 1) * HGRN_DK) for h in range(HGRN_HEADS)]
    inter = [[None] * len(chunks) for chunks in dirs]
    for k in range(len(dirs[0])):
        for di, (chunks, s_ref) in enumerate(zip(dirs, s_refs)):
            d = chunks[k]
            row = []
            for h, hs in enumerate(heads):
                st = s_ref[h]
                row.append(_dot_nt(d["qt"][:, hs], st.astype(BF16)))
                s_ref[h] = st * d["dec"][:, hs] + _dot_tn(d["v"][:, hs], d["kt"][:, hs])
            inter[di][k] = row
    scores = [[[[_dot_nt(d["qp"][blk][:, hs], d["kp"][blk][:, hs]) for blk in range(nsub)] for hs in heads]
               for d in chunks] for chunks in dirs]
    for chunks, sc_d, it_d, o_ref in zip(dirs, scores, inter, o_refs):
        for d, sc, it in zip(chunks, sc_d, it_d):
            outs = []
            for h, hs in enumerate(heads):
                parts = [_dot(jnp.where(d["keep"][blk], sc[h][blk], 0.0).astype(BF16), d["v"][d["cols"][blk], hs])
                         for blk in range(nsub)]
                outs.append(it[h] + jnp.concatenate(parts, axis=0))
            o_ref[0, d["rows"], :] = jnp.concatenate(outs, axis=1).astype(BF16)


def _hgrn_kernel(qf_ref, vf_ref, rf_ref, qb_ref, vb_ref, rb_ref, lb_ref, of_ref, ob_ref, sf_ref, sb_ref):
    @pl.when(pl.program_id(1) == 0)
    def _():
        sf_ref[...] = jnp.zeros(sf_ref.shape, F32)
        sb_ref[...] = jnp.zeros(sb_ref.shape, F32)

    n_layers = lb_ref.shape[0] // 2

    def lower_bound(direction):
        rows = [lb_ref[direction * n_layers + l:direction * n_layers + l + 1, :] for l in range(n_layers)]
        amax = functools.reduce(jnp.maximum, rows)
        e = [jnp.exp(a - amax) for a in rows]
        return e[0] / functools.reduce(lambda u, w: u + w, e)

    rows = [slice(k * HGRN_CHUNK, (k + 1) * HGRN_CHUNK) for k in range(HGRN_STEP_CHUNKS)]
    fwd = [_hgrn_chunk(qf_ref, vf_ref, rf_ref, lower_bound(0), False, rs) for rs in rows]
    bwd = [_hgrn_chunk(qb_ref, vb_ref, rb_ref, lower_bound(1), True, rs) for rs in reversed(rows)]
    _hgrn_emit([fwd, bwd], [sf_ref, sb_ref], [of_ref, ob_ref])


def _hgrn_scan(p, hgrn_lb, n_ctx):
    nb, lc, _ = p.shape
    c = HGRN_CHUNK * HGRN_STEP_CHUNKS
    assert n_ctx % c == 0 and lc % c == 0
    n_chunks = lc // c
    ctx_chunks = n_ctx // c

    def fwd(col):
        return lambda b, s: (b, s, col)

    def bwd_chunk(s):
        return jnp.where(s < ctx_chunks, ctx_chunks - 1 - s, n_chunks - 1 + ctx_chunks - s)

    def bwd(col):
        return lambda b, s: (b, bwd_chunk(s), col)

    blk = (1, c, HGRN_W)
    lb2 = hgrn_lb.reshape(-1, HGRN_W)
    return pl.pallas_call(
        _hgrn_kernel,
        grid=(nb, n_chunks),
        in_specs=[pl.BlockSpec(blk, fwd(COL_HQ // HGRN_W)), pl.BlockSpec(blk, fwd(COL_HI // HGRN_W)),
                  pl.BlockSpec(blk, fwd(COL_FF // HGRN_W)),
                  pl.BlockSpec(blk, bwd(COL_HQ // HGRN_W)), pl.BlockSpec(blk, bwd(COL_HI // HGRN_W)),
                  pl.BlockSpec(blk, bwd(COL_FB // HGRN_W)),
                  pl.BlockSpec(lb2.shape, lambda b, s: (0, 0))],
        out_specs=[pl.BlockSpec(blk, fwd(0)), pl.BlockSpec(blk, bwd(0))],
        out_shape=[jax.ShapeDtypeStruct((nb, lc, HGRN_W), BF16)] * 2,
        scratch_shapes=[pltpu.VMEM((HGRN_HEADS, HGRN_DK, HGRN_DK), F32)] * 2,
        compiler_params=pltpu.CompilerParams(dimension_semantics=("arbitrary", "arbitrary")),
        name="hgrn_scan",
    )(p, p, p, p, p, p, lb2)


def _merge_kernel(of_ref, ob_ref, hg_ref, gh_ref, ga_ref, oa_ref, x_ref, mod_ref, ng_ref, hng_ref,
                  wb0_ref, wb1_ref, wo_ref, rw_ref, rb_ref,
                  x1_ref, t_ref, ids_ref, wts_ref):
    b = pl.program_id(0)
    m = mod_ref[pl.ds(b, 1), :]
    gate_mix = m[:, 2 * D_MODEL:3 * D_MODEL]
    shift_f = m[:, 3 * D_MODEL:4 * D_MODEL]
    scale_f = m[:, 4 * D_MODEL:5 * D_MODEL]
    hng = hng_ref[...]
    rw = rw_ref[...]
    rw_hi = rw.astype(BF16)
    rw_lo = (rw - rw_hi.astype(F32)).astype(BF16)
    groups = [slice(r, r + MERGE_ROWS) for r in range(0, of_ref.shape[1], MERGE_ROWS)]

    branch = []
    for rs in groups:
        o = of_ref[0, rs, :].astype(F32) + ob_ref[0, rs, :].astype(F32)
        o_h = jnp.concatenate(
            [_rms(o[:, h * HGRN_DK:(h + 1) * HGRN_DK], hng) for h in range(HGRN_HEADS)], axis=1)
        g_raw = hg_ref[0, rs, :].astype(F32)
        o_h = o_h * (g_raw * jax.nn.sigmoid(g_raw))
        branch.append((_dot(o_h.astype(BF16), wb0_ref[...]), _dot(oa_ref[0, rs, :], wb1_ref[...])))
    mixed = []
    for rs, (y_h, y_a) in zip(groups, branch):
        y = (jax.nn.sigmoid(gh_ref[0, rs, :].astype(F32)) * y_h
             + jax.nn.sigmoid(ga_ref[0, rs, :].astype(F32)) * y_a)
        mixed.append(_dot(y.astype(BF16), wo_ref[...]))
    logits_all = []
    for rs, y in zip(groups, mixed):
        x1 = x_ref[0, rs, :] + gate_mix * _rms(y, ng_ref[1:2, :])
        x1_ref[0, rs, :] = x1
        t = _rms(x1, ng_ref[2:3, :]) * (1.0 + scale_f) + shift_f
        t_ref[rs, :] = t.astype(BF16)
        t_hi = t.astype(BF16)
        t_lo = (t - t_hi.astype(F32)).astype(BF16)
        logits_all.append(_dot_nt(rw_hi, t_hi) + _dot_nt(rw_hi, t_lo) + _dot_nt(rw_lo, t_hi) + rb_ref[...])
    for rs, logits in zip(groups, logits_all):
        eidx = lax.broadcasted_iota(I32, logits.shape, 0).astype(F32)
        vals = []
        for j in range(TOP_K):
            mx = jnp.max(logits, axis=0, keepdims=True)
            idx = jnp.min(jnp.where(logits == mx, eidx, float(N_EXPERTS)), axis=0, keepdims=True)
            ids_ref[j:j + 1, rs] = idx.astype(I32)
            vals.append(mx)
            logits = jnp.where(eidx == idx, -jnp.inf, logits)
        ex = [jnp.exp(vj - vals[0]) for vj in vals]
        den = ex[0] + ex[1] + ex[2] + ex[3]
        for j in range(TOP_K):
            wts_ref[j:j + 1, rs] = ex[j] / den


def _merge(o_f, o_b, p, o_a, x, mod, norm_g, hgrn_norm_g, wb0, wb1, wo, router_wt, router_b, n_ctx):
    nb, n_lat, d = x.shape
    tm = _largest_tile(n_lat, 256, 128)
    assert n_ctx % tm == 0
    off = n_ctx // tm
    nt = n_lat // tm
    row = (1, tm, d)

    def pcol(col):
        return pl.BlockSpec(row, lambda b, i: (b, off + i, col // d))

    def full(a):
        return pl.BlockSpec(a.shape, lambda b, i: (0,) * a.ndim)

    tok = nb * n_lat
    return pl.pallas_call(
        _merge_kernel,
        grid=(nb, nt),
        in_specs=[pl.BlockSpec(row, lambda b, i: (b, off + i, 0)),
                  pl.BlockSpec(row, lambda b, i: (b, off + i, 0)),
                  pcol(COL_HG), pcol(COL_GH), pcol(COL_GA),
                  pl.BlockSpec(row, lambda b, i: (b, i, 0)),
                  pl.BlockSpec(row, lambda b, i: (b, i, 0)),
                  full(mod), full(norm_g), full(hgrn_norm_g), full(wb0), full(wb1), full(wo),
                  full(router_wt), full(router_b)],
        out_specs=[pl.BlockSpec(row, lambda b, i: (b, i, 0)),
                   pl.BlockSpec((tm, d), lambda b, i: (b * nt + i, 0)),
                   pl.BlockSpec((TOP_K, tm), lambda b, i: (0, b * nt + i)),
                   pl.BlockSpec((TOP_K, tm), lambda b, i: (0, b * nt + i))],
        out_shape=[jax.ShapeDtypeStruct((nb, n_lat, d), F32),
                   jax.ShapeDtypeStruct((tok, d), BF16),
                   jax.ShapeDtypeStruct((TOP_K, tok), I32),
                   jax.ShapeDtypeStruct((TOP_K, tok), F32)],
        compiler_params=pltpu.CompilerParams(
            dimension_semantics=("arbitrary", "arbitrary"),
            vmem_limit_bytes=_vmem_limit(40 * 1024 * 1024)),
        name="merge_router",
    )(o_f, o_b, p, p, p, o_a, x, mod, norm_g, hgrn_norm_g, wb0, wb1, wo, router_wt, router_b)


def _rank_kernel(ids_ref, rank_ref, cnt_ref, *, tt):
    eidx = lax.broadcasted_iota(I32, (N_EXPERTS, tt), 0)
    si = lax.broadcasted_iota(I32, (tt, tt), 0)
    ti = lax.broadcasted_iota(I32, (tt, tt), 1)
    before = jnp.where(si < ti, 1.0, 0.0).astype(BF16)
    seen = jnp.zeros((N_EXPERTS, 1), F32)
    for j in range(TOP_K):
        onehot = eidx == ids_ref[j:j + 1, :]
        oh = jnp.where(onehot, 1.0, 0.0)
        earlier = _dot(oh.astype(BF16), before)
        rank = jnp.sum(jnp.where(onehot, seen + earlier, 0.0), axis=0, keepdims=True)
        rank_ref[j:j + 1, :] = rank.astype(I32)
        seen = seen + jnp.sum(oh, axis=1, keepdims=True)
    cnt_ref[0] = seen.astype(I32)


def _expert_rank(ids, tt):
    _, tok = ids.shape
    return pl.pallas_call(
        functools.partial(_rank_kernel, tt=tt),
        grid=(tok // tt,),
        in_specs=[pl.BlockSpec((TOP_K, tt), lambda i: (0, i))],
        out_specs=[pl.BlockSpec((TOP_K, tt), lambda i: (0, i)),
                   pl.BlockSpec((1, N_EXPERTS, 1), lambda i: (i, 0, 0))],
        out_shape=[jax.ShapeDtypeStruct((TOP_K, tok), I32),
                   jax.ShapeDtypeStruct((tok // tt, N_EXPERTS, 1), I32)],
        compiler_params=pltpu.CompilerParams(dimension_semantics=("arbitrary",)),
        name="expert_rank",
    )(ids)


def _start_blocks(n_blocks, make_copy):
    def start(b, carry):
        make_copy(b).start()
        return carry

    lax.fori_loop(0, n_blocks, start, 0)


def _wait_blocks(n_blocks, make_copy, make_group_copy):
    def wait_group(g, carry):
        make_group_copy().wait()
        return carry

    def wait_one(b, carry):
        make_copy(0).wait()
        return carry

    lax.fori_loop(0, lax.shift_right_logical(n_blocks, WAIT_GROUP_LOG2), wait_group, 0)
    lax.fori_loop(0, jnp.bitwise_and(n_blocks, (1 << WAIT_GROUP_LOG2) - 1), wait_one, 0)


def _scatter_kernel(zt_ref, zvalid_ref, nblk_ref, bdst_ref, lp_ref, t_ref, xs_ref, stage, zero_buf, sem, zsem,
                    *, tt, n_stage):
    @pl.when(pl.program_id(0) == 0)
    def _():
        zero_buf[...] = jnp.zeros(zero_buf.shape, F32)

        def zero_copy(k):
            return pltpu.make_async_copy(zero_buf, xs_ref.at[pl.ds(zt_ref[k] * MOE_TILE, MOE_TILE)], zsem)

        def start(k, carry):
            @pl.when(zvalid_ref[k] == 1)
            def _():
                zero_copy(k).start()
            return carry

        def wait(k, carry):
            @pl.when(zvalid_ref[k] == 1)
            def _():
                zero_copy(k).wait()
            return carry

        lax.fori_loop(0, zt_ref.shape[0], start, 0)
        lax.fori_loop(0, zt_ref.shape[0], wait, 0)

    i = pl.program_id(0)
    slot = i % 2
    t_bf = t_ref[...]
    lps = [lp_ref[j:j + 1, :] for j in range(TOP_K)]
    for rc in range(n_stage // PERM_ROWS):
        row = rc * PERM_ROWS + lax.broadcasted_iota(I32, (PERM_ROWS, tt), 0)
        sel = jnp.where(row == lps[TOP_K - 1], 1.0, 0.0)
        for j in range(TOP_K - 2, -1, -1):
            sel = jnp.where(row == lps[j], 1.0, sel)
        stage[slot, rc * PERM_ROWS:(rc + 1) * PERM_ROWS, :] = _dot(sel.astype(BF16), t_bf)

    def copy(b):
        src = pl.multiple_of(b * RUN_ALIGN, RUN_ALIGN)
        dst = pl.multiple_of(bdst_ref[0, 0, b], RUN_ALIGN)
        return pltpu.make_async_copy(stage.at[slot, pl.ds(src, RUN_ALIGN)], xs_ref.at[pl.ds(dst, RUN_ALIGN)],
                                     sem.at[slot])

    def waiter(s):
        rows = RUN_ALIGN << WAIT_GROUP_LOG2
        return (lambda b: pltpu.make_async_copy(stage.at[s, pl.ds(0, RUN_ALIGN)],
                                                xs_ref.at[pl.ds(0, RUN_ALIGN)], sem.at[s]),
                lambda: pltpu.make_async_copy(stage.at[s, pl.ds(0, rows)], xs_ref.at[pl.ds(0, rows)], sem.at[s]))

    _start_blocks(nblk_ref[i], copy)

    @pl.when(i > 0)
    def _():
        _wait_blocks(nblk_ref[i - 1], *waiter(1 - slot))

    @pl.when(i == pl.num_programs(0) - 1)
    def _():
        _wait_blocks(nblk_ref[i], *waiter(slot))


def _scatter_rows(zero_tiles, zero_valid, n_blocks, block_rows, lp, t, n_rows, tt, n_stage):
    tok, d = t.shape
    nb = block_rows.shape[2]
    return pl.pallas_call(
        functools.partial(_scatter_kernel, tt=tt, n_stage=n_stage),
        grid_spec=pltpu.PrefetchScalarGridSpec(
            num_scalar_prefetch=3,
            grid=(tok // tt,),
            in_specs=[pl.BlockSpec((1, 1, nb), lambda i, *_: (i, 0, 0), memory_space=pltpu.SMEM),
                      pl.BlockSpec((TOP_K, tt), lambda i, *_: (0, i)),
                      pl.BlockSpec((tt, d), lambda i, *_: (i, 0))],
            out_specs=pl.BlockSpec(memory_space=pl.ANY),
            scratch_shapes=[pltpu.VMEM((2, n_stage, d), F32), pltpu.VMEM((MOE_TILE, d), F32),
                            pltpu.SemaphoreType.DMA((2,)), pltpu.SemaphoreType.DMA(())]),
        out_shape=jax.ShapeDtypeStruct((n_rows, d), F32),
        compiler_params=pltpu.CompilerParams(
            dimension_semantics=("arbitrary",), vmem_limit_bytes=_vmem_limit(48 * 1024 * 1024)),
        name="moe_scatter",
    )(zero_tiles, zero_valid, n_blocks, block_rows, lp, t)


def _expert_kernel(te_ref, first_ref, nused_ref, slot_ref, next_ref, xs_ref, wu_hbm, bu_ref, wd_hbm, bd_ref,
                   ys_ref, wu32, wd32, wu_bf, wd_bf, sem_u, sem_d):
    i = pl.program_id(0)

    def weight_copies(e, s):
        return (pltpu.make_async_copy(wu_hbm.at[e], wu32.at[s], sem_u.at[s]),
                pltpu.make_async_copy(wd_hbm.at[e], wd32.at[s], sem_d.at[s]))

    @pl.when(i >= nused_ref[0])
    def _():
        ys_ref[...] = jnp.zeros(ys_ref.shape, F32)

    @pl.when(i < nused_ref[0])
    def _():
        @pl.when(first_ref[i] == 1)
        def _():
            s = slot_ref[i]

            @pl.when(i == 0)
            def _():
                for cp in weight_copies(te_ref[0], 0):
                    cp.start()

            for cp in weight_copies(te_ref[i], s):
                cp.wait()
            wu_bf[...] = wu32[s].astype(BF16)
            wd_bf[...] = wd32[s].astype(BF16)

            @pl.when(next_ref[i] >= 0)
            def _():
                for cp in weight_copies(next_ref[i], 1 - s):
                    cp.start()

        h = _dot(xs_ref[...].astype(BF16), wu_bf[...]) + bu_ref[0]
        glu = jnp.minimum(h[:, :D_EXPERT], SWIGLU_LIMIT)
        lin = jnp.clip(h[:, D_EXPERT:], -SWIGLU_LIMIT, SWIGLU_LIMIT)
        a = glu * jax.nn.sigmoid(SWIGLU_ALPHA * glu) * (lin + 1.0)
        ys_ref[...] = _dot(a.astype(BF16), wd_bf[...]) + bd_ref[0]


def _experts(xs, tile_expert, tile_first, n_used, tile_slot, next_expert, w_up, b_up, w_down, b_down):
    n_rows, d = xs.shape
    tm = MOE_TILE
    n_tiles = n_rows // tm

    def tile(i, te, first, nused, *_):
        return (jnp.minimum(i, nused[0] - 1), 0)

    def bsel(i, te, *_):
        return (te[i], 0, 0)

    vmem = 2 * (d * 2 * D_EXPERT * 4 + D_EXPERT * d * 4) + d * 2 * D_EXPERT * 2 + D_EXPERT * d * 2 \
        + 4 * tm * d * 4 + 6 * tm * 2 * D_EXPERT * 4
    return pl.pallas_call(
        _expert_kernel,
        grid_spec=pltpu.PrefetchScalarGridSpec(
            num_scalar_prefetch=5,
            grid=(n_tiles,),
            in_specs=[pl.BlockSpec((tm, d), tile),
                      pl.BlockSpec(memory_space=pl.ANY),
                      pl.BlockSpec((1, 1, 2 * D_EXPERT), bsel),
                      pl.BlockSpec(memory_space=pl.ANY),
                      pl.BlockSpec((1, 1, d), bsel)],
            out_specs=pl.BlockSpec((tm, d), lambda i, *_: (i, 0)),
            scratch_shapes=[pltpu.VMEM((2, d, 2 * D_EXPERT), F32), pltpu.VMEM((2, D_EXPERT, d), F32),
                            pltpu.VMEM((d, 2 * D_EXPERT), BF16), pltpu.VMEM((D_EXPERT, d), BF16),
                            pltpu.SemaphoreType.DMA((2,)), pltpu.SemaphoreType.DMA((2,))]),
        out_shape=jax.ShapeDtypeStruct((n_rows, d), F32),
        compiler_params=pltpu.CompilerParams(
            dimension_semantics=("arbitrary",), vmem_limit_bytes=_vmem_limit(vmem)),
        name="moe_experts",
    )(tile_expert, tile_first, n_used, tile_slot, next_expert, xs, w_up, b_up.reshape(N_EXPERTS, 1, -1), w_down,
      b_down.reshape(N_EXPERTS, 1, -1))


def _combine_kernel(nblk_ref, bsrc_ref, ys_ref, lp_ref, w_ref, x1_ref, mod_ref, ng_ref, o_ref, stage, sem,
                    *, tt, n_stage, tiles_per_batch):
    i = pl.program_id(0)
    slot = i % 2

    def fetch(tile, s):
        def copy(b):
            src = pl.multiple_of(bsrc_ref[tile, b], RUN_ALIGN)
            dst = pl.multiple_of(b * RUN_ALIGN, RUN_ALIGN)
            return pltpu.make_async_copy(ys_ref.at[pl.ds(src, RUN_ALIGN)], stage.at[s, pl.ds(dst, RUN_ALIGN)],
                                         sem.at[s])
        return copy

    def group_copy():
        rows = RUN_ALIGN << WAIT_GROUP_LOG2
        return pltpu.make_async_copy(ys_ref.at[pl.ds(0, rows)], stage.at[slot, pl.ds(0, rows)], sem.at[slot])

    @pl.when(i == 0)
    def _():
        stage[...] = jnp.zeros(stage.shape, F32)
        _start_blocks(nblk_ref[0], fetch(0, 0))

    @pl.when(i + 1 < pl.num_programs(0))
    def _():
        _start_blocks(nblk_ref[i + 1], fetch(i + 1, 1 - slot))

    _wait_blocks(nblk_ref[i], fetch(i, slot), group_copy)

    lp = [jnp.broadcast_to(lp_ref[:, j:j + 1], (tt, PERM_ROWS)) for j in range(TOP_K)]
    w = [jnp.broadcast_to(w_ref[:, j:j + 1], (tt, PERM_ROWS)) for j in range(TOP_K)]
    f = jnp.zeros((tt, D_MODEL), F32)
    for lc in range(n_stage // PERM_ROWS):
        col = lc * PERM_ROWS + lax.broadcasted_iota(I32, (tt, PERM_ROWS), 1)
        wp = jnp.where(col == lp[TOP_K - 1], w[TOP_K - 1], 0.0)
        for j in range(TOP_K - 2, -1, -1):
            wp = jnp.where(col == lp[j], w[j], wp)
        f = f + _dot(wp.astype(BF16), stage[slot, lc * PERM_ROWS:(lc + 1) * PERM_ROWS, :].astype(BF16))
    b = i // tiles_per_batch
    gate_f = mod_ref[pl.ds(b, 1), 5 * D_MODEL:6 * D_MODEL]
    o_ref[...] = x1_ref[...] + gate_f * _rms(f, ng_ref[3:4, :])


def _combine(n_blocks, block_rows, ys, lp_t, wts_t, x1, mod, norm_g, tt, n_stage, tiles_per_batch):
    tok, d = x1.shape
    kern = functools.partial(_combine_kernel, tt=tt, n_stage=n_stage, tiles_per_batch=tiles_per_batch)
    return pl.pallas_call(
        kern,
        grid_spec=pltpu.PrefetchScalarGridSpec(
            num_scalar_prefetch=2,
            grid=(tok // tt,),
            in_specs=[pl.BlockSpec(memory_space=pl.ANY),
                      pl.BlockSpec((tt, TOP_K), lambda i, *_: (i, 0)),
                      pl.BlockSpec((tt, TOP_K), lambda i, *_: (i, 0)),
                      pl.BlockSpec((tt, d), lambda i, *_: (i, 0)),
                      pl.BlockSpec(mod.shape, lambda i, *_: (0, 0)),
                      pl.BlockSpec(norm_g.shape, lambda i, *_: (0, 0))],
            out_specs=pl.BlockSpec((tt, d), lambda i, *_: (i, 0)),
            scratch_shapes=[pltpu.VMEM((2, n_stage, d), F32), pltpu.SemaphoreType.DMA((2,))]),
        out_shape=jax.ShapeDtypeStruct((tok, d), F32),
        compiler_params=pltpu.CompilerParams(
            dimension_semantics=("arbitrary",), vmem_limit_bytes=_vmem_limit(48 * 1024 * 1024)),
        name="moe_combine",
    )(n_blocks, block_rows[:, 0, :], ys, lp_t, wts_t, x1, mod, norm_g)


def _rope_tables(n_ctx, n_lat):
    inv = ROPE_THETA ** (-np.arange(0, ROPE_AXIS_DIM, 2, dtype=np.float32) / ROPE_AXIS_DIM)
    idx = jnp.arange(n_lat, dtype=I32)
    rows = (idx // GRID_W).astype(F32)
    cols = (idx % GRID_W).astype(F32)
    half = ROPE_AXIS_DIM // 2
    inv = jnp.asarray(inv, F32)
    ang_r = rows[:, None] * inv[None, :]
    ang_c = cols[:, None] * inv[None, :]
    ang = jnp.concatenate([ang_r, ang_r, ang_c, ang_c], axis=1)
    cos, sin = jnp.cos(ang), jnp.sin(ang)
    lane = np.arange(HEAD_DIM)
    first = jnp.asarray((lane % ROPE_AXIS_DIM) < half)
    sin_a = jnp.where(first, -sin, 0.0)
    sin_b = jnp.where(first, 0.0, sin)
    pad1 = jnp.ones((n_ctx, HEAD_DIM), F32)
    pad0 = jnp.zeros((n_ctx, HEAD_DIM), F32)
    return (jnp.concatenate([pad1, cos], axis=0), jnp.concatenate([pad0, sin_a], axis=0),
            jnp.concatenate([pad0, sin_b], axis=0))


def kernel(x, c, ctx, c_ctx, w_mod, b_mod, norm_g, w_in, hgrn_lb, hgrn_norm_g, qk_norm_g, w_branch, w_out,
           router_w, router_b, w_up, b_up, w_down, b_down):
    nb, n_lat, d = x.shape
    n_ctx = ctx.shape[1]
    assert d == D_MODEL and w_mod.shape[0] == 1, "single-layer kernel"
    tok = nb * n_lat

    cv = jnp.zeros((8, d), F32).at[:nb].set(c).at[nb].set(c_ctx)
    mod = _modulation(cv, w_mod[0], b_mod[0])

    xc = jnp.concatenate([ctx, x], axis=1)
    sizes = np.cumsum([0, HGRN_W, HGRN_W, HGRN_W, HGRN_W, HGRN_W, ATT_W, KV_W, KV_W, D_MODEL, D_MODEL])
    order = [0, 1, 2, 3, 4, 5, 8, 9, 6, 7]
    w_in_bf = jnp.concatenate([w_in[0][:, sizes[i]:sizes[i + 1]] for i in order], axis=1).astype(BF16)
    p = _in_projection(xc, mod, norm_g[0], w_in_bf, n_ctx)

    cos, sin_a, sin_b = _rope_tables(n_ctx, n_lat)
    qt, kn, vt = _qk_prep(p, cos, sin_a, sin_b, qk_norm_g[0])
    o_a = _attention(qt, kn, vt, n_ctx)
    o_f, o_b = _hgrn_scan(p, hgrn_lb, n_ctx)

    x1, t, ids, wts = _merge(
        o_f, o_b, p, o_a, x, mod, norm_g[0], hgrn_norm_g, w_branch[0, 0].astype(BF16),
        w_branch[0, 1].astype(BF16), w_out[0].astype(BF16), router_w[0].T, router_b[0].reshape(N_EXPERTS, 1),
        n_ctx)

    tt = _largest_tile(n_lat, DISPATCH_TILE, 128)
    n_tt = tok // tt
    rank, counts = _expert_rank(ids, tt)
    counts = counts[:, :, 0]
    run = (counts + RUN_ALIGN - 1) // RUN_ALIGN * RUN_ALIGN
    run_end = jnp.cumsum(run, axis=1)
    stage_off = run_end - run
    expert_rows = jnp.sum(run, axis=0)
    tiles_e = (expert_rows + MOE_TILE - 1) // MOE_TILE
    tile_end = jnp.cumsum(tiles_e)
    starts = (tile_end - tiles_e) * MOE_TILE
    run_dst = starts[None, :] + jnp.cumsum(run, axis=0) - run
    n_tiles = -(-(tok * TOP_K + N_EXPERTS * n_tt * (RUN_ALIGN - 1)) // MOE_TILE) + N_EXPERTS
    tile_idx = jnp.arange(n_tiles, dtype=I32)
    tile_expert = jnp.sum((tile_end[None, :] <= tile_idx[:, None]).astype(I32), axis=1)
    tile_expert = jnp.minimum(tile_expert, N_EXPERTS - 1)
    n_used = tile_end[-1:].astype(I32)
    tile_expert = jnp.where(tile_idx < n_used[0], tile_expert, tile_expert[jnp.maximum(n_used[0] - 1, 0)])
    tile_first = jnp.concatenate([jnp.ones((1,), I32), (tile_expert[1:] != tile_expert[:-1]).astype(I32)])
    e_idx = jnp.arange(N_EXPERTS, dtype=I32)
    used = tiles_e > 0
    e_slot = (jnp.cumsum(used.astype(I32)) - 1) % 2
    later = used[None, :] & (e_idx[None, :] > e_idx[:, None])
    e_next = jnp.min(jnp.where(later, e_idx[None, :], N_EXPERTS), axis=1)
    e_next = jnp.where(e_next == N_EXPERTS, -1, e_next)
    tile_onehot = tile_expert[:, None] == e_idx[None, :]
    tile_slot = jnp.sum(jnp.where(tile_onehot, e_slot[None, :], 0), axis=1).astype(I32)
    next_expert = jnp.sum(jnp.where(tile_onehot, e_next[None, :], 0), axis=1).astype(I32)

    n_stage = -(-(TOP_K * tt + N_EXPERTS * (RUN_ALIGN - 1)) // PERM_ROWS) * PERM_ROWS
    ids_t = ids.reshape(TOP_K, n_tt, tt)
    onehot = ids_t[..., None] == jnp.arange(N_EXPERTS, dtype=I32)
    lp = jnp.sum(jnp.where(onehot, stage_off[None, :, None, :], 0), axis=-1).reshape(TOP_K, tok) + rank
    blk_row = jnp.arange(n_stage // RUN_ALIGN, dtype=I32) * RUN_ALIGN
    blk_expert = jnp.minimum(jnp.sum((run_end[:, None, :] <= blk_row[None, :, None]).astype(I32), axis=-1),
                             N_EXPERTS - 1)
    blk_onehot = blk_expert[..., None] == jnp.arange(N_EXPERTS, dtype=I32)
    blk_dst = jnp.sum(jnp.where(blk_onehot, (run_dst - stage_off)[:, None, :], 0), axis=-1) + blk_row[None, :]
    n_blocks = (run_end[:, -1] // RUN_ALIGN).astype(I32)
    blk_dst = jnp.where(blk_row[None, :] < run_end[:, -1:], blk_dst, 0).astype(I32)[:, None, :]

    n_tail = n_tiles - (tok * TOP_K) // MOE_TILE
    tail = n_used[0] + jnp.arange(n_tail, dtype=I32)
    zero_tiles = jnp.clip(jnp.concatenate([tile_end.astype(I32) - 1, tail]), 0, n_tiles - 1)
    zero_valid = jnp.concatenate([expert_rows % MOE_TILE != 0, tail < n_tiles]).astype(I32)

    xs = _scatter_rows(zero_tiles, zero_valid, n_blocks, blk_dst, lp, t, n_tiles * MOE_TILE, tt, n_stage)
    ys = _experts(xs, tile_expert, tile_first, n_used, tile_slot, next_expert, w_up[0], b_up[0], w_down[0],
                  b_down[0])
    out = _combine(n_blocks, blk_dst, ys, lp.T, wts.T, x1.reshape(tok, d), mod, norm_g[0], tt, n_stage,
                   n_lat // tt)
    return out.reshape(nb, n_lat, d)
```

```python
import functools

import jax
import jax.numpy as jnp
import numpy as np
from jax import lax
from jax.experimental import pallas as pl
from jax.experimental.pallas import tpu as pltpu

F32 = jnp.float32
BF16 = jnp.bfloat16
I32 = jnp.int32

D_MODEL = 1024
GRID_W = 64
N_MOD = 6
EPS = 1e-6
HGRN_HEADS = 8
HGRN_DK = 128
HGRN_W = HGRN_HEADS * HGRN_DK
ATT_HEADS = 8
ATT_KV_HEADS = 2
ATT_GROUPS = ATT_HEADS // ATT_KV_HEADS
HEAD_DIM = 128
ATT_W = ATT_HEADS * HEAD_DIM
KV_W = ATT_KV_HEADS * HEAD_DIM
ROPE_AXIS_DIM = HEAD_DIM // 2
ROPE_THETA = 10000.0
N_EXPERTS = 32
TOP_K = 4
D_EXPERT = 1024
SWIGLU_LIMIT = 7.0
SWIGLU_ALPHA = 1.702

COL_HQ, COL_HI, COL_FF, COL_FB, COL_HG, COL_AQ, COL_GH, COL_GA = (i * 1024 for i in range(8))
COL_K = 8 * 1024
COL_V = COL_K + KV_W
IN_COLS = COL_V + KV_W

V7X_VMEM_BYTES = 64 * 1024 * 1024
LANES = 128

HGRN_CHUNK = 64
HGRN_STEP_CHUNKS = 2
HGRN_SUB = 16
MOE_TILE = 256
MERGE_ROWS = 128
INPROJ_CHAINS = 2
DISPATCH_TILE = 512
RUN_ALIGN = 8
PERM_ROWS = 256
WAIT_GROUP_LOG2 = 5


def _vmem_limit(nbytes):
    return int(min(V7X_VMEM_BYTES - 6 * 1024 * 1024, max(nbytes, 32 * 1024 * 1024)))


def _largest_tile(n, cap, mult):
    best = None
    for t in range(mult, min(n, cap) + 1, mult):
        if n % t == 0:
            best = t
    assert best is not None, (n, cap, mult)
    return best


def _rms(x, g):
    return x * lax.rsqrt(jnp.mean(x * x, axis=-1, keepdims=True) + EPS) * g


def _dot(a, b):
    return jnp.dot(a, b, preferred_element_type=F32)


def _dot_nt(a, b):
    return lax.dot_general(a, b, (((1,), (1,)), ((), ())), preferred_element_type=F32)


def _dot_tn(a, b):
    return lax.dot_general(a, b, (((0,), (0,)), ((), ())), preferred_element_type=F32)


def _mod_kernel(cv_ref, w_ref, b_ref, o_ref):
    cv = cv_ref[...]
    s = cv * jax.nn.sigmoid(cv)
    o_ref[...] = jnp.dot(s, w_ref[...], preferred_element_type=F32,
                         precision=lax.Precision.HIGHEST) + b_ref[...]


def _modulation(cv, w_mod, b_mod):
    rows, d = cv.shape
    n = w_mod.shape[1]
    tn = 1024
    return pl.pallas_call(
        _mod_kernel,
        grid=(n // tn,),
        in_specs=[pl.BlockSpec((rows, d), lambda j: (0, 0)),
                  pl.BlockSpec((d, tn), lambda j: (0, j)),
                  pl.BlockSpec((1, tn), lambda j: (0, j))],
        out_specs=pl.BlockSpec((rows, tn), lambda j: (0, j)),
        out_shape=jax.ShapeDtypeStruct((rows, n), F32),
        name="modulation",
    )(cv, w_mod, b_mod.reshape(1, n))


def _inproj_kernel(x_ref, mod_ref, g_ref, w_ref, o_ref, *, tm, n_ctx, n_batch):
    b = pl.program_id(1)
    i = pl.program_id(2)
    m_lat = mod_ref[pl.ds(b, 1), :]
    m_ctx = mod_ref[n_batch:n_batch + 1, :]
    rows = tm // INPROJ_CHAINS
    us = []
    for c in range(INPROJ_CHAINS):
        xn = _rms(x_ref[0, c * rows:(c + 1) * rows, :], g_ref[0:1, :])
        row = i * tm + c * rows + lax.broadcasted_iota(I32, (rows, 1), 0)
        is_ctx = row < n_ctx
        shift = jnp.where(is_ctx, m_ctx[:, 0:D_MODEL], m_lat[:, 0:D_MODEL])
        scale = jnp.where(is_ctx, m_ctx[:, D_MODEL:2 * D_MODEL], m_lat[:, D_MODEL:2 * D_MODEL])
        us.append((xn * (1.0 + scale) + shift).astype(BF16))
    for c in range(INPROJ_CHAINS):
        o_ref[0, c * rows:(c + 1) * rows, :] = _dot(us[c], w_ref[...]).astype(BF16)


def _in_projection(xc, mod, norm_g, w_in_bf, n_ctx):
    nb, lc, d = xc.shape
    n = w_in_bf.shape[1]
    tm = _largest_tile(lc, 1056, 16)
    tn = 2176
    assert n % tn == 0
    kern = functools.partial(_inproj_kernel, tm=tm, n_ctx=n_ctx, n_batch=nb)
    vmem = 2 * (tm * d * 4 + d * tn * 2 + tm * tn * 2) + 6 * tm * d * 4
    return pl.pallas_call(
        kern,
        grid=(n // tn, nb, lc // tm),
        in_specs=[pl.BlockSpec((1, tm, d), lambda j, b, i: (b, i, 0)),
                  pl.BlockSpec(mod.shape, lambda j, b, i: (0, 0)),
                  pl.BlockSpec(norm_g.shape, lambda j, b, i: (0, 0)),
                  pl.BlockSpec((d, tn), lambda j, b, i: (0, j))],
        out_specs=pl.BlockSpec((1, tm, tn), lambda j, b, i: (b, i, j)),
        out_shape=jax.ShapeDtypeStruct((nb, lc, n), BF16),
        compiler_params=pltpu.CompilerParams(
            dimension_semantics=("arbitrary", "arbitrary", "arbitrary"),
            vmem_limit_bytes=_vmem_limit(vmem)),
        name="in_projection",
    )(xc, mod, norm_g, w_in_bf)


def _rotate_half_matrix():
    half = ROPE_AXIS_DIM // 2
    i = lax.broadcasted_iota(I32, (HEAD_DIM, HEAD_DIM), 0)
    j = lax.broadcasted_iota(I32, (HEAD_DIM, HEAD_DIM), 1)
    first = jnp.bitwise_and(j, ROPE_AXIS_DIM - 1) < half
    r = jnp.where(i == j + half, jnp.where(first, -1.0, 0.0), jnp.where(i == j - half, jnp.where(first, 0.0, 1.0), 0.0))
    return r.astype(BF16)


def _rope(xn, cos, sin, rot):
    hi = xn.astype(BF16)
    lo = (xn - hi.astype(F32)).astype(BF16)
    return xn * cos + (_dot(hi, rot) + _dot(lo, rot)) * sin


def _qkprep_kernel(q_ref, k_ref, v_ref, cos_ref, sa_ref, sb_ref, g_ref, qt_ref, ko_ref, vt_ref):
    cos, sn = cos_ref[...], sb_ref[...] - sa_ref[...]
    rot = _rotate_half_matrix()
    gq, gk = g_ref[0:1, :], g_ref[1:2, :]
    qscale = HEAD_DIM ** -0.5 * np.log2(np.e)
    for h in range(ATT_HEADS):
        hs = slice(h * HEAD_DIM, (h + 1) * HEAD_DIM)
        xn = _rms(q_ref[0, :, hs].astype(F32), gq)
        qt_ref[0, hs, :] = (_rope(xn, cos, sn, rot) * qscale).T.astype(BF16)
    for h in range(ATT_KV_HEADS):
        hs = slice(h * HEAD_DIM, (h + 1) * HEAD_DIM)
        xn = _rms(k_ref[0, :, hs].astype(F32), gk)
        ko_ref[0, :, hs] = _rope(xn, cos, sn, rot).astype(BF16)
        vt_ref[0, hs, :] = v_ref[0, :, hs].astype(F32).T.astype(BF16)


def _qk_prep(p, cos, sin_a, sin_b, qk_norm_g):
    nb, lc, _ = p.shape
    tm = _largest_tile(lc, 512, 128)
    tab = pl.BlockSpec((tm, HEAD_DIM), lambda b, i: (i, 0))
    return pl.pallas_call(
        _qkprep_kernel,
        grid=(nb, lc // tm),
        in_specs=[pl.BlockSpec((1, tm, ATT_W), lambda b, i: (b, i, COL_AQ // ATT_W)),
                  pl.BlockSpec((1, tm, KV_W), lambda b, i: (b, i, COL_K // KV_W)),
                  pl.BlockSpec((1, tm, KV_W), lambda b, i: (b, i, COL_V // KV_W)),
                  tab, tab, tab,
                  pl.BlockSpec(qk_norm_g.shape, lambda b, i: (0, 0))],
        out_specs=[pl.BlockSpec((1, ATT_W, tm), lambda b, i: (b, 0, i)),
                   pl.BlockSpec((1, tm, KV_W), lambda b, i: (b, i, 0)),
                   pl.BlockSpec((1, KV_W, tm), lambda b, i: (b, 0, i))],
        out_shape=[jax.ShapeDtypeStruct((nb, ATT_W, lc), BF16),
                   jax.ShapeDtypeStruct((nb, lc, KV_W), BF16),
                   jax.ShapeDtypeStruct((nb, KV_W, lc), BF16)],
        name="qk_prep",
    )(p, p, p, cos, sin_a, sin_b, qk_norm_g)


def _attn_kernel(qt_ref, k_ref, vt_ref, o_ref, acc_ref, s_ref, *, tq, kc, n_kc):
    acc_ref[...] = jnp.zeros(acc_ref.shape, F32)
    qts = [qt_ref[0, g * HEAD_DIM:(g + 1) * HEAD_DIM, :] for g in range(ATT_GROUPS)]

    def put_scores(c, slot):
        kch = k_ref[0, pl.ds(pl.multiple_of(c * kc, kc), kc), :]
        for g in range(ATT_GROUPS):
            s_ref[slot, g] = _dot(kch, qts[g])

    def softmax_pv(c, slot, carry):
        vch = vt_ref[0, :, pl.ds(pl.multiple_of(c * kc, kc), kc)]
        out = []
        for g in range(ATT_GROUPS):
            m_old, l_old = carry[g]
            s = s_ref[slot, g]
            m_new = jnp.maximum(m_old, jnp.max(s, axis=0, keepdims=True))
            alpha = jnp.exp2(m_old - m_new)
            p = jnp.exp2(s - m_new)
            l_new = l_old * alpha + jnp.sum(p, axis=0, keepdims=True)
            acc_ref[g] = acc_ref[g] * alpha + _dot(vch, p.astype(BF16))
            out.append((m_new, l_new))
        return tuple(out)

    def body(i, carry):
        c = 2 * i
        put_scores(c + 1, 1)
        carry = softmax_pv(c, 0, carry)
        put_scores(c + 2, 0)
        return softmax_pv(c + 1, 1, carry)

    init = tuple((jnp.full((1, tq), -jnp.inf, F32), jnp.zeros((1, tq), F32)) for _ in range(ATT_GROUPS))
    put_scores(0, 0)
    fin = lax.fori_loop(0, (n_kc - 1) // 2, body, init)
    if n_kc % 2 == 0:
        put_scores(n_kc - 1, 1)
        fin = softmax_pv(n_kc - 2, 0, fin)
    fin = softmax_pv(n_kc - 1, (n_kc - 1) % 2, fin)
    for g in range(ATT_GROUPS):
        o_ref[0, :, g * HEAD_DIM:(g + 1) * HEAD_DIM] = (acc_ref[g] / fin[g][1]).T.astype(BF16)


def _attention(qt, kn, vt, n_ctx):
    nb, lc, _ = kn.shape
    n_lat = lc - n_ctx
    tq = _largest_tile(n_lat, 256, 128)
    assert n_ctx % tq == 0
    kc = _largest_tile(lc, 768, 256)
    gw = ATT_GROUPS * HEAD_DIM
    kern = functools.partial(_attn_kernel, tq=tq, kc=kc, n_kc=lc // kc)
    vmem = 8 * lc * HEAD_DIM * 2 + 4 * ATT_GROUPS * kc * tq * 4 + 8 * gw * tq * 4
    return pl.pallas_call(
        kern,
        grid=(nb, ATT_KV_HEADS, n_lat // tq),
        in_specs=[pl.BlockSpec((1, gw, tq), lambda b, h, i: (b, h, n_ctx // tq + i)),
                  pl.BlockSpec((1, lc, HEAD_DIM), lambda b, h, i: (b, 0, h)),
                  pl.BlockSpec((1, HEAD_DIM, lc), lambda b, h, i: (b, h, 0))],
        out_specs=pl.BlockSpec((1, tq, gw), lambda b, h, i: (b, i, h)),
        out_shape=jax.ShapeDtypeStruct((nb, n_lat, ATT_W), BF16),
        scratch_shapes=[pltpu.VMEM((ATT_GROUPS, HEAD_DIM, tq), F32),
                        pltpu.VMEM((2, ATT_GROUPS, kc, tq), F32)],
        compiler_params=pltpu.CompilerParams(
            dimension_semantics=("arbitrary", "arbitrary", "arbitrary"),
            vmem_limit_bytes=_vmem_limit(vmem)),
        name="attention",
    )(qt, kn, vt)


def _hgrn_chunk(q_ref, v_ref, r_ref, lb, reverse, rows):
    c = HGRN_CHUNK
    q = q_ref[0, rows, :].astype(F32)
    v = v_ref[0, rows, :]
    r = r_ref[0, rows, :].astype(F32)
    sig = jax.nn.sigmoid(r)
    f = lb + (1.0 - lb) * sig
    logf = jnp.log(f)
    k = (1.0 - lb) * (1.0 - sig)
    ti = lax.broadcasted_iota(I32, (c, c), 0)
    si = lax.broadcasted_iota(I32, (c, c), 1)
    tri = jnp.where((si >= ti) if reverse else (si <= ti), 1.0, 0.0).astype(BF16)
    hi = logf.astype(BF16)
    lo = (logf - hi.astype(F32)).astype(BF16)
    bcum = _dot(tri, hi) + _dot(tri, lo)
    b_end = bcum[0:1, :] if reverse else bcum[c - 1:c, :]
    qt = (q * jnp.exp(bcum)).astype(BF16)
    kt = (k * jnp.exp(b_end - bcum)).astype(BF16)
    dec = jnp.exp(b_end)

    nsub = c // HGRN_SUB
    qp, kp, cols = [], [], []
    for blk in range(nsub):
        rs = slice(blk * HGRN_SUB, (blk + 1) * HGRN_SUB)
        cs = slice(blk * HGRN_SUB, c) if reverse else slice(0, (blk + 1) * HGRN_SUB)
        mid = blk * HGRN_SUB + HGRN_SUB // 2
        ref = bcum[mid:mid + 1, :]
        qp.append((q[rs] * jnp.exp(bcum[rs] - ref)).astype(BF16))
        kp.append((k[cs] * jnp.exp(ref - bcum[cs])).astype(BF16))
        cols.append(cs)

    keep = []
    for blk in range(nsub):
        n_cols = cols[blk].stop - cols[blk].start
        trow = blk * HGRN_SUB + lax.broadcasted_iota(I32, (HGRN_SUB, n_cols), 0)
        scol = cols[blk].start + lax.broadcasted_iota(I32, (HGRN_SUB, n_cols), 1)
        keep.append((scol >= trow) if reverse else (scol <= trow))
    return dict(v=v, qt=qt, kt=kt, dec=dec, qp=qp, kp=kp, cols=cols, keep=keep, rows=rows)


def _hgrn_emit(dirs, s_refs, o_refs):
    nsub = HGRN_CHUNK // HGRN_SUB
    heads = [slice(h * HGRN_DK, (h + 1) * HGRN_DK) for h in range(HGRN_HEADS)]
    inter = [[None] * len(chunks) for chunks in dirs]
    for k in range(len(dirs[0])):
        for di, (chunks, s_ref) in enumerate(zip(dirs, s_refs)):
            d = chunks[k]
            row = []
            for h, hs in enumerate(heads):
                st = s_ref[h]
                row.append(_dot_nt(d["qt"][:, hs], st.astype(BF16)))
                s_ref[h] = st * d["dec"][:, hs] + _dot_tn(d["v"][:, hs], d["kt"][:, hs])
            inter[di][k] = row
    scores = [[[[_dot_nt(d["qp"][blk][:, hs], d["kp"][blk][:, hs]) for blk in range(nsub)] for hs in heads]
               for d in chunks] for chunks in dirs]
    for chunks, sc_d, it_d, o_ref in zip(dirs, scores, inter, o_refs):
        for d, sc, it in zip(chunks, sc_d, it_d):
            outs = []
            for h, hs in enumerate(heads):
                parts = [_dot(jnp.where(d["keep"][blk], sc[h][blk], 0.0).astype(BF16), d["v"][d["cols"][blk], hs])
                         for blk in range(nsub)]
                outs.append(it[h] + jnp.concatenate(parts, axis=0))
            o_ref[0, d["rows"], :] = jnp.concatenate(outs, axis=1).astype(BF16)


def _hgrn_kernel(qf_ref, vf_ref, rf_ref, qb_ref, vb_ref, rb_ref, lb_ref, of_ref, ob_ref, sf_ref, sb_ref):
    @pl.when(pl.program_id(1) == 0)
    def _():
        sf_ref[...] = jnp.zeros(sf_ref.shape, F32)
        sb_ref[...] = jnp.zeros(sb_ref.shape, F32)

    n_layers = lb_ref.shape[0] // 2

    def lower_bound(direction):
        rows = [lb_ref[direction * n_layers + l:direction * n_layers + l + 1, :] for l in range(n_layers)]
        amax = functools.reduce(jnp.maximum, rows)
        e = [jnp.exp(a - amax) for a in rows]
        return e[0] / functools.reduce(lambda u, w: u + w, e)

    rows = [slice(k * HGRN_CHUNK, (k + 1) * HGRN_CHUNK) for k in range(HGRN_STEP_CHUNKS)]
    fwd = [_hgrn_chunk(qf_ref, vf_ref, rf_ref, lower_bound(0), False, rs) for rs in rows]
    bwd = [_hgrn_chunk(qb_ref, vb_ref, rb_ref, lower_bound(1), True, rs) for rs in reversed(rows)]
    _hgrn_emit([fwd, bwd], [sf_ref, sb_ref], [of_ref, ob_ref])


def _hgrn_scan(p, hgrn_lb, n_ctx):
    nb, lc, _ = p.shape
    c = HGRN_CHUNK * HGRN_STEP_CHUNKS
    assert n_ctx % c == 0 and lc % c == 0
    n_chunks = lc // c
    ctx_chunks = n_ctx // c

    def fwd(col):
        return lambda b, s: (b, s, col)

    def bwd_chunk(s):
        return jnp.where(s < ctx_chunks, ctx_chunks - 1 - s, n_chunks - 1 + ctx_chunks - s)

    def bwd(col):
        return lambda b, s: (b, bwd_chunk(s), col)

    blk = (1, c, HGRN_W)
    lb2 = hgrn_lb.reshape(-1, HGRN_W)
    return pl.pallas_call(
        _hgrn_kernel,
        grid=(nb, n_chunks),
        in_specs=[pl.BlockSpec(blk, fwd(COL_HQ // HGRN_W)), pl.BlockSpec(blk, fwd(COL_HI // HGRN_W)),
                  pl.BlockSpec(blk, fwd(COL_FF // HGRN_W)),
                  pl.BlockSpec(blk, bwd(COL_HQ // HGRN_W)), pl.BlockSpec(blk, bwd(COL_HI // HGRN_W)),
                  pl.BlockSpec(blk, bwd(COL_FB // HGRN_W)),
                  pl.BlockSpec(lb2.shape, lambda b, s: (0, 0))],
        out_specs=[pl.BlockSpec(blk, fwd(0)), pl.BlockSpec(blk, bwd(0))],
        out_shape=[jax.ShapeDtypeStruct((nb, lc, HGRN_W), BF16)] * 2,
        scratch_shapes=[pltpu.VMEM((HGRN_HEADS, HGRN_DK, HGRN_DK), F32)] * 2,
        compiler_params=pltpu.CompilerParams(dimension_semantics=("arbitrary", "arbitrary")),
        name="hgrn_scan",
    )(p, p, p, p, p, p, lb2)


def _merge_kernel(of_ref, ob_ref, hg_ref, gh_ref, ga_ref, oa_ref, x_ref, mod_ref, ng_ref, hng_ref,
                  wb0_ref, wb1_ref, wo_ref, rw_ref, rb_ref,
                  x1_ref, t_ref, ids_ref, wts_ref):
    b = pl.program_id(0)
    m = mod_ref[pl.ds(b, 1), :]
    gate_mix = m[:, 2 * D_MODEL:3 * D_MODEL]
    shift_f = m[:, 3 * D_MODEL:4 * D_MODEL]
    scale_f = m[:, 4 * D_MODEL:5 * D_MODEL]
    hng = hng_ref[...]
    rw = rw_ref[...]
    rw_hi = rw.astype(BF16)
    rw_lo = (rw - rw_hi.astype(F32)).astype(BF16)
    groups = [slice(r, r + MERGE_ROWS) for r in range(0, of_ref.shape[1], MERGE_ROWS)]

    branch = []
    for rs in groups:
        o = of_ref[0, rs, :].astype(F32) + ob_ref[0, rs, :].astype(F32)
        o_h = jnp.concatenate(
            [_rms(o[:, h * HGRN_DK:(h + 1) * HGRN_DK], hng) for h in range(HGRN_HEADS)], axis=1)
        g_raw = hg_ref[0, rs, :].astype(F32)
        o_h = o_h * (g_raw * jax.nn.sigmoid(g_raw))
        branch.append((_dot(o_h.astype(BF16), wb0_ref[...]), _dot(oa_ref[0, rs, :], wb1_ref[...])))
    mixed = []
    for rs, (y_h, y_a) in zip(groups, branch):
        y = (jax.nn.sigmoid(gh_ref[0, rs, :].astype(F32)) * y_h
             + jax.nn.sigmoid(ga_ref[0, rs, :].astype(F32)) * y_a)
        mixed.append(_dot(y.astype(BF16), wo_ref[...]))
    logits_all = []
    for rs, y in zip(groups, mixed):
        x1 = x_ref[0, rs, :] + gate_mix * _rms(y, ng_ref[1:2, :])
        x1_ref[0, rs, :] = x1
        t = _rms(x1, ng_ref[2:3, :]) * (1.0 + scale_f) + shift_f
        t_ref[rs, :] = t.astype(BF16)
        t_hi = t.astype(BF16)
        t_lo = (t - t_hi.astype(F32)).astype(BF16)
        logits_all.append(_dot_nt(rw_hi, t_hi) + _dot_nt(rw_hi, t_lo) + _dot_nt(rw_lo, t_hi) + rb_ref[...])
    for rs, logits in zip(groups, logits_all):
        eidx = lax.broadcasted_iota(I32, logits.shape, 0).astype(F32)
        vals = []
        for j in range(TOP_K):
            mx = jnp.max(logits, axis=0, keepdims=True)
            idx = jnp.min(jnp.where(logits == mx, eidx, float(N_EXPERTS)), axis=0, keepdims=True)
            ids_ref[j:j + 1, rs] = idx.astype(I32)
            vals.append(mx)
            logits = jnp.where(eidx == idx, -jnp.inf, logits)
        ex = [jnp.exp(vj - vals[0]) for vj in vals]
        den = ex[0] + ex[1] + ex[2] + ex[3]
        for j in range(TOP_K):
            wts_ref[j:j + 1, rs] = ex[j] / den


def _merge(o_f, o_b, p, o_a, x, mod, norm_g, hgrn_norm_g, wb0, wb1, wo, router_wt, router_b, n_ctx):
    nb, n_lat, d = x.shape
    tm = _largest_tile(n_lat, 256, 128)
    assert n_ctx % tm == 0
    off = n_ctx // tm
    nt = n_lat // tm
    row = (1, tm, d)

    def pcol(col):
        return pl.BlockSpec(row, lambda b, i: (b, off + i, col // d))

    def full(a):
        return pl.BlockSpec(a.shape, lambda b, i: (0,) * a.ndim)

    tok = nb * n_lat
    return pl.pallas_call(
        _merge_kernel,
        grid=(nb, nt),
        in_specs=[pl.BlockSpec(row, lambda b, i: (b, off + i, 0)),
                  pl.BlockSpec(row, lambda b, i: (b, off + i, 0)),
                  pcol(COL_HG), pcol(COL_GH), pcol(COL_GA),
                  pl.BlockSpec(row, lambda b, i: (b, i, 0)),
                  pl.BlockSpec(row, lambda b, i: (b, i, 0)),
                  full(mod), full(norm_g), full(hgrn_norm_g), full(wb0), full(wb1), full(wo),
                  full(router_wt), full(router_b)],
        out_specs=[pl.BlockSpec(row, lambda b, i: (b, i, 0)),
                   pl.BlockSpec((tm, d), lambda b, i: (b * nt + i, 0)),
                   pl.BlockSpec((TOP_K, tm), lambda b, i: (0, b * nt + i)),
                   pl.BlockSpec((TOP_K, tm), lambda b, i: (0, b * nt + i))],
        out_shape=[jax.ShapeDtypeStruct((nb, n_lat, d), F32),
                   jax.ShapeDtypeStruct((tok, d), BF16),
                   jax.ShapeDtypeStruct((TOP_K, tok), I32),
                   jax.ShapeDtypeStruct((TOP_K, tok), F32)],
        compiler_params=pltpu.CompilerParams(
            dimension_semantics=("arbitrary", "arbitrary"),
            vmem_limit_bytes=_vmem_limit(40 * 1024 * 1024)),
        name="merge_router",
    )(o_f, o_b, p, p, p, o_a, x, mod, norm_g, hgrn_norm_g, wb0, wb1, wo, router_wt, router_b)


def _rank_kernel(ids_ref, rank_ref, cnt_ref, *, tt):
    eidx = lax.broadcasted_iota(I32, (N_EXPERTS, tt), 0)
    si = lax.broadcasted_iota(I32, (tt, tt), 0)
    ti = lax.broadcasted_iota(I32, (tt, tt), 1)
    before = jnp.where(si < ti, 1.0, 0.0).astype(BF16)
    seen = jnp.zeros((N_EXPERTS, 1), F32)
    for j in range(TOP_K):
        onehot = eidx == ids_ref[j:j + 1, :]
        oh = jnp.where(onehot, 1.0, 0.0)
        earlier = _dot(oh.astype(BF16), before)
        rank = jnp.sum(jnp.where(onehot, seen + earlier, 0.0), axis=0, keepdims=True)
        rank_ref[j:j + 1, :] = rank.astype(I32)
        seen = seen + jnp.sum(oh, axis=1, keepdims=True)
    cnt_ref[0] = seen.astype(I32)


def _expert_rank(ids, tt):
    _, tok = ids.shape
    return pl.pallas_call(
        functools.partial(_rank_kernel, tt=tt),
        grid=(tok // tt,),
        in_specs=[pl.BlockSpec((TOP_K, tt), lambda i: (0, i))],
        out_specs=[pl.BlockSpec((TOP_K, tt), lambda i: (0, i)),
                   pl.BlockSpec((1, N_EXPERTS, 1), lambda i: (i, 0, 0))],
        out_shape=[jax.ShapeDtypeStruct((TOP_K, tok), I32),
                   jax.ShapeDtypeStruct((tok // tt, N_EXPERTS, 1), I32)],
        compiler_params=pltpu.CompilerParams(dimension_semantics=("arbitrary",)),
        name="expert_rank",
    )(ids)


def _start_blocks(n_blocks, make_copy):
    def start(b, carry):
        make_copy(b).start()
        return carry

    lax.fori_loop(0, n_blocks, start, 0)


def _wait_blocks(n_blocks, make_copy, make_group_copy):
    def wait_group(g, carry):
        make_group_copy().wait()
        return carry

    def wait_one(b, carry):
        make_copy(0).wait()
        return carry

    lax.fori_loop(0, lax.shift_right_logical(n_blocks, WAIT_GROUP_LOG2), wait_group, 0)
    lax.fori_loop(0, jnp.bitwise_and(n_blocks, (1 << WAIT_GROUP_LOG2) - 1), wait_one, 0)


def _scatter_kernel(zt_ref, zvalid_ref, nblk_ref, bdst_ref, lp_ref, t_ref, xs_ref, stage, zero_buf, sem, zsem,
                    *, tt, n_stage):
    @pl.when(pl.program_id(0) == 0)
    def _():
        zero_buf[...] = jnp.zeros(zero_buf.shape, F32)

        def zero_copy(k):
            return pltpu.make_async_copy(zero_buf, xs_ref.at[pl.ds(zt_ref[k] * MOE_TILE, MOE_TILE)], zsem)

        def start(k, carry):
            @pl.when(zvalid_ref[k] == 1)
            def _():
                zero_copy(k).start()
            return carry

        def wait(k, carry):
            @pl.when(zvalid_ref[k] == 1)
            def _():
                zero_copy(k).wait()
            return carry

        lax.fori_loop(0, zt_ref.shape[0], start, 0)
        lax.fori_loop(0, zt_ref.shape[0], wait, 0)

    i = pl.program_id(0)
    slot = i % 2
    t_bf = t_ref[...]
    lps = [lp_ref[j:j + 1, :] for j in range(TOP_K)]
    for rc in range(n_stage // PERM_ROWS):
        row = rc * PERM_ROWS + lax.broadcasted_iota(I32, (PERM_ROWS, tt), 0)
        sel = jnp.where(row == lps[TOP_K - 1], 1.0, 0.0)
        for j in range(TOP_K - 2, -1, -1):
            sel = jnp.where(row == lps[j], 1.0, sel)
        stage[slot, rc * PERM_ROWS:(rc + 1) * PERM_ROWS, :] = _dot(sel.astype(BF16), t_bf)

    def copy(b):
        src = pl.multiple_of(b * RUN_ALIGN, RUN_ALIGN)
        dst = pl.multiple_of(bdst_ref[0, 0, b], RUN_ALIGN)
        return pltpu.make_async_copy(stage.at[slot, pl.ds(src, RUN_ALIGN)], xs_ref.at[pl.ds(dst, RUN_ALIGN)],
                                     sem.at[slot])

    def waiter(s):
        rows = RUN_ALIGN << WAIT_GROUP_LOG2
        return (lambda b: pltpu.make_async_copy(stage.at[s, pl.ds(0, RUN_ALIGN)],
                                                xs_ref.at[pl.ds(0, RUN_ALIGN)], sem.at[s]),
                lambda: pltpu.make_async_copy(stage.at[s, pl.ds(0, rows)], xs_ref.at[pl.ds(0, rows)], sem.at[s]))

    _start_blocks(nblk_ref[i], copy)

    @pl.when(i > 0)
    def _():
        _wait_blocks(nblk_ref[i - 1], *waiter(1 - slot))

    @pl.when(i == pl.num_programs(0) - 1)
    def _():
        _wait_blocks(nblk_ref[i], *waiter(slot))


def _scatter_rows(zero_tiles, zero_valid, n_blocks, block_rows, lp, t, n_rows, tt, n_stage):
    tok, d = t.shape
    nb = block_rows.shape[2]
    return pl.pallas_call(
        functools.partial(_scatter_kernel, tt=tt, n_stage=n_stage),
        grid_spec=pltpu.PrefetchScalarGridSpec(
            num_scalar_prefetch=3,
            grid=(tok // tt,),
            in_specs=[pl.BlockSpec((1, 1, nb), lambda i, *_: (i, 0, 0), memory_space=pltpu.SMEM),
                      pl.BlockSpec((TOP_K, tt), lambda i, *_: (0, i)),
                      pl.BlockSpec((tt, d), lambda i, *_: (i, 0))],
            out_specs=pl.BlockSpec(memory_space=pl.ANY),
            scratch_shapes=[pltpu.VMEM((2, n_stage, d), F32), pltpu.VMEM((MOE_TILE, d), F32),
                            pltpu.SemaphoreType.DMA((2,)), pltpu.SemaphoreType.DMA(())]),
        out_shape=jax.ShapeDtypeStruct((n_rows, d), F32),
        compiler_params=pltpu.CompilerParams(
            dimension_semantics=("arbitrary",), vmem_limit_bytes=_vmem_limit(48 * 1024 * 1024)),
        name="moe_scatter",
    )(zero_tiles, zero_valid, n_blocks, block_rows, lp, t)


def _expert_kernel(te_ref, first_ref, nused_ref, slot_ref, next_ref, xs_ref, wu_hbm, bu_ref, wd_hbm, bd_ref,
                   ys_ref, wu32, wd32, wu_bf, wd_bf, sem_u, sem_d):
    i = pl.program_id(0)

    def weight_copies(e, s):
        return (pltpu.make_async_copy(wu_hbm.at[e], wu32.at[s], sem_u.at[s]),
                pltpu.make_async_copy(wd_hbm.at[e], wd32.at[s], sem_d.at[s]))

    @pl.when(i >= nused_ref[0])
    def _():
        ys_ref[...] = jnp.zeros(ys_ref.shape, F32)

    @pl.when(i < nused_ref[0])
    def _():
        @pl.when(first_ref[i] == 1)
        def _():
            s = slot_ref[i]

            @pl.when(i == 0)
            def _():
                for cp in weight_copies(te_ref[0], 0):
                    cp.start()

            for cp in weight_copies(te_ref[i], s):
                cp.wait()
            wu_bf[...] = wu32[s].astype(BF16)
            wd_bf[...] = wd32[s].astype(BF16)

            @pl.when(next_ref[i] >= 0)
            def _():
                for cp in weight_copies(next_ref[i], 1 - s):
                    cp.start()

        h = _dot(xs_ref[...].astype(BF16), wu_bf[...]) + bu_ref[0]
        glu = jnp.minimum(h[:, :D_EXPERT], SWIGLU_LIMIT)
        lin = jnp.clip(h[:, D_EXPERT:], -SWIGLU_LIMIT, SWIGLU_LIMIT)
        a = glu * jax.nn.sigmoid(SWIGLU_ALPHA * glu) * (lin + 1.0)
        ys_ref[...] = _dot(a.astype(BF16), wd_bf[...]) + bd_ref[0]


def _experts(xs, tile_expert, tile_first, n_used, tile_slot, next_expert, w_up, b_up, w_down, b_down):
    n_rows, d = xs.shape
    tm = MOE_TILE
    n_tiles = n_rows // tm

    def tile(i, te, first, nused, *_):
        return (jnp.minimum(i, nused[0] - 1), 0)

    def bsel(i, te, *_):
        return (te[i], 0, 0)

    vmem = 2 * (d * 2 * D_EXPERT * 4 + D_EXPERT * d * 4) + d * 2 * D_EXPERT * 2 + D_EXPERT * d * 2 \
        + 4 * tm * d * 4 + 6 * tm * 2 * D_EXPERT * 4
    return pl.pallas_call(
        _expert_kernel,
        grid_spec=pltpu.PrefetchScalarGridSpec(
            num_scalar_prefetch=5,
            grid=(n_tiles,),
            in_specs=[pl.BlockSpec((tm, d), tile),
                      pl.BlockSpec(memory_space=pl.ANY),
                      pl.BlockSpec((1, 1, 2 * D_EXPERT), bsel),
                      pl.BlockSpec(memory_space=pl.ANY),
                      pl.BlockSpec((1, 1, d), bsel)],
            out_specs=pl.BlockSpec((tm, d), lambda i, *_: (i, 0)),
            scratch_shapes=[pltpu.VMEM((2, d, 2 * D_EXPERT), F32), pltpu.VMEM((2, D_EXPERT, d), F32),
                            pltpu.VMEM((d, 2 * D_EXPERT), BF16), pltpu.VMEM((D_EXPERT, d), BF16),
                            pltpu.SemaphoreType.DMA((2,)), pltpu.SemaphoreType.DMA((2,))]),
        out_shape=jax.ShapeDtypeStruct((n_rows, d), F32),
        compiler_params=pltpu.CompilerParams(
            dimension_semantics=("arbitrary",), vmem_limit_bytes=_vmem_limit(vmem)),
        name="moe_experts",
    )(tile_expert, tile_first, n_used, tile_slot, next_expert, xs, w_up, b_up.reshape(N_EXPERTS, 1, -1), w_down,
      b_down.reshape(N_EXPERTS, 1, -1))


def _combine_kernel(nblk_ref, bsrc_ref, ys_ref, lp_ref, w_ref, x1_ref, mod_ref, ng_ref, o_ref, stage, sem,
                    *, tt, n_stage, tiles_per_batch):
    i = pl.program_id(0)
    slot = i % 2

    def fetch(tile, s):
        def copy(b):
            src = pl.multiple_of(bsrc_ref[tile, b], RUN_ALIGN)
            dst = pl.multiple_of(b * RUN_ALIGN, RUN_ALIGN)
            return pltpu.make_async_copy(ys_ref.at[pl.ds(src, RUN_ALIGN)], stage.at[s, pl.ds(dst, RUN_ALIGN)],
                                         sem.at[s])
        return copy

    def group_copy():
        rows = RUN_ALIGN << WAIT_GROUP_LOG2
        return pltpu.make_async_copy(ys_ref.at[pl.ds(0, rows)], stage.at[slot, pl.ds(0, rows)], sem.at[slot])

    @pl.when(i == 0)
    def _():
        stage[...] = jnp.zeros(stage.shape, F32)
        _start_blocks(nblk_ref[0], fetch(0, 0))

    @pl.when(i + 1 < pl.num_programs(0))
    def _():
        _start_blocks(nblk_ref[i + 1], fetch(i + 1, 1 - slot))

    _wait_blocks(nblk_ref[i], fetch(i, slot), group_copy)

    lp = [jnp.broadcast_to(lp_ref[:, j:j + 1], (tt, PERM_ROWS)) for j in range(TOP_K)]
    w = [jnp.broadcast_to(w_ref[:, j:j + 1], (tt, PERM_ROWS)) for j in range(TOP_K)]
    f = jnp.zeros((tt, D_MODEL), F32)
    for lc in range(n_stage // PERM_ROWS):
        col = lc * PERM_ROWS + lax.broadcasted_iota(I32, (tt, PERM_ROWS), 1)
        wp = jnp.where(col == lp[TOP_K - 1], w[TOP_K - 1], 0.0)
        for j in range(TOP_K - 2, -1, -1):
            wp = jnp.where(col == lp[j], w[j], wp)
        f = f + _dot(wp.astype(BF16), stage[slot, lc * PERM_ROWS:(lc + 1) * PERM_ROWS, :].astype(BF16))
    b = i // tiles_per_batch
    gate_f = mod_ref[pl.ds(b, 1), 5 * D_MODEL:6 * D_MODEL]
    o_ref[...] = x1_ref[...] + gate_f * _rms(f, ng_ref[3:4, :])


def _combine(n_blocks, block_rows, ys, lp_t, wts_t, x1, mod, norm_g, tt, n_stage, tiles_per_batch):
    tok, d = x1.shape
    kern = functools.partial(_combine_kernel, tt=tt, n_stage=n_stage, tiles_per_batch=tiles_per_batch)
    return pl.pallas_call(
        kern,
        grid_spec=pltpu.PrefetchScalarGridSpec(
            num_scalar_prefetch=2,
            grid=(tok // tt,),
            in_specs=[pl.BlockSpec(memory_space=pl.ANY),
                      pl.BlockSpec((tt, TOP_K), lambda i, *_: (i, 0)),
                      pl.BlockSpec((tt, TOP_K), lambda i, *_: (i, 0)),
                      pl.BlockSpec((tt, d), lambda i, *_: (i, 0)),
                      pl.BlockSpec(mod.shape, lambda i, *_: (0, 0)),
                      pl.BlockSpec(norm_g.shape, lambda i, *_: (0, 0))],
            out_specs=pl.BlockSpec((tt, d), lambda i, *_: (i, 0)),
            scratch_shapes=[pltpu.VMEM((2, n_stage, d), F32), pltpu.SemaphoreType.DMA((2,))]),
        out_shape=jax.ShapeDtypeStruct((tok, d), F32),
        compiler_params=pltpu.CompilerParams(
            dimension_semantics=("arbitrary",), vmem_limit_bytes=_vmem_limit(48 * 1024 * 1024)),
        name="moe_combine",
    )(n_blocks, block_rows[:, 0, :], ys, lp_t, wts_t, x1, mod, norm_g)


def _rope_tables(n_ctx, n_lat):
    inv = ROPE_THETA ** (-np.arange(0, ROPE_AXIS_DIM, 2, dtype=np.float32) / ROPE_AXIS_DIM)
    idx = jnp.arange(n_lat, dtype=I32)
    rows = (idx // GRID_W).astype(F32)
    cols = (idx % GRID_W).astype(F32)
    half = ROPE_AXIS_DIM // 2
    inv = jnp.asarray(inv, F32)
    ang_r = rows[:, None] * inv[None, :]
    ang_c = cols[:, None] * inv[None, :]
    ang = jnp.concatenate([ang_r, ang_r, ang_c, ang_c], axis=1)
    cos, sin = jnp.cos(ang), jnp.sin(ang)
    lane = np.arange(HEAD_DIM)
    first = jnp.asarray((lane % ROPE_AXIS_DIM) < half)
    sin_a = jnp.where(first, -sin, 0.0)
    sin_b = jnp.where(first, 0.0, sin)
    pad1 = jnp.ones((n_ctx, HEAD_DIM), F32)
    pad0 = jnp.zeros((n_ctx, HEAD_DIM), F32)
    return (jnp.concatenate([pad1, cos], axis=0), jnp.concatenate([pad0, sin_a], axis=0),
            jnp.concatenate([pad0, sin_b], axis=0))


def kernel(x, c, ctx, c_ctx, w_mod, b_mod, norm_g, w_in, hgrn_lb, hgrn_norm_g, qk_norm_g, w_branch, w_out,
           router_w, router_b, w_up, b_up, w_down, b_down):
    nb, n_lat, d = x.shape
    n_ctx = ctx.shape[1]
    assert d == D_MODEL and w_mod.shape[0] == 1, "single-layer kernel"
    tok = nb * n_lat

    cv = jnp.zeros((8, d), F32).at[:nb].set(c).at[nb].set(c_ctx)
    mod = _modulation(cv, w_mod[0], b_mod[0])

    xc = jnp.concatenate([ctx, x], axis=1)
    sizes = np.cumsum([0, HGRN_W, HGRN_W, HGRN_W, HGRN_W, HGRN_W, ATT_W, KV_W, KV_W, D_MODEL, D_MODEL])
    order = [0, 1, 2, 3, 4, 5, 8, 9, 6, 7]
    w_in_bf = jnp.concatenate([w_in[0][:, sizes[i]:sizes[i + 1]] for i in order], axis=1).astype(BF16)
    p = _in_projection(xc, mod, norm_g[0], w_in_bf, n_ctx)

    cos, sin_a, sin_b = _rope_tables(n_ctx, n_lat)
    qt, kn, vt = _qk_prep(p, cos, sin_a, sin_b, qk_norm_g[0])
    o_a = _attention(qt, kn, vt, n_ctx)
    o_f, o_b = _hgrn_scan(p, hgrn_lb, n_ctx)

    x1, t, ids, wts = _merge(
        o_f, o_b, p, o_a, x, mod, norm_g[0], hgrn_norm_g, w_branch[0, 0].astype(BF16),
        w_branch[0, 1].astype(BF16), w_out[0].astype(BF16), router_w[0].T, router_b[0].reshape(N_EXPERTS, 1),
        n_ctx)

    tt = _largest_tile(n_lat, DISPATCH_TILE, 128)
    n_tt = tok // tt
    rank, counts = _expert_rank(ids, tt)
    counts = counts[:, :, 0]
    run = (counts + RUN_ALIGN - 1) // RUN_ALIGN * RUN_ALIGN
    run_end = jnp.cumsum(run, axis=1)
    stage_off = run_end - run
    expert_rows = jnp.sum(run, axis=0)
    tiles_e = (expert_rows + MOE_TILE - 1) // MOE_TILE
    tile_end = jnp.cumsum(tiles_e)
    starts = (tile_end - tiles_e) * MOE_TILE
    run_dst = starts[None, :] + jnp.cumsum(run, axis=0) - run
    n_tiles = -(-(tok * TOP_K + N_EXPERTS * n_tt * (RUN_ALIGN - 1)) // MOE_TILE) + N_EXPERTS
    tile_idx = jnp.arange(n_tiles, dtype=I32)
    tile_expert = jnp.sum((tile_end[None, :] <= tile_idx[:, None]).astype(I32), axis=1)
    tile_expert = jnp.minimum(tile_expert, N_EXPERTS - 1)
    n_used = tile_end[-1:].astype(I32)
    tile_expert = jnp.where(tile_idx < n_used[0], tile_expert, tile_expert[jnp.maximum(n_used[0] - 1, 0)])
    tile_first = jnp.concatenate([jnp.ones((1,), I32), (tile_expert[1:] != tile_expert[:-1]).astype(I32)])
    e_idx = jnp.arange(N_EXPERTS, dtype=I32)
    used = tiles_e > 0
    e_slot = (jnp.cumsum(used.astype(I32)) - 1) % 2
    later = used[None, :] & (e_idx[None, :] > e_idx[:, None])
    e_next = jnp.min(jnp.where(later, e_idx[None, :], N_EXPERTS), axis=1)
    e_next = jnp.where(e_next == N_EXPERTS, -1, e_next)
    tile_onehot = tile_expert[:, None] == e_idx[None, :]
    tile_slot = jnp.sum(jnp.where(tile_onehot, e_slot[None, :], 0), axis=1).astype(I32)
    next_expert = jnp.sum(jnp.where(tile_onehot, e_next[None, :], 0), axis=1).astype(I32)

    n_stage = -(-(TOP_K * tt + N_EXPERTS * (RUN_ALIGN - 1)) // PERM_ROWS) * PERM_ROWS
    ids_t = ids.reshape(TOP_K, n_tt, tt)
    onehot = ids_t[..., None] == jnp.arange(N_EXPERTS, dtype=I32)
    lp = jnp.sum(jnp.where(onehot, stage_off[None, :, None, :], 0), axis=-1).reshape(TOP_K, tok) + rank
    blk_row = jnp.arange(n_stage // RUN_ALIGN, dtype=I32) * RUN_ALIGN
    blk_expert = jnp.minimum(jnp.sum((run_end[:, None, :] <= blk_row[None, :, None]).astype(I32), axis=-1),
                             N_EXPERTS - 1)
    blk_onehot = blk_expert[..., None] == jnp.arange(N_EXPERTS, dtype=I32)
    blk_dst = jnp.sum(jnp.where(blk_onehot, (run_dst - stage_off)[:, None, :], 0), axis=-1) + blk_row[None, :]
    n_blocks = (run_end[:, -1] // RUN_ALIGN).astype(I32)
    blk_dst = jnp.where(blk_row[None, :] < run_end[:, -1:], blk_dst, 0).astype(I32)[:, None, :]

    n_tail = n_tiles - (tok * TOP_K) // MOE_TILE
    tail = n_used[0] + jnp.arange(n_tail, dtype=I32)
    zero_tiles = jnp.clip(jnp.concatenate([tile_end.astype(I32) - 1, tail]), 0, n_tiles - 1)
    zero_valid = jnp.concatenate([expert_rows % MOE_TILE != 0, tail < n_tiles]).astype(I32)

    xs = _scatter_rows(zero_tiles, zero_valid, n_blocks, blk_dst, lp, t, n_tiles * MOE_TILE, tt, n_stage)
    ys = _experts(xs, tile_expert, tile_first, n_used, tile_slot, next_expert, w_up[0], b_up[0], w_down[0],
                  b_down[0])
    out = _combine(n_blocks, blk_dst, ys, lp.T, wts.T, x1.reshape(tok, d), mod, norm_g[0], tt, n_stage,
                   n_lat // tt)
    return out.reshape(nb, n_lat, d)
```

```python
import functools

import jax
import jax.numpy as jnp
import numpy as np
from jax import lax
from jax.experimental import pallas as pl
from jax.experimental.pallas import tpu as pltpu

F32 = jnp.float32
BF16 = jnp.bfloat16
I32 = jnp.int32

D_MODEL = 1024
GRID_W = 64
N_MOD = 6
EPS = 1e-6
HGRN_HEADS = 8
HGRN_DK = 128
HGRN_W = HGRN_HEADS * HGRN_DK
ATT_HEADS = 8
ATT_KV_HEADS = 2
ATT_GROUPS = ATT_HEADS // ATT_KV_HEADS
HEAD_DIM = 128
ATT_W = ATT_HEADS * HEAD_DIM
KV_W = ATT_KV_HEADS * HEAD_DIM
ROPE_AXIS_DIM = HEAD_DIM // 2
ROPE_THETA = 10000.0
N_EXPERTS = 32
TOP_K = 4
D_EXPERT = 1024
SWIGLU_LIMIT = 7.0
SWIGLU_ALPHA = 1.702

COL_HQ, COL_HI, COL_FF, COL_FB, COL_HG, COL_AQ, COL_GH, COL_GA = (i * 1024 for i in range(8))
COL_K = 8 * 1024
COL_V = COL_K + KV_W
IN_COLS = COL_V + KV_W

V7X_VMEM_BYTES = 64 * 1024 * 1024
LANES = 128

HGRN_CHUNK = 64
HGRN_STEP_CHUNKS = 4
HGRN_SUB = 16
MOE_TILE = 256
MERGE_ROWS = 128
INPROJ_CHAINS = 2
DISPATCH_TILE = 512
RUN_ALIGN = 8
PERM_ROWS = 256
WAIT_GROUP_LOG2 = 5


def _vmem_limit(nbytes):
    return int(min(V7X_VMEM_BYTES - 6 * 1024 * 1024, max(nbytes, 32 * 1024 * 1024)))


def _largest_tile(n, cap, mult):
    best = None
    for t in range(mult, min(n, cap) + 1, mult):
        if n % t == 0:
            best = t
    assert best is not None, (n, cap, mult)
    return best


def _rms(x, g):
    return x * lax.rsqrt(jnp.mean(x * x, axis=-1, keepdims=True) + EPS) * g


def _dot(a, b):
    return jnp.dot(a, b, preferred_element_type=F32)


def _dot_nt(a, b):
    return lax.dot_general(a, b, (((1,), (1,)), ((), ())), preferred_element_type=F32)


def _dot_tn(a, b):
    return lax.dot_general(a, b, (((0,), (0,)), ((), ())), preferred_element_type=F32)


def _mod_kernel(cv_ref, w_ref, b_ref, o_ref):
    cv = cv_ref[...]
    s = cv * jax.nn.sigmoid(cv)
    o_ref[...] = jnp.dot(s, w_ref[...], preferred_element_type=F32,
                         precision=lax.Precision.HIGHEST) + b_ref[...]


def _modulation(cv, w_mod, b_mod):
    rows, d = cv.shape
    n = w_mod.shape[1]
    tn = 1024
    return pl.pallas_call(
        _mod_kernel,
        grid=(n // tn,),
        in_specs=[pl.BlockSpec((rows, d), lambda j: (0, 0)),
                  pl.BlockSpec((d, tn), lambda j: (0, j)),
                  pl.BlockSpec((1, tn), lambda j: (0, j))],
        out_specs=pl.BlockSpec((rows, tn), lambda j: (0, j)),
        out_shape=jax.ShapeDtypeStruct((rows, n), F32),
        name="modulation",
    )(cv, w_mod, b_mod.reshape(1, n))


def _inproj_kernel(x_ref, mod_ref, g_ref, w_ref, o_ref, *, tm, n_ctx, n_batch):
    b = pl.program_id(1)
    i = pl.program_id(2)
    m_lat = mod_ref[pl.ds(b, 1), :]
    m_ctx = mod_ref[n_batch:n_batch + 1, :]
    rows = tm // INPROJ_CHAINS
    us = []
    for c in range(INPROJ_CHAINS):
        xn = _rms(x_ref[0, c * rows:(c + 1) * rows, :], g_ref[0:1, :])
        row = i * tm + c * rows + lax.broadcasted_iota(I32, (rows, 1), 0)
        is_ctx = row < n_ctx
        shift = jnp.where(is_ctx, m_ctx[:, 0:D_MODEL], m_lat[:, 0:D_MODEL])
        scale = jnp.where(is_ctx, m_ctx[:, D_MODEL:2 * D_MODEL], m_lat[:, D_MODEL:2 * D_MODEL])
        us.append((xn * (1.0 + scale) + shift).astype(BF16))
    for c in range(INPROJ_CHAINS):
        o_ref[0, c * rows:(c + 1) * rows, :] = _dot(us[c], w_ref[...]).astype(BF16)


def _in_projection(xc, mod, norm_g, w_in_bf, n_ctx):
    nb, lc, d = xc.shape
    n = w_in_bf.shape[1]
    tm = _largest_tile(lc, 1056, 16)
    tn = 2176
    assert n % tn == 0
    kern = functools.partial(_inproj_kernel, tm=tm, n_ctx=n_ctx, n_batch=nb)
    vmem = 2 * (tm * d * 4 + d * tn * 2 + tm * tn * 2) + 6 * tm * d * 4
    return pl.pallas_call(
        kern,
        grid=(n // tn, nb, lc // tm),
        in_specs=[pl.BlockSpec((1, tm, d), lambda j, b, i: (b, i, 0)),
                  pl.BlockSpec(mod.shape, lambda j, b, i: (0, 0)),
                  pl.BlockSpec(norm_g.shape, lambda j, b, i: (0, 0)),
                  pl.BlockSpec((d, tn), lambda j, b, i: (0, j))],
        out_specs=pl.BlockSpec((1, tm, tn), lambda j, b, i: (b, i, j)),
        out_shape=jax.ShapeDtypeStruct((nb, lc, n), BF16),
        compiler_params=pltpu.CompilerParams(
            dimension_semantics=("arbitrary", "arbitrary", "arbitrary"),
            vmem_limit_bytes=_vmem_limit(vmem)),
        name="in_projection",
    )(xc, mod, norm_g, w_in_bf)


def _rotate_half_matrix():
    half = ROPE_AXIS_DIM // 2
    i = lax.broadcasted_iota(I32, (HEAD_DIM, HEAD_DIM), 0)
    j = lax.broadcasted_iota(I32, (HEAD_DIM, HEAD_DIM), 1)
    first = jnp.bitwise_and(j, ROPE_AXIS_DIM - 1) < half
    r = jnp.where(i == j + half, jnp.where(first, -1.0, 0.0), jnp.where(i == j - half, jnp.where(first, 0.0, 1.0), 0.0))
    return r.astype(BF16)


def _rope(xn, cos, sin, rot):
    hi = xn.astype(BF16)
    lo = (xn - hi.astype(F32)).astype(BF16)
    return xn * cos + (_dot(hi, rot) + _dot(lo, rot)) * sin


def _qkprep_kernel(q_ref, k_ref, v_ref, cos_ref, sa_ref, sb_ref, g_ref, qt_ref, ko_ref, vt_ref):
    cos, sn = cos_ref[...], sb_ref[...] - sa_ref[...]
    rot = _rotate_half_matrix()
    gq, gk = g_ref[0:1, :], g_ref[1:2, :]
    qscale = HEAD_DIM ** -0.5 * np.log2(np.e)
    for h in range(ATT_HEADS):
        hs = slice(h * HEAD_DIM, (h + 1) * HEAD_DIM)
        xn = _rms(q_ref[0, :, hs].astype(F32), gq)
        qt_ref[0, hs, :] = (_rope(xn, cos, sn, rot) * qscale).T.astype(BF16)
    for h in range(ATT_KV_HEADS):
        hs = slice(h * HEAD_DIM, (h + 1) * HEAD_DIM)
        xn = _rms(k_ref[0, :, hs].astype(F32), gk)
        ko_ref[0, :, hs] = _rope(xn, cos, sn, rot).astype(BF16)
        vt_ref[0, hs, :] = v_ref[0, :, hs].astype(F32).T.astype(BF16)


def _qk_prep(p, cos, sin_a, sin_b, qk_norm_g):
    nb, lc, _ = p.shape
    tm = _largest_tile(lc, 512, 128)
    tab = pl.BlockSpec((tm, HEAD_DIM), lambda b, i: (i, 0))
    return pl.pallas_call(
        _qkprep_kernel,
        grid=(nb, lc // tm),
        in_specs=[pl.BlockSpec((1, tm, ATT_W), lambda b, i: (b, i, COL_AQ // ATT_W)),
                  pl.BlockSpec((1, tm, KV_W), lambda b, i: (b, i, COL_K // KV_W)),
                  pl.BlockSpec((1, tm, KV_W), lambda b, i: (b, i, COL_V // KV_W)),
                  tab, tab, tab,
                  pl.BlockSpec(qk_norm_g.shape, lambda b, i: (0, 0))],
        out_specs=[pl.BlockSpec((1, ATT_W, tm), lambda b, i: (b, 0, i)),
                   pl.BlockSpec((1, tm, KV_W), lambda b, i: (b, i, 0)),
                   pl.BlockSpec((1, KV_W, tm), lambda b, i: (b, 0, i))],
        out_shape=[jax.ShapeDtypeStruct((nb, ATT_W, lc), BF16),
                   jax.ShapeDtypeStruct((nb, lc, KV_W), BF16),
                   jax.ShapeDtypeStruct((nb, KV_W, lc), BF16)],
        name="qk_prep",
    )(p, p, p, cos, sin_a, sin_b, qk_norm_g)


def _attn_kernel(qt_ref, k_ref, vt_ref, o_ref, acc_ref, s_ref, *, tq, kc, n_kc):
    acc_ref[...] = jnp.zeros(acc_ref.shape, F32)
    qts = [qt_ref[0, g * HEAD_DIM:(g + 1) * HEAD_DIM, :] for g in range(ATT_GROUPS)]

    def put_scores(c, slot):
        kch = k_ref[0, pl.ds(pl.multiple_of(c * kc, kc), kc), :]
        for g in range(ATT_GROUPS):
            s_ref[slot, g] = _dot(kch, qts[g])

    def softmax_pv(c, slot, carry):
        vch = vt_ref[0, :, pl.ds(pl.multiple_of(c * kc, kc), kc)]
        out = []
        for g in range(ATT_GROUPS):
            m_old, l_old = carry[g]
            s = s_ref[slot, g]
            m_new = jnp.maximum(m_old, jnp.max(s, axis=0, keepdims=True))
            alpha = jnp.exp2(m_old - m_new)
            p = jnp.exp2(s - m_new)
            l_new = l_old * alpha + jnp.sum(p, axis=0, keepdims=True)
            acc_ref[g] = acc_ref[g] * alpha + _dot(vch, p.astype(BF16))
            out.append((m_new, l_new))
        return tuple(out)

    def body(i, carry):
        c = 2 * i
        put_scores(c + 1, 1)
        carry = softmax_pv(c, 0, carry)
        put_scores(c + 2, 0)
        return softmax_pv(c + 1, 1, carry)

    init = tuple((jnp.full((1, tq), -jnp.inf, F32), jnp.zeros((1, tq), F32)) for _ in range(ATT_GROUPS))
    put_scores(0, 0)
    fin = lax.fori_loop(0, (n_kc - 1) // 2, body, init)
    if n_kc % 2 == 0:
        put_scores(n_kc - 1, 1)
        fin = softmax_pv(n_kc - 2, 0, fin)
    fin = softmax_pv(n_kc - 1, (n_kc - 1) % 2, fin)
    for g in range(ATT_GROUPS):
        o_ref[0, :, g * HEAD_DIM:(g + 1) * HEAD_DIM] = (acc_ref[g] / fin[g][1]).T.astype(BF16)


def _attention(qt, kn, vt, n_ctx):
    nb, lc, _ = kn.shape
    n_lat = lc - n_ctx
    tq = _largest_tile(n_lat, 256, 128)
    assert n_ctx % tq == 0
    kc = _largest_tile(lc, 768, 256)
    gw = ATT_GROUPS * HEAD_DIM
    kern = functools.partial(_attn_kernel, tq=tq, kc=kc, n_kc=lc // kc)
    vmem = 8 * lc * HEAD_DIM * 2 + 4 * ATT_GROUPS * kc * tq * 4 + 8 * gw * tq * 4
    return pl.pallas_call(
        kern,
        grid=(nb, ATT_KV_HEADS, n_lat // tq),
        in_specs=[pl.BlockSpec((1, gw, tq), lambda b, h, i: (b, h, n_ctx // tq + i)),
                  pl.BlockSpec((1, lc, HEAD_DIM), lambda b, h, i: (b, 0, h)),
                  pl.BlockSpec((1, HEAD_DIM, lc), lambda b, h, i: (b, h, 0))],
        out_specs=pl.BlockSpec((1, tq, gw), lambda b, h, i: (b, i, h)),
        out_shape=jax.ShapeDtypeStruct((nb, n_lat, ATT_W), BF16),
        scratch_shapes=[pltpu.VMEM((ATT_GROUPS, HEAD_DIM, tq), F32),
                        pltpu.VMEM((2, ATT_GROUPS, kc, tq), F32)],
        compiler_params=pltpu.CompilerParams(
            dimension_semantics=("arbitrary", "arbitrary", "arbitrary"),
            vmem_limit_bytes=_vmem_limit(vmem)),
        name="attention",
    )(qt, kn, vt)


def _hgrn_chunk(q_ref, v_ref, r_ref, lb, reverse, rows):
    c = HGRN_CHUNK
    q = q_ref[0, rows, :].astype(F32)
    v = v_ref[0, rows, :]
    r = r_ref[0, rows, :].astype(F32)
    sig = jax.nn.sigmoid(r)
    f = lb + (1.0 - lb) * sig
    logf = jnp.log(f)
    k = (1.0 - lb) * (1.0 - sig)
    ti = lax.broadcasted_iota(I32, (c, c), 0)
    si = lax.broadcasted_iota(I32, (c, c), 1)
    tri = jnp.where((si >= ti) if reverse else (si <= ti), 1.0, 0.0).astype(BF16)
    hi = logf.astype(BF16)
    lo = (logf - hi.astype(F32)).astype(BF16)
    bcum = _dot(tri, hi) + _dot(tri, lo)
    b_end = bcum[0:1, :] if reverse else bcum[c - 1:c, :]
    qt = (q * jnp.exp(bcum)).astype(BF16)
    kt = (k * jnp.exp(b_end - bcum)).astype(BF16)
    dec = jnp.exp(b_end)

    nsub = c // HGRN_SUB
    qp, kp, cols = [], [], []
    for blk in range(nsub):
        rs = slice(blk * HGRN_SUB, (blk + 1) * HGRN_SUB)
        cs = slice(blk * HGRN_SUB, c) if reverse else slice(0, (blk + 1) * HGRN_SUB)
        mid = blk * HGRN_SUB + HGRN_SUB // 2
        ref = bcum[mid:mid + 1, :]
        qp.append((q[rs] * jnp.exp(bcum[rs] - ref)).astype(BF16))
        kp.append((k[cs] * jnp.exp(ref - bcum[cs])).astype(BF16))
        cols.append(cs)

    keep = []
    for blk in range(nsub):
        n_cols = cols[blk].stop - cols[blk].start
        trow = blk * HGRN_SUB + lax.broadcasted_iota(I32, (HGRN_SUB, n_cols), 0)
        scol = cols[blk].start + lax.broadcasted_iota(I32, (HGRN_SUB, n_cols), 1)
        keep.append((scol >= trow) if reverse else (scol <= trow))
    return dict(v=v, qt=qt, kt=kt, dec=dec, qp=qp, kp=kp, cols=cols, keep=keep, rows=rows)


def _hgrn_emit(dirs, s_refs, o_refs):
    nsub = HGRN_CHUNK // HGRN_SUB
    heads = [slice(h * HGRN_DK, (h + 1) * HGRN_DK) for h in range(HGRN_HEADS)]
    inter = [[None] * len(chunks) for chunks in dirs]
    for k in range(len(dirs[0])):
        for di, (chunks, s_ref) in enumerate(zip(dirs, s_refs)):
            d = chunks[k]
            row = []
            for h, hs in enumerate(heads):
                st = s_ref[h]
                row.append(_dot_nt(d["qt"][:, hs], st.astype(BF16)))
                s_ref[h] = st * d["dec"][:, hs] + _dot_tn(d["v"][:, hs], d["kt"][:, hs])
            inter[di][k] = row
    scores = [[[[_dot_nt(d["qp"][blk][:, hs], d["kp"][blk][:, hs]) for blk in range(nsub)] for hs in heads]
               for d in chunks] for chunks in dirs]
    for chunks, sc_d, it_d, o_ref in zip(dirs, scores, inter, o_refs):
        for d, sc, it in zip(chunks, sc_d, it_d):
            outs = []
            for h, hs in enumerate(heads):
                parts = [_dot(jnp.where(d["keep"][blk], sc[h][blk], 0.0).astype(BF16), d["v"][d["cols"][blk], hs])
                         for blk in range(nsub)]
                outs.append(it[h] + jnp.concatenate(parts, axis=0))
            o_ref[0, d["rows"], :] = jnp.concatenate(outs, axis=1).astype(BF16)


def _hgrn_kernel(qf_ref, vf_ref, rf_ref, qb_ref, vb_ref, rb_ref, lb_ref, of_ref, ob_ref, sf_ref, sb_ref):
    @pl.when(pl.program_id(1) == 0)
    def _():
        sf_ref[...] = jnp.zeros(sf_ref.shape, F32)
        sb_ref[...] = jnp.zeros(sb_ref.shape, F32)

    n_layers = lb_ref.shape[0] // 2

    def lower_bound(direction):
        rows = [lb_ref[direction * n_layers + l:direction * n_layers + l + 1, :] for l in range(n_layers)]
        amax = functools.reduce(jnp.maximum, rows)
        e = [jnp.exp(a - amax) for a in rows]
        return e[0] / functools.reduce(lambda u, w: u + w, e)

    rows = [slice(k * HGRN_CHUNK, (k + 1) * HGRN_CHUNK) for k in range(HGRN_STEP_CHUNKS)]
    fwd = [_hgrn_chunk(qf_ref, vf_ref, rf_ref, lower_bound(0), False, rs) for rs in rows]
    bwd = [_hgrn_chunk(qb_ref, vb_ref, rb_ref, lower_bound(1), True, rs) for rs in reversed(rows)]
    _hgrn_emit([fwd, bwd], [sf_ref, sb_ref], [of_ref, ob_ref])


def _hgrn_scan(p, hgrn_lb, n_ctx):
    nb, lc, _ = p.shape
    c = HGRN_CHUNK * HGRN_STEP_CHUNKS
    assert n_ctx % c == 0 and lc % c == 0
    n_chunks = lc // c
    ctx_chunks = n_ctx // c

    def fwd(col):
        return lambda b, s: (b, s, col)

    def bwd_chunk(s):
        return jnp.where(s < ctx_chunks, ctx_chunks - 1 - s, n_chunks - 1 + ctx_chunks - s)

    def bwd(col):
        return lambda b, s: (b, bwd_chunk(s), col)

    blk = (1, c, HGRN_W)
    lb2 = hgrn_lb.reshape(-1, HGRN_W)
    return pl.pallas_call(
        _hgrn_kernel,
        grid=(nb, n_chunks),
        in_specs=[pl.BlockSpec(blk, fwd(COL_HQ // HGRN_W)), pl.BlockSpec(blk, fwd(COL_HI // HGRN_W)),
                  pl.BlockSpec(blk, fwd(COL_FF // HGRN_W)),
                  pl.BlockSpec(blk, bwd(COL_HQ // HGRN_W)), pl.BlockSpec(blk, bwd(COL_HI // HGRN_W)),
                  pl.BlockSpec(blk, bwd(COL_FB // HGRN_W)),
                  pl.BlockSpec(lb2.shape, lambda b, s: (0, 0))],
        out_specs=[pl.BlockSpec(blk, fwd(0)), pl.BlockSpec(blk, bwd(0))],
        out_shape=[jax.ShapeDtypeStruct((nb, lc, HGRN_W), BF16)] * 2,
        scratch_shapes=[pltpu.VMEM((HGRN_HEADS, HGRN_DK, HGRN_DK), F32)] * 2,
        compiler_params=pltpu.CompilerParams(dimension_semantics=("arbitrary", "arbitrary")),
        name="hgrn_scan",
    )(p, p, p, p, p, p, lb2)


def _merge_kernel(of_ref, ob_ref, hg_ref, gh_ref, ga_ref, oa_ref, x_ref, mod_ref, ng_ref, hng_ref,
                  wb0_ref, wb1_ref, wo_ref, rw_ref, rb_ref,
                  x1_ref, t_ref, ids_ref, wts_ref):
    b = pl.program_id(0)
    m = mod_ref[pl.ds(b, 1), :]
    gate_mix = m[:, 2 * D_MODEL:3 * D_MODEL]
    shift_f = m[:, 3 * D_MODEL:4 * D_MODEL]
    scale_f = m[:, 4 * D_MODEL:5 * D_MODEL]
    hng = hng_ref[...]
    rw = rw_ref[...]
    rw_hi = rw.astype(BF16)
    rw_lo = (rw - rw_hi.astype(F32)).astype(BF16)
    groups = [slice(r, r + MERGE_ROWS) for r in range(0, of_ref.shape[1], MERGE_ROWS)]

    branch = []
    for rs in groups:
        o = of_ref[0, rs, :].astype(F32) + ob_ref[0, rs, :].astype(F32)
        o_h = jnp.concatenate(
            [_rms(o[:, h * HGRN_DK:(h + 1) * HGRN_DK], hng) for h in range(HGRN_HEADS)], axis=1)
        g_raw = hg_ref[0, rs, :].astype(F32)
        o_h = o_h * (g_raw * jax.nn.sigmoid(g_raw))
        branch.append((_dot(o_h.astype(BF16), wb0_ref[...]), _dot(oa_ref[0, rs, :], wb1_ref[...])))
    mixed = []
    for rs, (y_h, y_a) in zip(groups, branch):
        y = (jax.nn.sigmoid(gh_ref[0, rs, :].astype(F32)) * y_h
             + jax.nn.sigmoid(ga_ref[0, rs, :].astype(F32)) * y_a)
        mixed.append(_dot(y.astype(BF16), wo_ref[...]))
    logits_all = []
    for rs, y in zip(groups, mixed):
        x1 = x_ref[0, rs, :] + gate_mix * _rms(y, ng_ref[1:2, :])
        x1_ref[0, rs, :] = x1
        t = _rms(x1, ng_ref[2:3, :]) * (1.0 + scale_f) + shift_f
        t_ref[rs, :] = t.astype(BF16)
        t_hi = t.astype(BF16)
        t_lo = (t - t_hi.astype(F32)).astype(BF16)
        logits_all.append(_dot_nt(rw_hi, t_hi) + _dot_nt(rw_hi, t_lo) + _dot_nt(rw_lo, t_hi) + rb_ref[...])
    for rs, logits in zip(groups, logits_all):
        eidx = lax.broadcasted_iota(I32, logits.shape, 0).astype(F32)
        vals = []
        for j in range(TOP_K):
            mx = jnp.max(logits, axis=0, keepdims=True)
            idx = jnp.min(jnp.where(logits == mx, eidx, float(N_EXPERTS)), axis=0, keepdims=True)
            ids_ref[j:j + 1, rs] = idx.astype(I32)
            vals.append(mx)
            logits = jnp.where(eidx == idx, -jnp.inf, logits)
        ex = [jnp.exp(vj - vals[0]) for vj in vals]
        den = ex[0] + ex[1] + ex[2] + ex[3]
        for j in range(TOP_K):
            wts_ref[j:j + 1, rs] = ex[j] / den


def _merge(o_f, o_b, p, o_a, x, mod, norm_g, hgrn_norm_g, wb0, wb1, wo, router_wt, router_b, n_ctx):
    nb, n_lat, d = x.shape
    tm = _largest_tile(n_lat, 256, 128)
    assert n_ctx % tm == 0
    off = n_ctx // tm
    nt = n_lat // tm
    row = (1, tm, d)

    def pcol(col):
        return pl.BlockSpec(row, lambda b, i: (b, off + i, col // d))

    def full(a):
        return pl.BlockSpec(a.shape, lambda b, i: (0,) * a.ndim)

    tok = nb * n_lat
    return pl.pallas_call(
        _merge_kernel,
        grid=(nb, nt),
        in_specs=[pl.BlockSpec(row, lambda b, i: (b, off + i, 0)),
                  pl.BlockSpec(row, lambda b, i: (b, off + i, 0)),
                  pcol(COL_HG), pcol(COL_GH), pcol(COL_GA),
                  pl.BlockSpec(row, lambda b, i: (b, i, 0)),
                  pl.BlockSpec(row, lambda b, i: (b, i, 0)),
                  full(mod), full(norm_g), full(hgrn_norm_g), full(wb0), full(wb1), full(wo),
                  full(router_wt), full(router_b)],
        out_specs=[pl.BlockSpec(row, lambda b, i: (b, i, 0)),
                   pl.BlockSpec((tm, d), lambda b, i: (b * nt + i, 0)),
                   pl.BlockSpec((TOP_K, tm), lambda b, i: (0, b * nt + i)),
                   pl.BlockSpec((TOP_K, tm), lambda b, i: (0, b * nt + i))],
        out_shape=[jax.ShapeDtypeStruct((nb, n_lat, d), F32),
                   jax.ShapeDtypeStruct((tok, d), BF16),
                   jax.ShapeDtypeStruct((TOP_K, tok), I32),
                   jax.ShapeDtypeStruct((TOP_K, tok), F32)],
        compiler_params=pltpu.CompilerParams(
            dimension_semantics=("arbitrary", "arbitrary"),
            vmem_limit_bytes=_vmem_limit(40 * 1024 * 1024)),
        name="merge_router",
    )(o_f, o_b, p, p, p, o_a, x, mod, norm_g, hgrn_norm_g, wb0, wb1, wo, router_wt, router_b)


def _rank_kernel(ids_ref, rank_ref, cnt_ref, *, tt):
    eidx = lax.broadcasted_iota(I32, (N_EXPERTS, tt), 0)
    si = lax.broadcasted_iota(I32, (tt, tt), 0)
    ti = lax.broadcasted_iota(I32, (tt, tt), 1)
    before = jnp.where(si < ti, 1.0, 0.0).astype(BF16)
    seen = jnp.zeros((N_EXPERTS, 1), F32)
    for j in range(TOP_K):
        onehot = eidx == ids_ref[j:j + 1, :]
        oh = jnp.where(onehot, 1.0, 0.0)
        earlier = _dot(oh.astype(BF16), before)
        rank = jnp.sum(jnp.where(onehot, seen + earlier, 0.0), axis=0, keepdims=True)
        rank_ref[j:j + 1, :] = rank.astype(I32)
        seen = seen + jnp.sum(oh, axis=1, keepdims=True)
    cnt_ref[0] = seen.astype(I32)


def _expert_rank(ids, tt):
    _, tok = ids.shape
    return pl.pallas_call(
        functools.partial(_rank_kernel, tt=tt),
        grid=(tok // tt,),
        in_specs=[pl.BlockSpec((TOP_K, tt), lambda i: (0, i))],
        out_specs=[pl.BlockSpec((TOP_K, tt), lambda i: (0, i)),
                   pl.BlockSpec((1, N_EXPERTS, 1), lambda i: (i, 0, 0))],
        out_shape=[jax.ShapeDtypeStruct((TOP_K, tok), I32),
                   jax.ShapeDtypeStruct((tok // tt, N_EXPERTS, 1), I32)],
        compiler_params=pltpu.CompilerParams(dimension_semantics=("arbitrary",)),
        name="expert_rank",
    )(ids)


def _start_blocks(n_blocks, make_copy):
    def start_pair(g, carry):
        make_copy(2 * g).start(priority=0)
        make_copy(2 * g + 1).start(priority=1)
        return carry

    lax.fori_loop(0, lax.shift_right_logical(n_blocks, 1), start_pair, 0)

    @pl.when(jnp.bitwise_and(n_blocks, 1) == 1)
    def _():
        make_copy(n_blocks - 1).start(priority=0)


def _wait_blocks(n_blocks, make_copy, make_group_copy):
    def wait_group(g, carry):
        make_group_copy().wait()
        return carry

    def wait_one(b, carry):
        make_copy(0).wait()
        return carry

    lax.fori_loop(0, lax.shift_right_logical(n_blocks, WAIT_GROUP_LOG2), wait_group, 0)
    lax.fori_loop(0, jnp.bitwise_and(n_blocks, (1 << WAIT_GROUP_LOG2) - 1), wait_one, 0)


def _scatter_kernel(zt_ref, zvalid_ref, nblk_ref, bdst_ref, lp_ref, t_ref, xs_ref, stage, zero_buf, sem, zsem,
                    *, tt, n_stage):
    @pl.when(pl.program_id(0) == 0)
    def _():
        zero_buf[...] = jnp.zeros(zero_buf.shape, F32)

        def zero_copy(k):
            return pltpu.make_async_copy(zero_buf, xs_ref.at[pl.ds(zt_ref[k] * MOE_TILE, MOE_TILE)], zsem)

        def start(k, carry):
            @pl.when(zvalid_ref[k] == 1)
            def _():
                zero_copy(k).start()
            return carry

        def wait(k, carry):
            @pl.when(zvalid_ref[k] == 1)
            def _():
                zero_copy(k).wait()
            return carry

        lax.fori_loop(0, zt_ref.shape[0], start, 0)
        lax.fori_loop(0, zt_ref.shape[0], wait, 0)

    i = pl.program_id(0)
    slot = i % 2
    t_bf = t_ref[...]
    lps = [lp_ref[j:j + 1, :] for j in range(TOP_K)]
    for rc in range(n_stage // PERM_ROWS):
        row = rc * PERM_ROWS + lax.broadcasted_iota(I32, (PERM_ROWS, tt), 0)
        sel = jnp.where(row == lps[TOP_K - 1], 1.0, 0.0)
        for j in range(TOP_K - 2, -1, -1):
            sel = jnp.where(row == lps[j], 1.0, sel)
        stage[slot, rc * PERM_ROWS:(rc + 1) * PERM_ROWS, :] = _dot(sel.astype(BF16), t_bf)

    def copy(b):
        src = pl.multiple_of(b * RUN_ALIGN, RUN_ALIGN)
        dst = pl.multiple_of(bdst_ref[0, 0, b], RUN_ALIGN)
        return pltpu.make_async_copy(stage.at[slot, pl.ds(src, RUN_ALIGN)], xs_ref.at[pl.ds(dst, RUN_ALIGN)],
                                     sem.at[slot])

    def waiter(s):
        rows = RUN_ALIGN << WAIT_GROUP_LOG2
        return (lambda b: pltpu.make_async_copy(stage.at[s, pl.ds(0, RUN_ALIGN)],
                                                xs_ref.at[pl.ds(0, RUN_ALIGN)], sem.at[s]),
                lambda: pltpu.make_async_copy(stage.at[s, pl.ds(0, rows)], xs_ref.at[pl.ds(0, rows)], sem.at[s]))

    _start_blocks(nblk_ref[i], copy)

    @pl.when(i > 0)
    def _():
        _wait_blocks(nblk_ref[i - 1], *waiter(1 - slot))

    @pl.when(i == pl.num_programs(0) - 1)
    def _():
        _wait_blocks(nblk_ref[i], *waiter(slot))


def _scatter_rows(zero_tiles, zero_valid, n_blocks, block_rows, lp, t, n_rows, tt, n_stage):
    tok, d = t.shape
    nb = block_rows.shape[2]
    return pl.pallas_call(
        functools.partial(_scatter_kernel, tt=tt, n_stage=n_stage),
        grid_spec=pltpu.PrefetchScalarGridSpec(
            num_scalar_prefetch=3,
            grid=(tok // tt,),
            in_specs=[pl.BlockSpec((1, 1, nb), lambda i, *_: (i, 0, 0), memory_space=pltpu.SMEM),
                      pl.BlockSpec((TOP_K, tt), lambda i, *_: (0, i)),
                      pl.BlockSpec((tt, d), lambda i, *_: (i, 0))],
            out_specs=pl.BlockSpec(memory_space=pl.ANY),
            scratch_shapes=[pltpu.VMEM((2, n_stage, d), F32), pltpu.VMEM((MOE_TILE, d), F32),
                            pltpu.SemaphoreType.DMA((2,)), pltpu.SemaphoreType.DMA(())]),
        out_shape=jax.ShapeDtypeStruct((n_rows, d), F32),
        compiler_params=pltpu.CompilerParams(
            dimension_semantics=("arbitrary",), vmem_limit_bytes=_vmem_limit(48 * 1024 * 1024)),
        name="moe_scatter",
    )(zero_tiles, zero_valid, n_blocks, block_rows, lp, t)


def _expert_kernel(te_ref, first_ref, nused_ref, slot_ref, next_ref, xs_ref, wu_hbm, bu_ref, wd_hbm, bd_ref,
                   ys_ref, wu32, wd32, wu_bf, wd_bf, sem_u, sem_d):
    i = pl.program_id(0)

    def weight_copies(e, s):
        return (pltpu.make_async_copy(wu_hbm.at[e], wu32.at[s], sem_u.at[s]),
                pltpu.make_async_copy(wd_hbm.at[e], wd32.at[s], sem_d.at[s]))

    @pl.when(i >= nused_ref[0])
    def _():
        ys_ref[...] = jnp.zeros(ys_ref.shape, F32)

    @pl.when(i < nused_ref[0])
    def _():
        @pl.when(first_ref[i] == 1)
        def _():
            s = slot_ref[i]

            @pl.when(i == 0)
            def _():
                for cp in weight_copies(te_ref[0], 0):
                    cp.start()

            for cp in weight_copies(te_ref[i], s):
                cp.wait()
            wu_bf[...] = wu32[s].astype(BF16)
            wd_bf[...] = wd32[s].astype(BF16)

            @pl.when(next_ref[i] >= 0)
            def _():
                for cp in weight_copies(next_ref[i], 1 - s):
                    cp.start()

        h = _dot(xs_ref[...].astype(BF16), wu_bf[...]) + bu_ref[0]
        glu = jnp.minimum(h[:, :D_EXPERT], SWIGLU_LIMIT)
        lin = jnp.clip(h[:, D_EXPERT:], -SWIGLU_LIMIT, SWIGLU_LIMIT)
        a = glu * jax.nn.sigmoid(SWIGLU_ALPHA * glu) * (lin + 1.0)
        ys_ref[...] = _dot(a.astype(BF16), wd_bf[...]) + bd_ref[0]


def _experts(xs, tile_expert, tile_first, n_used, tile_slot, next_expert, w_up, b_up, w_down, b_down):
    n_rows, d = xs.shape
    tm = MOE_TILE
    n_tiles = n_rows // tm

    def tile(i, te, first, nused, *_):
        return (jnp.minimum(i, nused[0] - 1), 0)

    def bsel(i, te, *_):
        return (te[i], 0, 0)

    vmem = 2 * (d * 2 * D_EXPERT * 4 + D_EXPERT * d * 4) + d * 2 * D_EXPERT * 2 + D_EXPERT * d * 2 \
        + 4 * tm * d * 4 + 6 * tm * 2 * D_EXPERT * 4
    return pl.pallas_call(
        _expert_kernel,
        grid_spec=pltpu.PrefetchScalarGridSpec(
            num_scalar_prefetch=5,
            grid=(n_tiles,),
            in_specs=[pl.BlockSpec((tm, d), tile),
                      pl.BlockSpec(memory_space=pl.ANY),
                      pl.BlockSpec((1, 1, 2 * D_EXPERT), bsel),
                      pl.BlockSpec(memory_space=pl.ANY),
                      pl.BlockSpec((1, 1, d), bsel)],
            out_specs=pl.BlockSpec((tm, d), lambda i, *_: (i, 0)),
            scratch_shapes=[pltpu.VMEM((2, d, 2 * D_EXPERT), F32), pltpu.VMEM((2, D_EXPERT, d), F32),
                            pltpu.VMEM((d, 2 * D_EXPERT), BF16), pltpu.VMEM((D_EXPERT, d), BF16),
                            pltpu.SemaphoreType.DMA((2,)), pltpu.SemaphoreType.DMA((2,))]),
        out_shape=jax.ShapeDtypeStruct((n_rows, d), F32),
        compiler_params=pltpu.CompilerParams(
            dimension_semantics=("arbitrary",), vmem_limit_bytes=_vmem_limit(vmem)),
        name="moe_experts",
    )(tile_expert, tile_first, n_used, tile_slot, next_expert, xs, w_up, b_up.reshape(N_EXPERTS, 1, -1), w_down,
      b_down.reshape(N_EXPERTS, 1, -1))


def _combine_kernel(nblk_ref, bsrc_ref, ys_ref, lp_ref, w_ref, x1_ref, mod_ref, ng_ref, o_ref, stage, sem,
                    *, tt, n_stage, tiles_per_batch):
    i = pl.program_id(0)
    slot = i % 2

    def fetch(tile, s):
        def copy(b):
            src = pl.multiple_of(bsrc_ref[tile, b], RUN_ALIGN)
            dst = pl.multiple_of(b * RUN_ALIGN, RUN_ALIGN)
            return pltpu.make_async_copy(ys_ref.at[pl.ds(src, RUN_ALIGN)], stage.at[s, pl.ds(dst, RUN_ALIGN)],
                                         sem.at[s])
        return copy

    def group_copy():
        rows = RUN_ALIGN << WAIT_GROUP_LOG2
        return pltpu.make_async_copy(ys_ref.at[pl.ds(0, rows)], stage.at[slot, pl.ds(0, rows)], sem.at[slot])

    @pl.when(i == 0)
    def _():
        stage[...] = jnp.zeros(stage.shape, F32)
        _start_blocks(nblk_ref[0], fetch(0, 0))

    @pl.when(i + 1 < pl.num_programs(0))
    def _():
        _start_blocks(nblk_ref[i + 1], fetch(i + 1, 1 - slot))

    _wait_blocks(nblk_ref[i], fetch(i, slot), group_copy)

    lp = [jnp.broadcast_to(lp_ref[:, j:j + 1], (tt, PERM_ROWS)) for j in range(TOP_K)]
    w = [jnp.broadcast_to(w_ref[:, j:j + 1], (tt, PERM_ROWS)) for j in range(TOP_K)]
    f = jnp.zeros((tt, D_MODEL), F32)
    for lc in range(n_stage // PERM_ROWS):
        col = lc * PERM_ROWS + lax.broadcasted_iota(I32, (tt, PERM_ROWS), 1)
        wp = jnp.where(col == lp[TOP_K - 1], w[TOP_K - 1], 0.0)
        for j in range(TOP_K - 2, -1, -1):
            wp = jnp.where(col == lp[j], w[j], wp)
        f = f + _dot(wp.astype(BF16), stage[slot, lc * PERM_ROWS:(lc + 1) * PERM_ROWS, :].astype(BF16))
    b = i // tiles_per_batch
    gate_f = mod_ref[pl.ds(b, 1), 5 * D_MODEL:6 * D_MODEL]
    o_ref[...] = x1_ref[...] + gate_f * _rms(f, ng_ref[3:4, :])


def _combine(n_blocks, block_rows, ys, lp_t, wts_t, x1, mod, norm_g, tt, n_stage, tiles_per_batch):
    tok, d = x1.shape
    kern = functools.partial(_combine_kernel, tt=tt, n_stage=n_stage, tiles_per_batch=tiles_per_batch)
    return pl.pallas_call(
        kern,
        grid_spec=pltpu.PrefetchScalarGridSpec(
            num_scalar_prefetch=2,
            grid=(tok // tt,),
            in_specs=[pl.BlockSpec(memory_space=pl.ANY),
                      pl.BlockSpec((tt, TOP_K), lambda i, *_: (i, 0)),
                      pl.BlockSpec((tt, TOP_K), lambda i, *_: (i, 0)),
                      pl.BlockSpec((tt, d), lambda i, *_: (i, 0)),
                      pl.BlockSpec(mod.shape, lambda i, *_: (0, 0)),
                      pl.BlockSpec(norm_g.shape, lambda i, *_: (0, 0))],
            out_specs=pl.BlockSpec((tt, d), lambda i, *_: (i, 0)),
            scratch_shapes=[pltpu.VMEM((2, n_stage, d), F32), pltpu.SemaphoreType.DMA((2,))]),
        out_shape=jax.ShapeDtypeStruct((tok, d), F32),
        compiler_params=pltpu.CompilerParams(
            dimension_semantics=("arbitrary",), vmem_limit_bytes=_vmem_limit(48 * 1024 * 1024)),
        name="moe_combine",
    )(n_blocks, block_rows[:, 0, :], ys, lp_t, wts_t, x1, mod, norm_g)


def _rope_tables(n_ctx, n_lat):
    inv = ROPE_THETA ** (-np.arange(0, ROPE_AXIS_DIM, 2, dtype=np.float32) / ROPE_AXIS_DIM)
    idx = jnp.arange(n_lat, dtype=I32)
    rows = (idx // GRID_W).astype(F32)
    cols = (idx % GRID_W).astype(F32)
    half = ROPE_AXIS_DIM // 2
    inv = jnp.asarray(inv, F32)
    ang_r = rows[:, None] * inv[None, :]
    ang_c = cols[:, None] * inv[None, :]
    ang = jnp.concatenate([ang_r, ang_r, ang_c, ang_c], axis=1)
    cos, sin = jnp.cos(ang), jnp.sin(ang)
    lane = np.arange(HEAD_DIM)
    first = jnp.asarray((lane % ROPE_AXIS_DIM) < half)
    sin_a = jnp.where(first, -sin, 0.0)
    sin_b = jnp.where(first, 0.0, sin)
    pad1 = jnp.ones((n_ctx, HEAD_DIM), F32)
    pad0 = jnp.zeros((n_ctx, HEAD_DIM), F32)
    return (jnp.concatenate([pad1, cos], axis=0), jnp.concatenate([pad0, sin_a], axis=0),
            jnp.concatenate([pad0, sin_b], axis=0))


def kernel(x, c, ctx, c_ctx, w_mod, b_mod, norm_g, w_in, hgrn_lb, hgrn_norm_g, qk_norm_g, w_branch, w_out,
           router_w, router_b, w_up, b_up, w_down, b_down):
    nb, n_lat, d = x.shape
    n_ctx = ctx.shape[1]
    assert d == D_MODEL and w_mod.shape[0] == 1, "single-layer kernel"
    tok = nb * n_lat

    cv = jnp.zeros((8, d), F32).at[:nb].set(c).at[nb].set(c_ctx)
    mod = _modulation(cv, w_mod[0], b_mod[0])

    xc = jnp.concatenate([ctx, x], axis=1)
    sizes = np.cumsum([0, HGRN_W, HGRN_W, HGRN_W, HGRN_W, HGRN_W, ATT_W, KV_W, KV_W, D_MODEL, D_MODEL])
    order = [0, 1, 2, 3, 4, 5, 8, 9, 6, 7]
    w_in_bf = jnp.concatenate([w_in[0][:, sizes[i]:sizes[i + 1]] for i in order], axis=1).astype(BF16)
    p = _in_projection(xc, mod, norm_g[0], w_in_bf, n_ctx)

    cos, sin_a, sin_b = _rope_tables(n_ctx, n_lat)
    qt, kn, vt = _qk_prep(p, cos, sin_a, sin_b, qk_norm_g[0])
    o_a = _attention(qt, kn, vt, n_ctx)
    o_f, o_b = _hgrn_scan(p, hgrn_lb, n_ctx)

    x1, t, ids, wts = _merge(
        o_f, o_b, p, o_a, x, mod, norm_g[0], hgrn_norm_g, w_branch[0, 0].astype(BF16),
        w_branch[0, 1].astype(BF16), w_out[0].astype(BF16), router_w[0].T, router_b[0].reshape(N_EXPERTS, 1),
        n_ctx)

    tt = _largest_tile(n_lat, DISPATCH_TILE, 128)
    n_tt = tok // tt
    rank, counts = _expert_rank(ids, tt)
    counts = counts[:, :, 0]
    run = (counts + RUN_ALIGN - 1) // RUN_ALIGN * RUN_ALIGN
    run_end = jnp.cumsum(run, axis=1)
    stage_off = run_end - run
    expert_rows = jnp.sum(run, axis=0)
    tiles_e = (expert_rows + MOE_TILE - 1) // MOE_TILE
    tile_end = jnp.cumsum(tiles_e)
    starts = (tile_end - tiles_e) * MOE_TILE
    run_dst = starts[None, :] + jnp.cumsum(run, axis=0) - run
    n_tiles = -(-(tok * TOP_K + N_EXPERTS * n_tt * (RUN_ALIGN - 1)) // MOE_TILE) + N_EXPERTS
    tile_idx = jnp.arange(n_tiles, dtype=I32)
    tile_expert = jnp.sum((tile_end[None, :] <= tile_idx[:, None]).astype(I32), axis=1)
    tile_expert = jnp.minimum(tile_expert, N_EXPERTS - 1)
    n_used = tile_end[-1:].astype(I32)
    tile_expert = jnp.where(tile_idx < n_used[0], tile_expert, tile_expert[jnp.maximum(n_used[0] - 1, 0)])
    tile_first = jnp.concatenate([jnp.ones((1,), I32), (tile_expert[1:] != tile_expert[:-1]).astype(I32)])
    e_idx = jnp.arange(N_EXPERTS, dtype=I32)
    used = tiles_e > 0
    e_slot = (jnp.cumsum(used.astype(I32)) - 1) % 2
    later = used[None, :] & (e_idx[None, :] > e_idx[:, None])
    e_next = jnp.min(jnp.where(later, e_idx[None, :], N_EXPERTS), axis=1)
    e_next = jnp.where(e_next == N_EXPERTS, -1, e_next)
    tile_onehot = tile_expert[:, None] == e_idx[None, :]
    tile_slot = jnp.sum(jnp.where(tile_onehot, e_slot[None, :], 0), axis=1).astype(I32)
    next_expert = jnp.sum(jnp.where(tile_onehot, e_next[None, :], 0), axis=1).astype(I32)

    n_stage = -(-(TOP_K * tt + N_EXPERTS * (RUN_ALIGN - 1)) // PERM_ROWS) * PERM_ROWS
    ids_t = ids.reshape(TOP_K, n_tt, tt)
    onehot = ids_t[..., None] == jnp.arange(N_EXPERTS, dtype=I32)
    lp = jnp.sum(jnp.where(onehot, stage_off[None, :, None, :], 0), axis=-1).reshape(TOP_K, tok) + rank
    blk_row = jnp.arange(n_stage // RUN_ALIGN, dtype=I32) * RUN_ALIGN
    blk_expert = jnp.minimum(jnp.sum((run_end[:, None, :] <= blk_row[None, :, None]).astype(I32), axis=-1),
                             N_EXPERTS - 1)
    blk_onehot = blk_expert[..., None] == jnp.arange(N_EXPERTS, dtype=I32)
    blk_dst = jnp.sum(jnp.where(blk_onehot, (run_dst - stage_off)[:, None, :], 0), axis=-1) + blk_row[None, :]
    n_blocks = (run_end[:, -1] // RUN_ALIGN).astype(I32)
    blk_dst = jnp.where(blk_row[None, :] < run_end[:, -1:], blk_dst, 0).astype(I32)[:, None, :]

    n_tail = n_tiles - (tok * TOP_K) // MOE_TILE
    tail = n_used[0] + jnp.arange(n_tail, dtype=I32)
    zero_tiles = jnp.clip(jnp.concatenate([tile_end.astype(I32) - 1, tail]), 0, n_tiles - 1)
    zero_valid = jnp.concatenate([expert_rows % MOE_TILE != 0, tail < n_tiles]).astype(I32)

    xs = _scatter_rows(zero_tiles, zero_valid, n_blocks, blk_dst, lp, t, n_tiles * MOE_TILE, tt, n_stage)
    ys = _experts(xs, tile_expert, tile_first, n_used, tile_slot, next_expert, w_up[0], b_up[0], w_down[0],
                  b_down[0])
    out = _combine(n_blocks, blk_dst, ys, lp.T, wts.T, x1.reshape(tok, d), mod, norm_g[0], tt, n_stage,
                   n_lat // tt)
    return out.reshape(nb, n_lat, d)
```
